```python
import jax, jax.numpy as jnp
from jax import lax
import numpy as np

D_MODEL = 2048
BATCH = 1
SEQ = 8192
DEPTH = 1

CTX_LEN = 256
GRID_W = 64
HEAD_DIM = 64
N_Q_HEADS = D_MODEL // (2 * HEAD_DIM)
N_KV_HEADS = N_Q_HEADS // 4
ATTN_WIDTH = N_Q_HEADS * HEAD_DIM
KV_WIDTH = N_KV_HEADS * HEAD_DIM
POOL_WINDOWS = (2, 4, 8, 16)
POOL_GROUPS = len(POOL_WINDOWS)
POOL_WIDTH = D_MODEL - ATTN_WIDTH
POOL_GROUP_DIM = POOL_WIDTH // POOL_GROUPS
MIX_WIDTH = ATTN_WIDTH + POOL_WIDTH
IN_WIDTH = ATTN_WIDTH + 2 * KV_WIDTH + POOL_WIDTH
D_FF = 4 * D_MODEL
WINDOW = 128
BLOCK = 128
ROPE_BASE = 10000.0
N_MOD = 6
EPS = 1e-6
NEG_INF = -1e30

kernel_name = "hymba_style_window_gqa_pool_diffusion_block"


def rms_norm(x, w):
    xf = x.astype(jnp.float32)
    y = xf * lax.rsqrt(jnp.mean(xf * xf, axis=-1, keepdims=True) + EPS)
    return (y * w.astype(jnp.float32)).astype(x.dtype)


def ada_modulation(cond, w, b):
    m = jax.nn.silu(cond) @ w + b
    return jnp.split(m[..., None, :], N_MOD, axis=-1)


def modulate(h, shift, scale):
    return h * (1.0 + scale) + shift


def split_projection(p):
    B, L, _ = p.shape
    q, k, v, u = jnp.split(p, [ATTN_WIDTH, ATTN_WIDTH + KV_WIDTH, ATTN_WIDTH + 2 * KV_WIDTH], axis=-1)
    return (q.reshape(B, L, N_Q_HEADS, HEAD_DIM), k.reshape(B, L, N_KV_HEADS, HEAD_DIM),
            v.reshape(B, L, N_KV_HEADS, HEAD_DIM), u)


def axial_positions(seq_len):
    rows = seq_len // GRID_W
    row = jnp.broadcast_to(jnp.arange(rows)[:, None], (rows, GRID_W)).reshape(-1)
    col = jnp.broadcast_to(jnp.arange(GRID_W)[None, :], (rows, GRID_W)).reshape(-1)
    return row, col


def rope_2d(x, row, col):
    half = HEAD_DIM // 2
    inv_freq = ROPE_BASE ** (-jnp.arange(0, half, 2, dtype=jnp.float32) / half)

    def rot(xa, pos):
        ang = pos.astype(jnp.float32)[:, None] * inv_freq[None, :]
        cos = jnp.cos(ang)[None, :, None, :]
        sin = jnp.sin(ang)[None, :, None, :]
        x1, x2 = jnp.split(xa, 2, axis=-1)
        return jnp.concatenate([x1 * cos - x2 * sin, x1 * sin + x2 * cos], axis=-1)

    xf = x.astype(jnp.float32)
    out = jnp.concatenate([rot(xf[..., :half], row), rot(xf[..., half:], col)], axis=-1)
    return out.astype(x.dtype)


def latent_window_attention(q, k, v, k_ctx, v_ctx, sink):
    B, L, H, D = q.shape
    G = H // N_KV_HEADS
    nb = L // BLOCK
    scale = HEAD_DIM ** -0.5
    qb = q.reshape(B, nb, BLOCK, N_KV_HEADS, G, D)
    pad = ((0, 0), (BLOCK, BLOCK), (0, 0), (0, 0))
    kp = jnp.pad(k, pad).reshape(B, nb + 2, BLOCK, N_KV_HEADS, D)
    vp = jnp.pad(v, pad).reshape(B, nb + 2, BLOCK, N_KV_HEADS, D)
    kb = jnp.concatenate([kp[:, :-2], kp[:, 1:-1], kp[:, 2:]], axis=2)
    vb = jnp.concatenate([vp[:, :-2], vp[:, 1:-1], vp[:, 2:]], axis=2)
    s_win = jnp.einsum('bnqhgd,bnkhd->bhgnqk', qb, kb).astype(jnp.float32) * scale
    blk = jnp.arange(nb)[:, None, None] * BLOCK
    qpos = blk + jnp.arange(BLOCK)[None, :, None]
    kpos = blk - BLOCK + jnp.arange(3 * BLOCK)[None, None, :]
    valid = (jnp.abs(qpos - kpos) <= WINDOW) & (kpos >= 0) & (kpos < L)
    s_win = jnp.where(valid, s_win, NEG_INF)
    s_ctx = jnp.einsum('bnqhgd,bchd->bhgnqc', qb, k_ctx).astype(jnp.float32) * scale
    s_sink = jnp.broadcast_to(sink.astype(jnp.float32).reshape(N_KV_HEADS, G)[None, :, :, None, None, None],
                              s_win.shape[:-1] + (1,))
    p = jax.nn.softmax(jnp.concatenate([s_win, s_ctx, s_sink], axis=-1), axis=-1)
    n_win = 3 * BLOCK
    n_ctx = k_ctx.shape[1]
    p_win = p[..., :n_win].astype(v.dtype)
    p_ctx = p[..., n_win:n_win + n_ctx].astype(v.dtype)
    out = (jnp.einsum('bhgnqk,bnkhd->bnqhgd', p_win, vb)
           + jnp.einsum('bhgnqc,bchd->bnqhgd', p_ctx, v_ctx))
    return out.reshape(B, L, H * D)


def context_attention(q, k, v, sink):
    B, C, H, D = q.shape
    G = H // N_KV_HEADS
    qg = q.reshape(B, C, N_KV_HEADS, G, D)
    s = jnp.einsum('bqhgd,bkhd->bhgqk', qg, k).astype(jnp.float32) * (HEAD_DIM ** -0.5)
    s_sink = jnp.broadcast_to(sink.astype(jnp.float32).reshape(N_KV_HEADS, G)[None, :, :, None, None],
                              s.shape[:-1] + (1,))
    p = jax.nn.softmax(jnp.concatenate([s, s_sink], axis=-1), axis=-1)[..., :C].astype(v.dtype)
    out = jnp.einsum('bhgqk,bkhd->bqhgd', p, v)
    return out.reshape(B, C, H * D)


def multiscale_pool(u, pool_w, pool_scale):
    B, L, _ = u.shape
    uf = u.astype(jnp.float32)
    csum = jnp.pad(jnp.cumsum(uf, axis=1), ((0, 0), (1, 0), (0, 0)))
    t = jnp.arange(L)
    outs = []
    for g, w in enumerate(POOL_WINDOWS):
        lo = jnp.clip(t - w // 2, 0, L)
        hi = jnp.clip(t - w // 2 + w, 0, L)
        cs = csum[..., g * POOL_GROUP_DIM:(g + 1) * POOL_GROUP_DIM]
        mean = (cs[:, hi] - cs[:, lo]) / (hi - lo).astype(jnp.float32)[None, :, None]
        outs.append(mean - uf[..., g * POOL_GROUP_DIM:(g + 1) * POOL_GROUP_DIM])
    pooled = jnp.stack(outs, axis=2)
    mixed = jnp.einsum('blgc,gcd->blgd', pooled, pool_w.astype(jnp.float32)).reshape(B, L, POOL_WIDTH)
    return (mixed * pool_scale.astype(jnp.float32)).astype(u.dtype)


def squared_relu_mlp(h, w_up, w_down):
    return jnp.square(jax.nn.relu(h @ w_up)) @ w_down


def setup_inputs(seed: int = 0) -> dict:
    key = jax.random.key(seed)
    ks = jax.random.split(key, 18)
    f32 = jnp.float32
    nrm = lambda k, shape, s: jax.random.normal(k, shape, f32) * s
    return {
        "x": nrm(ks[0], (BATCH, SEQ, D_MODEL), 1.0),
        "c": nrm(ks[1], (BATCH, D_MODEL), 1.0),
        "ctx": nrm(ks[2], (BATCH, CTX_LEN, D_MODEL), 1.0),
        "c_ctx": nrm(ks[3], (D_MODEL,), 1.0),
        "norm_attn_w": 1.0 + nrm(ks[4], (DEPTH, D_MODEL), 0.02),
        "norm_mlp_w": 1.0 + nrm(ks[5], (DEPTH, D_MODEL), 0.02),
        "w_ada": nrm(ks[6], (DEPTH, D_MODEL, N_MOD * D_MODEL), 0.5 * D_MODEL ** -0.5),
        "b_ada": nrm(ks[7], (DEPTH, N_MOD * D_MODEL), 0.02),
        "w_in": nrm(ks[8], (DEPTH, D_MODEL, IN_WIDTH), D_MODEL ** -0.5),
        "attn_sink": nrm(ks[9], (DEPTH, N_Q_HEADS), 1.0),
        "pool_w": nrm(ks[10], (DEPTH, POOL_GROUPS, POOL_GROUP_DIM, POOL_GROUP_DIM), POOL_GROUP_DIM ** -0.5),
        "pool_scale": 1.0 + nrm(ks[11], (DEPTH, POOL_WIDTH), 0.1),
        "w_out": nrm(ks[12], (DEPTH, MIX_WIDTH, D_MODEL), MIX_WIDTH ** -0.5),
        "w_mlp_up": nrm(ks[13], (DEPTH, D_MODEL, D_FF), D_MODEL ** -0.5),
        "w_mlp_down": nrm(ks[14], (DEPTH, D_FF, D_MODEL), D_FF ** -0.5),
        "final_norm_w": 1.0 + nrm(ks[15], (D_MODEL,), 0.02),
    }


def reference(x, c, ctx, c_ctx, norm_attn_w, norm_mlp_w, w_ada, b_ada, w_in, attn_sink,
              pool_w, pool_scale, w_out, w_mlp_up, w_mlp_down, final_norm_w):
    seq_len = x.shape[1]
    row, col = axial_positions(seq_len)
    for layer in range(DEPTH):
        sh_a, sc_a, g_a, sh_m, sc_m, g_m = ada_modulation(c, w_ada[layer], b_ada[layer])
        csh_a, csc_a, cg_a, csh_m, csc_m, cg_m = ada_modulation(c_ctx, w_ada[layer], b_ada[layer])

        h = modulate(rms_norm(x, norm_attn_w[layer]), sh_a, sc_a)
        hc = modulate(rms_norm(ctx, norm_attn_w[layer]), csh_a, csc_a)
        q, k, v, u = split_projection(h @ w_in[layer])
        qc, kc, vc, uc = split_projection(hc @ w_in[layer])
        q = rope_2d(q, row, col)
        k = rope_2d(k, row, col)
        attn = latent_window_attention(q, k, v, kc, vc, attn_sink[layer])
        pool = multiscale_pool(u, pool_w[layer], pool_scale[layer])
        x = x + g_a * (jnp.concatenate([attn, pool], axis=-1) @ w_out[layer])

        hm = modulate(rms_norm(x, norm_mlp_w[layer]), sh_m, sc_m)
        x = x + g_m * squared_relu_mlp(hm, w_mlp_up[layer], w_mlp_down[layer])

        if layer < DEPTH - 1:
            attn_c = context_attention(qc, kc, vc, attn_sink[layer])
            pool_c = multiscale_pool(uc, pool_w[layer], pool_scale[layer])
            ctx = ctx + cg_a * (jnp.concatenate([attn_c, pool_c], axis=-1) @ w_out[layer])
            hcm = modulate(rms_norm(ctx, norm_mlp_w[layer]), csh_m, csc_m)
            ctx = ctx + cg_m * squared_relu_mlp(hcm, w_mlp_up[layer], w_mlp_down[layer])
    return rms_norm(x, final_norm_w)
```

```python
import functools

import numpy as np
import jax
import jax.numpy as jnp
from jax import lax
from jax.experimental import pallas as pl
from jax.experimental.pallas import tpu as pltpu

F32 = jnp.float32
BF16 = jnp.bfloat16

D_MODEL = 2048
SEQ = 8192
CTX_LEN = 256
GRID_W = 64
HEAD_DIM = 64
N_Q_HEADS = 16
N_KV_HEADS = 4
GQA = N_Q_HEADS // N_KV_HEADS
ATTN_WIDTH = N_Q_HEADS * HEAD_DIM
KV_WIDTH = N_KV_HEADS * HEAD_DIM
POOL_WINDOWS = (2, 4, 8, 16)
POOL_WIDTH = D_MODEL - ATTN_WIDTH
POOL_GROUP_DIM = POOL_WIDTH // len(POOL_WINDOWS)
IN_WIDTH = ATTN_WIDTH + 2 * KV_WIDTH + POOL_WIDTH
D_FF = 4 * D_MODEL
BLOCK = 128
ROPE_BASE = 10000.0
N_MOD = 6
EPS = 1e-6
NEG_INF = -1e30

LANES = 128
SUBLANES = 8
POOL_HALO = 8
VMEM_LIMIT = 56 * 1024 * 1024

ADA_TN = 1024
INPROJ_TM = 512
ATTN_TQ = 256
MIX_TM = 512
MLP_TM = 512
MLP_TF = 1024


def _params(*sem):
    return pltpu.CompilerParams(dimension_semantics=sem, vmem_limit_bytes=VMEM_LIMIT)


def _rms_modulate(x, norm_w, shift, scale):
    y = x * lax.rsqrt(jnp.mean(x * x, axis=-1, keepdims=True) + EPS)
    return (y * norm_w) * (1.0 + scale) + shift


def _ada_kernel(cond_ref, w_ref, b_ref, o_ref):
    cnd = cond_ref[...]
    s = cnd * (1.0 / (1.0 + jnp.exp(-cnd)))
    o_ref[...] = jnp.dot(s, w_ref[...], preferred_element_type=F32) + b_ref[...]


def _ada(cond, w_ada, b_ada):
    rows = cond.shape[0]
    n = w_ada.shape[1]
    return pl.pallas_call(
        _ada_kernel,
        grid=(n // ADA_TN,),
        in_specs=[
            pl.BlockSpec((rows, D_MODEL), lambda j: (0, 0)),
            pl.BlockSpec((D_MODEL, ADA_TN), lambda j: (0, j)),
            pl.BlockSpec((1, ADA_TN), lambda j: (0, j)),
        ],
        out_specs=pl.BlockSpec((rows, ADA_TN), lambda j: (0, j)),
        out_shape=jax.ShapeDtypeStruct((rows, n), F32),
        compiler_params=_params("arbitrary"),
        name="ada",
    )(cond, w_ada, b_ada)


def _rope_tables():
    half = HEAD_DIM // 2
    inv_freq = ROPE_BASE ** (-np.arange(0, half, 2, dtype=np.float64) / half)
    t = np.arange(SEQ)
    row = (t // GRID_W).astype(np.float64)
    col = (t % GRID_W).astype(np.float64)
    ang_r = row[:, None] * inv_freq[None, :]
    ang_c = col[:, None] * inv_freq[None, :]
    cos_t = np.concatenate([np.cos(ang_r), np.cos(ang_c)], axis=1).T
    sin_t = np.concatenate([np.sin(ang_r), np.sin(ang_c)], axis=1).T
    cos_l = np.concatenate([np.cos(ang_r), np.cos(ang_r), np.cos(ang_c), np.cos(ang_c)], axis=1)
    zeros = np.zeros_like(ang_r)
    s_up = np.concatenate([-np.sin(ang_r), zeros, -np.sin(ang_c), zeros], axis=1)
    s_dn = np.concatenate([zeros, np.sin(ang_r), zeros, np.sin(ang_c)], axis=1)
    tile2 = lambda a: np.concatenate([a, a], axis=1).astype(np.float32)
    return (cos_t.astype(np.float32), sin_t.astype(np.float32),
            tile2(cos_l), tile2(s_up), tile2(s_dn))


def _inproj_kernel(x_ref, nw_ref, sh_ref, sc_ref, w_ref, cos_t_ref, sin_t_ref,
                   cos_l_ref, sup_ref, sdn_ref, qt_ref, k_ref, vt_ref, u_ref):
    h = _rms_modulate(x_ref[...], nw_ref[...], sh_ref[...], sc_ref[...]).astype(BF16)
    p = jnp.dot(h, w_ref[...], preferred_element_type=F32)

    qt = p[:, :ATTN_WIDTH].T
    cr, cc = cos_t_ref[0:16, :], cos_t_ref[16:32, :]
    sr, sc = sin_t_ref[0:16, :], sin_t_ref[16:32, :]
    scale = HEAD_DIM ** -0.5
    pieces = []
    for hd in range(N_Q_HEADS):
        b = hd * HEAD_DIM
        x1r, x2r = qt[b:b + 16, :], qt[b + 16:b + 32, :]
        x1c, x2c = qt[b + 32:b + 48, :], qt[b + 48:b + 64, :]
        pieces += [x1r * cr - x2r * sr, x1r * sr + x2r * cr,
                   x1c * cc - x2c * sc, x1c * sc + x2c * cc]
    qt_ref[...] = (jnp.concatenate(pieces, axis=0) * scale).astype(BF16)

    k = p[:, ATTN_WIDTH:ATTN_WIDTH + KV_WIDTH]
    cos_l, s_up, s_dn = cos_l_ref[...], sup_ref[...], sdn_ref[...]
    kparts = []
    for g in range(KV_WIDTH // LANES):
        kg = k[:, g * LANES:(g + 1) * LANES]
        kparts.append(kg * cos_l + pltpu.roll(kg, LANES - 16, 1) * s_up
                      + pltpu.roll(kg, 16, 1) * s_dn)
    k_ref[...] = jnp.concatenate(kparts, axis=1).astype(BF16)

    vt_ref[...] = p[:, ATTN_WIDTH + KV_WIDTH:ATTN_WIDTH + 2 * KV_WIDTH].T.astype(BF16)
    u_ref[...] = p[:, ATTN_WIDTH + 2 * KV_WIDTH:]


def _inproj(x, norm_w, shift, scale, w_in_bf16, tables):
    cos_t, sin_t, cos_l, s_up, s_dn = tables
    tm = INPROJ_TM
    row = lambda i: (i, 0)
    colb = lambda i: (0, i)
    fixed = lambda i: (0, 0)
    return pl.pallas_call(
        _inproj_kernel,
        grid=(SEQ // tm,),
        in_specs=[
            pl.BlockSpec((tm, D_MODEL), row),
            pl.BlockSpec((1, D_MODEL), fixed),
            pl.BlockSpec((1, D_MODEL), fixed),
            pl.BlockSpec((1, D_MODEL), fixed),
            pl.BlockSpec((D_MODEL, IN_WIDTH), fixed),
            pl.BlockSpec((32, tm), colb),
            pl.BlockSpec((32, tm), colb),
            pl.BlockSpec((tm, LANES), row),
            pl.BlockSpec((tm, LANES), row),
            pl.BlockSpec((tm, LANES), row),
        ],
        out_specs=[
            pl.BlockSpec((ATTN_WIDTH, tm), colb),
            pl.BlockSpec((tm, KV_WIDTH), row),
            pl.BlockSpec((KV_WIDTH, tm), colb),
            pl.BlockSpec((tm, POOL_WIDTH), row),
        ],
        out_shape=[
            jax.ShapeDtypeStruct((ATTN_WIDTH, SEQ), BF16),
            jax.ShapeDtypeStruct((SEQ, KV_WIDTH), BF16),
            jax.ShapeDtypeStruct((KV_WIDTH, SEQ), BF16),
            jax.ShapeDtypeStruct((SEQ, POOL_WIDTH), F32),
        ],
        compiler_params=_params("arbitrary"),
        name="inproj",
    )(x, norm_w, shift, scale, w_in_bf16, cos_t, sin_t, cos_l, s_up, s_dn)


def _ctx_kv_kernel(x_ref, nw_ref, sh_ref, sc_ref, w_ref, k_ref, vt_ref):
    h = _rms_modulate(x_ref[...], nw_ref[...], sh_ref[...], sc_ref[...]).astype(BF16)
    p = jnp.dot(h, w_ref[...], preferred_element_type=F32)
    k_ref[...] = p[:, :KV_WIDTH].astype(BF16)
    vt_ref[...] = p[:, KV_WIDTH:].T.astype(BF16)


def _ctx_kv(ctx, norm_w, shift, scale, w_in_bf16):
    fixed = lambda i: (0, 0)
    kv_block = ATTN_WIDTH // (2 * KV_WIDTH)
    return pl.pallas_call(
        _ctx_kv_kernel,
        grid=(1,),
        in_specs=[
            pl.BlockSpec((CTX_LEN, D_MODEL), fixed),
            pl.BlockSpec((1, D_MODEL), fixed),
            pl.BlockSpec((1, D_MODEL), fixed),
            pl.BlockSpec((1, D_MODEL), fixed),
            pl.BlockSpec((D_MODEL, 2 * KV_WIDTH), lambda i: (0, kv_block)),
        ],
        out_specs=[
            pl.BlockSpec((CTX_LEN, KV_WIDTH), fixed),
            pl.BlockSpec((KV_WIDTH, CTX_LEN), fixed),
        ],
        out_shape=[
            jax.ShapeDtypeStruct((CTX_LEN, KV_WIDTH), BF16),
            jax.ShapeDtypeStruct((KV_WIDTH, CTX_LEN), BF16),
        ],
        compiler_params=_params("arbitrary"),
        name="ctx_kv",
    )(ctx, norm_w, shift, scale, w_in_bf16)


def _attn_kernel(qt_ref, kp_ref, km_ref, kn_ref, vtp_ref, vtm_ref, vtn_ref,
                 kc_ref, vtc_ref, sink_ref, o_ref):
    i = pl.program_id(0)
    nsub = ATTN_TQ // BLOCK
    nblk = SEQ // BLOCK
    gw = GQA * BLOCK

    key_j = lax.broadcasted_iota(jnp.int32, (BLOCK, gw), 0)
    qry_i = lax.broadcasted_iota(jnp.int32, (BLOCK, gw), 1) % BLOCK
    neg = jnp.full((BLOCK, gw), NEG_INF, F32)
    zero = jnp.zeros((BLOCK, gw), F32)
    band_prev = jnp.where(key_j >= qry_i, zero, neg)
    band_next = jnp.where(key_j <= qry_i, zero, neg)

    kc = kc_ref[...]
    vtc = vtc_ref[...]
    for sb in range(nsub):
        n = i * nsub + sb
        lo, hi = sb * BLOCK, (sb + 1) * BLOCK
        k_prev = kp_ref[...] if sb == 0 else km_ref[lo - BLOCK:lo, :]
        k_next = kn_ref[...] if sb == nsub - 1 else km_ref[hi:hi + BLOCK, :]
        vt_prev = vtp_ref[...] if sb == 0 else vtm_ref[:, lo - BLOCK:lo]
        vt_next = vtn_ref[...] if sb == nsub - 1 else vtm_ref[:, hi:hi + BLOCK]
        kwin = jnp.concatenate([k_prev, km_ref[lo:hi, :], k_next, kc], axis=0)
        vtwin = jnp.concatenate([vt_prev, vtm_ref[:, lo:hi], vt_next, vtc], axis=1)
        bias_prev = band_prev + jnp.where(n > 0, 0.0, NEG_INF)
        bias_next = band_next + jnp.where(n < nblk - 1, 0.0, NEG_INF)

        for h in range(N_KV_HEADS):
            qh = jnp.concatenate(
                [qt_ref[(GQA * h + g) * HEAD_DIM:(GQA * h + g + 1) * HEAD_DIM, lo:hi]
                 for g in range(GQA)], axis=1)
            blocks = []
            if h > 0:
                blocks.append(jnp.zeros((h * HEAD_DIM, gw), BF16))
            blocks.append(qh)
            if h < N_KV_HEADS - 1:
                blocks.append(jnp.zeros(((N_KV_HEADS - 1 - h) * HEAD_DIM, gw), BF16))
            qpad = jnp.concatenate(blocks, axis=0)
            s = jnp.dot(kwin, qpad, preferred_element_type=F32)
            s = jnp.concatenate([s[0:BLOCK] + bias_prev, s[BLOCK:2 * BLOCK],
                                 s[2 * BLOCK:3 * BLOCK] + bias_next, s[3 * BLOCK:]], axis=0)
            sink = sink_ref[:, h * gw:(h + 1) * gw]
            m = jnp.maximum(jnp.max(s, axis=0, keepdims=True), sink)
            e = jnp.exp(s - m)
            denom = jnp.sum(e, axis=0, keepdims=True) + jnp.exp(sink - m)
            o = jnp.dot(vtwin[h * HEAD_DIM:(h + 1) * HEAD_DIM, :], e.astype(BF16),
                        preferred_element_type=F32)
            o = o * (1.0 / denom)
            for pair in range(GQA // 2):
                blk = jnp.concatenate([o[:, (2 * pair) * BLOCK:(2 * pair + 1) * BLOCK],
                                       o[:, (2 * pair + 1) * BLOCK:(2 * pair + 2) * BLOCK]], axis=0)
                c0 = (GQA * h + 2 * pair) * HEAD_DIM
                o_ref[lo:hi, c0:c0 + 2 * HEAD_DIM] = blk.T.astype(BF16)


def _attn(qt, k, vt, kc, vtc, sink_row):
    tq = ATTN_TQ
    r = tq // BLOCK
    nblk = SEQ // BLOCK
    fixed = lambda i: (0, 0)
    prev = lambda i: jnp.maximum(i * r - 1, 0)
    nxt = lambda i: jnp.minimum((i + 1) * r, nblk - 1)
    return pl.pallas_call(
        _attn_kernel,
        grid=(SEQ // tq,),
        in_specs=[
            pl.BlockSpec((ATTN_WIDTH, tq), lambda i: (0, i)),
            pl.BlockSpec((BLOCK, KV_WIDTH), lambda i: (prev(i), 0)),
            pl.BlockSpec((tq, KV_WIDTH), lambda i: (i, 0)),
            pl.BlockSpec((BLOCK, KV_WIDTH), lambda i: (nxt(i), 0)),
            pl.BlockSpec((KV_WIDTH, BLOCK), lambda i: (0, prev(i))),
            pl.BlockSpec((KV_WIDTH, tq), lambda i: (0, i)),
            pl.BlockSpec((KV_WIDTH, BLOCK), lambda i: (0, nxt(i))),
            pl.BlockSpec((CTX_LEN, KV_WIDTH), fixed),
            pl.BlockSpec((KV_WIDTH, CTX_LEN), fixed),
            pl.BlockSpec((1, N_KV_HEADS * GQA * BLOCK), fixed),
        ],
        out_specs=pl.BlockSpec((tq, ATTN_WIDTH), lambda i: (i, 0)),
        out_shape=jax.ShapeDtypeStruct((SEQ, ATTN_WIDTH), BF16),
        compiler_params=_params("arbitrary"),
        name="attn",
    )(qt, k, k, k, vt, vt, vt, kc, vtc, sink_row)


def _mixout_kernel(x_ref, attn_ref, up_ref, um_ref, un_ref, wout_ref, pw_ref, ps_ref,
                   ga_ref, nw_ref, sh_ref, sc_ref, x1_ref, hm_ref, ubuf):
    i = pl.program_id(0)
    tm = MIX_TM
    nsteps = SEQ // tm
    ubuf[0:POOL_HALO, :] = jnp.where(i > 0, up_ref[...], 0.0)
    ubuf[POOL_HALO:POOL_HALO + tm, :] = um_ref[...]
    ubuf[POOL_HALO + tm:2 * POOL_HALO + tm, :] = jnp.where(i < nsteps - 1, un_ref[...], 0.0)

    t = i * tm + lax.broadcasted_iota(jnp.int32, (tm, 1), 0)
    pooled = []
    for g, w in enumerate(POOL_WINDOWS):
        c0, c1 = g * POOL_GROUP_DIM, (g + 1) * POOL_GROUP_DIM
        acc = ubuf[POOL_HALO - w // 2:POOL_HALO - w // 2 + tm, c0:c1]
        for d in range(-w // 2 + 1, w // 2):
            acc = acc + ubuf[POOL_HALO + d:POOL_HALO + d + tm, c0:c1]
        lo = jnp.clip(t - w // 2, 0, SEQ)
        hi = jnp.clip(t - w // 2 + w, 0, SEQ)
        cnt = (hi - lo).astype(F32)
        pg = acc / cnt - um_ref[:, c0:c1]
        mixed = jnp.dot(pg.astype(BF16), pw_ref[g], preferred_element_type=F32)
        pooled.append((mixed * ps_ref[:, c0:c1]).astype(BF16))
    mix = jnp.concatenate([attn_ref[...]] + pooled, axis=1)
    o = jnp.dot(mix, wout_ref[...], preferred_element_type=F32)
    x1 = x_ref[...] + ga_ref[...] * o
    x1_ref[...] = x1
    hm_ref[...] = _rms_modulate(x1, nw_ref[...], sh_ref[...], sc_ref[...]).astype(BF16)


def _mixout(x, attn, u, w_out_bf16, pool_w_bf16, pool_scale, g_a, norm_w, shift, scale):
    tm = MIX_TM
    hb = tm // POOL_HALO
    nhalo = SEQ // POOL_HALO
    row = lambda i: (i, 0)
    fixed = lambda i: (0, 0)
    return pl.pallas_call(
        _mixout_kernel,
        grid=(SEQ // tm,),
        in_specs=[
            pl.BlockSpec((tm, D_MODEL), row),
            pl.BlockSpec((tm, ATTN_WIDTH), row),
            pl.BlockSpec((POOL_HALO, POOL_WIDTH), lambda i: (jnp.maximum(i * hb - 1, 0), 0)),
            pl.BlockSpec((tm, POOL_WIDTH), row),
            pl.BlockSpec((POOL_HALO, POOL_WIDTH), lambda i: (jnp.minimum((i + 1) * hb, nhalo - 1), 0)),
            pl.BlockSpec((D_MODEL, D_MODEL), fixed),
            pl.BlockSpec((len(POOL_WINDOWS), POOL_GROUP_DIM, POOL_GROUP_DIM), lambda i: (0, 0, 0)),
            pl.BlockSpec((1, POOL_WIDTH), fixed),
            pl.BlockSpec((1, D_MODEL), fixed),
            pl.BlockSpec((1, D_MODEL), fixed),
            pl.BlockSpec((1, D_MODEL), fixed),
            pl.BlockSpec((1, D_MODEL), fixed),
        ],
        out_specs=[pl.BlockSpec((tm, D_MODEL), row), pl.BlockSpec((tm, D_MODEL), row)],
        out_shape=[jax.ShapeDtypeStruct((SEQ, D_MODEL), F32),
                   jax.ShapeDtypeStruct((SEQ, D_MODEL), BF16)],
        scratch_shapes=[pltpu.VMEM((tm + 2 * POOL_HALO, POOL_WIDTH), F32)],
        compiler_params=_params("arbitrary"),
        name="mixout",
    )(x, attn, u, u, u, w_out_bf16, pool_w_bf16, pool_scale, g_a, norm_w, shift, scale)


def _mlp_kernel(hm_ref, x1_ref, wup_ref, wdn_ref, gm_ref, fw_ref, o_ref):
    f = pl.program_id(1)
    a = jnp.dot(hm_ref[...], wup_ref[...], preferred_element_type=F32)
    a = jnp.maximum(a, 0.0)
    part = jnp.dot((a * a).astype(BF16), wdn_ref[...], preferred_element_type=F32)

    @pl.when(f == 0)
    def _():
        o_ref[...] = part

    @pl.when(f > 0)
    def _():
        o_ref[...] += part

    @pl.when(f == pl.num_programs(1) - 1)
    def _():
        x2 = x1_ref[...] + gm_ref[...] * o_ref[...]
        y = x2 * lax.rsqrt(jnp.mean(x2 * x2, axis=-1, keepdims=True) + EPS)
        o_ref[...] = y * fw_ref[...]


def _mlp(hm, x1, w_up_bf16, w_dn_bf16, g_m, final_w):
    tm, tf = MLP_TM, MLP_TF
    return pl.pallas_call(
        _mlp_kernel,
        grid=(SEQ // tm, D_FF // tf),
        in_specs=[
            pl.BlockSpec((tm, D_MODEL), lambda i, f: (i, 0)),
            pl.BlockSpec((tm, D_MODEL), lambda i, f: (i, 0)),
            pl.BlockSpec((D_MODEL, tf), lambda i, f: (0, f)),
            pl.BlockSpec((tf, D_MODEL), lambda i, f: (f, 0)),
            pl.BlockSpec((1, D_MODEL), lambda i, f: (0, 0)),
            pl.BlockSpec((1, D_MODEL), lambda i, f: (0, 0)),
        ],
        out_specs=pl.BlockSpec((tm, D_MODEL), lambda i, f: (i, 0)),
        out_shape=jax.ShapeDtypeStruct((SEQ, D_MODEL), F32),
        compiler_params=_params("arbitrary", "arbitrary"),
        name="mlp",
    )(hm, x1, w_up_bf16, w_dn_bf16, g_m, final_w)


def kernel(x, c, ctx, c_ctx, norm_attn_w, norm_mlp_w, w_ada, b_ada, w_in, attn_sink,
           pool_w, pool_scale, w_out, w_mlp_up, w_mlp_down, final_norm_w):
    assert x.shape == (1, SEQ, D_MODEL) and ctx.shape == (1, CTX_LEN, D_MODEL)
    assert w_ada.shape[0] == 1, "single layer"
    x2d, ctx2d = x[0], ctx[0]

    cond = jnp.concatenate([c, c_ctx[None, :], jnp.zeros((SUBLANES - 2, D_MODEL), F32)], axis=0)
    mod = _ada(cond, w_ada[0], b_ada[0][None, :])
    chunk = lambda r, j: mod[r:r + 1, j * D_MODEL:(j + 1) * D_MODEL]
    sh_a, sc_a, g_a, sh_m, sc_m, g_m = (chunk(0, j) for j in range(N_MOD))
    csh_a, csc_a = chunk(1, 0), chunk(1, 1)

    w_in_b = w_in[0].astype(BF16)
    nw_a = norm_attn_w[0][None, :]
    tables = tuple(jnp.asarray(t) for t in _rope_tables())
    qt, k, vt, u = _inproj(x2d, nw_a, sh_a, sc_a, w_in_b, tables)
    kc, vtc = _ctx_kv(ctx2d, nw_a, csh_a, csc_a, w_in_b)

    sink_row = jnp.repeat(attn_sink[0].astype(F32), BLOCK)[None, :]
    attn = _attn(qt, k, vt, kc, vtc, sink_row)

    x1, hm = _mixout(x2d, attn, u, w_out[0].astype(BF16), pool_w[0].astype(BF16),
                     pool_scale[0][None, :], g_a, norm_mlp_w[0][None, :], sh_m, sc_m)
    out = _mlp(hm, x1, w_mlp_up[0].astype(BF16), w_mlp_down[0].astype(BF16),
               g_m, final_norm_w[None, :])
    return out[None]
```

```python
import functools

import numpy as np
import jax
import jax.numpy as jnp
from jax import lax
from jax.experimental import pallas as pl
from jax.experimental.pallas import tpu as pltpu

F32 = jnp.float32
BF16 = jnp.bfloat16

D_MODEL = 2048
SEQ = 8192
CTX_LEN = 256
GRID_W = 64
HEAD_DIM = 64
N_Q_HEADS = 16
N_KV_HEADS = 4
GQA = N_Q_HEADS // N_KV_HEADS
ATTN_WIDTH = N_Q_HEADS * HEAD_DIM
KV_WIDTH = N_KV_HEADS * HEAD_DIM
POOL_WINDOWS = (2, 4, 8, 16)
POOL_WIDTH = D_MODEL - ATTN_WIDTH
POOL_GROUP_DIM = POOL_WIDTH // len(POOL_WINDOWS)
IN_WIDTH = ATTN_WIDTH + 2 * KV_WIDTH + POOL_WIDTH
D_FF = 4 * D_MODEL
BLOCK = 128
ROPE_BASE = 10000.0
N_MOD = 6
EPS = 1e-6
NEG_INF = -1e30
LOG2E = 1.4426950408889634
Q_SCALE = HEAD_DIM ** -0.5 * LOG2E

LANES = 128
SUBLANES = 8
POOL_HALO = 8
VMEM_LIMIT = 56 * 1024 * 1024

ADA_TN = 1024
INPROJ_TM = 512
ATTN_TQ = 256
MIX_TM = 512
MIX_CHUNK = 256
MLP_TM = 512
MLP_TF = 1024


def _params(*sem):
    return pltpu.CompilerParams(dimension_semantics=sem, vmem_limit_bytes=VMEM_LIMIT)


def _rms_modulate(x, norm_w, shift, scale):
    y = x * lax.rsqrt(jnp.mean(x * x, axis=-1, keepdims=True) + EPS)
    return (y * norm_w) * (1.0 + scale) + shift


def _ada_kernel(cond_ref, w_ref, b_ref, o_ref):
    cnd = cond_ref[...]
    s = cnd * (1.0 / (1.0 + jnp.exp(-cnd)))
    o_ref[...] = jnp.dot(s, w_ref[...], preferred_element_type=F32) + b_ref[...]


def _ada(cond, w_ada, b_ada):
    rows = cond.shape[0]
    n = w_ada.shape[1]
    return pl.pallas_call(
        _ada_kernel,
        grid=(n // ADA_TN,),
        in_specs=[
            pl.BlockSpec((rows, D_MODEL), lambda j: (0, 0)),
            pl.BlockSpec((D_MODEL, ADA_TN), lambda j: (0, j)),
            pl.BlockSpec((1, ADA_TN), lambda j: (0, j)),
        ],
        out_specs=pl.BlockSpec((rows, ADA_TN), lambda j: (0, j)),
        out_shape=jax.ShapeDtypeStruct((rows, n), F32),
        compiler_params=_params("arbitrary"),
        name="ada",
    )(cond, w_ada, b_ada)


def _rope_tables():
    half = HEAD_DIM // 2
    inv_freq = ROPE_BASE ** (-np.arange(0, half, 2, dtype=np.float64) / half)
    t = np.arange(SEQ)
    row = (t // GRID_W).astype(np.float64)
    col = (t % GRID_W).astype(np.float64)
    ang_r = row[:, None] * inv_freq[None, :]
    ang_c = col[:, None] * inv_freq[None, :]
    cos_t = Q_SCALE * np.concatenate([np.cos(ang_r), np.cos(ang_c)], axis=1).T
    sin_t = Q_SCALE * np.concatenate([np.sin(ang_r), np.sin(ang_c)], axis=1).T
    cos_l = np.concatenate([np.cos(ang_r), np.cos(ang_r), np.cos(ang_c), np.cos(ang_c)], axis=1)
    zeros = np.zeros_like(ang_r)
    s_up = np.concatenate([-np.sin(ang_r), zeros, -np.sin(ang_c), zeros], axis=1)
    s_dn = np.concatenate([zeros, np.sin(ang_r), zeros, np.sin(ang_c)], axis=1)
    tile2 = lambda a: np.concatenate([a, a], axis=1).astype(np.float32)
    return (cos_t.astype(np.float32), sin_t.astype(np.float32),
            tile2(cos_l), tile2(s_up), tile2(s_dn))


def _inproj_kernel(x_ref, nw_ref, sh_ref, sc_ref, w_ref, cos_t_ref, sin_t_ref,
                   cos_l_ref, sup_ref, sdn_ref, qt_ref, k_ref, vt_ref, u_ref):
    h = _rms_modulate(x_ref[...], nw_ref[...], sh_ref[...], sc_ref[...]).astype(BF16)
    p = jnp.dot(h, w_ref[...], preferred_element_type=F32)

    qt = p[:, :ATTN_WIDTH].T
    cr, cc = cos_t_ref[0:16, :], cos_t_ref[16:32, :]
    sr, sc = sin_t_ref[0:16, :], sin_t_ref[16:32, :]
    pieces = []
    for hd in range(N_Q_HEADS):
        b = hd * HEAD_DIM
        x1r, x2r = qt[b:b + 16, :], qt[b + 16:b + 32, :]
        x1c, x2c = qt[b + 32:b + 48, :], qt[b + 48:b + 64, :]
        pieces += [x1r * cr - x2r * sr, x1r * sr + x2r * cr,
                   x1c * cc - x2c * sc, x1c * sc + x2c * cc]
    qt_ref[...] = jnp.concatenate(pieces, axis=0).astype(BF16)

    k = p[:, ATTN_WIDTH:ATTN_WIDTH + KV_WIDTH]
    cos_l, s_up, s_dn = cos_l_ref[...], sup_ref[...], sdn_ref[...]
    kparts = []
    for g in range(KV_WIDTH // LANES):
        kg = k[:, g * LANES:(g + 1) * LANES]
        kparts.append(kg * cos_l + pltpu.roll(kg, LANES - 16, 1) * s_up
                      + pltpu.roll(kg, 16, 1) * s_dn)
    k_ref[...] = jnp.concatenate(kparts, axis=1).astype(BF16)

    vt_ref[...] = p[:, ATTN_WIDTH + KV_WIDTH:ATTN_WIDTH + 2 * KV_WIDTH].T.astype(BF16)
    u_ref[...] = p[:, ATTN_WIDTH + 2 * KV_WIDTH:]


def _inproj(x, norm_w, shift, scale, w_in_bf16, tables):
    cos_t, sin_t, cos_l, s_up, s_dn = tables
    tm = INPROJ_TM
    row = lambda i: (i, 0)
    colb = lambda i: (0, i)
    fixed = lambda i: (0, 0)
    return pl.pallas_call(
        _inproj_kernel,
        grid=(SEQ // tm,),
        in_specs=[
            pl.BlockSpec((tm, D_MODEL), row),
            pl.BlockSpec((1, D_MODEL), fixed),
            pl.BlockSpec((1, D_MODEL), fixed),
            pl.BlockSpec((1, D_MODEL), fixed),
            pl.BlockSpec((D_MODEL, IN_WIDTH), fixed),
            pl.BlockSpec((32, tm), colb),
            pl.BlockSpec((32, tm), colb),
            pl.BlockSpec((tm, LANES), row),
            pl.BlockSpec((tm, LANES), row),
            pl.BlockSpec((tm, LANES), row),
        ],
        out_specs=[
            pl.BlockSpec((ATTN_WIDTH, tm), colb),
            pl.BlockSpec((tm, KV_WIDTH), row),
            pl.BlockSpec((KV_WIDTH, tm), colb),
            pl.BlockSpec((tm, POOL_WIDTH), row),
        ],
        out_shape=[
            jax.ShapeDtypeStruct((ATTN_WIDTH, SEQ), BF16),
            jax.ShapeDtypeStruct((SEQ, KV_WIDTH), BF16),
            jax.ShapeDtypeStruct((KV_WIDTH, SEQ), BF16),
            jax.ShapeDtypeStruct((SEQ, POOL_WIDTH), F32),
        ],
        compiler_params=_params("arbitrary"),
        name="inproj",
    )(x, norm_w, shift, scale, w_in_bf16, cos_t, sin_t, cos_l, s_up, s_dn)


def _ctx_kv_kernel(x_ref, nw_ref, sh_ref, sc_ref, w_ref, k_ref, vt_ref):
    h = _rms_modulate(x_ref[...], nw_ref[...], sh_ref[...], sc_ref[...]).astype(BF16)
    p = jnp.dot(h, w_ref[...], preferred_element_type=F32)
    k_ref[...] = p[:, :KV_WIDTH].astype(BF16)
    vt_ref[...] = p[:, KV_WIDTH:].T.astype(BF16)


def _ctx_kv(ctx, norm_w, shift, scale, w_in_bf16):
    fixed = lambda i: (0, 0)
    kv_block = ATTN_WIDTH // (2 * KV_WIDTH)
    return pl.pallas_call(
        _ctx_kv_kernel,
        grid=(1,),
        in_specs=[
            pl.BlockSpec((CTX_LEN, D_MODEL), fixed),
            pl.BlockSpec((1, D_MODEL), fixed),
            pl.BlockSpec((1, D_MODEL), fixed),
            pl.BlockSpec((1, D_MODEL), fixed),
            pl.BlockSpec((D_MODEL, 2 * KV_WIDTH), lambda i: (0, kv_block)),
        ],
        out_specs=[
            pl.BlockSpec((CTX_LEN, KV_WIDTH), fixed),
            pl.BlockSpec((KV_WIDTH, CTX_LEN), fixed),
        ],
        out_shape=[
            jax.ShapeDtypeStruct((CTX_LEN, KV_WIDTH), BF16),
            jax.ShapeDtypeStruct((KV_WIDTH, CTX_LEN), BF16),
        ],
        compiler_params=_params("arbitrary"),
        name="ctx_kv",
    )(ctx, norm_w, shift, scale, w_in_bf16)


def _attn_kernel(qt_ref, kp_ref, km_ref, kn_ref, vtp_ref, vtm_ref, vtn_ref,
                 kc_ref, vtc_ref, sink_ref, wout_ref, wup_ref, wdn_ref,
                 o_ref, wout_b_ref, wup_b_ref, wdn_b_ref):
    wout_b_ref[...] = wout_ref[...].astype(BF16)
    wup_b_ref[...] = wup_ref[...].astype(BF16)
    wdn_b_ref[...] = wdn_ref[...].astype(BF16)

    i = pl.program_id(0)
    nsub = ATTN_TQ // BLOCK
    nblk = SEQ // BLOCK
    gw = GQA * BLOCK

    key_j = lax.broadcasted_iota(jnp.int32, (BLOCK, gw), 0)
    qry_i = lax.broadcasted_iota(jnp.int32, (BLOCK, gw), 1) % BLOCK
    neg = jnp.full((BLOCK, gw), NEG_INF, F32)
    zero = jnp.zeros((BLOCK, gw), F32)
    band_prev = jnp.where(key_j >= qry_i, zero, neg)
    band_next = jnp.where(key_j <= qry_i, zero, neg)

    kc = kc_ref[...]
    vtc = vtc_ref[...]
    ones_rows = jnp.ones((2 * SUBLANES, 3 * BLOCK + CTX_LEN), BF16)
    for sb in range(nsub):
        n = i * nsub + sb
        lo, hi = sb * BLOCK, (sb + 1) * BLOCK
        k_prev = kp_ref[...] if sb == 0 else km_ref[lo - BLOCK:lo, :]
        k_next = kn_ref[...] if sb == nsub - 1 else km_ref[hi:hi + BLOCK, :]
        vt_prev = vtp_ref[...] if sb == 0 else vtm_ref[:, lo - BLOCK:lo]
        vt_next = vtn_ref[...] if sb == nsub - 1 else vtm_ref[:, hi:hi + BLOCK]
        kwin = jnp.concatenate([k_prev, km_ref[lo:hi, :], k_next, kc], axis=0)
        vtwin = jnp.concatenate([vt_prev, vtm_ref[:, lo:hi], vt_next, vtc], axis=1)
        bias_prev = band_prev + jnp.where(n > 0, 0.0, NEG_INF)
        bias_next = band_next + jnp.where(n < nblk - 1, 0.0, NEG_INF)

        for h in range(N_KV_HEADS):
            qh = jnp.concatenate(
                [qt_ref[(GQA * h + g) * HEAD_DIM:(GQA * h + g + 1) * HEAD_DIM, lo:hi]
                 for g in range(GQA)], axis=1)
            blocks = []
            if h > 0:
                blocks.append(jnp.zeros((h * HEAD_DIM, gw), BF16))
            blocks.append(qh)
            if h < N_KV_HEADS - 1:
                blocks.append(jnp.zeros(((N_KV_HEADS - 1 - h) * HEAD_DIM, gw), BF16))
            qpad = jnp.concatenate(blocks, axis=0)
            s = jnp.dot(kwin, qpad, preferred_element_type=F32)
            s = jnp.concatenate([s[0:BLOCK] + bias_prev, s[BLOCK:2 * BLOCK],
                                 s[2 * BLOCK:3 * BLOCK] + bias_next, s[3 * BLOCK:]], axis=0)
            sink = sink_ref[:, h * gw:(h + 1) * gw] * LOG2E
            m = jnp.maximum(jnp.max(s, axis=0, keepdims=True), sink)
            e = jnp.exp2(s - m).astype(BF16)
            vt_ones = jnp.concatenate([vtwin[h * HEAD_DIM:(h + 1) * HEAD_DIM, :], ones_rows], axis=0)
            o = jnp.dot(vt_ones, e, preferred_element_type=F32)
            denom = o[HEAD_DIM:HEAD_DIM + 1, :] + jnp.exp2(sink - m)
            o = o[:HEAD_DIM, :] * (1.0 / denom)
            for pair in range(GQA // 2):
                blk = jnp.concatenate([o[:, (2 * pair) * BLOCK:(2 * pair + 1) * BLOCK],
                                       o[:, (2 * pair + 1) * BLOCK:(2 * pair + 2) * BLOCK]], axis=0)
                c0 = (GQA * h + 2 * pair) * HEAD_DIM
                o_ref[lo:hi, c0:c0 + 2 * HEAD_DIM] = blk.T.astype(BF16)


def _attn(qt, k, vt, kc, vtc, sink_row, w_out, w_up, w_dn):
    tq = ATTN_TQ
    r = tq // BLOCK
    nblk = SEQ // BLOCK
    nsteps = SEQ // tq
    fixed = lambda i: (0, 0)
    row = lambda i: (i, 0)
    slab = lambda w: pl.BlockSpec((w.shape[0] // nsteps, w.shape[1]), row)
    bf16_like = lambda w: jax.ShapeDtypeStruct(w.shape, BF16)
    prev = lambda i: jnp.maximum(i * r - 1, 0)
    nxt = lambda i: jnp.minimum((i + 1) * r, nblk - 1)
    return pl.pallas_call(
        _attn_kernel,
        grid=(SEQ // tq,),
        in_specs=[
            pl.BlockSpec((ATTN_WIDTH, tq), lambda i: (0, i)),
            pl.BlockSpec((BLOCK, KV_WIDTH), lambda i: (prev(i), 0)),
            pl.BlockSpec((tq, KV_WIDTH), lambda i: (i, 0)),
            pl.BlockSpec((BLOCK, KV_WIDTH), lambda i: (nxt(i), 0)),
            pl.BlockSpec((KV_WIDTH, BLOCK), lambda i: (0, prev(i))),
            pl.BlockSpec((KV_WIDTH, tq), lambda i: (0, i)),
            pl.BlockSpec((KV_WIDTH, BLOCK), lambda i: (0, nxt(i))),
            pl.BlockSpec((CTX_LEN, KV_WIDTH), fixed),
            pl.BlockSpec((KV_WIDTH, CTX_LEN), fixed),
            pl.BlockSpec((1, N_KV_HEADS * GQA * BLOCK), fixed),
            slab(w_out), slab(w_up), slab(w_dn),
        ],
        out_specs=[pl.BlockSpec((tq, ATTN_WIDTH), row), slab(w_out), slab(w_up), slab(w_dn)],
        out_shape=[jax.ShapeDtypeStruct((SEQ, ATTN_WIDTH), BF16),
                   bf16_like(w_out), bf16_like(w_up), bf16_like(w_dn)],
        compiler_params=_params("arbitrary"),
        name="attn",
    )(qt, k, k, k, vt, vt, vt, kc, vtc, sink_row, w_out, w_up, w_dn)


def _mixout_kernel(x_ref, attn_ref, up_ref, um_ref, un_ref, wout_ref, pw_ref, ps_ref,
                   ga_ref, nw_ref, sh_ref, sc_ref, x1_ref, hm_ref, ubuf):
    i = pl.program_id(0)
    tm = MIX_TM
    nsteps = SEQ // tm
    ubuf[0:POOL_HALO, :] = jnp.where(i > 0, up_ref[...], 0.0)
    ubuf[POOL_HALO:POOL_HALO + tm, :] = um_ref[...]
    ubuf[POOL_HALO + tm:2 * POOL_HALO + tm, :] = jnp.where(i < nsteps - 1, un_ref[...], 0.0)

    rc = MIX_CHUNK
    n = rc + 2 * POOL_HALO
    c = POOL_HALO
    for r0 in range(0, tm, rc):
        rows = slice(r0, r0 + rc)
        o = jnp.dot(attn_ref[rows, :], wout_ref[0:ATTN_WIDTH, :], preferred_element_type=F32)
        x1_ref[rows, :] = x_ref[rows, :] + ga_ref[...] * o

        t = i * tm + r0 + lax.broadcasted_iota(jnp.int32, (rc, 1), 0)
        pooled = []
        for g, w in enumerate(POOL_WINDOWS):
            c0, c1 = g * POOL_GROUP_DIM, (g + 1) * POOL_GROUP_DIM
            e = ubuf[r0:r0 + n, c0:c1]
            if w == 2:
                win = e[c - 1:c - 1 + rc] + e[c:c + rc]
            else:
                f2 = e[0:n - 1] + e[1:n]
                if w == 4:
                    win = f2[c - 2:c - 2 + rc] + f2[c:c + rc]
                else:
                    f4 = f2[0:n - 3] + f2[2:n - 1]
                    if w == 8:
                        win = f4[c - 4:c - 4 + rc] + f4[c:c + rc]
                    else:
                        f8 = f4[0:n - 7] + f4[4:n - 3]
                        win = f8[c - 8:c - 8 + rc] + f8[c:c + rc]
            lo = jnp.clip(t - w // 2, 0, SEQ)
            hi = jnp.clip(t - w // 2 + w, 0, SEQ)
            cnt = (hi - lo).astype(F32)
            pg = win / cnt - um_ref[rows, c0:c1]
            mixed = jnp.dot(pg.astype(BF16), pw_ref[g], preferred_element_type=F32)
            pooled.append((mixed * ps_ref[:, c0:c1]).astype(BF16))
        o = jnp.dot(jnp.concatenate(pooled, axis=1), wout_ref[ATTN_WIDTH:, :],
                    preferred_element_type=F32)
        x1 = x1_ref[rows, :] + ga_ref[...] * o
        x1_ref[rows, :] = x1
        hm_ref[rows, :] = _rms_modulate(x1, nw_ref[...], sh_ref[...], sc_ref[...]).astype(BF16)


def _mixout(x, attn, u, w_out_bf16, pool_w_bf16, pool_scale, g_a, norm_w, shift, scale):
    tm = MIX_TM
    hb = tm // POOL_HALO
    nhalo = SEQ // POOL_HALO
    row = lambda i: (i, 0)
    fixed = lambda i: (0, 0)
    return pl.pallas_call(
        _mixout_kernel,
        grid=(SEQ // tm,),
        in_specs=[
            pl.BlockSpec((tm, D_MODEL), row),
            pl.BlockSpec((tm, ATTN_WIDTH), row),
            pl.BlockSpec((POOL_HALO, POOL_WIDTH), lambda i: (jnp.maximum(i * hb - 1, 0), 0)),
            pl.BlockSpec((tm, POOL_WIDTH), row),
            pl.BlockSpec((POOL_HALO, POOL_WIDTH), lambda i: (jnp.minimum((i + 1) * hb, nhalo - 1), 0)),
            pl.BlockSpec((D_MODEL, D_MODEL), fixed),
            pl.BlockSpec((len(POOL_WINDOWS), POOL_GROUP_DIM, POOL_GROUP_DIM), lambda i: (0, 0, 0)),
            pl.BlockSpec((1, POOL_WIDTH), fixed),
            pl.BlockSpec((1, D_MODEL), fixed),
            pl.BlockSpec((1, D_MODEL), fixed),
            pl.BlockSpec((1, D_MODEL), fixed),
            pl.BlockSpec((1, D_MODEL), fixed),
        ],
        out_specs=[pl.BlockSpec((tm, D_MODEL), row), pl.BlockSpec((tm, D_MODEL), row)],
        out_shape=[jax.ShapeDtypeStruct((SEQ, D_MODEL), F32),
                   jax.ShapeDtypeStruct((SEQ, D_MODEL), BF16)],
        scratch_shapes=[pltpu.VMEM((tm + 2 * POOL_HALO, POOL_WIDTH), F32)],
        compiler_params=_params("arbitrary"),
        name="mixout",
    )(x, attn, u, u, u, w_out_bf16, pool_w_bf16, pool_scale, g_a, norm_w, shift, scale)


def _mlp_kernel(hm_ref, x1_ref, wup_ref, wdn_ref, gm_ref, fw_ref, o_ref):
    f = pl.program_id(1)

    @pl.when(f == 0)
    def _():
        o_ref[...] = jnp.zeros_like(o_ref)

    a = jnp.dot(hm_ref[...], wup_ref[...], preferred_element_type=F32)
    a = jnp.maximum(a, 0.0)
    o_ref[...] += jnp.dot((a * a).astype(BF16), wdn_ref[...], preferred_element_type=F32)

    @pl.when(f == pl.num_programs(1) - 1)
    def _():
        x2 = x1_ref[...] + gm_ref[...] * o_ref[...]
        y = x2 * lax.rsqrt(jnp.mean(x2 * x2, axis=-1, keepdims=True) + EPS)
        o_ref[...] = y * fw_ref[...]


def _mlp(hm, x1, w_up_bf16, w_dn_bf16, g_m, final_w):
    tm, tf = MLP_TM, MLP_TF
    return pl.pallas_call(
        _mlp_kernel,
        grid=(SEQ // tm, D_FF // tf),
        in_specs=[
            pl.BlockSpec((tm, D_MODEL), lambda i, f: (i, 0)),
            pl.BlockSpec((tm, D_MODEL), lambda i, f: (i, 0)),
            pl.BlockSpec((D_MODEL, tf), lambda i, f: (0, f)),
            pl.BlockSpec((tf, D_MODEL), lambda i, f: (f, 0)),
            pl.BlockSpec((1, D_MODEL), lambda i, f: (0, 0)),
            pl.BlockSpec((1, D_MODEL), lambda i, f: (0, 0)),
        ],
        out_specs=pl.BlockSpec((tm, D_MODEL), lambda i, f: (i, 0)),
        out_shape=jax.ShapeDtypeStruct((SEQ, D_MODEL), F32),
        compiler_params=_params("arbitrary", "arbitrary"),
        name="mlp",
    )(hm, x1, w_up_bf16, w_dn_bf16, g_m, final_w)


def kernel(x, c, ctx, c_ctx, norm_attn_w, norm_mlp_w, w_ada, b_ada, w_in, attn_sink,
           pool_w, pool_scale, w_out, w_mlp_up, w_mlp_down, final_norm_w):
    assert x.shape == (1, SEQ, D_MODEL) and ctx.shape == (1, CTX_LEN, D_MODEL)
    assert w_ada.shape[0] == 1, "single layer"
    x2d, ctx2d = x[0], ctx[0]

    cond = jnp.concatenate([c, c_ctx[None, :], jnp.zeros((SUBLANES - 2, D_MODEL), F32)], axis=0)
    mod = _ada(cond, w_ada[0], b_ada[0][None, :])
    chunk = lambda r, j: mod[r:r + 1, j * D_MODEL:(j + 1) * D_MODEL]
    sh_a, sc_a, g_a, sh_m, sc_m, g_m = (chunk(0, j) for j in range(N_MOD))
    csh_a, csc_a = chunk(1, 0), chunk(1, 1)

    w_in_b = w_in[0].astype(BF16)
    nw_a = norm_attn_w[0][None, :]
    tables = tuple(jnp.asarray(t) for t in _rope_tables())
    qt, k, vt, u = _inproj(x2d, nw_a, sh_a, sc_a, w_in_b, tables)
    kc, vtc = _ctx_kv(ctx2d, nw_a, csh_a, csc_a, w_in_b)

    sink_row = jnp.repeat(attn_sink[0].astype(F32), BLOCK)[None, :]
    attn, w_out_b, w_up_b, w_dn_b = _attn(qt, k, vt, kc, vtc, sink_row,
                                          w_out[0], w_mlp_up[0], w_mlp_down[0])

    x1, hm = _mixout(x2d, attn, u, w_out_b, pool_w[0].astype(BF16),
                     pool_scale[0][None, :], g_a, norm_mlp_w[0][None, :], sh_m, sc_m)
    out = _mlp(hm, x1, w_up_b, w_dn_b, g_m, final_norm_w[None, :])
    return out[None]
```

```python
import functools

import numpy as np
import jax
import jax.numpy as jnp
from jax import lax
from jax.experimental import pallas as pl
from jax.experimental.pallas import tpu as pltpu

F32 = jnp.float32
BF16 = jnp.bfloat16

D_MODEL = 2048
SEQ = 8192
CTX_LEN = 256
GRID_W = 64
HEAD_DIM = 64
N_Q_HEADS = 16
N_KV_HEADS = 4
GQA = N_Q_HEADS // N_KV_HEADS
ATTN_WIDTH = N_Q_HEADS * HEAD_DIM
KV_WIDTH = N_KV_HEADS * HEAD_DIM
POOL_WINDOWS = (2, 4, 8, 16)
POOL_WIDTH = D_MODEL - ATTN_WIDTH
POOL_GROUP_DIM = POOL_WIDTH // len(POOL_WINDOWS)
IN_WIDTH = ATTN_WIDTH + 2 * KV_WIDTH + POOL_WIDTH
D_FF = 4 * D_MODEL
BLOCK = 128
ROPE_BASE = 10000.0
N_MOD = 6
EPS = 1e-6
NEG_INF = -1e30
LOG2E = 1.4426950408889634
Q_SCALE = HEAD_DIM ** -0.5 * LOG2E

LANES = 128
SUBLANES = 8
POOL_HALO = 8
VMEM_LIMIT = 56 * 1024 * 1024

ADA_TN = 1024
INPROJ_TM = 512
ATTN_TQ = 256
MIX_TM = 512
MIX_CHUNK = 256
MLP_TM = 1024
MLP_TF = 1024
MLP_RC = 256
MLP_NF = D_FF // MLP_TF
MLP_XC = D_MODEL // MLP_NF


def _params(*sem):
    return pltpu.CompilerParams(dimension_semantics=sem, vmem_limit_bytes=VMEM_LIMIT)


def _rms_modulate(x, norm_w, shift, scale):
    gain = norm_w * (1.0 + scale)
    y = x * lax.rsqrt(jnp.mean(x * x, axis=-1, keepdims=True) + EPS)
    return y * gain + shift


def _ada_kernel(cond_ref, w_ref, b_ref, o_ref):
    cnd = cond_ref[...]
    s = cnd * (1.0 / (1.0 + jnp.exp(-cnd)))
    o_ref[...] = jnp.dot(s, w_ref[...], preferred_element_type=F32) + b_ref[...]


def _ada(cond, w_ada, b_ada):
    rows = cond.shape[0]
    n = w_ada.shape[1]
    return pl.pallas_call(
        _ada_kernel,
        grid=(n // ADA_TN,),
        in_specs=[
            pl.BlockSpec((rows, D_MODEL), lambda j: (0, 0)),
            pl.BlockSpec((D_MODEL, ADA_TN), lambda j: (0, j)),
            pl.BlockSpec((1, ADA_TN), lambda j: (0, j)),
        ],
        out_specs=pl.BlockSpec((rows, ADA_TN), lambda j: (0, j)),
        out_shape=jax.ShapeDtypeStruct((rows, n), F32),
        compiler_params=_params("arbitrary"),
        name="ada",
    )(cond, w_ada, b_ada)


def _rope_tables():
    half = HEAD_DIM // 2
    inv_freq = ROPE_BASE ** (-np.arange(0, half, 2, dtype=np.float64) / half)
    t = np.arange(SEQ)
    row = (t // GRID_W).astype(np.float64)
    col = (t % GRID_W).astype(np.float64)
    ang_r = row[:, None] * inv_freq[None, :]
    ang_c = col[:, None] * inv_freq[None, :]
    cos_t = Q_SCALE * np.concatenate([np.cos(ang_r), np.cos(ang_c)], axis=1).T
    sin_t = Q_SCALE * np.concatenate([np.sin(ang_r), np.sin(ang_c)], axis=1).T
    cos_l = np.concatenate([np.cos(ang_r), np.cos(ang_r), np.cos(ang_c), np.cos(ang_c)], axis=1)
    zeros = np.zeros_like(ang_r)
    s_up = np.concatenate([-np.sin(ang_r), zeros, -np.sin(ang_c), zeros], axis=1)
    s_dn = np.concatenate([zeros, np.sin(ang_r), zeros, np.sin(ang_c)], axis=1)
    tile2 = lambda a: np.concatenate([a, a], axis=1).astype(np.float32)
    return (cos_t.astype(np.float32), sin_t.astype(np.float32),
            tile2(cos_l), tile2(s_up), tile2(s_dn))


def _inproj_kernel(x_ref, nw_ref, sh_ref, sc_ref, w_ref, cos_t_ref, sin_t_ref,
                   cos_l_ref, sup_ref, sdn_ref, qt_ref, k_ref, vt_ref, u_ref):
    h = _rms_modulate(x_ref[...], nw_ref[...], sh_ref[...], sc_ref[...]).astype(BF16)
    p = jnp.dot(h, w_ref[...], preferred_element_type=F32)

    qt = p[:, :ATTN_WIDTH].T
    cr, cc = cos_t_ref[0:16, :], cos_t_ref[16:32, :]
    sr, sc = sin_t_ref[0:16, :], sin_t_ref[16:32, :]
    pieces = []
    for hd in range(N_Q_HEADS):
        b = hd * HEAD_DIM
        x1r, x2r = qt[b:b + 16, :], qt[b + 16:b + 32, :]
        x1c, x2c = qt[b + 32:b + 48, :], qt[b + 48:b + 64, :]
        pieces += [x1r * cr - x2r * sr, x1r * sr + x2r * cr,
                   x1c * cc - x2c * sc, x1c * sc + x2c * cc]
    qt_ref[...] = jnp.concatenate(pieces, axis=0).astype(BF16)

    k = p[:, ATTN_WIDTH:ATTN_WIDTH + KV_WIDTH]
    cos_l, s_up, s_dn = cos_l_ref[...], sup_ref[...], sdn_ref[...]
    kparts = []
    for g in range(KV_WIDTH // LANES):
        kg = k[:, g * LANES:(g + 1) * LANES]
        kparts.append(kg * cos_l + pltpu.roll(kg, LANES - 16, 1) * s_up
                      + pltpu.roll(kg, 16, 1) * s_dn)
    k_ref[...] = jnp.concatenate(kparts, axis=1).astype(BF16)

    vt_ref[...] = p[:, ATTN_WIDTH + KV_WIDTH:ATTN_WIDTH + 2 * KV_WIDTH].T.astype(BF16)
    u_ref[...] = p[:, ATTN_WIDTH + 2 * KV_WIDTH:]


def _inproj(x, norm_w, shift, scale, w_in_bf16, tables):
    cos_t, sin_t, cos_l, s_up, s_dn = tables
    tm = INPROJ_TM
    row = lambda i: (i, 0)
    colb = lambda i: (0, i)
    fixed = lambda i: (0, 0)
    return pl.pallas_call(
        _inproj_kernel,
        grid=(SEQ // tm,),
        in_specs=[
            pl.BlockSpec((tm, D_MODEL), row),
            pl.BlockSpec((1, D_MODEL), fixed),
            pl.BlockSpec((1, D_MODEL), fixed),
            pl.BlockSpec((1, D_MODEL), fixed),
            pl.BlockSpec((D_MODEL, IN_WIDTH), fixed),
            pl.BlockSpec((32, tm), colb),
            pl.BlockSpec((32, tm), colb),
            pl.BlockSpec((tm, LANES), row),
            pl.BlockSpec((tm, LANES), row),
            pl.BlockSpec((tm, LANES), row),
        ],
        out_specs=[
            pl.BlockSpec((ATTN_WIDTH, tm), colb),
            pl.BlockSpec((tm, KV_WIDTH), row),
            pl.BlockSpec((KV_WIDTH, tm), colb),
            pl.BlockSpec((tm, POOL_WIDTH), row),
        ],
        out_shape=[
            jax.ShapeDtypeStruct((ATTN_WIDTH, SEQ), BF16),
            jax.ShapeDtypeStruct((SEQ, KV_WIDTH), BF16),
            jax.ShapeDtypeStruct((KV_WIDTH, SEQ), BF16),
            jax.ShapeDtypeStruct((SEQ, POOL_WIDTH), F32),
        ],
        compiler_params=_params("arbitrary"),
        name="inproj",
    )(x, norm_w, shift, scale, w_in_bf16, cos_t, sin_t, cos_l, s_up, s_dn)


def _ctx_kv_kernel(x_ref, nw_ref, sh_ref, sc_ref, w_ref, k_ref, vt_ref):
    h = _rms_modulate(x_ref[...], nw_ref[...], sh_ref[...], sc_ref[...]).astype(BF16)
    p = jnp.dot(h, w_ref[...], preferred_element_type=F32)
    k_ref[...] = p[:, :KV_WIDTH].astype(BF16)
    vt_ref[...] = p[:, KV_WIDTH:].T.astype(BF16)


def _ctx_kv(ctx, norm_w, shift, scale, w_in_bf16):
    fixed = lambda i: (0, 0)
    kv_block = ATTN_WIDTH // (2 * KV_WIDTH)
    return pl.pallas_call(
        _ctx_kv_kernel,
        grid=(1,),
        in_specs=[
            pl.BlockSpec((CTX_LEN, D_MODEL), fixed),
            pl.BlockSpec((1, D_MODEL), fixed),
            pl.BlockSpec((1, D_MODEL), fixed),
            pl.BlockSpec((1, D_MODEL), fixed),
            pl.BlockSpec((D_MODEL, 2 * KV_WIDTH), lambda i: (0, kv_block)),
        ],
        out_specs=[
            pl.BlockSpec((CTX_LEN, KV_WIDTH), fixed),
            pl.BlockSpec((KV_WIDTH, CTX_LEN), fixed),
        ],
        out_shape=[
            jax.ShapeDtypeStruct((CTX_LEN, KV_WIDTH), BF16),
            jax.ShapeDtypeStruct((KV_WIDTH, CTX_LEN), BF16),
        ],
        compiler_params=_params("arbitrary"),
        name="ctx_kv",
    )(ctx, norm_w, shift, scale, w_in_bf16)


def _attn_kernel(qt_ref, kp_ref, km_ref, kn_ref, vtp_ref, vtm_ref, vtn_ref,
                 kc_ref, vtc_ref, sink_ref, wout_ref, wup_ref, wdn_ref,
                 o_ref, wout_b_ref, wup_b_ref, wdn_b_ref):
    i = pl.program_id(0)
    nsub = ATTN_TQ // BLOCK
    nblk = SEQ // BLOCK
    pw = 2 * BLOCK
    nkeys = 3 * BLOCK + CTX_LEN

    key_j = lax.broadcasted_iota(jnp.int32, (BLOCK, pw), 0)
    qry_i = lax.broadcasted_iota(jnp.int32, (BLOCK, pw), 1) % BLOCK
    neg = jnp.full((BLOCK, pw), NEG_INF, F32)
    zero = jnp.zeros((BLOCK, pw), F32)
    band_prev = jnp.where(key_j >= qry_i, zero, neg)
    band_next = jnp.where(key_j <= qry_i, zero, neg)

    kc = kc_ref[...]
    vtc = vtc_ref[...]
    ones_rows = jnp.ones((2 * SUBLANES, nkeys), BF16)

    windows = []
    for sb in range(nsub):
        n = i * nsub + sb
        lo, hi = sb * BLOCK, (sb + 1) * BLOCK
        k_prev = kp_ref[...] if sb == 0 else km_ref[lo - BLOCK:lo, :]
        k_next = kn_ref[...] if sb == nsub - 1 else km_ref[hi:hi + BLOCK, :]
        vt_prev = vtp_ref[...] if sb == 0 else vtm_ref[:, lo - BLOCK:lo]
        vt_next = vtn_ref[...] if sb == nsub - 1 else vtm_ref[:, hi:hi + BLOCK]
        windows.append(dict(
            kwin=jnp.concatenate([k_prev, km_ref[lo:hi, :], k_next, kc], axis=0),
            vtwin=jnp.concatenate([vt_prev, vtm_ref[:, lo:hi], vt_next, vtc], axis=1),
            bias_prev=band_prev + jnp.where(n > 0, 0.0, NEG_INF),
            bias_next=band_next + jnp.where(n < nblk - 1, 0.0, NEG_INF)))

    units = [(sb, h, half) for sb in range(nsub) for h in range(N_KV_HEADS)
             for half in range(GQA // 2)]

    def scores(unit):
        sb, h, half = unit
        win = windows[sb]
        lo, hi = sb * BLOCK, (sb + 1) * BLOCK
        hd0 = GQA * h + 2 * half
        qh = jnp.concatenate([qt_ref[(hd0 + g) * HEAD_DIM:(hd0 + g + 1) * HEAD_DIM, lo:hi]
                              for g in range(2)], axis=1)
        blocks = []
        if h > 0:
            blocks.append(jnp.zeros((h * HEAD_DIM, pw), BF16))
        blocks.append(qh)
        if h < N_KV_HEADS - 1:
            blocks.append(jnp.zeros(((N_KV_HEADS - 1 - h) * HEAD_DIM, pw), BF16))
        s = jnp.dot(win["kwin"], jnp.concatenate(blocks, axis=0),
                    preferred_element_type=F32)
        return jnp.concatenate([s[0:BLOCK] + win["bias_prev"], s[BLOCK:2 * BLOCK],
                                s[2 * BLOCK:3 * BLOCK] + win["bias_next"], s[3 * BLOCK:]], axis=0)

    def probs(unit, s):
        sb, h, half = unit
        hd0 = GQA * h + 2 * half
        sink = sink_ref[:, hd0 * BLOCK:(hd0 + 2) * BLOCK] * LOG2E
        m = jnp.maximum(jnp.max(s, axis=0, keepdims=True), sink)
        return jnp.exp2(s - m).astype(BF16), jnp.exp2(sink - m)

    def finish(unit, e, e_sink):
        sb, h, half = unit
        lo, hi = sb * BLOCK, (sb + 1) * BLOCK
        hd0 = GQA * h + 2 * half
        vt_ones = jnp.concatenate(
            [windows[sb]["vtwin"][h * HEAD_DIM:(h + 1) * HEAD_DIM, :], ones_rows], axis=0)
        o = jnp.dot(vt_ones, e, preferred_element_type=F32)
        denom = o[HEAD_DIM:HEAD_DIM + 1, :] + e_sink
        o = o[:HEAD_DIM, :] * (1.0 / denom)
        blk = jnp.concatenate([o[:, 0:BLOCK], o[:, BLOCK:2 * BLOCK]], axis=0)
        o_ref[lo:hi, hd0 * HEAD_DIM:(hd0 + 2) * HEAD_DIM] = blk.T.astype(BF16)

    def cast_piece(idx, n):
        for src, dst in ((wout_ref, wout_b_ref), (wup_ref, wup_b_ref), (wdn_ref, wdn_b_ref)):
            wc = src.shape[1] // n
            dst[:, idx * wc:(idx + 1) * wc] = src[:, idx * wc:(idx + 1) * wc].astype(BF16)

    nu = len(units)
    s_vals, p_vals = {}, {}
    for step in range(nu + 2):
        if step < nu:
            s_vals[step] = scores(units[step])
        if 1 <= step <= nu:
            p_vals[step - 1] = probs(units[step - 1], s_vals.pop(step - 1))
            cast_piece(step - 1, nu)
        if step >= 2:
            finish(units[step - 2], *p_vals.pop(step - 2))


def _attn(qt, k, vt, kc, vtc, sink_row, w_out, w_up, w_dn):
    tq = ATTN_TQ
    r = tq // BLOCK
    nblk = SEQ // BLOCK
    nsteps = SEQ // tq
    fixed = lambda i: (0, 0)
    row = lambda i: (i, 0)
    slab = lambda w: pl.BlockSpec((w.shape[0] // nsteps, w.shape[1]), row)
    bf16_like = lambda w: jax.ShapeDtypeStruct(w.shape, BF16)
    prev = lambda i: jnp.maximum(i * r - 1, 0)
    nxt = lambda i: jnp.minimum((i + 1) * r, nblk - 1)
    return pl.pallas_call(
        _attn_kernel,
        grid=(SEQ // tq,),
        in_specs=[
            pl.BlockSpec((ATTN_WIDTH, tq), lambda i: (0, i)),
            pl.BlockSpec((BLOCK, KV_WIDTH), lambda i: (prev(i), 0)),
            pl.BlockSpec((tq, KV_WIDTH), lambda i: (i, 0)),
            pl.BlockSpec((BLOCK, KV_WIDTH), lambda i: (nxt(i), 0)),
            pl.BlockSpec((KV_WIDTH, BLOCK), lambda i: (0, prev(i))),
            pl.BlockSpec((KV_WIDTH, tq), lambda i: (0, i)),
            pl.BlockSpec((KV_WIDTH, BLOCK), lambda i: (0, nxt(i))),
            pl.BlockSpec((CTX_LEN, KV_WIDTH), fixed),
            pl.BlockSpec((KV_WIDTH, CTX_LEN), fixed),
            pl.BlockSpec((1, N_KV_HEADS * GQA * BLOCK), fixed),
            slab(w_out), slab(w_up), slab(w_dn),
        ],
        out_specs=[pl.BlockSpec((tq, ATTN_WIDTH), row), slab(w_out), slab(w_up), slab(w_dn)],
        out_shape=[jax.ShapeDtypeStruct((SEQ, ATTN_WIDTH), BF16),
                   bf16_like(w_out), bf16_like(w_up), bf16_like(w_dn)],
        compiler_params=_params("arbitrary"),
        name="attn",
    )(qt, k, k, k, vt, vt, vt, kc, vtc, sink_row, w_out, w_up, w_dn)


def _mixout_kernel(x_ref, attn_ref, up_ref, um_ref, un_ref, wout_ref, pw_ref, ps_ref,
                   ga_ref, nw_ref, sh_ref, sc_ref, x1_ref, hm_ref, ubuf):
    i = pl.program_id(0)
    tm = MIX_TM
    nsteps = SEQ // tm
    ubuf[0:POOL_HALO, :] = jnp.where(i > 0, up_ref[...], 0.0)
    ubuf[POOL_HALO:POOL_HALO + tm, :] = um_ref[...]
    ubuf[POOL_HALO + tm:2 * POOL_HALO + tm, :] = jnp.where(i < nsteps - 1, un_ref[...], 0.0)

    rc = MIX_CHUNK
    n = rc + 2 * POOL_HALO
    c = POOL_HALO
    ngroups = len(POOL_WINDOWS)
    ncol = D_MODEL // ngroups

    def pool_group(r0, g):
        w = POOL_WINDOWS[g]
        c0, c1 = g * POOL_GROUP_DIM, (g + 1) * POOL_GROUP_DIM
        e = ubuf[r0:r0 + n, c0:c1]
        if w == 2:
            win = e[c - 1:c - 1 + rc] + e[c:c + rc]
        else:
            f2 = e[0:n - 1] + e[1:n]
            if w == 4:
                win = f2[c - 2:c - 2 + rc] + f2[c:c + rc]
            else:
                f4 = f2[0:n - 3] + f2[2:n - 1]
                if w == 8:
                    win = f4[c - 4:c - 4 + rc] + f4[c:c + rc]
                else:
                    f8 = f4[0:n - 7] + f4[4:n - 3]
                    win = f8[c - 8:c - 8 + rc] + f8[c:c + rc]
        t = i * tm + r0 + lax.broadcasted_iota(jnp.int32, (rc, 1), 0)
        lo = jnp.clip(t - w // 2, 0, SEQ)
        hi = jnp.clip(t - w // 2 + w, 0, SEQ)
        cnt = (hi - lo).astype(F32)
        pg = win / cnt - um_ref[r0:r0 + rc, c0:c1]
        mixed = jnp.dot(pg.astype(BF16), pw_ref[g], preferred_element_type=F32)
        return (mixed * ps_ref[:, c0:c1]).astype(BF16)

    def prenorm(r0, nrows):
        rows = slice(r0, r0 + nrows)
        hm_ref[rows, :] = _rms_modulate(x1_ref[rows, :], nw_ref[...], sh_ref[...],
                                        sc_ref[...]).astype(BF16)

    chunk_starts = list(range(0, tm, rc))
    for ci, r0 in enumerate(chunk_starts):
        rows = slice(r0, r0 + rc)
        attn_rows = attn_ref[rows, :]
        pooled = []
        for j in range(ngroups):
            cols = slice(j * ncol, (j + 1) * ncol)
            o = jnp.dot(attn_rows, wout_ref[0:ATTN_WIDTH, cols], preferred_element_type=F32)
            x1_ref[rows, cols] = x_ref[rows, cols] + ga_ref[:, cols] * o
            pooled.append(pool_group(r0, j))
        pool_rows = jnp.concatenate(pooled, axis=1)
        for j in range(ngroups):
            cols = slice(j * ncol, (j + 1) * ncol)
            o = jnp.dot(pool_rows, wout_ref[ATTN_WIDTH:, cols], preferred_element_type=F32)
            x1_ref[rows, cols] += ga_ref[:, cols] * o
            if ci > 0:
                prenorm(chunk_starts[ci - 1] + j * (rc // ngroups), rc // ngroups)
    for j in range(ngroups):
        prenorm(chunk_starts[-1] + j * (rc // ngroups), rc // ngroups)


def _mixout(x, attn, u, w_out_bf16, pool_w_bf16, pool_scale, g_a, norm_w, shift, scale):
    tm = MIX_TM
    hb = tm // POOL_HALO
    nhalo = SEQ // POOL_HALO
    row = lambda i: (i, 0)
    fixed = lambda i: (0, 0)
    return pl.pallas_call(
        _mixout_kernel,
        grid=(SEQ // tm,),
        in_specs=[
            pl.BlockSpec((tm, D_MODEL), row),
            pl.BlockSpec((tm, ATTN_WIDTH), row),
            pl.BlockSpec((POOL_HALO, POOL_WIDTH), lambda i: (jnp.maximum(i * hb - 1, 0), 0)),
            pl.BlockSpec((tm, POOL_WIDTH), row),
            pl.BlockSpec((POOL_HALO, POOL_WIDTH), lambda i: (jnp.minimum((i + 1) * hb, nhalo - 1), 0)),
            pl.BlockSpec((D_MODEL, D_MODEL), fixed),
            pl.BlockSpec((len(POOL_WINDOWS), POOL_GROUP_DIM, POOL_GROUP_DIM), lambda i: (0, 0, 0)),
            pl.BlockSpec((1, POOL_WIDTH), fixed),
            pl.BlockSpec((1, D_MODEL), fixed),
            pl.BlockSpec((1, D_MODEL), fixed),
            pl.BlockSpec((1, D_MODEL), fixed),
            pl.BlockSpec((1, D_MODEL), fixed),
        ],
        out_specs=[pl.BlockSpec((tm, D_MODEL), row), pl.BlockSpec((tm, D_MODEL), row)],
        out_shape=[jax.ShapeDtypeStruct((SEQ, D_MODEL), F32),
                   jax.ShapeDtypeStruct((SEQ, D_MODEL), BF16)],
        scratch_shapes=[pltpu.VMEM((tm + 2 * POOL_HALO, POOL_WIDTH), F32)],
        compiler_params=_params("arbitrary"),
        name="mixout",
    )(x, attn, u, u, u, w_out_bf16, pool_w_bf16, pool_scale, g_a, norm_w, shift, scale)


def _mlp_kernel(hm_ref, x1c_ref, wup_ref, wdn_ref, gm_ref, fw_ref, o_ref):
    f = pl.program_id(1)
    row_chunks = [slice(r0, r0 + MLP_RC) for r0 in range(0, MLP_TM, MLP_RC)]

    @pl.when(f == 0)
    def _():
        o_ref[...] = jnp.zeros_like(o_ref)

    gm = gm_ref[...]
    for rows in row_chunks:
        a = jnp.maximum(jnp.dot(hm_ref[rows, :], wup_ref[...], preferred_element_type=F32), 0.0)
        o_ref[rows, :] += gm * jnp.dot((a * a).astype(BF16), wdn_ref[...],
                                      preferred_element_type=F32)

    for j in range(MLP_NF):
        @pl.when(f == j)
        def _(j=j):
            o_ref[:, j * MLP_XC:(j + 1) * MLP_XC] += x1c_ref[...]

    @pl.when(f == MLP_NF - 1)
    def _():
        for rows in row_chunks:
            x2 = o_ref[rows, :]
            y = x2 * lax.rsqrt(jnp.mean(x2 * x2, axis=-1, keepdims=True) + EPS)
            o_ref[rows, :] = y * fw_ref[...]


def _mlp(hm, x1, w_up_bf16, w_dn_bf16, g_m, final_w):
    tm, tf = MLP_TM, MLP_TF
    return pl.pallas_call(
        _mlp_kernel,
        grid=(SEQ // tm, MLP_NF),
        in_specs=[
            pl.BlockSpec((tm, D_MODEL), lambda i, f: (i, 0)),
            pl.BlockSpec((tm, MLP_XC), lambda i, f: (i, f)),
            pl.BlockSpec((D_MODEL, tf), lambda i, f: (0, f)),
            pl.BlockSpec((tf, D_MODEL), lambda i, f: (f, 0)),
            pl.BlockSpec((1, D_MODEL), lambda i, f: (0, 0)),
            pl.BlockSpec((1, D_MODEL), lambda i, f: (0, 0)),
        ],
        out_specs=pl.BlockSpec((tm, D_MODEL), lambda i, f: (i, 0)),
        out_shape=jax.ShapeDtypeStruct((SEQ, D_MODEL), F32),
        compiler_params=_params("arbitrary", "arbitrary"),
        name="mlp",
    )(hm, x1, w_up_bf16, w_dn_bf16, g_m, final_w)


def kernel(x, c, ctx, c_ctx, norm_attn_w, norm_mlp_w, w_ada, b_ada, w_in, attn_sink,
           pool_w, pool_scale, w_out, w_mlp_up, w_mlp_down, final_norm_w):
    assert x.shape == (1, SEQ, D_MODEL) and ctx.shape == (1, CTX_LEN, D_MODEL)
    assert w_ada.shape[0] == 1, "single layer"
    x2d, ctx2d = x[0], ctx[0]

    cond = jnp.concatenate([c, c_ctx[None, :], jnp.zeros((SUBLANES - 2, D_MODEL), F32)], axis=0)
    mod = _ada(cond, w_ada[0], b_ada[0][None, :])
    chunk = lambda r, j: mod[r:r + 1, j * D_MODEL:(j + 1) * D_MODEL]
    sh_a, sc_a, g_a, sh_m, sc_m, g_m = (chunk(0, j) for j in range(N_MOD))
    csh_a, csc_a = chunk(1, 0), chunk(1, 1)

    w_in_b = w_in[0].astype(BF16)
    nw_a = norm_attn_w[0][None, :]
    tables = tuple(jnp.asarray(t) for t in _rope_tables())
    qt, k, vt, u = _inproj(x2d, nw_a, sh_a, sc_a, w_in_b, tables)
    kc, vtc = _ctx_kv(ctx2d, nw_a, csh_a, csc_a, w_in_b)

    sink_row = jnp.repeat(attn_sink[0].astype(F32), BLOCK)[None, :]
    attn, w_out_b, w_up_b, w_dn_b = _attn(qt, k, vt, kc, vtc, sink_row,
                                          w_out[0], w_mlp_up[0], w_mlp_down[0])

    x1, hm = _mixout(x2d, attn, u, w_out_b, pool_w[0].astype(BF16),
                     pool_scale[0][None, :], g_a, norm_mlp_w[0][None, :], sh_m, sc_m)
    out = _mlp(hm, x1, w_up_b, w_dn_b, g_m, final_norm_w[None, :])
    return out[None]
```

```python
import functools

import numpy as np
import jax
import jax.numpy as jnp
from jax import lax
from jax.experimental import pallas as pl
from jax.experimental.pallas import tpu as pltpu

F32 = jnp.float32
BF16 = jnp.bfloat16

D_MODEL = 2048
SEQ = 8192
CTX_LEN = 256
GRID_W = 64
HEAD_DIM = 64
N_Q_HEADS = 16
N_KV_HEADS = 4
GQA = N_Q_HEADS // N_KV_HEADS
ATTN_WIDTH = N_Q_HEADS * HEAD_DIM
KV_WIDTH = N_KV_HEADS * HEAD_DIM
POOL_WINDOWS = (2, 4, 8, 16)
POOL_WIDTH = D_MODEL - ATTN_WIDTH
POOL_GROUP_DIM = POOL_WIDTH // len(POOL_WINDOWS)
IN_WIDTH = ATTN_WIDTH + 2 * KV_WIDTH + POOL_WIDTH
D_FF = 4 * D_MODEL
BLOCK = 128
ROPE_BASE = 10000.0
N_MOD = 6
EPS = 1e-6
NEG_INF = -1e30
LOG2E = 1.4426950408889634
Q_SCALE = HEAD_DIM ** -0.5 * LOG2E

LANES = 128
SUBLANES = 8
POOL_HALO = 8
VMEM_LIMIT = 56 * 1024 * 1024

ADA_TN = 1024
INPROJ_TM = 512
ATTN_TQ = 512
CAST_JOB_ELEMS = 64 * 1024
MIX_TM = 512
MIX_CHUNK = 256
MLP_TM = 1024
MLP_TF = 1024
MLP_RC = 256
MLP_NF = D_FF // MLP_TF
MLP_XC = D_MODEL // MLP_NF


def _params(*sem):
    return pltpu.CompilerParams(dimension_semantics=sem, vmem_limit_bytes=VMEM_LIMIT)


def _rms_modulate(x, norm_w, shift, scale):
    gain = norm_w * (1.0 + scale)
    y = x * lax.rsqrt(jnp.mean(x * x, axis=-1, keepdims=True) + EPS)
    return y * gain + shift


def _ada_block(cond_ref, w_ref, b_ref):
    cnd = cond_ref[...]
    s = cnd * (1.0 / (1.0 + jnp.exp(-cnd)))
    return jnp.dot(s, w_ref[...], preferred_element_type=F32) + b_ref[...]


def _ada_kernel(cond_ref, w_ref, b_ref, o_ref):
    o_ref[...] = _ada_block(cond_ref, w_ref, b_ref)


def _ada(cond, w_ada, b_ada, n):
    rows = cond.shape[0]
    return pl.pallas_call(
        _ada_kernel,
        grid=(n // ADA_TN,),
        in_specs=[
            pl.BlockSpec((rows, D_MODEL), lambda j: (0, 0)),
            pl.BlockSpec((D_MODEL, ADA_TN), lambda j: (0, j)),
            pl.BlockSpec((1, ADA_TN), lambda j: (0, j)),
        ],
        out_specs=pl.BlockSpec((rows, ADA_TN), lambda j: (0, j)),
        out_shape=jax.ShapeDtypeStruct((rows, n), F32),
        compiler_params=_params("arbitrary"),
        name="ada",
    )(cond, w_ada, b_ada)


def _rope_tables():
    half = HEAD_DIM // 2
    inv_freq = ROPE_BASE ** (-np.arange(0, half, 2, dtype=np.float64) / half)
    t = np.arange(SEQ)
    row = (t // GRID_W).astype(np.float64)
    col = (t % GRID_W).astype(np.float64)
    ang_r = row[:, None] * inv_freq[None, :]
    ang_c = col[:, None] * inv_freq[None, :]
    cos_t = Q_SCALE * np.concatenate([np.cos(ang_r), np.cos(ang_c)], axis=1).T
    sin_t = Q_SCALE * np.concatenate([np.sin(ang_r), np.sin(ang_c)], axis=1).T
    cos_l = np.concatenate([np.cos(ang_r), np.cos(ang_r), np.cos(ang_c), np.cos(ang_c)], axis=1)
    zeros = np.zeros_like(ang_r)
    s_up = np.concatenate([-np.sin(ang_r), zeros, -np.sin(ang_c), zeros], axis=1)
    s_dn = np.concatenate([zeros, np.sin(ang_r), zeros, np.sin(ang_c)], axis=1)
    tile2 = lambda a: np.concatenate([a, a], axis=1).astype(np.float32)
    return (cos_t.astype(np.float32), sin_t.astype(np.float32),
            tile2(cos_l), tile2(s_up), tile2(s_dn))


def _inproj_kernel(x_ref, nw_ref, sh_ref, sc_ref, w_ref, cos_t_ref, sin_t_ref,
                   cos_l_ref, sup_ref, sdn_ref, qt_ref, k_ref, vt_ref, u_ref):
    h = _rms_modulate(x_ref[...], nw_ref[...], sh_ref[...], sc_ref[...]).astype(BF16)
    p = jnp.dot(h, w_ref[...], preferred_element_type=F32)

    qt = p[:, :ATTN_WIDTH].T
    cr, cc = cos_t_ref[0:16, :], cos_t_ref[16:32, :]
    sr, sc = sin_t_ref[0:16, :], sin_t_ref[16:32, :]
    pieces = []
    for hd in range(N_Q_HEADS):
        b = hd * HEAD_DIM
        x1r, x2r = qt[b:b + 16, :], qt[b + 16:b + 32, :]
        x1c, x2c = qt[b + 32:b + 48, :], qt[b + 48:b + 64, :]
        pieces += [x1r * cr - x2r * sr, x1r * sr + x2r * cr,
                   x1c * cc - x2c * sc, x1c * sc + x2c * cc]
    qt_ref[...] = jnp.concatenate(pieces, axis=0).astype(BF16)

    k = p[:, ATTN_WIDTH:ATTN_WIDTH + KV_WIDTH]
    cos_l, s_up, s_dn = cos_l_ref[...], sup_ref[...], sdn_ref[...]
    kparts = []
    for g in range(KV_WIDTH // LANES):
        kg = k[:, g * LANES:(g + 1) * LANES]
        kparts.append(kg * cos_l + pltpu.roll(kg, LANES - 16, 1) * s_up
                      + pltpu.roll(kg, 16, 1) * s_dn)
    k_ref[...] = jnp.concatenate(kparts, axis=1).astype(BF16)

    vt_ref[...] = p[:, ATTN_WIDTH + KV_WIDTH:ATTN_WIDTH + 2 * KV_WIDTH].T.astype(BF16)
    u_ref[...] = p[:, ATTN_WIDTH + 2 * KV_WIDTH:]


def _inproj(x, norm_w, shift, scale, w_in_bf16, tables):
    cos_t, sin_t, cos_l, s_up, s_dn = tables
    tm = INPROJ_TM
    row = lambda i: (i, 0)
    colb = lambda i: (0, i)
    fixed = lambda i: (0, 0)
    return pl.pallas_call(
        _inproj_kernel,
        grid=(SEQ // tm,),
        in_specs=[
            pl.BlockSpec((tm, D_MODEL), row),
            pl.BlockSpec((1, D_MODEL), fixed),
            pl.BlockSpec((1, D_MODEL), fixed),
            pl.BlockSpec((1, D_MODEL), fixed),
            pl.BlockSpec((D_MODEL, IN_WIDTH), fixed),
            pl.BlockSpec((32, tm), colb),
            pl.BlockSpec((32, tm), colb),
            pl.BlockSpec((tm, LANES), row),
            pl.BlockSpec((tm, LANES), row),
            pl.BlockSpec((tm, LANES), row),
        ],
        out_specs=[
            pl.BlockSpec((ATTN_WIDTH, tm), colb),
            pl.BlockSpec((tm, KV_WIDTH), row),
            pl.BlockSpec((KV_WIDTH, tm), colb),
            pl.BlockSpec((tm, POOL_WIDTH), row),
        ],
        out_shape=[
            jax.ShapeDtypeStruct((ATTN_WIDTH, SEQ), BF16),
            jax.ShapeDtypeStruct((SEQ, KV_WIDTH), BF16),
            jax.ShapeDtypeStruct((KV_WIDTH, SEQ), BF16),
            jax.ShapeDtypeStruct((SEQ, POOL_WIDTH), F32),
        ],
        compiler_params=_params("arbitrary"),
        name="inproj",
    )(x, norm_w, shift, scale, w_in_bf16, cos_t, sin_t, cos_l, s_up, s_dn)


def _ctx_kv_kernel(x_ref, nw_ref, sh_ref, sc_ref, w_ref, k_ref, vt_ref):
    h = _rms_modulate(x_ref[...], nw_ref[...], sh_ref[...], sc_ref[...]).astype(BF16)
    p = jnp.dot(h, w_ref[...], preferred_element_type=F32)
    k_ref[...] = p[:, :KV_WIDTH].astype(BF16)
    vt_ref[...] = p[:, KV_WIDTH:].T.astype(BF16)


def _ctx_kv(ctx, norm_w, shift, scale, w_in_bf16):
    fixed = lambda i: (0, 0)
    kv_block = ATTN_WIDTH // (2 * KV_WIDTH)
    return pl.pallas_call(
        _ctx_kv_kernel,
        grid=(1,),
        in_specs=[
            pl.BlockSpec((CTX_LEN, D_MODEL), fixed),
            pl.BlockSpec((1, D_MODEL), fixed),
            pl.BlockSpec((1, D_MODEL), fixed),
            pl.BlockSpec((1, D_MODEL), fixed),
            pl.BlockSpec((D_MODEL, 2 * KV_WIDTH), lambda i: (0, kv_block)),
        ],
        out_specs=[
            pl.BlockSpec((CTX_LEN, KV_WIDTH), fixed),
            pl.BlockSpec((KV_WIDTH, CTX_LEN), fixed),
        ],
        out_shape=[
            jax.ShapeDtypeStruct((CTX_LEN, KV_WIDTH), BF16),
            jax.ShapeDtypeStruct((KV_WIDTH, CTX_LEN), BF16),
        ],
        compiler_params=_params("arbitrary"),
        name="ctx_kv",
    )(ctx, norm_w, shift, scale, w_in_bf16)


def _attn_kernel(qt_ref, kp_ref, km_ref, kn_ref, vtp_ref, vtm_ref, vtn_ref,
                 kc_ref, vtc_ref, sink_ref, wout_ref, wup_ref, wdn_ref,
                 cond_ref, wada_ref, bada_ref,
                 o_ref, wout_b_ref, wup_b_ref, wdn_b_ref, mod_ref):
    mod_ref[...] = _ada_block(cond_ref, wada_ref, bada_ref)

    i = pl.program_id(0)
    nsub = ATTN_TQ // BLOCK
    nblk = SEQ // BLOCK
    gw = GQA * BLOCK
    pw = 2 * BLOCK
    nkeys = 3 * BLOCK + CTX_LEN

    key_j = lax.broadcasted_iota(jnp.int32, (BLOCK, gw), 0)
    qry_i = lax.broadcasted_iota(jnp.int32, (BLOCK, gw), 1) % BLOCK
    neg = jnp.full((BLOCK, gw), NEG_INF, F32)
    zero = jnp.zeros((BLOCK, gw), F32)
    band_prev = jnp.where(key_j >= qry_i, zero, neg)
    band_next = jnp.where(key_j <= qry_i, zero, neg)

    kc = kc_ref[...]
    vtc = vtc_ref[...]
    ones_rows = jnp.ones((2 * SUBLANES, nkeys), BF16)

    windows = []
    for sb in range(nsub):
        n = i * nsub + sb
        lo, hi = sb * BLOCK, (sb + 1) * BLOCK
        k_prev = kp_ref[...] if sb == 0 else km_ref[lo - BLOCK:lo, :]
        k_next = kn_ref[...] if sb == nsub - 1 else km_ref[hi:hi + BLOCK, :]
        vt_prev = vtp_ref[...] if sb == 0 else vtm_ref[:, lo - BLOCK:lo]
        vt_next = vtn_ref[...] if sb == nsub - 1 else vtm_ref[:, hi:hi + BLOCK]
        windows.append(dict(
            kwin=jnp.concatenate([k_prev, km_ref[lo:hi, :], k_next, kc], axis=0),
            vtwin=jnp.concatenate([vt_prev, vtm_ref[:, lo:hi], vt_next, vtc], axis=1),
            bias_prev=band_prev + jnp.where(n > 0, 0.0, NEG_INF),
            bias_next=band_next + jnp.where(n < nblk - 1, 0.0, NEG_INF)))

    units = [(sb, h, half) for sb in range(nsub) for h in range(N_KV_HEADS)
             for half in range(GQA // 2)]

    def scores(sb, h):
        win = windows[sb]
        lo, hi = sb * BLOCK, (sb + 1) * BLOCK
        qh = jnp.concatenate(
            [qt_ref[(GQA * h + g) * HEAD_DIM:(GQA * h + g + 1) * HEAD_DIM, lo:hi]
             for g in range(GQA)], axis=1)
        blocks = []
        if h > 0:
            blocks.append(jnp.zeros((h * HEAD_DIM, gw), BF16))
        blocks.append(qh)
        if h < N_KV_HEADS - 1:
            blocks.append(jnp.zeros(((N_KV_HEADS - 1 - h) * HEAD_DIM, gw), BF16))
        s = jnp.dot(win["kwin"], jnp.concatenate(blocks, axis=0),
                    preferred_element_type=F32)
        s = jnp.concatenate([s[0:BLOCK] + win["bias_prev"], s[BLOCK:2 * BLOCK],
                             s[2 * BLOCK:3 * BLOCK] + win["bias_next"], s[3 * BLOCK:]], axis=0)
        return s, jnp.max(s, axis=0, keepdims=True)

    def probs(unit, s, smax):
        sb, h, half = unit
        hd0 = GQA * h + 2 * half
        cols = slice(half * pw, (half + 1) * pw)
        sink = sink_ref[:, hd0 * BLOCK:(hd0 + 2) * BLOCK] * LOG2E
        m = jnp.maximum(smax[:, cols], sink)
        return jnp.exp2(s[:, cols] - m).astype(BF16), jnp.exp2(sink - m)

    def finish(unit, e, e_sink):
        sb, h, half = unit
        lo, hi = sb * BLOCK, (sb + 1) * BLOCK
        hd0 = GQA * h + 2 * half
        vt_ones = jnp.concatenate(
            [windows[sb]["vtwin"][h * HEAD_DIM:(h + 1) * HEAD_DIM, :], ones_rows], axis=0)
        o = jnp.dot(vt_ones, e, preferred_element_type=F32)
        denom = o[HEAD_DIM:HEAD_DIM + 1, :] + e_sink
        o = o[:HEAD_DIM, :] * (1.0 / denom)
        blk = jnp.concatenate([o[:, 0:BLOCK], o[:, BLOCK:2 * BLOCK]], axis=0)
        o_ref[lo:hi, hd0 * HEAD_DIM:(hd0 + 2) * HEAD_DIM] = blk.T.astype(BF16)

    cast_jobs = []
    for src, dst in ((wout_ref, wout_b_ref), (wup_ref, wup_b_ref), (wdn_ref, wdn_b_ref)):
        wc = max(LANES, CAST_JOB_ELEMS // src.shape[0])
        cast_jobs += [(src, dst, c0) + (wc,) for c0 in range(0, src.shape[1], wc)]

    def cast_piece(idx, n):
        for src, dst, c0, wc in cast_jobs[idx * len(cast_jobs) // n:(idx + 1) * len(cast_jobs) // n]:
            dst[:, c0:c0 + wc] = src[:, c0:c0 + wc].astype(BF16)

    nu = len(units)
    s_vals, p_vals = {}, {}
    for step in range(nu + 2):
        if step < nu and units[step][2] == 0:
            s_vals[units[step][:2]] = scores(*units[step][:2])
        if 1 <= step <= nu:
            unit = units[step - 1]
            p_vals[step - 1] = probs(unit, *s_vals[unit[:2]])
            if unit[2] == GQA // 2 - 1:
                del s_vals[unit[:2]]
            cast_piece(step - 1, nu)
        if step >= 2:
            finish(units[step - 2], *p_vals.pop(step - 2))


def _attn(qt, k, vt, kc, vtc, sink_row, w_out, w_up, w_dn, cond, w_ada, b_ada, ada_col0):
    tq = ATTN_TQ
    r = tq // BLOCK
    nblk = SEQ // BLOCK
    nsteps = SEQ // tq
    ada_tn = (w_ada.shape[1] - ada_col0) // nsteps
    ada_b0 = ada_col0 // ada_tn
    fixed = lambda i: (0, 0)
    row = lambda i: (i, 0)
    slab = lambda w: pl.BlockSpec((w.shape[0] // nsteps, w.shape[1]), row)
    bf16_like = lambda w: jax.ShapeDtypeStruct(w.shape, BF16)
    prev = lambda i: jnp.maximum(i * r - 1, 0)
    nxt = lambda i: jnp.minimum((i + 1) * r, nblk - 1)
    return pl.pallas_call(
        _attn_kernel,
        grid=(SEQ // tq,),
        in_specs=[
            pl.BlockSpec((ATTN_WIDTH, tq), lambda i: (0, i)),
            pl.BlockSpec((BLOCK, KV_WIDTH), lambda i: (prev(i), 0)),
            pl.BlockSpec((tq, KV_WIDTH), lambda i: (i, 0)),
            pl.BlockSpec((BLOCK, KV_WIDTH), lambda i: (nxt(i), 0)),
            pl.BlockSpec((KV_WIDTH, BLOCK), lambda i: (0, prev(i))),
            pl.BlockSpec((KV_WIDTH, tq), lambda i: (0, i)),
            pl.BlockSpec((KV_WIDTH, BLOCK), lambda i: (0, nxt(i))),
            pl.BlockSpec((CTX_LEN, KV_WIDTH), fixed),
            pl.BlockSpec((KV_WIDTH, CTX_LEN), fixed),
            pl.BlockSpec((1, N_KV_HEADS * GQA * BLOCK), fixed),
            slab(w_out), slab(w_up), slab(w_dn),
            pl.BlockSpec(cond.shape, fixed),
            pl.BlockSpec((D_MODEL, ada_tn), lambda i: (0, ada_b0 + i)),
            pl.BlockSpec((1, ada_tn), lambda i: (0, ada_b0 + i)),
        ],
        out_specs=[pl.BlockSpec((tq, ATTN_WIDTH), row), slab(w_out), slab(w_up), slab(w_dn),
                   pl.BlockSpec((cond.shape[0], ada_tn), lambda i: (0, i))],
        out_shape=[jax.ShapeDtypeStruct((SEQ, ATTN_WIDTH), BF16),
                   bf16_like(w_out), bf16_like(w_up), bf16_like(w_dn),
                   jax.ShapeDtypeStruct((cond.shape[0], w_ada.shape[1] - ada_col0), F32)],
        compiler_params=_params("arbitrary"),
        name="attn",
    )(qt, k, k, k, vt, vt, vt, kc, vtc, sink_row, w_out, w_up, w_dn, cond, w_ada, b_ada)


def _mixout_kernel(x_ref, attn_ref, up_ref, um_ref, un_ref, wout_ref, pw_ref, ps_ref,
                   ga_ref, nw_ref, sh_ref, sc_ref, x1_ref, hm_ref, ubuf):
    i = pl.program_id(0)
    tm = MIX_TM
    nsteps = SEQ // tm
    ubuf[0:POOL_HALO, :] = jnp.where(i > 0, up_ref[...], 0.0)
    ubuf[POOL_HALO:POOL_HALO + tm, :] = um_ref[...]
    ubuf[POOL_HALO + tm:2 * POOL_HALO + tm, :] = jnp.where(i < nsteps - 1, un_ref[...], 0.0)

    rc = MIX_CHUNK
    n = rc + 2 * POOL_HALO
    c = POOL_HALO
    ngroups = len(POOL_WINDOWS)
    ncol = D_MODEL // ngroups

    def pool_group(r0, g):
        w = POOL_WINDOWS[g]
        c0, c1 = g * POOL_GROUP_DIM, (g + 1) * POOL_GROUP_DIM
        e = ubuf[r0:r0 + n, c0:c1]
        if w == 2:
            win = e[c - 1:c - 1 + rc] + e[c:c + rc]
        else:
            f2 = e[0:n - 1] + e[1:n]
            if w == 4:
                win = f2[c - 2:c - 2 + rc] + f2[c:c + rc]
            else:
                f4 = f2[0:n - 3] + f2[2:n - 1]
                if w == 8:
                    win = f4[c - 4:c - 4 + rc] + f4[c:c + rc]
                else:
                    f8 = f4[0:n - 7] + f4[4:n - 3]
                    win = f8[c - 8:c - 8 + rc] + f8[c:c + rc]
        t = i * tm + r0 + lax.broadcasted_iota(jnp.int32, (rc, 1), 0)
        lo = jnp.clip(t - w // 2, 0, SEQ)
        hi = jnp.clip(t - w // 2 + w, 0, SEQ)
        cnt = (hi - lo).astype(F32)
        pg = win / cnt - um_ref[r0:r0 + rc, c0:c1]
        mixed = jnp.dot(pg.astype(BF16), pw_ref[g], preferred_element_type=F32)
        return (mixed * ps_ref[:, c0:c1]).astype(BF16)

    def prenorm(r0, nrows):
        rows = slice(r0, r0 + nrows)
        hm_ref[rows, :] = _rms_modulate(x1_ref[rows, :], nw_ref[...], sh_ref[...],
                                        sc_ref[...]).astype(BF16)

    chunk_starts = list(range(0, tm, rc))
    for ci, r0 in enumerate(chunk_starts):
        rows = slice(r0, r0 + rc)
        attn_rows = attn_ref[rows, :]
        pooled = []
        for j in range(ngroups):
            cols = slice(j * ncol, (j + 1) * ncol)
            o = jnp.dot(attn_rows, wout_ref[0:ATTN_WIDTH, cols], preferred_element_type=F32)
            x1_ref[rows, cols] = x_ref[rows, cols] + ga_ref[:, cols] * o
            pooled.append(pool_group(r0, j))
        pool_rows = jnp.concatenate(pooled, axis=1)
        for j in range(ngroups):
            cols = slice(j * ncol, (j + 1) * ncol)
            o = jnp.dot(pool_rows, wout_ref[ATTN_WIDTH:, cols], preferred_element_type=F32)
            x1_ref[rows, cols] += ga_ref[:, cols] * o
            if ci > 0:
                prenorm(chunk_starts[ci - 1] + j * (rc // ngroups), rc // ngroups)
    for j in range(ngroups):
        prenorm(chunk_starts[-1] + j * (rc // ngroups), rc // ngroups)


def _mixout(x, attn, u, w_out_bf16, pool_w_bf16, pool_scale, g_a, norm_w, shift, scale):
    tm = MIX_TM
    hb = tm // POOL_HALO
    nhalo = SEQ // POOL_HALO
    row = lambda i: (i, 0)
    fixed = lambda i: (0, 0)
    return pl.pallas_call(
        _mixout_kernel,
        grid=(SEQ // tm,),
        in_specs=[
            pl.BlockSpec((tm, D_MODEL), row),
            pl.BlockSpec((tm, ATTN_WIDTH), row),
            pl.BlockSpec((POOL_HALO, POOL_WIDTH), lambda i: (jnp.maximum(i * hb - 1, 0), 0)),
            pl.BlockSpec((tm, POOL_WIDTH), row),
            pl.BlockSpec((POOL_HALO, POOL_WIDTH), lambda i: (jnp.minimum((i + 1) * hb, nhalo - 1), 0)),
            pl.BlockSpec((D_MODEL, D_MODEL), fixed),
            pl.BlockSpec((len(POOL_WINDOWS), POOL_GROUP_DIM, POOL_GROUP_DIM), lambda i: (0, 0, 0)),
            pl.BlockSpec((1, POOL_WIDTH), fixed),
            pl.BlockSpec((1, D_MODEL), fixed),
            pl.BlockSpec((1, D_MODEL), fixed),
            pl.BlockSpec((1, D_MODEL), fixed),
            pl.BlockSpec((1, D_MODEL), fixed),
        ],
        out_specs=[pl.BlockSpec((tm, D_MODEL), row), pl.BlockSpec((tm, D_MODEL), row)],
        out_shape=[jax.ShapeDtypeStruct((SEQ, D_MODEL), F32),
                   jax.ShapeDtypeStruct((SEQ, D_MODEL), BF16)],
        scratch_shapes=[pltpu.VMEM((tm + 2 * POOL_HALO, POOL_WIDTH), F32)],
        compiler_params=_params("arbitrary"),
        name="mixout",
    )(x, attn, u, u, u, w_out_bf16, pool_w_bf16, pool_scale, g_a, norm_w, shift, scale)


def _mlp_kernel(hm_ref, x1c_ref, wup_ref, wdn_ref, gm_ref, fw_ref, o_ref):
    f = pl.program_id(1)
    row_chunks = [slice(r0, r0 + MLP_RC) for r0 in range(0, MLP_TM, MLP_RC)]

    @pl.when(f == 0)
    def _():
        o_ref[...] = jnp.zeros_like(o_ref)

    gm = gm_ref[...]
    for rows in row_chunks:
        a = jnp.maximum(jnp.dot(hm_ref[rows, :], wup_ref[...], preferred_element_type=F32), 0.0)
        o_ref[rows, :] += gm * jnp.dot((a * a).astype(BF16), wdn_ref[...],
                                      preferred_element_type=F32)

    for j in range(MLP_NF):
        @pl.when(f == j)
        def _(j=j):
            o_ref[:, j * MLP_XC:(j + 1) * MLP_XC] += x1c_ref[...]

    @pl.when(f == MLP_NF - 1)
    def _():
        for rows in row_chunks:
            x2 = o_ref[rows, :]
            y = x2 * lax.rsqrt(jnp.mean(x2 * x2, axis=-1, keepdims=True) + EPS)
            o_ref[rows, :] = y * fw_ref[...]


def _mlp(hm, x1, w_up_bf16, w_dn_bf16, g_m, final_w):
    tm, tf = MLP_TM, MLP_TF
    return pl.pallas_call(
        _mlp_kernel,
        grid=(SEQ // tm, MLP_NF),
        in_specs=[
            pl.BlockSpec((tm, D_MODEL), lambda i, f: (i, 0)),
            pl.BlockSpec((tm, MLP_XC), lambda i, f: (i, f)),
            pl.BlockSpec((D_MODEL, tf), lambda i, f: (0, f)),
            pl.BlockSpec((tf, D_MODEL), lambda i, f: (f, 0)),
            pl.BlockSpec((1, D_MODEL), lambda i, f: (0, 0)),
            pl.BlockSpec((1, D_MODEL), lambda i, f: (0, 0)),
        ],
        out_specs=pl.BlockSpec((tm, D_MODEL), lambda i, f: (i, 0)),
        out_shape=jax.ShapeDtypeStruct((SEQ, D_MODEL), F32),
        compiler_params=_params("arbitrary", "arbitrary"),
        name="mlp",
    )(hm, x1, w_up_bf16, w_dn_bf16, g_m, final_w)


def kernel(x, c, ctx, c_ctx, norm_attn_w, norm_mlp_w, w_ada, b_ada, w_in, attn_sink,
           pool_w, pool_scale, w_out, w_mlp_up, w_mlp_down, final_norm_w):
    assert x.shape == (1, SEQ, D_MODEL) and ctx.shape == (1, CTX_LEN, D_MODEL)
    assert w_ada.shape[0] == 1, "single layer"
    x2d, ctx2d = x[0], ctx[0]

    cond = jnp.concatenate([c, c_ctx[None, :], jnp.zeros((SUBLANES - 2, D_MODEL), F32)], axis=0)
    b_row = b_ada[0][None, :]
    n_early = 2 * D_MODEL
    mod_a = _ada(cond, w_ada[0], b_row, n_early)
    chunk = lambda m, r, j: m[r:r + 1, j * D_MODEL:(j + 1) * D_MODEL]
    sh_a, sc_a = chunk(mod_a, 0, 0), chunk(mod_a, 0, 1)
    csh_a, csc_a = chunk(mod_a, 1, 0), chunk(mod_a, 1, 1)

    w_in_b = w_in[0].astype(BF16)
    nw_a = norm_attn_w[0][None, :]
    tables = tuple(jnp.asarray(t) for t in _rope_tables())
    qt, k, vt, u = _inproj(x2d, nw_a, sh_a, sc_a, w_in_b, tables)
    kc, vtc = _ctx_kv(ctx2d, nw_a, csh_a, csc_a, w_in_b)

    sink_row = jnp.repeat(attn_sink[0].astype(F32), BLOCK)[None, :]
    attn, w_out_b, w_up_b, w_dn_b, mod_b = _attn(
        qt, k, vt, kc, vtc, sink_row, w_out[0], w_mlp_up[0], w_mlp_down[0],
        cond, w_ada[0], b_row, n_early)
    g_a, sh_m, sc_m, g_m = (chunk(mod_b, 0, j) for j in range(N_MOD - 2))

    x1, hm = _mixout(x2d, attn, u, w_out_b, pool_w[0].astype(BF16),
                     pool_scale[0][None, :], g_a, norm_mlp_w[0][None, :], sh_m, sc_m)
    out = _mlp(hm, x1, w_up_b, w_dn_b, g_m, final_norm_w[None, :])
    return out[None]
```

```python
import functools

import numpy as np
import jax
import jax.numpy as jnp
from jax import lax
from jax.experimental import pallas as pl
from jax.experimental.pallas import tpu as pltpu

F32 = jnp.float32
BF16 = jnp.bfloat16

D_MODEL = 2048
SEQ = 8192
CTX_LEN = 256
GRID_W = 64
HEAD_DIM = 64
N_Q_HEADS = 16
N_KV_HEADS = 4
GQA = N_Q_HEADS // N_KV_HEADS
ATTN_WIDTH = N_Q_HEADS * HEAD_DIM
KV_WIDTH = N_KV_HEADS * HEAD_DIM
POOL_WINDOWS = (2, 4, 8, 16)
POOL_WIDTH = D_MODEL - ATTN_WIDTH
POOL_GROUP_DIM = POOL_WIDTH // len(POOL_WINDOWS)
IN_WIDTH = ATTN_WIDTH + 2 * KV_WIDTH + POOL_WIDTH
D_FF = 4 * D_MODEL
BLOCK = 128
ROPE_BASE = 10000.0
N_MOD = 6
EPS = 1e-6
NEG_INF = -1e30
LOG2E = 1.4426950408889634
Q_SCALE = HEAD_DIM ** -0.5 * LOG2E

LANES = 128
SUBLANES = 8
POOL_HALO = 8
VMEM_LIMIT = 56 * 1024 * 1024

ADA_TN = 1024
INPROJ_TM = 512
ATTN_TQ = 512
CAST_JOB_ELEMS = 64 * 1024
MIX_TM = 512
MIX_CHUNK = 256
MLP_TM = 1024
MLP_TF = 1024
MLP_RC = 256
MLP_NF = D_FF // MLP_TF
MLP_XR = MLP_TM // MLP_NF


def _params(*sem):
    return pltpu.CompilerParams(dimension_semantics=sem, vmem_limit_bytes=VMEM_LIMIT)


def _rms_modulate(x, norm_w, shift, scale):
    gain = norm_w * (1.0 + scale)
    y = x * lax.rsqrt(jnp.mean(x * x, axis=-1, keepdims=True) + EPS)
    return y * gain + shift


def _ada_block(cond_ref, w_ref, b_ref):
    cnd = cond_ref[...]
    s = cnd * (1.0 / (1.0 + jnp.exp(-cnd)))
    return jnp.dot(s, w_ref[...], preferred_element_type=F32) + b_ref[...]


def _ada_kernel(cond_ref, w_ref, b_ref, o_ref):
    o_ref[...] = _ada_block(cond_ref, w_ref, b_ref)


def _ada(cond, w_ada, b_ada, n):
    rows = cond.shape[0]
    return pl.pallas_call(
        _ada_kernel,
        grid=(n // ADA_TN,),
        in_specs=[
            pl.BlockSpec((rows, D_MODEL), lambda j: (0, 0)),
            pl.BlockSpec((D_MODEL, ADA_TN), lambda j: (0, j)),
            pl.BlockSpec((1, ADA_TN), lambda j: (0, j)),
        ],
        out_specs=pl.BlockSpec((rows, ADA_TN), lambda j: (0, j)),
        out_shape=jax.ShapeDtypeStruct((rows, n), F32),
        compiler_params=_params("arbitrary"),
        name="ada",
    )(cond, w_ada, b_ada)


def _rope_tables():
    half = HEAD_DIM // 2
    inv_freq = ROPE_BASE ** (-np.arange(0, half, 2, dtype=np.float64) / half)
    t = np.arange(SEQ)
    row = (t // GRID_W).astype(np.float64)
    col = (t % GRID_W).astype(np.float64)
    ang_r = row[:, None] * inv_freq[None, :]
    ang_c = col[:, None] * inv_freq[None, :]
    cos_t = Q_SCALE * np.concatenate([np.cos(ang_r), np.cos(ang_c)], axis=1).T
    sin_t = Q_SCALE * np.concatenate([np.sin(ang_r), np.sin(ang_c)], axis=1).T
    cos_l = np.concatenate([np.cos(ang_r), np.cos(ang_r), np.cos(ang_c), np.cos(ang_c)], axis=1)
    zeros = np.zeros_like(ang_r)
    s_up = np.concatenate([-np.sin(ang_r), zeros, -np.sin(ang_c), zeros], axis=1)
    s_dn = np.concatenate([zeros, np.sin(ang_r), zeros, np.sin(ang_c)], axis=1)
    tile2 = lambda a: np.concatenate([a, a], axis=1).astype(np.float32)
    return (cos_t.astype(np.float32), sin_t.astype(np.float32),
            tile2(cos_l), tile2(s_up), tile2(s_dn))


def _inproj_kernel(x_ref, nw_ref, sh_ref, sc_ref, w_ref, cos_t_ref, sin_t_ref,
                   cos_l_ref, sup_ref, sdn_ref, qt_ref, k_ref, vt_ref, u_ref):
    h = _rms_modulate(x_ref[...], nw_ref[...], sh_ref[...], sc_ref[...]).astype(BF16)
    p = jnp.dot(h, w_ref[...], preferred_element_type=F32)

    qt = p[:, :ATTN_WIDTH].T
    cr, cc = cos_t_ref[0:16, :], cos_t_ref[16:32, :]
    sr, sc = sin_t_ref[0:16, :], sin_t_ref[16:32, :]
    pieces = []
    for hd in range(N_Q_HEADS):
        b = hd * HEAD_DIM
        x1r, x2r = qt[b:b + 16, :], qt[b + 16:b + 32, :]
        x1c, x2c = qt[b + 32:b + 48, :], qt[b + 48:b + 64, :]
        pieces += [x1r * cr - x2r * sr, x1r * sr + x2r * cr,
                   x1c * cc - x2c * sc, x1c * sc + x2c * cc]
    qt_ref[...] = jnp.concatenate(pieces, axis=0).astype(BF16)

    k = p[:, ATTN_WIDTH:ATTN_WIDTH + KV_WIDTH]
    cos_l, s_up, s_dn = cos_l_ref[...], sup_ref[...], sdn_ref[...]
    kparts = []
    for g in range(KV_WIDTH // LANES):
        kg = k[:, g * LANES:(g + 1) * LANES]
        kparts.append(kg * cos_l + pltpu.roll(kg, LANES - 16, 1) * s_up
                      + pltpu.roll(kg, 16, 1) * s_dn)
    k_ref[...] = jnp.concatenate(kparts, axis=1).astype(BF16)

    vt_ref[...] = p[:, ATTN_WIDTH + KV_WIDTH:ATTN_WIDTH + 2 * KV_WIDTH].T.astype(BF16)
    u_ref[...] = p[:, ATTN_WIDTH + 2 * KV_WIDTH:]


def _inproj(x, norm_w, shift, scale, w_in_bf16, tables):
    cos_t, sin_t, cos_l, s_up, s_dn = tables
    tm = INPROJ_TM
    row = lambda i: (i, 0)
    colb = lambda i: (0, i)
    fixed = lambda i: (0, 0)
    return pl.pallas_call(
        _inproj_kernel,
        grid=(SEQ // tm,),
        in_specs=[
            pl.BlockSpec((tm, D_MODEL), row),
            pl.BlockSpec((1, D_MODEL), fixed),
            pl.BlockSpec((1, D_MODEL), fixed),
            pl.BlockSpec((1, D_MODEL), fixed),
            pl.BlockSpec((D_MODEL, IN_WIDTH), fixed),
            pl.BlockSpec((32, tm), colb),
            pl.BlockSpec((32, tm), colb),
            pl.BlockSpec((tm, LANES), row),
            pl.BlockSpec((tm, LANES), row),
            pl.BlockSpec((tm, LANES), row),
        ],
        out_specs=[
            pl.BlockSpec((ATTN_WIDTH, tm), colb),
            pl.BlockSpec((tm, KV_WIDTH), row),
            pl.BlockSpec((KV_WIDTH, tm), colb),
            pl.BlockSpec((tm, POOL_WIDTH), row),
        ],
        out_shape=[
            jax.ShapeDtypeStruct((ATTN_WIDTH, SEQ), BF16),
            jax.ShapeDtypeStruct((SEQ, KV_WIDTH), BF16),
            jax.ShapeDtypeStruct((KV_WIDTH, SEQ), BF16),
            jax.ShapeDtypeStruct((SEQ, POOL_WIDTH), F32),
        ],
        compiler_params=_params("arbitrary"),
        name="inproj",
    )(x, norm_w, shift, scale, w_in_bf16, cos_t, sin_t, cos_l, s_up, s_dn)


def _ctx_kv_kernel(x_ref, nw_ref, sh_ref, sc_ref, w_ref, k_ref, vt_ref):
    h = _rms_modulate(x_ref[...], nw_ref[...], sh_ref[...], sc_ref[...]).astype(BF16)
    p = jnp.dot(h, w_ref[...], preferred_element_type=F32)
    k_ref[...] = p[:, :KV_WIDTH].astype(BF16)
    vt_ref[...] = p[:, KV_WIDTH:].T.astype(BF16)


def _ctx_kv(ctx, norm_w, shift, scale, w_in_bf16):
    fixed = lambda i: (0, 0)
    kv_block = ATTN_WIDTH // (2 * KV_WIDTH)
    return pl.pallas_call(
        _ctx_kv_kernel,
        grid=(1,),
        in_specs=[
            pl.BlockSpec((CTX_LEN, D_MODEL), fixed),
            pl.BlockSpec((1, D_MODEL), fixed),
            pl.BlockSpec((1, D_MODEL), fixed),
            pl.BlockSpec((1, D_MODEL), fixed),
            pl.BlockSpec((D_MODEL, 2 * KV_WIDTH), lambda i: (0, kv_block)),
        ],
        out_specs=[
            pl.BlockSpec((CTX_LEN, KV_WIDTH), fixed),
            pl.BlockSpec((KV_WIDTH, CTX_LEN), fixed),
        ],
        out_shape=[
            jax.ShapeDtypeStruct((CTX_LEN, KV_WIDTH), BF16),
            jax.ShapeDtypeStruct((KV_WIDTH, CTX_LEN), BF16),
        ],
        compiler_params=_params("arbitrary"),
        name="ctx_kv",
    )(ctx, norm_w, shift, scale, w_in_bf16)


def _attn_kernel(qt_ref, kp_ref, km_ref, kn_ref, vtp_ref, vtm_ref, vtn_ref,
                 kc_ref, vtc_ref, sink_ref, wout_ref, wup_ref, wdn_ref,
                 cond_ref, wada_ref, bada_ref,
                 o_ref, wout_b_ref, wup_b_ref, wdn_b_ref, mod_ref):
    mod_ref[...] = _ada_block(cond_ref, wada_ref, bada_ref)

    i = pl.program_id(0)
    nsub = ATTN_TQ // BLOCK
    nblk = SEQ // BLOCK
    gw = GQA * BLOCK
    pw = 2 * BLOCK
    nkeys = 3 * BLOCK + CTX_LEN

    key_j = lax.broadcasted_iota(jnp.int32, (BLOCK, gw), 0)
    qry_i = lax.broadcasted_iota(jnp.int32, (BLOCK, gw), 1) % BLOCK
    neg = jnp.full((BLOCK, gw), NEG_INF, F32)
    zero = jnp.zeros((BLOCK, gw), F32)
    band_prev = jnp.where(key_j >= qry_i, zero, neg)
    band_next = jnp.where(key_j <= qry_i, zero, neg)

    kc = kc_ref[...]
    vtc = vtc_ref[...]
    ones_rows = jnp.ones((2 * SUBLANES, nkeys), BF16)

    windows = []
    for sb in range(nsub):
        n = i * nsub + sb
        lo, hi = sb * BLOCK, (sb + 1) * BLOCK
        k_prev = kp_ref[...] if sb == 0 else km_ref[lo - BLOCK:lo, :]
        k_next = kn_ref[...] if sb == nsub - 1 else km_ref[hi:hi + BLOCK, :]
        vt_prev = vtp_ref[...] if sb == 0 else vtm_ref[:, lo - BLOCK:lo]
        vt_next = vtn_ref[...] if sb == nsub - 1 else vtm_ref[:, hi:hi + BLOCK]
        windows.append(dict(
            kwin=jnp.concatenate([k_prev, km_ref[lo:hi, :], k_next, kc], axis=0),
            vtwin=jnp.concatenate([vt_prev, vtm_ref[:, lo:hi], vt_next, vtc], axis=1),
            bias_prev=band_prev + jnp.where(n > 0, 0.0, NEG_INF),
            bias_next=band_next + jnp.where(n < nblk - 1, 0.0, NEG_INF)))

    units = [(sb, h, half) for sb in range(nsub) for h in range(N_KV_HEADS)
             for half in range(GQA // 2)]

    def scores(sb, h):
        win = windows[sb]
        lo, hi = sb * BLOCK, (sb + 1) * BLOCK
        qh = jnp.concatenate(
            [qt_ref[(GQA * h + g) * HEAD_DIM:(GQA * h + g + 1) * HEAD_DIM, lo:hi]
             for g in range(GQA)], axis=1)
        blocks = []
        if h > 0:
            blocks.append(jnp.zeros((h * HEAD_DIM, gw), BF16))
        blocks.append(qh)
        if h < N_KV_HEADS - 1:
            blocks.append(jnp.zeros(((N_KV_HEADS - 1 - h) * HEAD_DIM, gw), BF16))
        s = jnp.dot(win["kwin"], jnp.concatenate(blocks, axis=0),
                    preferred_element_type=F32)
        s = jnp.concatenate([s[0:BLOCK] + win["bias_prev"], s[BLOCK:2 * BLOCK],
                             s[2 * BLOCK:3 * BLOCK] + win["bias_next"], s[3 * BLOCK:]], axis=0)
        return s, jnp.max(s, axis=0, keepdims=True)

    def probs(unit, s, smax):
        sb, h, half = unit
        hd0 = GQA * h + 2 * half
        cols = slice(half * pw, (half + 1) * pw)
        sink = sink_ref[:, hd0 * BLOCK:(hd0 + 2) * BLOCK] * LOG2E
        m = jnp.maximum(smax[:, cols], sink)
        return jnp.exp2(s[:, cols] - m).astype(BF16), jnp.exp2(sink - m)

    def finish(unit, e, e_sink):
        sb, h, half = unit
        lo, hi = sb * BLOCK, (sb + 1) * BLOCK
        hd0 = GQA * h + 2 * half
        vt_ones = jnp.concatenate(
            [windows[sb]["vtwin"][h * HEAD_DIM:(h + 1) * HEAD_DIM, :], ones_rows], axis=0)
        o = jnp.dot(vt_ones, e, preferred_element_type=F32)
        denom = o[HEAD_DIM:HEAD_DIM + 1, :] + e_sink
        o = o[:HEAD_DIM, :] * (1.0 / denom)
        blk = jnp.concatenate([o[:, 0:BLOCK], o[:, BLOCK:2 * BLOCK]], axis=0)
        o_ref[lo:hi, hd0 * HEAD_DIM:(hd0 + 2) * HEAD_DIM] = blk.T.astype(BF16)

    cast_jobs = []
    for src, dst in ((wout_ref, wout_b_ref), (wup_ref, wup_b_ref), (wdn_ref, wdn_b_ref)):
        wc = max(LANES, CAST_JOB_ELEMS // src.shape[0])
        cast_jobs += [(src, dst, c0) + (wc,) for c0 in range(0, src.shape[1], wc)]

    def cast_piece(idx, n):
        for src, dst, c0, wc in cast_jobs[idx * len(cast_jobs) // n:(idx + 1) * len(cast_jobs) // n]:
            dst[:, c0:c0 + wc] = src[:, c0:c0 + wc].astype(BF16)

    nu = len(units)
    s_vals, p_vals = {}, {}
    for step in range(nu + 2):
        if step < nu and units[step][2] == 0:
            s_vals[units[step][:2]] = scores(*units[step][:2])
        if 1 <= step <= nu:
            unit = units[step - 1]
            p_vals[step - 1] = probs(unit, *s_vals[unit[:2]])
            if unit[2] == GQA // 2 - 1:
                del s_vals[unit[:2]]
            cast_piece(step - 1, nu)
        if step >= 2:
            finish(units[step - 2], *p_vals.pop(step - 2))


def _attn(qt, k, vt, kc, vtc, sink_row, w_out, w_up, w_dn, cond, w_ada, b_ada, ada_col0):
    tq = ATTN_TQ
    r = tq // BLOCK
    nblk = SEQ // BLOCK
    nsteps = SEQ // tq
    ada_tn = (w_ada.shape[1] - ada_col0) // nsteps
    ada_b0 = ada_col0 // ada_tn
    fixed = lambda i: (0, 0)
    row = lambda i: (i, 0)
    slab = lambda w: pl.BlockSpec((w.shape[0] // nsteps, w.shape[1]), row)
    bf16_like = lambda w: jax.ShapeDtypeStruct(w.shape, BF16)
    prev = lambda i: jnp.maximum(i * r - 1, 0)
    nxt = lambda i: jnp.minimum((i + 1) * r, nblk - 1)
    return pl.pallas_call(
        _attn_kernel,
        grid=(SEQ // tq,),
        in_specs=[
            pl.BlockSpec((ATTN_WIDTH, tq), lambda i: (0, i)),
            pl.BlockSpec((BLOCK, KV_WIDTH), lambda i: (prev(i), 0)),
            pl.BlockSpec((tq, KV_WIDTH), lambda i: (i, 0)),
            pl.BlockSpec((BLOCK, KV_WIDTH), lambda i: (nxt(i), 0)),
            pl.BlockSpec((KV_WIDTH, BLOCK), lambda i: (0, prev(i))),
            pl.BlockSpec((KV_WIDTH, tq), lambda i: (0, i)),
            pl.BlockSpec((KV_WIDTH, BLOCK), lambda i: (0, nxt(i))),
            pl.BlockSpec((CTX_LEN, KV_WIDTH), fixed),
            pl.BlockSpec((KV_WIDTH, CTX_LEN), fixed),
            pl.BlockSpec((1, N_KV_HEADS * GQA * BLOCK), fixed),
            slab(w_out), slab(w_up), slab(w_dn),
            pl.BlockSpec(cond.shape, fixed),
            pl.BlockSpec((D_MODEL, ada_tn), lambda i: (0, ada_b0 + i)),
            pl.BlockSpec((1, ada_tn), lambda i: (0, ada_b0 + i)),
        ],
        out_specs=[pl.BlockSpec((tq, ATTN_WIDTH), row), slab(w_out), slab(w_up), slab(w_dn),
                   pl.BlockSpec((cond.shape[0], ada_tn), lambda i: (0, i))],
        out_shape=[jax.ShapeDtypeStruct((SEQ, ATTN_WIDTH), BF16),
                   bf16_like(w_out), bf16_like(w_up), bf16_like(w_dn),
                   jax.ShapeDtypeStruct((cond.shape[0], w_ada.shape[1] - ada_col0), F32)],
        compiler_params=_params("arbitrary"),
        name="attn",
    )(qt, k, k, k, vt, vt, vt, kc, vtc, sink_row, w_out, w_up, w_dn, cond, w_ada, b_ada)


def _mixout_kernel(x_ref, attn_ref, up_ref, um_ref, un_ref, wout_ref, pw_ref, ps_ref,
                   ga_ref, nw_ref, sh_ref, sc_ref, x1_ref, hm_ref, ubuf):
    i = pl.program_id(0)
    tm = MIX_TM
    nsteps = SEQ // tm
    ubuf[0:POOL_HALO, :] = jnp.where(i > 0, up_ref[...], 0.0)
    ubuf[POOL_HALO:POOL_HALO + tm, :] = um_ref[...]
    ubuf[POOL_HALO + tm:2 * POOL_HALO + tm, :] = jnp.where(i < nsteps - 1, un_ref[...], 0.0)

    rc = MIX_CHUNK
    n = rc + 2 * POOL_HALO
    c = POOL_HALO
    ngroups = len(POOL_WINDOWS)
    ncol = D_MODEL // ngroups

    def pool_group(r0, g):
        w = POOL_WINDOWS[g]
        c0, c1 = g * POOL_GROUP_DIM, (g + 1) * POOL_GROUP_DIM
        e = ubuf[r0:r0 + n, c0:c1]
        if w == 2:
            win = e[c - 1:c - 1 + rc] + e[c:c + rc]
        else:
            f2 = e[0:n - 1] + e[1:n]
            if w == 4:
                win = f2[c - 2:c - 2 + rc] + f2[c:c + rc]
            else:
                f4 = f2[0:n - 3] + f2[2:n - 1]
                if w == 8:
                    win = f4[c - 4:c - 4 + rc] + f4[c:c + rc]
                else:
                    f8 = f4[0:n - 7] + f4[4:n - 3]
                    win = f8[c - 8:c - 8 + rc] + f8[c:c + rc]
        t = i * tm + r0 + lax.broadcasted_iota(jnp.int32, (rc, 1), 0)
        lo = jnp.clip(t - w // 2, 0, SEQ)
        hi = jnp.clip(t - w // 2 + w, 0, SEQ)
        cnt = (hi - lo).astype(F32)
        pg = win / cnt - um_ref[r0:r0 + rc, c0:c1]
        mixed = jnp.dot(pg.astype(BF16), pw_ref[g], preferred_element_type=F32)
        return (mixed * ps_ref[:, c0:c1]).astype(BF16)

    def prenorm(r0, nrows):
        rows = slice(r0, r0 + nrows)
        hm_ref[rows, :] = _rms_modulate(x1_ref[rows, :], nw_ref[...], sh_ref[...],
                                        sc_ref[...]).astype(BF16)

    chunk_starts = list(range(0, tm, rc))
    for ci, r0 in enumerate(chunk_starts):
        rows = slice(r0, r0 + rc)
        attn_rows = attn_ref[rows, :]
        pooled = []
        for j in range(ngroups):
            cols = slice(j * ncol, (j + 1) * ncol)
            o = jnp.dot(attn_rows, wout_ref[0:ATTN_WIDTH, cols], preferred_element_type=F32)
            x1_ref[rows, cols] = x_ref[rows, cols] + ga_ref[:, cols] * o
            pooled.append(pool_group(r0, j))
        pool_rows = jnp.concatenate(pooled, axis=1)
        for j in range(ngroups):
            cols = slice(j * ncol, (j + 1) * ncol)
            o = jnp.dot(pool_rows, wout_ref[ATTN_WIDTH:, cols], preferred_element_type=F32)
            x1_ref[rows, cols] += ga_ref[:, cols] * o
            if ci > 0:
                prenorm(chunk_starts[ci - 1] + j * (rc // ngroups), rc // ngroups)
    for j in range(ngroups):
        prenorm(chunk_starts[-1] + j * (rc // ngroups), rc // ngroups)


def _mixout(x, attn, u, w_out_bf16, pool_w_bf16, pool_scale, g_a, norm_w, shift, scale):
    tm = MIX_TM
    hb = tm // POOL_HALO
    nhalo = SEQ // POOL_HALO
    row = lambda i: (i, 0)
    fixed = lambda i: (0, 0)
    return pl.pallas_call(
        _mixout_kernel,
        grid=(SEQ // tm,),
        in_specs=[
            pl.BlockSpec((tm, D_MODEL), row),
            pl.BlockSpec((tm, ATTN_WIDTH), row),
            pl.BlockSpec((POOL_HALO, POOL_WIDTH), lambda i: (jnp.maximum(i * hb - 1, 0), 0)),
            pl.BlockSpec((tm, POOL_WIDTH), row),
            pl.BlockSpec((POOL_HALO, POOL_WIDTH), lambda i: (jnp.minimum((i + 1) * hb, nhalo - 1), 0)),
            pl.BlockSpec((D_MODEL, D_MODEL), fixed),
            pl.BlockSpec((len(POOL_WINDOWS), POOL_GROUP_DIM, POOL_GROUP_DIM), lambda i: (0, 0, 0)),
            pl.BlockSpec((1, POOL_WIDTH), fixed),
            pl.BlockSpec((1, D_MODEL), fixed),
            pl.BlockSpec((1, D_MODEL), fixed),
            pl.BlockSpec((1, D_MODEL), fixed),
            pl.BlockSpec((1, D_MODEL), fixed),
        ],
        out_specs=[pl.BlockSpec((tm, D_MODEL), row), pl.BlockSpec((tm, D_MODEL), row)],
        out_shape=[jax.ShapeDtypeStruct((SEQ, D_MODEL), F32),
                   jax.ShapeDtypeStruct((SEQ, D_MODEL), BF16)],
        scratch_shapes=[pltpu.VMEM((tm + 2 * POOL_HALO, POOL_WIDTH), F32)],
        compiler_params=_params("arbitrary"),
        name="mixout",
    )(x, attn, u, u, u, w_out_bf16, pool_w_bf16, pool_scale, g_a, norm_w, shift, scale)


def _mlp_kernel(hm_ref, x1r_ref, wup_ref, wdn_ref, gm_ref, fw_ref, o_ref):
    f = pl.program_id(1)
    last = MLP_NF - 1
    row_chunks = [slice(r0, r0 + MLP_RC) for r0 in range(0, MLP_TM, MLP_RC)]
    gm = gm_ref[...]

    def branch_out(rows):
        a = jnp.maximum(jnp.dot(hm_ref[rows, :], wup_ref[...], preferred_element_type=F32), 0.0)
        return gm * jnp.dot((a * a).astype(BF16), wdn_ref[...], preferred_element_type=F32)

    def final_norm(rows):
        x2 = o_ref[rows, :]
        y = x2 * lax.rsqrt(jnp.mean(x2 * x2, axis=-1, keepdims=True) + EPS)
        o_ref[rows, :] = y * fw_ref[...]

    @pl.when(f == 0)
    def _():
        for rows in row_chunks:
            o_ref[rows, :] = branch_out(rows)
        o_ref[0:MLP_XR, :] += x1r_ref[...]

    @pl.when(jnp.logical_and(f > 0, f < last))
    def _():
        x_rows = pl.ds(pl.multiple_of(f * MLP_XR, MLP_XR), MLP_XR)
        o_ref[x_rows, :] += x1r_ref[...]
        for rows in row_chunks:
            o_ref[rows, :] += branch_out(rows)

    @pl.when(f == last)
    def _():
        o_ref[last * MLP_XR:(last + 1) * MLP_XR, :] += x1r_ref[...]
        npiece = 4
        cw, rp = D_MODEL // npiece, MLP_RC // npiece
        for ci, rows in enumerate(row_chunks):
            a = jnp.maximum(jnp.dot(hm_ref[rows, :], wup_ref[...], preferred_element_type=F32), 0.0)
            a = (a * a).astype(BF16)
            for j in range(npiece):
                cols = slice(j * cw, (j + 1) * cw)
                o_ref[rows, cols] += gm[:, cols] * jnp.dot(a, wdn_ref[:, cols],
                                                           preferred_element_type=F32)
                if ci > 0:
                    r0 = row_chunks[ci - 1].start + j * rp
                    final_norm(slice(r0, r0 + rp))
        final_norm(row_chunks[-1])


def _mlp(hm, x1, w_up_bf16, w_dn_bf16, g_m, final_w):
    tm, tf = MLP_TM, MLP_TF
    return pl.pallas_call(
        _mlp_kernel,
        grid=(SEQ // tm, MLP_NF),
        in_specs=[
            pl.BlockSpec((tm, D_MODEL), lambda i, f: (i, 0)),
            pl.BlockSpec((MLP_XR, D_MODEL), lambda i, f: (i * MLP_NF + f, 0)),
            pl.BlockSpec((D_MODEL, tf), lambda i, f: (0, f)),
            pl.BlockSpec((tf, D_MODEL), lambda i, f: (f, 0)),
            pl.BlockSpec((1, D_MODEL), lambda i, f: (0, 0)),
            pl.BlockSpec((1, D_MODEL), lambda i, f: (0, 0)),
        ],
        out_specs=pl.BlockSpec((tm, D_MODEL), lambda i, f: (i, 0)),
        out_shape=jax.ShapeDtypeStruct((SEQ, D_MODEL), F32),
        compiler_params=_params("arbitrary", "arbitrary"),
        name="mlp",
    )(hm, x1, w_up_bf16, w_dn_bf16, g_m, final_w)


def kernel(x, c, ctx, c_ctx, norm_attn_w, norm_mlp_w, w_ada, b_ada, w_in, attn_sink,
           pool_w, pool_scale, w_out, w_mlp_up, w_mlp_down, final_norm_w):
    assert x.shape == (1, SEQ, D_MODEL) and ctx.shape == (1, CTX_LEN, D_MODEL)
    assert w_ada.shape[0] == 1, "single layer"
    x2d, ctx2d = x[0], ctx[0]

    cond = jnp.concatenate([c, c_ctx[None, :], jnp.zeros((SUBLANES - 2, D_MODEL), F32)], axis=0)
    b_row = b_ada[0][None, :]
    n_early = 2 * D_MODEL
    mod_a = _ada(cond, w_ada[0], b_row, n_early)
    chunk = lambda m, r, j: m[r:r + 1, j * D_MODEL:(j + 1) * D_MODEL]
    sh_a, sc_a = chunk(mod_a, 0, 0), chunk(mod_a, 0, 1)
    csh_a, csc_a = chunk(mod_a, 1, 0), chunk(mod_a, 1, 1)

    w_in_b = w_in[0].astype(BF16)
    nw_a = norm_attn_w[0][None, :]
    tables = tuple(jnp.asarray(t) for t in _rope_tables())
    qt, k, vt, u = _inproj(x2d, nw_a, sh_a, sc_a, w_in_b, tables)
    kc, vtc = _ctx_kv(ctx2d, nw_a, csh_a, csc_a, w_in_b)

    sink_row = jnp.repeat(attn_sink[0].astype(F32), BLOCK)[None, :]
    attn, w_out_b, w_up_b, w_dn_b, mod_b = _attn(
        qt, k, vt, kc, vtc, sink_row, w_out[0], w_mlp_up[0], w_mlp_down[0],
        cond, w_ada[0], b_row, n_early)
    g_a, sh_m, sc_m, g_m = (chunk(mod_b, 0, j) for j in range(N_MOD - 2))

    x1, hm = _mixout(x2d, attn, u, w_out_b, pool_w[0].astype(BF16),
                     pool_scale[0][None, :], g_a, norm_mlp_w[0][None, :], sh_m, sc_m)
    out = _mlp(hm, x1, w_up_b, w_dn_b, g_m, final_norm_w[None, :])
    return out[None]
```

```python
import functools

import numpy as np
import jax
import jax.numpy as jnp
from jax import lax
from jax.experimental import pallas as pl
from jax.experimental.pallas import tpu as pltpu

F32 = jnp.float32
BF16 = jnp.bfloat16

D_MODEL = 2048
SEQ = 8192
CTX_LEN = 256
GRID_W = 64
HEAD_DIM = 64
N_Q_HEADS = 16
N_KV_HEADS = 4
GQA = N_Q_HEADS // N_KV_HEADS
ATTN_WIDTH = N_Q_HEADS * HEAD_DIM
KV_WIDTH = N_KV_HEADS * HEAD_DIM
POOL_WINDOWS = (2, 4, 8, 16)
POOL_WIDTH = D_MODEL - ATTN_WIDTH
POOL_GROUP_DIM = POOL_WIDTH // len(POOL_WINDOWS)
IN_WIDTH = ATTN_WIDTH + 2 * KV_WIDTH + POOL_WIDTH
D_FF = 4 * D_MODEL
BLOCK = 128
ROPE_BASE = 10000.0
N_MOD = 6
EPS = 1e-6
NEG_INF = -1e30
LOG2E = 1.4426950408889634
Q_SCALE = HEAD_DIM ** -0.5 * LOG2E

LANES = 128
SUBLANES = 8
POOL_HALO = 8
VMEM_LIMIT = 56 * 1024 * 1024

ADA_TN = 1024
INPROJ_TM = 512
ATTN_TQ = 512
CAST_JOB_ELEMS = 64 * 1024
MIX_TM = 512
MIX_CHUNK = 256
MLP_TM = 1024
MLP_TF = 1024
MLP_RC = 512
MLP_NF = D_FF // MLP_TF
MLP_XR = MLP_TM // MLP_NF


def _params(*sem):
    return pltpu.CompilerParams(dimension_semantics=sem, vmem_limit_bytes=VMEM_LIMIT)


def _rms_modulate(x, norm_w, shift, scale):
    gain = norm_w * (1.0 + scale)
    y = x * lax.rsqrt(jnp.mean(x * x, axis=-1, keepdims=True) + EPS)
    return y * gain + shift


def _ada_block(cond_ref, w_ref, b_ref):
    cnd = cond_ref[...]
    s = cnd * (1.0 / (1.0 + jnp.exp(-cnd)))
    return jnp.dot(s, w_ref[...], preferred_element_type=F32) + b_ref[...]


def _ada_kernel(cond_ref, w_ref, b_ref, o_ref):
    o_ref[...] = _ada_block(cond_ref, w_ref, b_ref)


def _ada(cond, w_ada, b_ada, n):
    rows = cond.shape[0]
    return pl.pallas_call(
        _ada_kernel,
        grid=(n // ADA_TN,),
        in_specs=[
            pl.BlockSpec((rows, D_MODEL), lambda j: (0, 0)),
            pl.BlockSpec((D_MODEL, ADA_TN), lambda j: (0, j)),
            pl.BlockSpec((1, ADA_TN), lambda j: (0, j)),
        ],
        out_specs=pl.BlockSpec((rows, ADA_TN), lambda j: (0, j)),
        out_shape=jax.ShapeDtypeStruct((rows, n), F32),
        compiler_params=_params("arbitrary"),
        name="ada",
    )(cond, w_ada, b_ada)


def _rope_tables():
    half = HEAD_DIM // 2
    inv_freq = ROPE_BASE ** (-np.arange(0, half, 2, dtype=np.float64) / half)
    t = np.arange(SEQ)
    row = (t // GRID_W).astype(np.float64)
    col = (t % GRID_W).astype(np.float64)
    ang_r = row[:, None] * inv_freq[None, :]
    ang_c = col[:, None] * inv_freq[None, :]
    cos_t = Q_SCALE * np.concatenate([np.cos(ang_r), np.cos(ang_c)], axis=1).T
    sin_t = Q_SCALE * np.concatenate([np.sin(ang_r), np.sin(ang_c)], axis=1).T
    cos_l = np.concatenate([np.cos(ang_r), np.cos(ang_r), np.cos(ang_c), np.cos(ang_c)], axis=1)
    zeros = np.zeros_like(ang_r)
    s_up = np.concatenate([-np.sin(ang_r), zeros, -np.sin(ang_c), zeros], axis=1)
    s_dn = np.concatenate([zeros, np.sin(ang_r), zeros, np.sin(ang_c)], axis=1)
    tile2 = lambda a: np.concatenate([a, a], axis=1).astype(np.float32)
    return (cos_t.astype(np.float32), sin_t.astype(np.float32),
            tile2(cos_l), tile2(s_up), tile2(s_dn))


def _inproj_kernel(x_ref, nw_ref, sh_ref, sc_ref, w_ref, cos_t_ref, sin_t_ref,
                   cos_l_ref, sup_ref, sdn_ref, qt_ref, k_ref, vt_ref, u_ref):
    h = _rms_modulate(x_ref[...], nw_ref[...], sh_ref[...], sc_ref[...]).astype(BF16)
    p = jnp.dot(h, w_ref[...], preferred_element_type=F32)

    qt = p[:, :ATTN_WIDTH].T
    cr, cc = cos_t_ref[0:16, :], cos_t_ref[16:32, :]
    sr, sc = sin_t_ref[0:16, :], sin_t_ref[16:32, :]
    pieces = []
    for hd in range(N_Q_HEADS):
        b = hd * HEAD_DIM
        x1r, x2r = qt[b:b + 16, :], qt[b + 16:b + 32, :]
        x1c, x2c = qt[b + 32:b + 48, :], qt[b + 48:b + 64, :]
        pieces += [x1r * cr - x2r * sr, x1r * sr + x2r * cr,
                   x1c * cc - x2c * sc, x1c * sc + x2c * cc]
    qt_ref[...] = jnp.concatenate(pieces, axis=0).astype(BF16)

    k = p[:, ATTN_WIDTH:ATTN_WIDTH + KV_WIDTH]
    cos_l, s_up, s_dn = cos_l_ref[...], sup_ref[...], sdn_ref[...]
    kparts = []
    for g in range(KV_WIDTH // LANES):
        kg = k[:, g * LANES:(g + 1) * LANES]
        kparts.append(kg * cos_l + pltpu.roll(kg, LANES - 16, 1) * s_up
                      + pltpu.roll(kg, 16, 1) * s_dn)
    k_ref[...] = jnp.concatenate(kparts, axis=1).astype(BF16)

    vt_ref[...] = p[:, ATTN_WIDTH + KV_WIDTH:ATTN_WIDTH + 2 * KV_WIDTH].T.astype(BF16)
    u_ref[...] = p[:, ATTN_WIDTH + 2 * KV_WIDTH:]


def _inproj(x, norm_w, shift, scale, w_in_bf16, tables):
    cos_t, sin_t, cos_l, s_up, s_dn = tables
    tm = INPROJ_TM
    row = lambda i: (i, 0)
    colb = lambda i: (0, i)
    fixed = lambda i: (0, 0)
    return pl.pallas_call(
        _inproj_kernel,
        grid=(SEQ // tm,),
        in_specs=[
            pl.BlockSpec((tm, D_MODEL), row),
            pl.BlockSpec((1, D_MODEL), fixed),
            pl.BlockSpec((1, D_MODEL), fixed),
            pl.BlockSpec((1, D_MODEL), fixed),
            pl.BlockSpec((D_MODEL, IN_WIDTH), fixed),
            pl.BlockSpec((32, tm), colb),
            pl.BlockSpec((32, tm), colb),
            pl.BlockSpec((tm, LANES), row),
            pl.BlockSpec((tm, LANES), row),
            pl.BlockSpec((tm, LANES), row),
        ],
        out_specs=[
            pl.BlockSpec((ATTN_WIDTH, tm), colb),
            pl.BlockSpec((tm, KV_WIDTH), row),
            pl.BlockSpec((KV_WIDTH, tm), colb),
            pl.BlockSpec((tm, POOL_WIDTH), row),
        ],
        out_shape=[
            jax.ShapeDtypeStruct((ATTN_WIDTH, SEQ), BF16),
            jax.ShapeDtypeStruct((SEQ, KV_WIDTH), BF16),
            jax.ShapeDtypeStruct((KV_WIDTH, SEQ), BF16),
            jax.ShapeDtypeStruct((SEQ, POOL_WIDTH), F32),
        ],
        compiler_params=_params("arbitrary"),
        name="inproj",
    )(x, norm_w, shift, scale, w_in_bf16, cos_t, sin_t, cos_l, s_up, s_dn)


def _ctx_kv_kernel(x_ref, nw_ref, sh_ref, sc_ref, w_ref, k_ref, vt_ref):
    h = _rms_modulate(x_ref[...], nw_ref[...], sh_ref[...], sc_ref[...]).astype(BF16)
    p = jnp.dot(h, w_ref[...], preferred_element_type=F32)
    k_ref[...] = p[:, :KV_WIDTH].astype(BF16)
    vt_ref[...] = p[:, KV_WIDTH:].T.astype(BF16)


def _ctx_kv(ctx, norm_w, shift, scale, w_in_bf16):
    fixed = lambda i: (0, 0)
    kv_block = ATTN_WIDTH // (2 * KV_WIDTH)
    return pl.pallas_call(
        _ctx_kv_kernel,
        grid=(1,),
        in_specs=[
            pl.BlockSpec((CTX_LEN, D_MODEL), fixed),
            pl.BlockSpec((1, D_MODEL), fixed),
            pl.BlockSpec((1, D_MODEL), fixed),
            pl.BlockSpec((1, D_MODEL), fixed),
            pl.BlockSpec((D_MODEL, 2 * KV_WIDTH), lambda i: (0, kv_block)),
        ],
        out_specs=[
            pl.BlockSpec((CTX_LEN, KV_WIDTH), fixed),
            pl.BlockSpec((KV_WIDTH, CTX_LEN), fixed),
        ],
        out_shape=[
            jax.ShapeDtypeStruct((CTX_LEN, KV_WIDTH), BF16),
            jax.ShapeDtypeStruct((KV_WIDTH, CTX_LEN), BF16),
        ],
        compiler_params=_params("arbitrary"),
        name="ctx_kv",
    )(ctx, norm_w, shift, scale, w_in_bf16)


def _attn_kernel(qt_ref, kp_ref, km_ref, kn_ref, vtp_ref, vtm_ref, vtn_ref,
                 kc_ref, vtc_ref, sink_ref, wout_ref, wup_ref, wdn_ref,
                 cond_ref, wada_ref, bada_ref,
                 o_ref, wout_b_ref, wup_b_ref, wdn_b_ref, mod_ref):
    mod_ref[...] = _ada_block(cond_ref, wada_ref, bada_ref)

    i = pl.program_id(0)
    nsub = ATTN_TQ // BLOCK
    nblk = SEQ // BLOCK
    gw = GQA * BLOCK
    pw = 2 * BLOCK
    nkeys = 3 * BLOCK + CTX_LEN

    key_j = lax.broadcasted_iota(jnp.int32, (BLOCK, gw), 0)
    qry_i = lax.broadcasted_iota(jnp.int32, (BLOCK, gw), 1) % BLOCK
    neg = jnp.full((BLOCK, gw), NEG_INF, F32)
    zero = jnp.zeros((BLOCK, gw), F32)
    band_prev = jnp.where(key_j >= qry_i, zero, neg)
    band_next = jnp.where(key_j <= qry_i, zero, neg)

    kc = kc_ref[...]
    vtc = vtc_ref[...]
    ones_rows = jnp.ones((2 * SUBLANES, nkeys), BF16)

    windows = []
    for sb in range(nsub):
        n = i * nsub + sb
        lo, hi = sb * BLOCK, (sb + 1) * BLOCK
        k_prev = kp_ref[...] if sb == 0 else km_ref[lo - BLOCK:lo, :]
        k_next = kn_ref[...] if sb == nsub - 1 else km_ref[hi:hi + BLOCK, :]
        vt_prev = vtp_ref[...] if sb == 0 else vtm_ref[:, lo - BLOCK:lo]
        vt_next = vtn_ref[...] if sb == nsub - 1 else vtm_ref[:, hi:hi + BLOCK]
        windows.append(dict(
            kwin=jnp.concatenate([k_prev, km_ref[lo:hi, :], k_next, kc], axis=0),
            vtwin=jnp.concatenate([vt_prev, vtm_ref[:, lo:hi], vt_next, vtc], axis=1),
            bias_prev=band_prev + jnp.where(n > 0, 0.0, NEG_INF),
            bias_next=band_next + jnp.where(n < nblk - 1, 0.0, NEG_INF)))

    units = [(sb, h, half) for sb in range(nsub) for h in range(N_KV_HEADS)
             for half in range(GQA // 2)]

    def scores(sb, h):
        win = windows[sb]
        lo, hi = sb * BLOCK, (sb + 1) * BLOCK
        qh = jnp.concatenate(
            [qt_ref[(GQA * h + g) * HEAD_DIM:(GQA * h + g + 1) * HEAD_DIM, lo:hi]
             for g in range(GQA)], axis=1)
        blocks = []
        if h > 0:
            blocks.append(jnp.zeros((h * HEAD_DIM, gw), BF16))
        blocks.append(qh)
        if h < N_KV_HEADS - 1:
            blocks.append(jnp.zeros(((N_KV_HEADS - 1 - h) * HEAD_DIM, gw), BF16))
        s = jnp.dot(win["kwin"], jnp.concatenate(blocks, axis=0),
                    preferred_element_type=F32)
        s = jnp.concatenate([s[0:BLOCK] + win["bias_prev"], s[BLOCK:2 * BLOCK],
                             s[2 * BLOCK:3 * BLOCK] + win["bias_next"], s[3 * BLOCK:]], axis=0)
        return s, jnp.max(s, axis=0, keepdims=True)

    def probs(unit, s, smax):
        sb, h, half = unit
        hd0 = GQA * h + 2 * half
        cols = slice(half * pw, (half + 1) * pw)
        sink = sink_ref[:, hd0 * BLOCK:(hd0 + 2) * BLOCK] * LOG2E
        m = jnp.maximum(smax[:, cols], sink)
        return jnp.exp2(s[:, cols] - m).astype(BF16), jnp.exp2(sink - m)

    def finish(unit, e, e_sink):
        sb, h, half = unit
        lo, hi = sb * BLOCK, (sb + 1) * BLOCK
        hd0 = GQA * h + 2 * half
        vt_ones = jnp.concatenate(
            [windows[sb]["vtwin"][h * HEAD_DIM:(h + 1) * HEAD_DIM, :], ones_rows], axis=0)
        o = jnp.dot(vt_ones, e, preferred_element_type=F32)
        denom = o[HEAD_DIM:HEAD_DIM + 1, :] + e_sink
        o = o[:HEAD_DIM, :] * (1.0 / denom)
        blk = jnp.concatenate([o[:, 0:BLOCK], o[:, BLOCK:2 * BLOCK]], axis=0)
        o_ref[lo:hi, hd0 * HEAD_DIM:(hd0 + 2) * HEAD_DIM] = blk.T.astype(BF16)

    cast_jobs = []
    for src, dst in ((wout_ref, wout_b_ref), (wup_ref, wup_b_ref), (wdn_ref, wdn_b_ref)):
        wc = max(LANES, CAST_JOB_ELEMS // src.shape[0])
        cast_jobs += [(src, dst, c0) + (wc,) for c0 in range(0, src.shape[1], wc)]

    def cast_piece(idx, n):
        for src, dst, c0, wc in cast_jobs[idx * len(cast_jobs) // n:(idx + 1) * len(cast_jobs) // n]:
            dst[:, c0:c0 + wc] = src[:, c0:c0 + wc].astype(BF16)

    nu = len(units)
    s_vals, p_vals = {}, {}
    for step in range(nu + 2):
        if step < nu and units[step][2] == 0:
            s_vals[units[step][:2]] = scores(*units[step][:2])
        if 1 <= step <= nu:
            unit = units[step - 1]
            p_vals[step - 1] = probs(unit, *s_vals[unit[:2]])
            if unit[2] == GQA // 2 - 1:
                del s_vals[unit[:2]]
            cast_piece(step - 1, nu)
        if step >= 2:
            finish(units[step - 2], *p_vals.pop(step - 2))


def _attn(qt, k, vt, kc, vtc, sink_row, w_out, w_up, w_dn, cond, w_ada, b_ada, ada_col0):
    tq = ATTN_TQ
    r = tq // BLOCK
    nblk = SEQ // BLOCK
    nsteps = SEQ // tq
    ada_tn = (w_ada.shape[1] - ada_col0) // nsteps
    ada_b0 = ada_col0 // ada_tn
    fixed = lambda i: (0, 0)
    row = lambda i: (i, 0)
    slab = lambda w: pl.BlockSpec((w.shape[0] // nsteps, w.shape[1]), row)
    bf16_like = lambda w: jax.ShapeDtypeStruct(w.shape, BF16)
    prev = lambda i: jnp.maximum(i * r - 1, 0)
    nxt = lambda i: jnp.minimum((i + 1) * r, nblk - 1)
    return pl.pallas_call(
        _attn_kernel,
        grid=(SEQ // tq,),
        in_specs=[
            pl.BlockSpec((ATTN_WIDTH, tq), lambda i: (0, i)),
            pl.BlockSpec((BLOCK, KV_WIDTH), lambda i: (prev(i), 0)),
            pl.BlockSpec((tq, KV_WIDTH), lambda i: (i, 0)),
            pl.BlockSpec((BLOCK, KV_WIDTH), lambda i: (nxt(i), 0)),
            pl.BlockSpec((KV_WIDTH, BLOCK), lambda i: (0, prev(i))),
            pl.BlockSpec((KV_WIDTH, tq), lambda i: (0, i)),
            pl.BlockSpec((KV_WIDTH, BLOCK), lambda i: (0, nxt(i))),
            pl.BlockSpec((CTX_LEN, KV_WIDTH), fixed),
            pl.BlockSpec((KV_WIDTH, CTX_LEN), fixed),
            pl.BlockSpec((1, N_KV_HEADS * GQA * BLOCK), fixed),
            slab(w_out), slab(w_up), slab(w_dn),
            pl.BlockSpec(cond.shape, fixed),
            pl.BlockSpec((D_MODEL, ada_tn), lambda i: (0, ada_b0 + i)),
            pl.BlockSpec((1, ada_tn), lambda i: (0, ada_b0 + i)),
        ],
        out_specs=[pl.BlockSpec((tq, ATTN_WIDTH), row), slab(w_out), slab(w_up), slab(w_dn),
                   pl.BlockSpec((cond.shape[0], ada_tn), lambda i: (0, i))],
        out_shape=[jax.ShapeDtypeStruct((SEQ, ATTN_WIDTH), BF16),
                   bf16_like(w_out), bf16_like(w_up), bf16_like(w_dn),
                   jax.ShapeDtypeStruct((cond.shape[0], w_ada.shape[1] - ada_col0), F32)],
        compiler_params=_params("arbitrary"),
        name="attn",
    )(qt, k, k, k, vt, vt, vt, kc, vtc, sink_row, w_out, w_up, w_dn, cond, w_ada, b_ada)


def _mixout_kernel(x_ref, attn_ref, up_ref, um_ref, un_ref, wout_ref, pw_ref, ps_ref,
                   ga_ref, nw_ref, sh_ref, sc_ref, x1_ref, hm_ref, ubuf):
    i = pl.program_id(0)
    tm = MIX_TM
    nsteps = SEQ // tm
    ubuf[0:POOL_HALO, :] = jnp.where(i > 0, up_ref[...], 0.0)
    ubuf[POOL_HALO:POOL_HALO + tm, :] = um_ref[...]
    ubuf[POOL_HALO + tm:2 * POOL_HALO + tm, :] = jnp.where(i < nsteps - 1, un_ref[...], 0.0)

    rc = MIX_CHUNK
    n = rc + 2 * POOL_HALO
    c = POOL_HALO
    ngroups = len(POOL_WINDOWS)
    ncol = D_MODEL // ngroups

    def pool_group(r0, g):
        w = POOL_WINDOWS[g]
        c0, c1 = g * POOL_GROUP_DIM, (g + 1) * POOL_GROUP_DIM
        e = ubuf[r0:r0 + n, c0:c1]
        if w == 2:
            win = e[c - 1:c - 1 + rc] + e[c:c + rc]
        else:
            f2 = e[0:n - 1] + e[1:n]
            if w == 4:
                win = f2[c - 2:c - 2 + rc] + f2[c:c + rc]
            else:
                f4 = f2[0:n - 3] + f2[2:n - 1]
                if w == 8:
                    win = f4[c - 4:c - 4 + rc] + f4[c:c + rc]
                else:
                    f8 = f4[0:n - 7] + f4[4:n - 3]
                    win = f8[c - 8:c - 8 + rc] + f8[c:c + rc]
        t = i * tm + r0 + lax.broadcasted_iota(jnp.int32, (rc, 1), 0)
        lo = jnp.clip(t - w // 2, 0, SEQ)
        hi = jnp.clip(t - w // 2 + w, 0, SEQ)
        cnt = (hi - lo).astype(F32)
        pg = win / cnt - um_ref[r0:r0 + rc, c0:c1]
        mixed = jnp.dot(pg.astype(BF16), pw_ref[g], preferred_element_type=F32)
        return (mixed * ps_ref[:, c0:c1]).astype(BF16)

    def prenorm(r0, nrows):
        rows = slice(r0, r0 + nrows)
        hm_ref[rows, :] = _rms_modulate(x1_ref[rows, :], nw_ref[...], sh_ref[...],
                                        sc_ref[...]).astype(BF16)

    chunk_starts = list(range(0, tm, rc))
    for ci, r0 in enumerate(chunk_starts):
        rows = slice(r0, r0 + rc)
        attn_rows = attn_ref[rows, :]
        pooled = []
        for j in range(ngroups):
            cols = slice(j * ncol, (j + 1) * ncol)
            o = jnp.dot(attn_rows, wout_ref[0:ATTN_WIDTH, cols], preferred_element_type=F32)
            x1_ref[rows, cols] = x_ref[rows, cols] + ga_ref[:, cols] * o
            pooled.append(pool_group(r0, j))
        pool_rows = jnp.concatenate(pooled, axis=1)
        for j in range(ngroups):
            cols = slice(j * ncol, (j + 1) * ncol)
            o = jnp.dot(pool_rows, wout_ref[ATTN_WIDTH:, cols], preferred_element_type=F32)
            x1_ref[rows, cols] += ga_ref[:, cols] * o
            if ci > 0:
                prenorm(chunk_starts[ci - 1] + j * (rc // ngroups), rc // ngroups)
    for j in range(ngroups):
        prenorm(chunk_starts[-1] + j * (rc // ngroups), rc // ngroups)


def _mixout(x, attn, u, w_out_bf16, pool_w_bf16, pool_scale, g_a, norm_w, shift, scale):
    tm = MIX_TM
    hb = tm // POOL_HALO
    nhalo = SEQ // POOL_HALO
    row = lambda i: (i, 0)
    fixed = lambda i: (0, 0)
    return pl.pallas_call(
        _mixout_kernel,
        grid=(SEQ // tm,),
        in_specs=[
            pl.BlockSpec((tm, D_MODEL), row),
            pl.BlockSpec((tm, ATTN_WIDTH), row),
            pl.BlockSpec((POOL_HALO, POOL_WIDTH), lambda i: (jnp.maximum(i * hb - 1, 0), 0)),
            pl.BlockSpec((tm, POOL_WIDTH), row),
            pl.BlockSpec((POOL_HALO, POOL_WIDTH), lambda i: (jnp.minimum((i + 1) * hb, nhalo - 1), 0)),
            pl.BlockSpec((D_MODEL, D_MODEL), fixed),
            pl.BlockSpec((len(POOL_WINDOWS), POOL_GROUP_DIM, POOL_GROUP_DIM), lambda i: (0, 0, 0)),
            pl.BlockSpec((1, POOL_WIDTH), fixed),
            pl.BlockSpec((1, D_MODEL), fixed),
            pl.BlockSpec((1, D_MODEL), fixed),
            pl.BlockSpec((1, D_MODEL), fixed),
            pl.BlockSpec((1, D_MODEL), fixed),
        ],
        out_specs=[pl.BlockSpec((tm, D_MODEL), row), pl.BlockSpec((tm, D_MODEL), row)],
        out_shape=[jax.ShapeDtypeStruct((SEQ, D_MODEL), F32),
                   jax.ShapeDtypeStruct((SEQ, D_MODEL), BF16)],
        scratch_shapes=[pltpu.VMEM((tm + 2 * POOL_HALO, POOL_WIDTH), F32)],
        compiler_params=_params("arbitrary"),
        name="mixout",
    )(x, attn, u, u, u, w_out_bf16, pool_w_bf16, pool_scale, g_a, norm_w, shift, scale)


def _mlp_kernel(hm_ref, x1r_ref, wup_ref, wdn_ref, gm_ref, fw_ref, o_ref):
    f = pl.program_id(1)
    last = MLP_NF - 1
    row_chunks = [slice(r0, r0 + MLP_RC) for r0 in range(0, MLP_TM, MLP_RC)]
    gm = gm_ref[...]

    def branch_out(rows):
        a = jnp.maximum(jnp.dot(hm_ref[rows, :], wup_ref[...], preferred_element_type=F32), 0.0)
        return gm * jnp.dot((a * a).astype(BF16), wdn_ref[...], preferred_element_type=F32)

    def final_norm(rows):
        x2 = o_ref[rows, :]
        y = x2 * lax.rsqrt(jnp.mean(x2 * x2, axis=-1, keepdims=True) + EPS)
        o_ref[rows, :] = y * fw_ref[...]

    @pl.when(f == 0)
    def _():
        for rows in row_chunks:
            o_ref[rows, :] = branch_out(rows)
        o_ref[0:MLP_XR, :] += x1r_ref[...]

    @pl.when(jnp.logical_and(f > 0, f < last))
    def _():
        x_rows = pl.ds(pl.multiple_of(f * MLP_XR, MLP_XR), MLP_XR)
        o_ref[x_rows, :] += x1r_ref[...]
        for rows in row_chunks:
            o_ref[rows, :] += branch_out(rows)

    @pl.when(f == last)
    def _():
        o_ref[last * MLP_XR:(last + 1) * MLP_XR, :] += x1r_ref[...]
        npiece = 4
        cw, rp = D_MODEL // npiece, MLP_RC // npiece
        for ci, rows in enumerate(row_chunks):
            a = jnp.maximum(jnp.dot(hm_ref[rows, :], wup_ref[...], preferred_element_type=F32), 0.0)
            a = (a * a).astype(BF16)
            for j in range(npiece):
                cols = slice(j * cw, (j + 1) * cw)
                o_ref[rows, cols] += gm[:, cols] * jnp.dot(a, wdn_ref[:, cols],
                                                           preferred_element_type=F32)
                if ci > 0:
                    r0 = row_chunks[ci - 1].start + j * rp
                    final_norm(slice(r0, r0 + rp))
        final_norm(row_chunks[-1])


def _mlp(hm, x1, w_up_bf16, w_dn_bf16, g_m, final_w):
    tm, tf = MLP_TM, MLP_TF
    return pl.pallas_call(
        _mlp_kernel,
        grid=(SEQ // tm, MLP_NF),
        in_specs=[
            pl.BlockSpec((tm, D_MODEL), lambda i, f: (i, 0)),
            pl.BlockSpec((MLP_XR, D_MODEL), lambda i, f: (i * MLP_NF + f, 0)),
            pl.BlockSpec((D_MODEL, tf), lambda i, f: (0, f)),
            pl.BlockSpec((tf, D_MODEL), lambda i, f: (f, 0)),
            pl.BlockSpec((1, D_MODEL), lambda i, f: (0, 0)),
            pl.BlockSpec((1, D_MODEL), lambda i, f: (0, 0)),
        ],
        out_specs=pl.BlockSpec((tm, D_MODEL), lambda i, f: (i, 0)),
        out_shape=jax.ShapeDtypeStruct((SEQ, D_MODEL), F32),
        compiler_params=_params("arbitrary", "arbitrary"),
        name="mlp",
    )(hm, x1, w_up_bf16, w_dn_bf16, g_m, final_w)


def kernel(x, c, ctx, c_ctx, norm_attn_w, norm_mlp_w, w_ada, b_ada, w_in, attn_sink,
           pool_w, pool_scale, w_out, w_mlp_up, w_mlp_down, final_norm_w):
    assert x.shape == (1, SEQ, D_MODEL) and ctx.shape == (1, CTX_LEN, D_MODEL)
    assert w_ada.shape[0] == 1, "single layer"
    x2d, ctx2d = x[0], ctx[0]

    cond = jnp.concatenate([c, c_ctx[None, :], jnp.zeros((SUBLANES - 2, D_MODEL), F32)], axis=0)
    b_row = b_ada[0][None, :]
    n_early = 2 * D_MODEL
    mod_a = _ada(cond, w_ada[0], b_row, n_early)
    chunk = lambda m, r, j: m[r:r + 1, j * D_MODEL:(j + 1) * D_MODEL]
    sh_a, sc_a = chunk(mod_a, 0, 0), chunk(mod_a, 0, 1)
    csh_a, csc_a = chunk(mod_a, 1, 0), chunk(mod_a, 1, 1)

    w_in_b = w_in[0].astype(BF16)
    nw_a = norm_attn_w[0][None, :]
    tables = tuple(jnp.asarray(t) for t in _rope_tables())
    qt, k, vt, u = _inproj(x2d, nw_a, sh_a, sc_a, w_in_b, tables)
    kc, vtc = _ctx_kv(ctx2d, nw_a, csh_a, csc_a, w_in_b)

    sink_row = jnp.repeat(attn_sink[0].astype(F32), BLOCK)[None, :]
    attn, w_out_b, w_up_b, w_dn_b, mod_b = _attn(
        qt, k, vt, kc, vtc, sink_row, w_out[0], w_mlp_up[0], w_mlp_down[0],
        cond, w_ada[0], b_row, n_early)
    g_a, sh_m, sc_m, g_m = (chunk(mod_b, 0, j) for j in range(N_MOD - 2))

    x1, hm = _mixout(x2d, attn, u, w_out_b, pool_w[0].astype(BF16),
                     pool_scale[0][None, :], g_a, norm_mlp_w[0][None, :], sh_m, sc_m)
    out = _mlp(hm, x1, w_up_b, w_dn_b, g_m, final_norm_w[None, :])
    return out[None]
```

```python
import functools

import numpy as np
import jax
import jax.numpy as jnp
from jax import lax
from jax.experimental import pallas as pl
from jax.experimental.pallas import tpu as pltpu

F32 = jnp.float32
BF16 = jnp.bfloat16

D_MODEL = 2048
SEQ = 8192
CTX_LEN = 256
GRID_W = 64
HEAD_DIM = 64
N_Q_HEADS = 16
N_KV_HEADS = 4
GQA = N_Q_HEADS // N_KV_HEADS
ATTN_WIDTH = N_Q_HEADS * HEAD_DIM
KV_WIDTH = N_KV_HEADS * HEAD_DIM
POOL_WINDOWS = (2, 4, 8, 16)
POOL_WIDTH = D_MODEL - ATTN_WIDTH
POOL_GROUP_DIM = POOL_WIDTH // len(POOL_WINDOWS)
IN_WIDTH = ATTN_WIDTH + 2 * KV_WIDTH + POOL_WIDTH
D_FF = 4 * D_MODEL
BLOCK = 128
ROPE_BASE = 10000.0
N_MOD = 6
EPS = 1e-6
NEG_INF = -1e30
LOG2E = 1.4426950408889634
Q_SCALE = HEAD_DIM ** -0.5 * LOG2E

LANES = 128
SUBLANES = 8
POOL_HALO = 8
VMEM_LIMIT = 56 * 1024 * 1024

ADA_TN = 1024
INPROJ_TM = 512
ATTN_TQ = 512
CAST_JOB_ELEMS = 64 * 1024
MIX_TM = 512
MIX_CHUNK = 256
MLP_TM = 1024
MLP_TF = 1024
MLP_RC = 1024
MLP_NF = D_FF // MLP_TF
MLP_XR = MLP_TM // MLP_NF


def _params(*sem):
    return pltpu.CompilerParams(dimension_semantics=sem, vmem_limit_bytes=VMEM_LIMIT)


def _rms_modulate(x, norm_w, shift, scale):
    gain = norm_w * (1.0 + scale)
    y = x * lax.rsqrt(jnp.mean(x * x, axis=-1, keepdims=True) + EPS)
    return y * gain + shift


def _ada_block(cond_ref, w_ref, b_ref):
    cnd = cond_ref[...]
    s = cnd * (1.0 / (1.0 + jnp.exp(-cnd)))
    return jnp.dot(s, w_ref[...], preferred_element_type=F32) + b_ref[...]


def _ada_kernel(cond_ref, w_ref, b_ref, o_ref):
    o_ref[...] = _ada_block(cond_ref, w_ref, b_ref)


def _ada(cond, w_ada, b_ada, n):
    rows = cond.shape[0]
    return pl.pallas_call(
        _ada_kernel,
        grid=(n // ADA_TN,),
        in_specs=[
            pl.BlockSpec((rows, D_MODEL), lambda j: (0, 0)),
            pl.BlockSpec((D_MODEL, ADA_TN), lambda j: (0, j)),
            pl.BlockSpec((1, ADA_TN), lambda j: (0, j)),
        ],
        out_specs=pl.BlockSpec((rows, ADA_TN), lambda j: (0, j)),
        out_shape=jax.ShapeDtypeStruct((rows, n), F32),
        compiler_params=_params("arbitrary"),
        name="ada",
    )(cond, w_ada, b_ada)


def _rope_tables():
    half = HEAD_DIM // 2
    inv_freq = ROPE_BASE ** (-np.arange(0, half, 2, dtype=np.float64) / half)
    t = np.arange(SEQ)
    row = (t // GRID_W).astype(np.float64)
    col = (t % GRID_W).astype(np.float64)
    ang_r = row[:, None] * inv_freq[None, :]
    ang_c = col[:, None] * inv_freq[None, :]
    cos_t = Q_SCALE * np.concatenate([np.cos(ang_r), np.cos(ang_c)], axis=1).T
    sin_t = Q_SCALE * np.concatenate([np.sin(ang_r), np.sin(ang_c)], axis=1).T
    cos_l = np.concatenate([np.cos(ang_r), np.cos(ang_r), np.cos(ang_c), np.cos(ang_c)], axis=1)
    zeros = np.zeros_like(ang_r)
    s_up = np.concatenate([-np.sin(ang_r), zeros, -np.sin(ang_c), zeros], axis=1)
    s_dn = np.concatenate([zeros, np.sin(ang_r), zeros, np.sin(ang_c)], axis=1)
    tile2 = lambda a: np.concatenate([a, a], axis=1).astype(np.float32)
    return (cos_t.astype(np.float32), sin_t.astype(np.float32),
            tile2(cos_l), tile2(s_up), tile2(s_dn))


def _inproj_kernel(x_ref, nw_ref, sh_ref, sc_ref, w_ref, cos_t_ref, sin_t_ref,
                   cos_l_ref, sup_ref, sdn_ref, qt_ref, k_ref, vt_ref, u_ref, wb_ref):
    @pl.when(pl.program_id(0) == 0)
    def _():
        for c0 in range(0, IN_WIDTH, 2 * LANES):
            wb_ref[:, c0:c0 + 2 * LANES] = w_ref[:, c0:c0 + 2 * LANES].astype(BF16)

    h = _rms_modulate(x_ref[...], nw_ref[...], sh_ref[...], sc_ref[...]).astype(BF16)
    p = jnp.dot(h, wb_ref[...], preferred_element_type=F32)

    qt = p[:, :ATTN_WIDTH].T
    cr, cc = cos_t_ref[0:16, :], cos_t_ref[16:32, :]
    sr, sc = sin_t_ref[0:16, :], sin_t_ref[16:32, :]
    pieces = []
    for hd in range(N_Q_HEADS):
        b = hd * HEAD_DIM
        x1r, x2r = qt[b:b + 16, :], qt[b + 16:b + 32, :]
        x1c, x2c = qt[b + 32:b + 48, :], qt[b + 48:b + 64, :]
        pieces += [x1r * cr - x2r * sr, x1r * sr + x2r * cr,
                   x1c * cc - x2c * sc, x1c * sc + x2c * cc]
    qt_ref[...] = jnp.concatenate(pieces, axis=0).astype(BF16)

    k = p[:, ATTN_WIDTH:ATTN_WIDTH + KV_WIDTH]
    cos_l, s_up, s_dn = cos_l_ref[...], sup_ref[...], sdn_ref[...]
    kparts = []
    for g in range(KV_WIDTH // LANES):
        kg = k[:, g * LANES:(g + 1) * LANES]
        kparts.append(kg * cos_l + pltpu.roll(kg, LANES - 16, 1) * s_up
                      + pltpu.roll(kg, 16, 1) * s_dn)
    k_ref[...] = jnp.concatenate(kparts, axis=1).astype(BF16)

    vt_ref[...] = p[:, ATTN_WIDTH + KV_WIDTH:ATTN_WIDTH + 2 * KV_WIDTH].T.astype(BF16)
    u_ref[...] = p[:, ATTN_WIDTH + 2 * KV_WIDTH:]


def _inproj(x, norm_w, shift, scale, w_in, tables):
    cos_t, sin_t, cos_l, s_up, s_dn = tables
    tm = INPROJ_TM
    row = lambda i: (i, 0)
    colb = lambda i: (0, i)
    fixed = lambda i: (0, 0)
    return pl.pallas_call(
        _inproj_kernel,
        grid=(SEQ // tm,),
        in_specs=[
            pl.BlockSpec((tm, D_MODEL), row),
            pl.BlockSpec((1, D_MODEL), fixed),
            pl.BlockSpec((1, D_MODEL), fixed),
            pl.BlockSpec((1, D_MODEL), fixed),
            pl.BlockSpec((D_MODEL, IN_WIDTH), fixed, pipeline_mode=pl.Buffered(1)),
            pl.BlockSpec((32, tm), colb),
            pl.BlockSpec((32, tm), colb),
            pl.BlockSpec((tm, LANES), row),
            pl.BlockSpec((tm, LANES), row),
            pl.BlockSpec((tm, LANES), row),
        ],
        out_specs=[
            pl.BlockSpec((ATTN_WIDTH, tm), colb),
            pl.BlockSpec((tm, KV_WIDTH), row),
            pl.BlockSpec((KV_WIDTH, tm), colb),
            pl.BlockSpec((tm, POOL_WIDTH), row),
        ],
        out_shape=[
            jax.ShapeDtypeStruct((ATTN_WIDTH, SEQ), BF16),
            jax.ShapeDtypeStruct((SEQ, KV_WIDTH), BF16),
            jax.ShapeDtypeStruct((KV_WIDTH, SEQ), BF16),
            jax.ShapeDtypeStruct((SEQ, POOL_WIDTH), F32),
        ],
        scratch_shapes=[pltpu.VMEM((D_MODEL, IN_WIDTH), BF16)],
        compiler_params=_params("arbitrary"),
        name="inproj",
    )(x, norm_w, shift, scale, w_in, cos_t, sin_t, cos_l, s_up, s_dn)


def _ctx_kv_kernel(x_ref, nw_ref, sh_ref, sc_ref, w_ref, k_ref, vt_ref):
    h = _rms_modulate(x_ref[...], nw_ref[...], sh_ref[...], sc_ref[...]).astype(BF16)
    p = jnp.dot(h, w_ref[...].astype(BF16), preferred_element_type=F32)
    k_ref[...] = p[:, :KV_WIDTH].astype(BF16)
    vt_ref[...] = p[:, KV_WIDTH:].T.astype(BF16)


def _ctx_kv(ctx, norm_w, shift, scale, w_in):
    fixed = lambda i: (0, 0)
    kv_block = ATTN_WIDTH // (2 * KV_WIDTH)
    return pl.pallas_call(
        _ctx_kv_kernel,
        grid=(1,),
        in_specs=[
            pl.BlockSpec((CTX_LEN, D_MODEL), fixed),
            pl.BlockSpec((1, D_MODEL), fixed),
            pl.BlockSpec((1, D_MODEL), fixed),
            pl.BlockSpec((1, D_MODEL), fixed),
            pl.BlockSpec((D_MODEL, 2 * KV_WIDTH), lambda i: (0, kv_block)),
        ],
        out_specs=[
            pl.BlockSpec((CTX_LEN, KV_WIDTH), fixed),
            pl.BlockSpec((KV_WIDTH, CTX_LEN), fixed),
        ],
        out_shape=[
            jax.ShapeDtypeStruct((CTX_LEN, KV_WIDTH), BF16),
            jax.ShapeDtypeStruct((KV_WIDTH, CTX_LEN), BF16),
        ],
        compiler_params=_params("arbitrary"),
        name="ctx_kv",
    )(ctx, norm_w, shift, scale, w_in)


def _attn_kernel(qt_ref, kp_ref, km_ref, kn_ref, vtp_ref, vtm_ref, vtn_ref,
                 kc_ref, vtc_ref, sink_ref, wout_ref, wup_ref, wdn_ref,
                 cond_ref, wada_ref, bada_ref,
                 o_ref, wout_b_ref, wup_b_ref, wdn_b_ref, mod_ref):
    mod_ref[...] = _ada_block(cond_ref, wada_ref, bada_ref)

    i = pl.program_id(0)
    nsub = ATTN_TQ // BLOCK
    nblk = SEQ // BLOCK
    gw = GQA * BLOCK
    pw = 2 * BLOCK
    nkeys = 3 * BLOCK + CTX_LEN

    key_j = lax.broadcasted_iota(jnp.int32, (BLOCK, gw), 0)
    qry_i = lax.broadcasted_iota(jnp.int32, (BLOCK, gw), 1) % BLOCK
    neg = jnp.full((BLOCK, gw), NEG_INF, F32)
    zero = jnp.zeros((BLOCK, gw), F32)
    band_prev = jnp.where(key_j >= qry_i, zero, neg)
    band_next = jnp.where(key_j <= qry_i, zero, neg)

    kc = kc_ref[...]
    vtc = vtc_ref[...]
    ones_rows = jnp.ones((2 * SUBLANES, nkeys), BF16)

    windows = []
    for sb in range(nsub):
        n = i * nsub + sb
        lo, hi = sb * BLOCK, (sb + 1) * BLOCK
        k_prev = kp_ref[...] if sb == 0 else km_ref[lo - BLOCK:lo, :]
        k_next = kn_ref[...] if sb == nsub - 1 else km_ref[hi:hi + BLOCK, :]
        vt_prev = vtp_ref[...] if sb == 0 else vtm_ref[:, lo - BLOCK:lo]
        vt_next = vtn_ref[...] if sb == nsub - 1 else vtm_ref[:, hi:hi + BLOCK]
        windows.append(dict(
            kwin=jnp.concatenate([k_prev, km_ref[lo:hi, :], k_next, kc], axis=0),
            vtwin=jnp.concatenate([vt_prev, vtm_ref[:, lo:hi], vt_next, vtc], axis=1),
            bias_prev=band_prev + jnp.where(n > 0, 0.0, NEG_INF),
            bias_next=band_next + jnp.where(n < nblk - 1, 0.0, NEG_INF)))

    units = [(sb, h, half) for sb in range(nsub) for h in range(N_KV_HEADS)
             for half in range(GQA // 2)]

    def scores(sb, h):
        win = windows[sb]
        lo, hi = sb * BLOCK, (sb + 1) * BLOCK
        qh = jnp.concatenate(
            [qt_ref[(GQA * h + g) * HEAD_DIM:(GQA * h + g + 1) * HEAD_DIM, lo:hi]
             for g in range(GQA)], axis=1)
        blocks = []
        if h > 0:
            blocks.append(jnp.zeros((h * HEAD_DIM, gw), BF16))
        blocks.append(qh)
        if h < N_KV_HEADS - 1:
            blocks.append(jnp.zeros(((N_KV_HEADS - 1 - h) * HEAD_DIM, gw), BF16))
        s = jnp.dot(win["kwin"], jnp.concatenate(blocks, axis=0),
                    preferred_element_type=F32)
        s = jnp.concatenate([s[0:BLOCK] + win["bias_prev"], s[BLOCK:2 * BLOCK],
                             s[2 * BLOCK:3 * BLOCK] + win["bias_next"], s[3 * BLOCK:]], axis=0)
        return s, jnp.max(s, axis=0, keepdims=True)

    def probs(unit, s, smax):
        sb, h, half = unit
        hd0 = GQA * h + 2 * half
        cols = slice(half * pw, (half + 1) * pw)
        sink = sink_ref[:, hd0 * BLOCK:(hd0 + 2) * BLOCK] * LOG2E
        m = jnp.maximum(smax[:, cols], sink)
        return jnp.exp2(s[:, cols] - m).astype(BF16), jnp.exp2(sink - m)

    def finish(unit, e, e_sink):
        sb, h, half = unit
        lo, hi = sb * BLOCK, (sb + 1) * BLOCK
        hd0 = GQA * h + 2 * half
        vt_ones = jnp.concatenate(
            [windows[sb]["vtwin"][h * HEAD_DIM:(h + 1) * HEAD_DIM, :], ones_rows], axis=0)
        o = jnp.dot(vt_ones, e, preferred_element_type=F32)
        denom = o[HEAD_DIM:HEAD_DIM + 1, :] + e_sink
        o = o[:HEAD_DIM, :] * (1.0 / denom)
        blk = jnp.concatenate([o[:, 0:BLOCK], o[:, BLOCK:2 * BLOCK]], axis=0)
        o_ref[lo:hi, hd0 * HEAD_DIM:(hd0 + 2) * HEAD_DIM] = blk.T.astype(BF16)

    cast_jobs = []
    for src, dst in ((wout_ref, wout_b_ref), (wup_ref, wup_b_ref), (wdn_ref, wdn_b_ref)):
        wc = max(LANES, CAST_JOB_ELEMS // src.shape[0])
        cast_jobs += [(src, dst, c0) + (wc,) for c0 in range(0, src.shape[1], wc)]

    def cast_piece(idx, n):
        for src, dst, c0, wc in cast_jobs[idx * len(cast_jobs) // n:(idx + 1) * len(cast_jobs) // n]:
            dst[:, c0:c0 + wc] = src[:, c0:c0 + wc].astype(BF16)

    nu = len(units)
    s_vals, p_vals = {}, {}
    for step in range(nu + 2):
        if step < nu and units[step][2] == 0:
            s_vals[units[step][:2]] = scores(*units[step][:2])
        if 1 <= step <= nu:
            unit = units[step - 1]
            p_vals[step - 1] = probs(unit, *s_vals[unit[:2]])
            if unit[2] == GQA // 2 - 1:
                del s_vals[unit[:2]]
            cast_piece(step - 1, nu)
        if step >= 2:
            finish(units[step - 2], *p_vals.pop(step - 2))


def _attn(qt, k, vt, kc, vtc, sink_row, w_out, w_up, w_dn, cond, w_ada, b_ada, ada_col0):
    tq = ATTN_TQ
    r = tq // BLOCK
    nblk = SEQ // BLOCK
    nsteps = SEQ // tq
    ada_tn = (w_ada.shape[1] - ada_col0) // nsteps
    ada_b0 = ada_col0 // ada_tn
    fixed = lambda i: (0, 0)
    row = lambda i: (i, 0)
    slab = lambda w: pl.BlockSpec((w.shape[0] // nsteps, w.shape[1]), row)
    bf16_like = lambda w: jax.ShapeDtypeStruct(w.shape, BF16)
    prev = lambda i: jnp.maximum(i * r - 1, 0)
    nxt = lambda i: jnp.minimum((i + 1) * r, nblk - 1)
    return pl.pallas_call(
        _attn_kernel,
        grid=(SEQ // tq,),
        in_specs=[
            pl.BlockSpec((ATTN_WIDTH, tq), lambda i: (0, i)),
            pl.BlockSpec((BLOCK, KV_WIDTH), lambda i: (prev(i), 0)),
            pl.BlockSpec((tq, KV_WIDTH), lambda i: (i, 0)),
            pl.BlockSpec((BLOCK, KV_WIDTH), lambda i: (nxt(i), 0)),
            pl.BlockSpec((KV_WIDTH, BLOCK), lambda i: (0, prev(i))),
            pl.BlockSpec((KV_WIDTH, tq), lambda i: (0, i)),
            pl.BlockSpec((KV_WIDTH, BLOCK), lambda i: (0, nxt(i))),
            pl.BlockSpec((CTX_LEN, KV_WIDTH), fixed),
            pl.BlockSpec((KV_WIDTH, CTX_LEN), fixed),
            pl.BlockSpec((1, N_KV_HEADS * GQA * BLOCK), fixed),
            slab(w_out), slab(w_up), slab(w_dn),
            pl.BlockSpec(cond.shape, fixed),
            pl.BlockSpec((D_MODEL, ada_tn), lambda i: (0, ada_b0 + i)),
            pl.BlockSpec((1, ada_tn), lambda i: (0, ada_b0 + i)),
        ],
        out_specs=[pl.BlockSpec((tq, ATTN_WIDTH), row), slab(w_out), slab(w_up), slab(w_dn),
                   pl.BlockSpec((cond.shape[0], ada_tn), lambda i: (0, i))],
        out_shape=[jax.ShapeDtypeStruct((SEQ, ATTN_WIDTH), BF16),
                   bf16_like(w_out), bf16_like(w_up), bf16_like(w_dn),
                   jax.ShapeDtypeStruct((cond.shape[0], w_ada.shape[1] - ada_col0), F32)],
        compiler_params=_params("arbitrary"),
        name="attn",
    )(qt, k, k, k, vt, vt, vt, kc, vtc, sink_row, w_out, w_up, w_dn, cond, w_ada, b_ada)


def _mixout_kernel(x_ref, attn_ref, up_ref, um_ref, un_ref, wout_ref, pw_ref, ps_ref,
                   ga_ref, nw_ref, sh_ref, sc_ref, x1_ref, hm_ref, ubuf):
    i = pl.program_id(0)
    tm = MIX_TM
    nsteps = SEQ // tm
    ubuf[0:POOL_HALO, :] = jnp.where(i > 0, up_ref[...], 0.0)
    ubuf[POOL_HALO:POOL_HALO + tm, :] = um_ref[...]
    ubuf[POOL_HALO + tm:2 * POOL_HALO + tm, :] = jnp.where(i < nsteps - 1, un_ref[...], 0.0)

    rc = MIX_CHUNK
    n = rc + 2 * POOL_HALO
    c = POOL_HALO
    ngroups = len(POOL_WINDOWS)
    ncol = D_MODEL // ngroups

    def pool_group(r0, g):
        w = POOL_WINDOWS[g]
        c0, c1 = g * POOL_GROUP_DIM, (g + 1) * POOL_GROUP_DIM
        e = ubuf[r0:r0 + n, c0:c1]
        if w == 2:
            win = e[c - 1:c - 1 + rc] + e[c:c + rc]
        else:
            f2 = e[0:n - 1] + e[1:n]
            if w == 4:
                win = f2[c - 2:c - 2 + rc] + f2[c:c + rc]
            else:
                f4 = f2[0:n - 3] + f2[2:n - 1]
                if w == 8:
                    win = f4[c - 4:c - 4 + rc] + f4[c:c + rc]
                else:
                    f8 = f4[0:n - 7] + f4[4:n - 3]
                    win = f8[c - 8:c - 8 + rc] + f8[c:c + rc]
        t = i * tm + r0 + lax.broadcasted_iota(jnp.int32, (rc, 1), 0)
        lo = jnp.clip(t - w // 2, 0, SEQ)
        hi = jnp.clip(t - w // 2 + w, 0, SEQ)
        cnt = (hi - lo).astype(F32)
        pg = win / cnt - um_ref[r0:r0 + rc, c0:c1]
        mixed = jnp.dot(pg.astype(BF16), pw_ref[g], preferred_element_type=F32)
        return (mixed * ps_ref[:, c0:c1]).astype(BF16)

    def prenorm(r0, nrows):
        rows = slice(r0, r0 + nrows)
        hm_ref[rows, :] = _rms_modulate(x1_ref[rows, :], nw_ref[...], sh_ref[...],
                                        sc_ref[...]).astype(BF16)

    chunk_starts = list(range(0, tm, rc))
    for ci, r0 in enumerate(chunk_starts):
        rows = slice(r0, r0 + rc)
        attn_rows = attn_ref[rows, :]
        pooled = []
        for j in range(ngroups):
            cols = slice(j * ncol, (j + 1) * ncol)
            o = jnp.dot(attn_rows, wout_ref[0:ATTN_WIDTH, cols], preferred_element_type=F32)
            x1_ref[rows, cols] = x_ref[rows, cols] + ga_ref[:, cols] * o
            pooled.append(pool_group(r0, j))
        pool_rows = jnp.concatenate(pooled, axis=1)
        for j in range(ngroups):
            cols = slice(j * ncol, (j + 1) * ncol)
            o = jnp.dot(pool_rows, wout_ref[ATTN_WIDTH:, cols], preferred_element_type=F32)
            x1_ref[rows, cols] += ga_ref[:, cols] * o
            if ci > 0:
                prenorm(chunk_starts[ci - 1] + j * (rc // ngroups), rc // ngroups)
    for j in range(ngroups):
        prenorm(chunk_starts[-1] + j * (rc // ngroups), rc // ngroups)


def _mixout(x, attn, u, w_out_bf16, pool_w_bf16, pool_scale, g_a, norm_w, shift, scale):
    tm = MIX_TM
    hb = tm // POOL_HALO
    nhalo = SEQ // POOL_HALO
    row = lambda i: (i, 0)
    fixed = lambda i: (0, 0)
    return pl.pallas_call(
        _mixout_kernel,
        grid=(SEQ // tm,),
        in_specs=[
            pl.BlockSpec((tm, D_MODEL), row),
            pl.BlockSpec((tm, ATTN_WIDTH), row),
            pl.BlockSpec((POOL_HALO, POOL_WIDTH), lambda i: (jnp.maximum(i * hb - 1, 0), 0)),
            pl.BlockSpec((tm, POOL_WIDTH), row),
            pl.BlockSpec((POOL_HALO, POOL_WIDTH), lambda i: (jnp.minimum((i + 1) * hb, nhalo - 1), 0)),
            pl.BlockSpec((D_MODEL, D_MODEL), fixed),
            pl.BlockSpec((len(POOL_WINDOWS), POOL_GROUP_DIM, POOL_GROUP_DIM), lambda i: (0, 0, 0)),
            pl.BlockSpec((1, POOL_WIDTH), fixed),
            pl.BlockSpec((1, D_MODEL), fixed),
            pl.BlockSpec((1, D_MODEL), fixed),
            pl.BlockSpec((1, D_MODEL), fixed),
            pl.BlockSpec((1, D_MODEL), fixed),
        ],
        out_specs=[pl.BlockSpec((tm, D_MODEL), row), pl.BlockSpec((tm, D_MODEL), row)],
        out_shape=[jax.ShapeDtypeStruct((SEQ, D_MODEL), F32),
                   jax.ShapeDtypeStruct((SEQ, D_MODEL), BF16)],
        scratch_shapes=[pltpu.VMEM((tm + 2 * POOL_HALO, POOL_WIDTH), F32)],
        compiler_params=_params("arbitrary"),
        name="mixout",
    )(x, attn, u, u, u, w_out_bf16, pool_w_bf16, pool_scale, g_a, norm_w, shift, scale)


def _mlp_kernel(hm_ref, x1r_ref, wup_ref, wdn_ref, gm_ref, fw_ref, o_ref):
    f = pl.program_id(1)
    last = MLP_NF - 1
    row_chunks = [slice(r0, r0 + MLP_RC) for r0 in range(0, MLP_TM, MLP_RC)]
    gm = gm_ref[...]

    def branch_out(rows):
        a = jnp.maximum(jnp.dot(hm_ref[rows, :], wup_ref[...], preferred_element_type=F32), 0.0)
        return gm * jnp.dot((a * a).astype(BF16), wdn_ref[...], preferred_element_type=F32)

    def final_norm(rows):
        x2 = o_ref[rows, :]
        y = x2 * lax.rsqrt(jnp.mean(x2 * x2, axis=-1, keepdims=True) + EPS)
        o_ref[rows, :] = y * fw_ref[...]

    @pl.when(f == 0)
    def _():
        for rows in row_chunks:
            o_ref[rows, :] = branch_out(rows)
        o_ref[0:MLP_XR, :] += x1r_ref[...]

    @pl.when(jnp.logical_and(f > 0, f < last))
    def _():
        x_rows = pl.ds(pl.multiple_of(f * MLP_XR, MLP_XR), MLP_XR)
        o_ref[x_rows, :] += x1r_ref[...]
        for rows in row_chunks:
            o_ref[rows, :] += branch_out(rows)

    @pl.when(f == last)
    def _():
        o_ref[last * MLP_XR:(last + 1) * MLP_XR, :] += x1r_ref[...]
        npiece = 4
        cw, rp = D_MODEL // npiece, MLP_RC // npiece
        for ci, rows in enumerate(row_chunks):
            a = jnp.maximum(jnp.dot(hm_ref[rows, :], wup_ref[...], preferred_element_type=F32), 0.0)
            a = (a * a).astype(BF16)
            for j in range(npiece):
                cols = slice(j * cw, (j + 1) * cw)
                o_ref[rows, cols] += gm[:, cols] * jnp.dot(a, wdn_ref[:, cols],
                                                           preferred_element_type=F32)
                if ci > 0:
                    r0 = row_chunks[ci - 1].start + j * rp
                    final_norm(slice(r0, r0 + rp))
        final_norm(row_chunks[-1])


def _mlp(hm, x1, w_up_bf16, w_dn_bf16, g_m, final_w):
    tm, tf = MLP_TM, MLP_TF
    return pl.pallas_call(
        _mlp_kernel,
        grid=(SEQ // tm, MLP_NF),
        in_specs=[
            pl.BlockSpec((tm, D_MODEL), lambda i, f: (i, 0)),
            pl.BlockSpec((MLP_XR, D_MODEL), lambda i, f: (i * MLP_NF + f, 0)),
            pl.BlockSpec((D_MODEL, tf), lambda i, f: (0, f)),
            pl.BlockSpec((tf, D_MODEL), lambda i, f: (f, 0)),
            pl.BlockSpec((1, D_MODEL), lambda i, f: (0, 0)),
            pl.BlockSpec((1, D_MODEL), lambda i, f: (0, 0)),
        ],
        out_specs=pl.BlockSpec((tm, D_MODEL), lambda i, f: (i, 0)),
        out_shape=jax.ShapeDtypeStruct((SEQ, D_MODEL), F32),
        compiler_params=_params("arbitrary", "arbitrary"),
        name="mlp",
    )(hm, x1, w_up_bf16, w_dn_bf16, g_m, final_w)


def kernel(x, c, ctx, c_ctx, norm_attn_w, norm_mlp_w, w_ada, b_ada, w_in, attn_sink,
           pool_w, pool_scale, w_out, w_mlp_up, w_mlp_down, final_norm_w):
    assert x.shape == (1, SEQ, D_MODEL) and ctx.shape == (1, CTX_LEN, D_MODEL)
    assert w_ada.shape[0] == 1, "single layer"
    x2d, ctx2d = x[0], ctx[0]

    cond = jnp.concatenate([c, c_ctx[None, :], jnp.zeros((SUBLANES - 2, D_MODEL), F32)], axis=0)
    b_row = b_ada[0][None, :]
    n_early = 2 * D_MODEL
    mod_a = _ada(cond, w_ada[0], b_row, n_early)
    chunk = lambda m, r, j: m[r:r + 1, j * D_MODEL:(j + 1) * D_MODEL]
    sh_a, sc_a = chunk(mod_a, 0, 0), chunk(mod_a, 0, 1)
    csh_a, csc_a = chunk(mod_a, 1, 0), chunk(mod_a, 1, 1)

    nw_a = norm_attn_w[0][None, :]
    tables = tuple(jnp.asarray(t) for t in _rope_tables())
    qt, k, vt, u = _inproj(x2d, nw_a, sh_a, sc_a, w_in[0], tables)
    kc, vtc = _ctx_kv(ctx2d, nw_a, csh_a, csc_a, w_in[0])

    sink_row = jnp.repeat(attn_sink[0].astype(F32), BLOCK)[None, :]
    attn, w_out_b, w_up_b, w_dn_b, mod_b = _attn(
        qt, k, vt, kc, vtc, sink_row, w_out[0], w_mlp_up[0], w_mlp_down[0],
        cond, w_ada[0], b_row, n_early)
    g_a, sh_m, sc_m, g_m = (chunk(mod_b, 0, j) for j in range(N_MOD - 2))

    x1, hm = _mixout(x2d, attn, u, w_out_b, pool_w[0].astype(BF16),
                     pool_scale[0][None, :], g_a, norm_mlp_w[0][None, :], sh_m, sc_m)
    out = _mlp(hm, x1, w_up_b, w_dn_b, g_m, final_norm_w[None, :])
    return out[None]
```

```python
import functools

import numpy as np
import jax
import jax.numpy as jnp
from jax import lax
from jax.experimental import pallas as pl
from jax.experimental.pallas import tpu as pltpu

F32 = jnp.float32
BF16 = jnp.bfloat16

D_MODEL = 2048
SEQ = 8192
CTX_LEN = 256
GRID_W = 64
HEAD_DIM = 64
N_Q_HEADS = 16
N_KV_HEADS = 4
GQA = N_Q_HEADS // N_KV_HEADS
ATTN_WIDTH = N_Q_HEADS * HEAD_DIM
KV_WIDTH = N_KV_HEADS * HEAD_DIM
POOL_WINDOWS = (2, 4, 8, 16)
POOL_WIDTH = D_MODEL - ATTN_WIDTH
POOL_GROUP_DIM = POOL_WIDTH // len(POOL_WINDOWS)
IN_WIDTH = ATTN_WIDTH + 2 * KV_WIDTH + POOL_WIDTH
D_FF = 4 * D_MODEL
BLOCK = 128
ROPE_BASE = 10000.0
N_MOD = 6
EPS = 1e-6
NEG_INF = -1e30
LOG2E = 1.4426950408889634
Q_SCALE = HEAD_DIM ** -0.5 * LOG2E

LANES = 128
SUBLANES = 8
POOL_HALO = 8
VMEM_LIMIT = 56 * 1024 * 1024

ADA_TN = 512
INPROJ_TM = 512
ATTN_TQ = 512
CAST_JOB_ELEMS = 64 * 1024
MIX_TM = 512
MIX_CHUNK = 256
MLP_TM = 1024
MLP_TF = 1024
MLP_RC = 512
MLP_NF = D_FF // MLP_TF
MLP_XR = MLP_TM // MLP_NF


def _params(*sem):
    return pltpu.CompilerParams(dimension_semantics=sem, vmem_limit_bytes=VMEM_LIMIT)


def _rms_modulate(x, norm_w, shift, scale):
    gain = norm_w * (1.0 + scale)
    y = x * lax.rsqrt(jnp.mean(x * x, axis=-1, keepdims=True) + EPS)
    return y * gain + shift


def _ada_block(cond_ref, w_ref, b_ref):
    cnd = cond_ref[...]
    s = cnd * (1.0 / (1.0 + jnp.exp(-cnd)))
    return jnp.dot(s, w_ref[...], preferred_element_type=F32) + b_ref[...]


def _ada_kernel(cond_ref, w_ref, b_ref, o_ref):
    o_ref[...] = _ada_block(cond_ref, w_ref, b_ref)


def _ada(cond, w_ada, b_ada, n):
    rows = cond.shape[0]
    return pl.pallas_call(
        _ada_kernel,
        grid=(n // ADA_TN,),
        in_specs=[
            pl.BlockSpec((rows, D_MODEL), lambda j: (0, 0)),
            pl.BlockSpec((D_MODEL, ADA_TN), lambda j: (0, j)),
            pl.BlockSpec((1, ADA_TN), lambda j: (0, j)),
        ],
        out_specs=pl.BlockSpec((rows, ADA_TN), lambda j: (0, j)),
        out_shape=jax.ShapeDtypeStruct((rows, n), F32),
        compiler_params=_params("arbitrary"),
        name="ada",
    )(cond, w_ada, b_ada)


def _rope_tables():
    half = HEAD_DIM // 2
    inv_freq = ROPE_BASE ** (-np.arange(0, half, 2, dtype=np.float64) / half)
    t = np.arange(SEQ)
    row = (t // GRID_W).astype(np.float64)
    col = (t % GRID_W).astype(np.float64)
    ang_r = row[:, None] * inv_freq[None, :]
    ang_c = col[:, None] * inv_freq[None, :]
    cos_t = Q_SCALE * np.concatenate([np.cos(ang_r), np.cos(ang_c)], axis=1).T
    sin_t = Q_SCALE * np.concatenate([np.sin(ang_r), np.sin(ang_c)], axis=1).T
    cos_l = np.concatenate([np.cos(ang_r), np.cos(ang_r), np.cos(ang_c), np.cos(ang_c)], axis=1)
    zeros = np.zeros_like(ang_r)
    s_up = np.concatenate([-np.sin(ang_r), zeros, -np.sin(ang_c), zeros], axis=1)
    s_dn = np.concatenate([zeros, np.sin(ang_r), zeros, np.sin(ang_c)], axis=1)
    tile2 = lambda a: np.concatenate([a, a], axis=1).astype(np.float32)
    return (cos_t.astype(np.float32), sin_t.astype(np.float32),
            tile2(cos_l), tile2(s_up), tile2(s_dn))


def _inproj_kernel(x_ref, nw_ref, sh_ref, sc_ref, w_ref, cos_t_ref, sin_t_ref,
                   cos_l_ref, sup_ref, sdn_ref, qt_ref, k_ref, vt_ref, u_ref, wb_ref):
    @pl.when(pl.program_id(0) == 0)
    def _():
        for c0 in range(0, IN_WIDTH, 2 * LANES):
            wb_ref[:, c0:c0 + 2 * LANES] = w_ref[:, c0:c0 + 2 * LANES].astype(BF16)

    h = _rms_modulate(x_ref[...], nw_ref[...], sh_ref[...], sc_ref[...]).astype(BF16)
    p = jnp.dot(h, wb_ref[...], preferred_element_type=F32)

    qt = p[:, :ATTN_WIDTH].T
    cr, cc = cos_t_ref[0:16, :], cos_t_ref[16:32, :]
    sr, sc = sin_t_ref[0:16, :], sin_t_ref[16:32, :]
    pieces = []
    for hd in range(N_Q_HEADS):
        b = hd * HEAD_DIM
        x1r, x2r = qt[b:b + 16, :], qt[b + 16:b + 32, :]
        x1c, x2c = qt[b + 32:b + 48, :], qt[b + 48:b + 64, :]
        pieces += [x1r * cr - x2r * sr, x1r * sr + x2r * cr,
                   x1c * cc - x2c * sc, x1c * sc + x2c * cc]
    qt_ref[...] = jnp.concatenate(pieces, axis=0).astype(BF16)

    k = p[:, ATTN_WIDTH:ATTN_WIDTH + KV_WIDTH]
    cos_l, s_up, s_dn = cos_l_ref[...], sup_ref[...], sdn_ref[...]
    kparts = []
    for g in range(KV_WIDTH // LANES):
        kg = k[:, g * LANES:(g + 1) * LANES]
        kparts.append(kg * cos_l + pltpu.roll(kg, LANES - 16, 1) * s_up
                      + pltpu.roll(kg, 16, 1) * s_dn)
    k_ref[...] = jnp.concatenate(kparts, axis=1).astype(BF16)

    vt_ref[...] = p[:, ATTN_WIDTH + KV_WIDTH:ATTN_WIDTH + 2 * KV_WIDTH].T.astype(BF16)
    u_ref[...] = p[:, ATTN_WIDTH + 2 * KV_WIDTH:]


def _inproj(x, norm_w, shift, scale, w_in, tables):
    cos_t, sin_t, cos_l, s_up, s_dn = tables
    tm = INPROJ_TM
    row = lambda i: (i, 0)
    colb = lambda i: (0, i)
    fixed = lambda i: (0, 0)
    return pl.pallas_call(
        _inproj_kernel,
        grid=(SEQ // tm,),
        in_specs=[
            pl.BlockSpec((tm, D_MODEL), row),
            pl.BlockSpec((1, D_MODEL), fixed),
            pl.BlockSpec((1, D_MODEL), fixed),
            pl.BlockSpec((1, D_MODEL), fixed),
            pl.BlockSpec((D_MODEL, IN_WIDTH), fixed, pipeline_mode=pl.Buffered(1)),
            pl.BlockSpec((32, tm), colb),
            pl.BlockSpec((32, tm), colb),
            pl.BlockSpec((tm, LANES), row),
            pl.BlockSpec((tm, LANES), row),
            pl.BlockSpec((tm, LANES), row),
        ],
        out_specs=[
            pl.BlockSpec((ATTN_WIDTH, tm), colb),
            pl.BlockSpec((tm, KV_WIDTH), row),
            pl.BlockSpec((KV_WIDTH, tm), colb),
            pl.BlockSpec((tm, POOL_WIDTH), row),
        ],
        out_shape=[
            jax.ShapeDtypeStruct((ATTN_WIDTH, SEQ), BF16),
            jax.ShapeDtypeStruct((SEQ, KV_WIDTH), BF16),
            jax.ShapeDtypeStruct((KV_WIDTH, SEQ), BF16),
            jax.ShapeDtypeStruct((SEQ, POOL_WIDTH), F32),
        ],
        scratch_shapes=[pltpu.VMEM((D_MODEL, IN_WIDTH), BF16)],
        compiler_params=_params("arbitrary"),
        name="inproj",
    )(x, norm_w, shift, scale, w_in, cos_t, sin_t, cos_l, s_up, s_dn)


def _ctx_kv_kernel(x_ref, nw_ref, sh_ref, sc_ref, w_ref, k_ref, vt_ref):
    h = _rms_modulate(x_ref[...], nw_ref[...], sh_ref[...], sc_ref[...]).astype(BF16)
    p = jnp.dot(h, w_ref[...].astype(BF16), preferred_element_type=F32)
    k_ref[...] = p[:, :KV_WIDTH].astype(BF16)
    vt_ref[...] = p[:, KV_WIDTH:].T.astype(BF16)


def _ctx_kv(ctx, norm_w, shift, scale, w_in):
    fixed = lambda i: (0, 0)
    kv_block = ATTN_WIDTH // (2 * KV_WIDTH)
    return pl.pallas_call(
        _ctx_kv_kernel,
        grid=(1,),
        in_specs=[
            pl.BlockSpec((CTX_LEN, D_MODEL), fixed),
            pl.BlockSpec((1, D_MODEL), fixed),
            pl.BlockSpec((1, D_MODEL), fixed),
            pl.BlockSpec((1, D_MODEL), fixed),
            pl.BlockSpec((D_MODEL, 2 * KV_WIDTH), lambda i: (0, kv_block)),
        ],
        out_specs=[
            pl.BlockSpec((CTX_LEN, KV_WIDTH), fixed),
            pl.BlockSpec((KV_WIDTH, CTX_LEN), fixed),
        ],
        out_shape=[
            jax.ShapeDtypeStruct((CTX_LEN, KV_WIDTH), BF16),
            jax.ShapeDtypeStruct((KV_WIDTH, CTX_LEN), BF16),
        ],
        compiler_params=_params("arbitrary"),
        name="ctx_kv",
    )(ctx, norm_w, shift, scale, w_in)


def _attn_kernel(qt_ref, kp_ref, km_ref, kn_ref, vtp_ref, vtm_ref, vtn_ref,
                 kc_ref, vtc_ref, sink_ref, wout_ref, wup_ref, wdn_ref,
                 cond_ref, wada_ref, bada_ref,
                 o_ref, wout_b_ref, wup_b_ref, wdn_b_ref, mod_ref):
    mod_ref[...] = _ada_block(cond_ref, wada_ref, bada_ref)

    i = pl.program_id(0)
    nsub = ATTN_TQ // BLOCK
    nblk = SEQ // BLOCK
    gw = GQA * BLOCK
    pw = 2 * BLOCK
    nkeys = 3 * BLOCK + CTX_LEN

    key_j = lax.broadcasted_iota(jnp.int32, (BLOCK, gw), 0)
    qry_i = lax.broadcasted_iota(jnp.int32, (BLOCK, gw), 1) % BLOCK
    neg = jnp.full((BLOCK, gw), NEG_INF, F32)
    zero = jnp.zeros((BLOCK, gw), F32)
    band_prev = jnp.where(key_j >= qry_i, zero, neg)
    band_next = jnp.where(key_j <= qry_i, zero, neg)

    kc = kc_ref[...]
    vtc = vtc_ref[...]
    ones_rows = jnp.ones((2 * SUBLANES, nkeys), BF16)

    windows = []
    for sb in range(nsub):
        n = i * nsub + sb
        lo, hi = sb * BLOCK, (sb + 1) * BLOCK
        k_prev = kp_ref[...] if sb == 0 else km_ref[lo - BLOCK:lo, :]
        k_next = kn_ref[...] if sb == nsub - 1 else km_ref[hi:hi + BLOCK, :]
        vt_prev = vtp_ref[...] if sb == 0 else vtm_ref[:, lo - BLOCK:lo]
        vt_next = vtn_ref[...] if sb == nsub - 1 else vtm_ref[:, hi:hi + BLOCK]
        windows.append(dict(
            kwin=jnp.concatenate([k_prev, km_ref[lo:hi, :], k_next, kc], axis=0),
            vtwin=jnp.concatenate([vt_prev, vtm_ref[:, lo:hi], vt_next, vtc], axis=1),
            bias_prev=band_prev + jnp.where(n > 0, 0.0, NEG_INF),
            bias_next=band_next + jnp.where(n < nblk - 1, 0.0, NEG_INF)))

    units = [(sb, h, half) for sb in range(nsub) for h in range(N_KV_HEADS)
             for half in range(GQA // 2)]

    def scores(sb, h):
        win = windows[sb]
        lo, hi = sb * BLOCK, (sb + 1) * BLOCK
        qh = jnp.concatenate(
            [qt_ref[(GQA * h + g) * HEAD_DIM:(GQA * h + g + 1) * HEAD_DIM, lo:hi]
             for g in range(GQA)], axis=1)
        blocks = []
        if h > 0:
            blocks.append(jnp.zeros((h * HEAD_DIM, gw), BF16))
        blocks.append(qh)
        if h < N_KV_HEADS - 1:
            blocks.append(jnp.zeros(((N_KV_HEADS - 1 - h) * HEAD_DIM, gw), BF16))
        s = jnp.dot(win["kwin"], jnp.concatenate(blocks, axis=0),
                    preferred_element_type=F32)
        s = jnp.concatenate([s[0:BLOCK] + win["bias_prev"], s[BLOCK:2 * BLOCK],
                             s[2 * BLOCK:3 * BLOCK] + win["bias_next"], s[3 * BLOCK:]], axis=0)
        return s, jnp.max(s, axis=0, keepdims=True)

    def probs(unit, s, smax):
        sb, h, half = unit
        hd0 = GQA * h + 2 * half
        cols = slice(half * pw, (half + 1) * pw)
        sink = sink_ref[:, hd0 * BLOCK:(hd0 + 2) * BLOCK] * LOG2E
        m = jnp.maximum(smax[:, cols], sink)
        return jnp.exp2(s[:, cols] - m).astype(BF16), jnp.exp2(sink - m)

    def finish(unit, e, e_sink):
        sb, h, half = unit
        lo, hi = sb * BLOCK, (sb + 1) * BLOCK
        hd0 = GQA * h + 2 * half
        vt_ones = jnp.concatenate(
            [windows[sb]["vtwin"][h * HEAD_DIM:(h + 1) * HEAD_DIM, :], ones_rows], axis=0)
        o = jnp.dot(vt_ones, e, preferred_element_type=F32)
        denom = o[HEAD_DIM:HEAD_DIM + 1, :] + e_sink
        o = o[:HEAD_DIM, :] * (1.0 / denom)
        blk = jnp.concatenate([o[:, 0:BLOCK], o[:, BLOCK:2 * BLOCK]], axis=0)
        o_ref[lo:hi, hd0 * HEAD_DIM:(hd0 + 2) * HEAD_DIM] = blk.T.astype(BF16)

    cast_jobs = []
    for src, dst in ((wout_ref, wout_b_ref), (wup_ref, wup_b_ref), (wdn_ref, wdn_b_ref)):
        wc = max(LANES, CAST_JOB_ELEMS // src.shape[0])
        cast_jobs += [(src, dst, c0) + (wc,) for c0 in range(0, src.shape[1], wc)]

    def cast_piece(idx, n):
        for src, dst, c0, wc in cast_jobs[idx * len(cast_jobs) // n:(idx + 1) * len(cast_jobs) // n]:
            val = src[:, c0:c0 + wc].astype(BF16)
            if len(dst.shape) == 3:
                tf = dst.shape[2]
                dst[c0 // tf, :, c0 % tf:c0 % tf + wc] = val
            else:
                dst[:, c0:c0 + wc] = val

    nu = len(units)
    s_vals, p_vals = {}, {}
    for step in range(nu + 2):
        if step < nu and units[step][2] == 0:
            s_vals[units[step][:2]] = scores(*units[step][:2])
        if 1 <= step <= nu:
            unit = units[step - 1]
            p_vals[step - 1] = probs(unit, *s_vals[unit[:2]])
            if unit[2] == GQA // 2 - 1:
                del s_vals[unit[:2]]
            cast_piece(step - 1, nu)
        if step >= 2:
            finish(units[step - 2], *p_vals.pop(step - 2))


def _attn(qt, k, vt, kc, vtc, sink_row, w_out, w_up, w_dn, cond, w_ada, b_ada, ada_col0):
    tq = ATTN_TQ
    r = tq // BLOCK
    nblk = SEQ // BLOCK
    nsteps = SEQ // tq
    ada_tn = (w_ada.shape[1] - ada_col0) // nsteps
    ada_b0 = ada_col0 // ada_tn
    fixed = lambda i: (0, 0)
    row = lambda i: (i, 0)
    slab = lambda w: pl.BlockSpec((w.shape[0] // nsteps, w.shape[1]), row)
    bf16_like = lambda w: jax.ShapeDtypeStruct(w.shape, BF16)
    prev = lambda i: jnp.maximum(i * r - 1, 0)
    nxt = lambda i: jnp.minimum((i + 1) * r, nblk - 1)
    return pl.pallas_call(
        _attn_kernel,
        grid=(SEQ // tq,),
        in_specs=[
            pl.BlockSpec((ATTN_WIDTH, tq), lambda i: (0, i)),
            pl.BlockSpec((BLOCK, KV_WIDTH), lambda i: (prev(i), 0)),
            pl.BlockSpec((tq, KV_WIDTH), lambda i: (i, 0)),
            pl.BlockSpec((BLOCK, KV_WIDTH), lambda i: (nxt(i), 0)),
            pl.BlockSpec((KV_WIDTH, BLOCK), lambda i: (0, prev(i))),
            pl.BlockSpec((KV_WIDTH, tq), lambda i: (0, i)),
            pl.BlockSpec((KV_WIDTH, BLOCK), lambda i: (0, nxt(i))),
            pl.BlockSpec((CTX_LEN, KV_WIDTH), fixed),
            pl.BlockSpec((KV_WIDTH, CTX_LEN), fixed),
            pl.BlockSpec((1, N_KV_HEADS * GQA * BLOCK), fixed),
            slab(w_out), slab(w_up), slab(w_dn),
            pl.BlockSpec(cond.shape, fixed),
            pl.BlockSpec((D_MODEL, ada_tn), lambda i: (0, ada_b0 + i)),
            pl.BlockSpec((1, ada_tn), lambda i: (0, ada_b0 + i)),
        ],
        out_specs=[pl.BlockSpec((tq, ATTN_WIDTH), row), slab(w_out),
                   pl.BlockSpec((MLP_NF, w_up.shape[0] // nsteps, MLP_TF), lambda i: (0, i, 0)),
                   slab(w_dn),
                   pl.BlockSpec((cond.shape[0], ada_tn), lambda i: (0, i))],
        out_shape=[jax.ShapeDtypeStruct((SEQ, ATTN_WIDTH), BF16),
                   bf16_like(w_out),
                   jax.ShapeDtypeStruct((MLP_NF, w_up.shape[0], MLP_TF), BF16),
                   bf16_like(w_dn),
                   jax.ShapeDtypeStruct((cond.shape[0], w_ada.shape[1] - ada_col0), F32)],
        compiler_params=_params("arbitrary"),
        name="attn",
    )(qt, k, k, k, vt, vt, vt, kc, vtc, sink_row, w_out, w_up, w_dn, cond, w_ada, b_ada)


def _mixout_kernel(x_ref, attn_ref, up_ref, um_ref, un_ref, wout_ref, pw_ref, ps_ref,
                   ga_ref, nw_ref, sh_ref, sc_ref, x1_ref, hm_ref, ubuf):
    i = pl.program_id(0)
    tm = MIX_TM
    nsteps = SEQ // tm
    ubuf[0:POOL_HALO, :] = jnp.where(i > 0, up_ref[...], 0.0)
    ubuf[POOL_HALO:POOL_HALO + tm, :] = um_ref[...]
    ubuf[POOL_HALO + tm:2 * POOL_HALO + tm, :] = jnp.where(i < nsteps - 1, un_ref[...], 0.0)

    rc = MIX_CHUNK
    n = rc + 2 * POOL_HALO
    c = POOL_HALO
    ngroups = len(POOL_WINDOWS)
    ncol = D_MODEL // ngroups

    def pool_group(r0, g):
        w = POOL_WINDOWS[g]
        c0, c1 = g * POOL_GROUP_DIM, (g + 1) * POOL_GROUP_DIM
        e = ubuf[r0:r0 + n, c0:c1]
        if w == 2:
            win = e[c - 1:c - 1 + rc] + e[c:c + rc]
        else:
            f2 = e[0:n - 1] + e[1:n]
            if w == 4:
                win = f2[c - 2:c - 2 + rc] + f2[c:c + rc]
            else:
                f4 = f2[0:n - 3] + f2[2:n - 1]
                if w == 8:
                    win = f4[c - 4:c - 4 + rc] + f4[c:c + rc]
                else:
                    f8 = f4[0:n - 7] + f4[4:n - 3]
                    win = f8[c - 8:c - 8 + rc] + f8[c:c + rc]
        t = i * tm + r0 + lax.broadcasted_iota(jnp.int32, (rc, 1), 0)
        lo = jnp.clip(t - w // 2, 0, SEQ)
        hi = jnp.clip(t - w // 2 + w, 0, SEQ)
        cnt = (hi - lo).astype(F32)
        pg = win / cnt - um_ref[r0:r0 + rc, c0:c1]
        mixed = jnp.dot(pg.astype(BF16), pw_ref[g], preferred_element_type=F32)
        return (mixed * ps_ref[:, c0:c1]).astype(BF16)

    def prenorm(r0, nrows):
        rows = slice(r0, r0 + nrows)
        hm_ref[rows, :] = _rms_modulate(x1_ref[rows, :], nw_ref[...], sh_ref[...],
                                        sc_ref[...]).astype(BF16)

    chunk_starts = list(range(0, tm, rc))
    for ci, r0 in enumerate(chunk_starts):
        rows = slice(r0, r0 + rc)
        attn_rows = attn_ref[rows, :]
        pooled = []
        for j in range(ngroups):
            cols = slice(j * ncol, (j + 1) * ncol)
            o = jnp.dot(attn_rows, wout_ref[0:ATTN_WIDTH, cols], preferred_element_type=F32)
            x1_ref[rows, cols] = x_ref[rows, cols] + ga_ref[:, cols] * o
            pooled.append(pool_group(r0, j))
        pool_rows = jnp.concatenate(pooled, axis=1)
        for j in range(ngroups):
            cols = slice(j * ncol, (j + 1) * ncol)
            o = jnp.dot(pool_rows, wout_ref[ATTN_WIDTH:, cols], preferred_element_type=F32)
            x1_ref[rows, cols] += ga_ref[:, cols] * o
            if ci > 0:
                prenorm(chunk_starts[ci - 1] + j * (rc // ngroups), rc // ngroups)
    for j in range(ngroups):
        prenorm(chunk_starts[-1] + j * (rc // ngroups), rc // ngroups)


def _mixout(x, attn, u, w_out_bf16, pool_w_bf16, pool_scale, g_a, norm_w, shift, scale):
    tm = MIX_TM
    hb = tm // POOL_HALO
    nhalo = SEQ // POOL_HALO
    row = lambda i: (i, 0)
    fixed = lambda i: (0, 0)
    return pl.pallas_call(
        _mixout_kernel,
        grid=(SEQ // tm,),
        in_specs=[
            pl.BlockSpec((tm, D_MODEL), row),
            pl.BlockSpec((tm, ATTN_WIDTH), row),
            pl.BlockSpec((POOL_HALO, POOL_WIDTH), lambda i: (jnp.maximum(i * hb - 1, 0), 0)),
            pl.BlockSpec((tm, POOL_WIDTH), row),
            pl.BlockSpec((POOL_HALO, POOL_WIDTH), lambda i: (jnp.minimum((i + 1) * hb, nhalo - 1), 0)),
            pl.BlockSpec((D_MODEL, D_MODEL), fixed),
            pl.BlockSpec((len(POOL_WINDOWS), POOL_GROUP_DIM, POOL_GROUP_DIM), lambda i: (0, 0, 0)),
            pl.BlockSpec((1, POOL_WIDTH), fixed),
            pl.BlockSpec((1, D_MODEL), fixed),
            pl.BlockSpec((1, D_MODEL), fixed),
            pl.BlockSpec((1, D_MODEL), fixed),
            pl.BlockSpec((1, D_MODEL), fixed),
        ],
        out_specs=[pl.BlockSpec((tm, D_MODEL), row), pl.BlockSpec((tm, D_MODEL), row)],
        out_shape=[jax.ShapeDtypeStruct((SEQ, D_MODEL), F32),
                   jax.ShapeDtypeStruct((SEQ, D_MODEL), BF16)],
        scratch_shapes=[pltpu.VMEM((tm + 2 * POOL_HALO, POOL_WIDTH), F32)],
        compiler_params=_params("arbitrary"),
        name="mixout",
    )(x, attn, u, u, u, w_out_bf16, pool_w_bf16, pool_scale, g_a, norm_w, shift, scale)


def _mlp_kernel(hm_ref, x1r_ref, wup_ref, wdn_ref, gm_ref, fw_ref, o_ref):
    f = pl.program_id(1)
    last = MLP_NF - 1
    row_chunks = [slice(r0, r0 + MLP_RC) for r0 in range(0, MLP_TM, MLP_RC)]
    gm = gm_ref[...]

    def branch_out(rows):
        a = jnp.maximum(jnp.dot(hm_ref[rows, :], wup_ref[...], preferred_element_type=F32), 0.0)
        return gm * jnp.dot((a * a).astype(BF16), wdn_ref[...], preferred_element_type=F32)

    def final_norm(rows):
        x2 = o_ref[rows, :]
        y = x2 * lax.rsqrt(jnp.mean(x2 * x2, axis=-1, keepdims=True) + EPS)
        o_ref[rows, :] = y * fw_ref[...]

    @pl.when(f == 0)
    def _():
        for rows in row_chunks:
            o_ref[rows, :] = branch_out(rows)
        o_ref[0:MLP_XR, :] += x1r_ref[...]

    @pl.when(jnp.logical_and(f > 0, f < last))
    def _():
        x_rows = pl.ds(pl.multiple_of(f * MLP_XR, MLP_XR), MLP_XR)
        o_ref[x_rows, :] += x1r_ref[...]
        for rows in row_chunks:
            o_ref[rows, :] += branch_out(rows)

    @pl.when(f == last)
    def _():
        o_ref[last * MLP_XR:(last + 1) * MLP_XR, :] += x1r_ref[...]
        npiece = 4
        cw, rp = D_MODEL // npiece, MLP_RC // npiece
        for ci, rows in enumerate(row_chunks):
            a = jnp.maximum(jnp.dot(hm_ref[rows, :], wup_ref[...], preferred_element_type=F32), 0.0)
            a = (a * a).astype(BF16)
            for j in range(npiece):
                cols = slice(j * cw, (j + 1) * cw)
                o_ref[rows, cols] += gm[:, cols] * jnp.dot(a, wdn_ref[:, cols],
                                                           preferred_element_type=F32)
                if ci > 0:
                    r0 = row_chunks[ci - 1].start + j * rp
                    final_norm(slice(r0, r0 + rp))
        final_norm(row_chunks[-1])


def _mlp(hm, x1, w_up_bf16, w_dn_bf16, g_m, final_w):
    tm, tf = MLP_TM, MLP_TF
    return pl.pallas_call(
        _mlp_kernel,
        grid=(SEQ // tm, MLP_NF),
        in_specs=[
            pl.BlockSpec((tm, D_MODEL), lambda i, f: (i, 0)),
            pl.BlockSpec((MLP_XR, D_MODEL), lambda i, f: (i * MLP_NF + f, 0)),
            pl.BlockSpec((None, D_MODEL, tf), lambda i, f: (f, 0, 0)),
            pl.BlockSpec((tf, D_MODEL), lambda i, f: (f, 0)),
            pl.BlockSpec((1, D_MODEL), lambda i, f: (0, 0)),
            pl.BlockSpec((1, D_MODEL), lambda i, f: (0, 0)),
        ],
        out_specs=pl.BlockSpec((tm, D_MODEL), lambda i, f: (i, 0)),
        out_shape=jax.ShapeDtypeStruct((SEQ, D_MODEL), F32),
        compiler_params=_params("arbitrary", "arbitrary"),
        name="mlp",
    )(hm, x1, w_up_bf16, w_dn_bf16, g_m, final_w)


def kernel(x, c, ctx, c_ctx, norm_attn_w, norm_mlp_w, w_ada, b_ada, w_in, attn_sink,
           pool_w, pool_scale, w_out, w_mlp_up, w_mlp_down, final_norm_w):
    assert x.shape == (1, SEQ, D_MODEL) and ctx.shape == (1, CTX_LEN, D_MODEL)
    assert w_ada.shape[0] == 1, "single layer"
    x2d, ctx2d = x[0], ctx[0]

    cond = jnp.concatenate([c, c_ctx[None, :], jnp.zeros((SUBLANES - 2, D_MODEL), F32)], axis=0)
    b_row = b_ada[0][None, :]
    n_early = 2 * D_MODEL
    mod_a = _ada(cond, w_ada[0], b_row, n_early)
    chunk = lambda m, r, j: m[r:r + 1, j * D_MODEL:(j + 1) * D_MODEL]
    sh_a, sc_a = chunk(mod_a, 0, 0), chunk(mod_a, 0, 1)
    csh_a, csc_a = chunk(mod_a, 1, 0), chunk(mod_a, 1, 1)

    nw_a = norm_attn_w[0][None, :]
    tables = tuple(jnp.asarray(t) for t in _rope_tables())
    qt, k, vt, u = _inproj(x2d, nw_a, sh_a, sc_a, w_in[0], tables)
    kc, vtc = _ctx_kv(ctx2d, nw_a, csh_a, csc_a, w_in[0])

    sink_row = jnp.repeat(attn_sink[0].astype(F32), BLOCK)[None, :]
    attn, w_out_b, w_up_b, w_dn_b, mod_b = _attn(
        qt, k, vt, kc, vtc, sink_row, w_out[0], w_mlp_up[0], w_mlp_down[0],
        cond, w_ada[0], b_row, n_early)
    g_a, sh_m, sc_m, g_m = (chunk(mod_b, 0, j) for j in range(N_MOD - 2))

    x1, hm = _mixout(x2d, attn, u, w_out_b, pool_w[0].astype(BF16),
                     pool_scale[0][None, :], g_a, norm_mlp_w[0][None, :], sh_m, sc_m)
    out = _mlp(hm, x1, w_up_b, w_dn_b, g_m, final_norm_w[None, :])
    return out[None]
```

```python
import functools

import numpy as np
import jax
import jax.numpy as jnp
from jax import lax
from jax.experimental import pallas as pl
from jax.experimental.pallas import tpu as pltpu

F32 = jnp.float32
BF16 = jnp.bfloat16

D_MODEL = 2048
SEQ = 8192
CTX_LEN = 256
GRID_W = 64
HEAD_DIM = 64
N_Q_HEADS = 16
N_KV_HEADS = 4
GQA = N_Q_HEADS // N_KV_HEADS
ATTN_WIDTH = N_Q_HEADS * HEAD_DIM
KV_WIDTH = N_KV_HEADS * HEAD_DIM
POOL_WINDOWS = (2, 4, 8, 16)
POOL_WIDTH = D_MODEL - ATTN_WIDTH
POOL_GROUP_DIM = POOL_WIDTH // len(POOL_WINDOWS)
IN_WIDTH = ATTN_WIDTH + 2 * KV_WIDTH + POOL_WIDTH
D_FF = 4 * D_MODEL
BLOCK = 128
ROPE_BASE = 10000.0
N_MOD = 6
EPS = 1e-6
NEG_INF = -1e30
LOG2E = 1.4426950408889634
Q_SCALE = HEAD_DIM ** -0.5 * LOG2E

LANES = 128
SUBLANES = 8
POOL_HALO = 8
VMEM_LIMIT = 56 * 1024 * 1024

ADA_TN = 512
INPROJ_TM = 512
ATTN_TQ = 512
CAST_JOB_ELEMS = 64 * 1024
MIX_TM = 512
MIX_CHUNK = 128
MLP_TM = 1024
MLP_TF = 1024
MLP_RC = 512
MLP_NF = D_FF // MLP_TF
MLP_XR = MLP_TM // MLP_NF


def _params(*sem):
    return pltpu.CompilerParams(dimension_semantics=sem, vmem_limit_bytes=VMEM_LIMIT)


def _rms_modulate(x, norm_w, shift, scale):
    gain = norm_w * (1.0 + scale)
    y = x * lax.rsqrt(jnp.mean(x * x, axis=-1, keepdims=True) + EPS)
    return y * gain + shift


def _ada_block(cond_ref, w_ref, b_ref):
    cnd = cond_ref[...]
    s = cnd * (1.0 / (1.0 + jnp.exp(-cnd)))
    return jnp.dot(s, w_ref[...], preferred_element_type=F32) + b_ref[...]


def _ada_kernel(cond_ref, w_ref, b_ref, o_ref):
    o_ref[...] = _ada_block(cond_ref, w_ref, b_ref)


def _ada(cond, w_ada, b_ada, n):
    rows = cond.shape[0]
    return pl.pallas_call(
        _ada_kernel,
        grid=(n // ADA_TN,),
        in_specs=[
            pl.BlockSpec((rows, D_MODEL), lambda j: (0, 0)),
            pl.BlockSpec((D_MODEL, ADA_TN), lambda j: (0, j)),
            pl.BlockSpec((1, ADA_TN), lambda j: (0, j)),
        ],
        out_specs=pl.BlockSpec((rows, ADA_TN), lambda j: (0, j)),
        out_shape=jax.ShapeDtypeStruct((rows, n), F32),
        compiler_params=_params("arbitrary"),
        name="ada",
    )(cond, w_ada, b_ada)


def _rope_tables():
    half = HEAD_DIM // 2
    inv_freq = ROPE_BASE ** (-np.arange(0, half, 2, dtype=np.float64) / half)
    t = np.arange(SEQ)
    row = (t // GRID_W).astype(np.float64)
    col = (t % GRID_W).astype(np.float64)
    ang_r = row[:, None] * inv_freq[None, :]
    ang_c = col[:, None] * inv_freq[None, :]
    cos_t = Q_SCALE * np.concatenate([np.cos(ang_r), np.cos(ang_c)], axis=1).T
    sin_t = Q_SCALE * np.concatenate([np.sin(ang_r), np.sin(ang_c)], axis=1).T
    cos_l = np.concatenate([np.cos(ang_r), np.cos(ang_r), np.cos(ang_c), np.cos(ang_c)], axis=1)
    zeros = np.zeros_like(ang_r)
    s_up = np.concatenate([-np.sin(ang_r), zeros, -np.sin(ang_c), zeros], axis=1)
    s_dn = np.concatenate([zeros, np.sin(ang_r), zeros, np.sin(ang_c)], axis=1)
    tile2 = lambda a: np.concatenate([a, a], axis=1).astype(np.float32)
    return (cos_t.astype(np.float32), sin_t.astype(np.float32),
            tile2(cos_l), tile2(s_up), tile2(s_dn))


def _inproj_kernel(x_ref, nw_ref, sh_ref, sc_ref, w_ref, cos_t_ref, sin_t_ref,
                   cos_l_ref, sup_ref, sdn_ref, qt_ref, k_ref, vt_ref, u_ref, wb_ref):
    @pl.when(pl.program_id(0) == 0)
    def _():
        for c0 in range(0, IN_WIDTH, 2 * LANES):
            wb_ref[:, c0:c0 + 2 * LANES] = w_ref[:, c0:c0 + 2 * LANES].astype(BF16)

    h = _rms_modulate(x_ref[...], nw_ref[...], sh_ref[...], sc_ref[...]).astype(BF16)
    p = jnp.dot(h, wb_ref[...], preferred_element_type=F32)

    qt = p[:, :ATTN_WIDTH].T
    cr, cc = cos_t_ref[0:16, :], cos_t_ref[16:32, :]
    sr, sc = sin_t_ref[0:16, :], sin_t_ref[16:32, :]
    pieces = []
    for hd in range(N_Q_HEADS):
        b = hd * HEAD_DIM
        x1r, x2r = qt[b:b + 16, :], qt[b + 16:b + 32, :]
        x1c, x2c = qt[b + 32:b + 48, :], qt[b + 48:b + 64, :]
        pieces += [x1r * cr - x2r * sr, x1r * sr + x2r * cr,
                   x1c * cc - x2c * sc, x1c * sc + x2c * cc]
    qt_ref[...] = jnp.concatenate(pieces, axis=0).astype(BF16)

    k = p[:, ATTN_WIDTH:ATTN_WIDTH + KV_WIDTH]
    cos_l, s_up, s_dn = cos_l_ref[...], sup_ref[...], sdn_ref[...]
    kparts = []
    for g in range(KV_WIDTH // LANES):
        kg = k[:, g * LANES:(g + 1) * LANES]
        kparts.append(kg * cos_l + pltpu.roll(kg, LANES - 16, 1) * s_up
                      + pltpu.roll(kg, 16, 1) * s_dn)
    k_ref[...] = jnp.concatenate(kparts, axis=1).astype(BF16)

    vt_ref[...] = p[:, ATTN_WIDTH + KV_WIDTH:ATTN_WIDTH + 2 * KV_WIDTH].T.astype(BF16)
    u_ref[...] = p[:, ATTN_WIDTH + 2 * KV_WIDTH:]


def _inproj(x, norm_w, shift, scale, w_in, tables):
    cos_t, sin_t, cos_l, s_up, s_dn = tables
    tm = INPROJ_TM
    row = lambda i: (i, 0)
    colb = lambda i: (0, i)
    fixed = lambda i: (0, 0)
    return pl.pallas_call(
        _inproj_kernel,
        grid=(SEQ // tm,),
        in_specs=[
            pl.BlockSpec((tm, D_MODEL), row),
            pl.BlockSpec((1, D_MODEL), fixed),
            pl.BlockSpec((1, D_MODEL), fixed),
            pl.BlockSpec((1, D_MODEL), fixed),
            pl.BlockSpec((D_MODEL, IN_WIDTH), fixed, pipeline_mode=pl.Buffered(1)),
            pl.BlockSpec((32, tm), colb),
            pl.BlockSpec((32, tm), colb),
            pl.BlockSpec((tm, LANES), row),
            pl.BlockSpec((tm, LANES), row),
            pl.BlockSpec((tm, LANES), row),
        ],
        out_specs=[
            pl.BlockSpec((ATTN_WIDTH, tm), colb),
            pl.BlockSpec((tm, KV_WIDTH), row),
            pl.BlockSpec((KV_WIDTH, tm), colb),
            pl.BlockSpec((tm, POOL_WIDTH), row),
        ],
        out_shape=[
            jax.ShapeDtypeStruct((ATTN_WIDTH, SEQ), BF16),
            jax.ShapeDtypeStruct((SEQ, KV_WIDTH), BF16),
            jax.ShapeDtypeStruct((KV_WIDTH, SEQ), BF16),
            jax.ShapeDtypeStruct((SEQ, POOL_WIDTH), F32),
        ],
        scratch_shapes=[pltpu.VMEM((D_MODEL, IN_WIDTH), BF16)],
        compiler_params=_params("arbitrary"),
        name="inproj",
    )(x, norm_w, shift, scale, w_in, cos_t, sin_t, cos_l, s_up, s_dn)


def _ctx_kv_kernel(x_ref, nw_ref, sh_ref, sc_ref, w_ref, k_ref, vt_ref):
    h = _rms_modulate(x_ref[...], nw_ref[...], sh_ref[...], sc_ref[...]).astype(BF16)
    p = jnp.dot(h, w_ref[...].astype(BF16), preferred_element_type=F32)
    k_ref[...] = p[:, :KV_WIDTH].astype(BF16)
    vt_ref[...] = p[:, KV_WIDTH:].T.astype(BF16)


def _ctx_kv(ctx, norm_w, shift, scale, w_in):
    fixed = lambda i: (0, 0)
    kv_block = ATTN_WIDTH // (2 * KV_WIDTH)
    return pl.pallas_call(
        _ctx_kv_kernel,
        grid=(1,),
        in_specs=[
            pl.BlockSpec((CTX_LEN, D_MODEL), fixed),
            pl.BlockSpec((1, D_MODEL), fixed),
            pl.BlockSpec((1, D_MODEL), fixed),
            pl.BlockSpec((1, D_MODEL), fixed),
            pl.BlockSpec((D_MODEL, 2 * KV_WIDTH), lambda i: (0, kv_block)),
        ],
        out_specs=[
            pl.BlockSpec((CTX_LEN, KV_WIDTH), fixed),
            pl.BlockSpec((KV_WIDTH, CTX_LEN), fixed),
        ],
        out_shape=[
            jax.ShapeDtypeStruct((CTX_LEN, KV_WIDTH), BF16),
            jax.ShapeDtypeStruct((KV_WIDTH, CTX_LEN), BF16),
        ],
        compiler_params=_params("arbitrary"),
        name="ctx_kv",
    )(ctx, norm_w, shift, scale, w_in)


def _attn_kernel(qt_ref, kp_ref, km_ref, kn_ref, vtp_ref, vtm_ref, vtn_ref,
                 kc_ref, vtc_ref, sink_ref, wout_ref, wup_ref, wdn_ref,
                 cond_ref, wada_ref, bada_ref,
                 o_ref, wout_b_ref, wup_b_ref, wdn_b_ref, mod_ref):
    mod_ref[...] = _ada_block(cond_ref, wada_ref, bada_ref)

    i = pl.program_id(0)
    nsub = ATTN_TQ // BLOCK
    nblk = SEQ // BLOCK
    gw = GQA * BLOCK
    pw = 2 * BLOCK
    nkeys = 3 * BLOCK + CTX_LEN

    key_j = lax.broadcasted_iota(jnp.int32, (BLOCK, gw), 0)
    qry_i = lax.broadcasted_iota(jnp.int32, (BLOCK, gw), 1) % BLOCK
    neg = jnp.full((BLOCK, gw), NEG_INF, F32)
    zero = jnp.zeros((BLOCK, gw), F32)
    band_prev = jnp.where(key_j >= qry_i, zero, neg)
    band_next = jnp.where(key_j <= qry_i, zero, neg)

    kc = kc_ref[...]
    vtc = vtc_ref[...]
    ones_rows = jnp.ones((2 * SUBLANES, nkeys), BF16)

    windows = []
    for sb in range(nsub):
        n = i * nsub + sb
        lo, hi = sb * BLOCK, (sb + 1) * BLOCK
        k_prev = kp_ref[...] if sb == 0 else km_ref[lo - BLOCK:lo, :]
        k_next = kn_ref[...] if sb == nsub - 1 else km_ref[hi:hi + BLOCK, :]
        vt_prev = vtp_ref[...] if sb == 0 else vtm_ref[:, lo - BLOCK:lo]
        vt_next = vtn_ref[...] if sb == nsub - 1 else vtm_ref[:, hi:hi + BLOCK]
        windows.append(dict(
            kwin=jnp.concatenate([k_prev, km_ref[lo:hi, :], k_next, kc], axis=0),
            vtwin=jnp.concatenate([vt_prev, vtm_ref[:, lo:hi], vt_next, vtc], axis=1),
            bias_prev=band_prev + jnp.where(n > 0, 0.0, NEG_INF),
            bias_next=band_next + jnp.where(n < nblk - 1, 0.0, NEG_INF)))

    units = [(sb, h, half) for sb in range(nsub) for h in range(N_KV_HEADS)
             for half in range(GQA // 2)]

    def scores(sb, h):
        win = windows[sb]
        lo, hi = sb * BLOCK, (sb + 1) * BLOCK
        qh = jnp.concatenate(
            [qt_ref[(GQA * h + g) * HEAD_DIM:(GQA * h + g + 1) * HEAD_DIM, lo:hi]
             for g in range(GQA)], axis=1)
        blocks = []
        if h > 0:
            blocks.append(jnp.zeros((h * HEAD_DIM, gw), BF16))
        blocks.append(qh)
        if h < N_KV_HEADS - 1:
            blocks.append(jnp.zeros(((N_KV_HEADS - 1 - h) * HEAD_DIM, gw), BF16))
        s = jnp.dot(win["kwin"], jnp.concatenate(blocks, axis=0),
                    preferred_element_type=F32)
        s = jnp.concatenate([s[0:BLOCK] + win["bias_prev"], s[BLOCK:2 * BLOCK],
                             s[2 * BLOCK:3 * BLOCK] + win["bias_next"], s[3 * BLOCK:]], axis=0)
        return s, jnp.max(s, axis=0, keepdims=True)

    def probs(unit, s, smax):
        sb, h, half = unit
        hd0 = GQA * h + 2 * half
        cols = slice(half * pw, (half + 1) * pw)
        sink = sink_ref[:, hd0 * BLOCK:(hd0 + 2) * BLOCK] * LOG2E
        m = jnp.maximum(smax[:, cols], sink)
        return jnp.exp2(s[:, cols] - m).astype(BF16), jnp.exp2(sink - m)

    def finish(unit, e, e_sink):
        sb, h, half = unit
        lo, hi = sb * BLOCK, (sb + 1) * BLOCK
        hd0 = GQA * h + 2 * half
        vt_ones = jnp.concatenate(
            [windows[sb]["vtwin"][h * HEAD_DIM:(h + 1) * HEAD_DIM, :], ones_rows], axis=0)
        o = jnp.dot(vt_ones, e, preferred_element_type=F32)
        denom = o[HEAD_DIM:HEAD_DIM + 1, :] + e_sink
        o = o[:HEAD_DIM, :] * (1.0 / denom)
        blk = jnp.concatenate([o[:, 0:BLOCK], o[:, BLOCK:2 * BLOCK]], axis=0)
        o_ref[lo:hi, hd0 * HEAD_DIM:(hd0 + 2) * HEAD_DIM] = blk.T.astype(BF16)

    cast_jobs = []
    for src, dst in ((wout_ref, wout_b_ref), (wup_ref, wup_b_ref), (wdn_ref, wdn_b_ref)):
        wc = max(LANES, CAST_JOB_ELEMS // src.shape[0])
        cast_jobs += [(src, dst, c0) + (wc,) for c0 in range(0, src.shape[1], wc)]

    def cast_piece(idx, n):
        for src, dst, c0, wc in cast_jobs[idx * len(cast_jobs) // n:(idx + 1) * len(cast_jobs) // n]:
            val = src[:, c0:c0 + wc].astype(BF16)
            if len(dst.shape) == 3:
                tf = dst.shape[2]
                dst[c0 // tf, :, c0 % tf:c0 % tf + wc] = val
            else:
                dst[:, c0:c0 + wc] = val

    nu = len(units)
    s_vals, p_vals = {}, {}
    for step in range(nu + 2):
        if step < nu and units[step][2] == 0:
            s_vals[units[step][:2]] = scores(*units[step][:2])
        if 1 <= step <= nu:
            unit = units[step - 1]
            p_vals[step - 1] = probs(unit, *s_vals[unit[:2]])
            if unit[2] == GQA // 2 - 1:
                del s_vals[unit[:2]]
            cast_piece(step - 1, nu)
        if step >= 2:
            finish(units[step - 2], *p_vals.pop(step - 2))


def _attn(qt, k, vt, kc, vtc, sink_row, w_out, w_up, w_dn, cond, w_ada, b_ada, ada_col0):
    tq = ATTN_TQ
    r = tq // BLOCK
    nblk = SEQ // BLOCK
    nsteps = SEQ // tq
    ada_tn = (w_ada.shape[1] - ada_col0) // nsteps
    ada_b0 = ada_col0 // ada_tn
    fixed = lambda i: (0, 0)
    row = lambda i: (i, 0)
    slab = lambda w: pl.BlockSpec((w.shape[0] // nsteps, w.shape[1]), row)
    bf16_like = lambda w: jax.ShapeDtypeStruct(w.shape, BF16)
    prev = lambda i: jnp.maximum(i * r - 1, 0)
    nxt = lambda i: jnp.minimum((i + 1) * r, nblk - 1)
    return pl.pallas_call(
        _attn_kernel,
        grid=(SEQ // tq,),
        in_specs=[
            pl.BlockSpec((ATTN_WIDTH, tq), lambda i: (0, i)),
            pl.BlockSpec((BLOCK, KV_WIDTH), lambda i: (prev(i), 0)),
            pl.BlockSpec((tq, KV_WIDTH), lambda i: (i, 0)),
            pl.BlockSpec((BLOCK, KV_WIDTH), lambda i: (nxt(i), 0)),
            pl.BlockSpec((KV_WIDTH, BLOCK), lambda i: (0, prev(i))),
            pl.BlockSpec((KV_WIDTH, tq), lambda i: (0, i)),
            pl.BlockSpec((KV_WIDTH, BLOCK), lambda i: (0, nxt(i))),
            pl.BlockSpec((CTX_LEN, KV_WIDTH), fixed),
            pl.BlockSpec((KV_WIDTH, CTX_LEN), fixed),
            pl.BlockSpec((1, N_KV_HEADS * GQA * BLOCK), fixed),
            slab(w_out), slab(w_up), slab(w_dn),
            pl.BlockSpec(cond.shape, fixed),
            pl.BlockSpec((D_MODEL, ada_tn), lambda i: (0, ada_b0 + i)),
            pl.BlockSpec((1, ada_tn), lambda i: (0, ada_b0 + i)),
        ],
        out_specs=[pl.BlockSpec((tq, ATTN_WIDTH), row), slab(w_out),
                   pl.BlockSpec((MLP_NF, w_up.shape[0] // nsteps, MLP_TF), lambda i: (0, i, 0)),
                   slab(w_dn),
                   pl.BlockSpec((cond.shape[0], ada_tn), lambda i: (0, i))],
        out_shape=[jax.ShapeDtypeStruct((SEQ, ATTN_WIDTH), BF16),
                   bf16_like(w_out),
                   jax.ShapeDtypeStruct((MLP_NF, w_up.shape[0], MLP_TF), BF16),
                   bf16_like(w_dn),
                   jax.ShapeDtypeStruct((cond.shape[0], w_ada.shape[1] - ada_col0), F32)],
        compiler_params=_params("arbitrary"),
        name="attn",
    )(qt, k, k, k, vt, vt, vt, kc, vtc, sink_row, w_out, w_up, w_dn, cond, w_ada, b_ada)


def _mixout_kernel(x_ref, attn_ref, up_ref, um_ref, un_ref, wout_ref, pw_ref, ps_ref,
                   ga_ref, nw_ref, sh_ref, sc_ref, x1_ref, hm_ref, ubuf):
    i = pl.program_id(0)
    tm = MIX_TM
    nsteps = SEQ // tm
    ubuf[0:POOL_HALO, :] = jnp.where(i > 0, up_ref[...], 0.0)
    ubuf[POOL_HALO:POOL_HALO + tm, :] = um_ref[...]
    ubuf[POOL_HALO + tm:2 * POOL_HALO + tm, :] = jnp.where(i < nsteps - 1, un_ref[...], 0.0)

    rc = MIX_CHUNK
    n = rc + 2 * POOL_HALO
    c = POOL_HALO
    ngroups = len(POOL_WINDOWS)
    ncol = D_MODEL // ngroups

    def pool_group(r0, g):
        w = POOL_WINDOWS[g]
        c0, c1 = g * POOL_GROUP_DIM, (g + 1) * POOL_GROUP_DIM
        e = ubuf[r0:r0 + n, c0:c1]
        if w == 2:
            win = e[c - 1:c - 1 + rc] + e[c:c + rc]
        else:
            f2 = e[0:n - 1] + e[1:n]
            if w == 4:
                win = f2[c - 2:c - 2 + rc] + f2[c:c + rc]
            else:
                f4 = f2[0:n - 3] + f2[2:n - 1]
                if w == 8:
                    win = f4[c - 4:c - 4 + rc] + f4[c:c + rc]
                else:
                    f8 = f4[0:n - 7] + f4[4:n - 3]
                    win = f8[c - 8:c - 8 + rc] + f8[c:c + rc]
        t = i * tm + r0 + lax.broadcasted_iota(jnp.int32, (rc, 1), 0)
        lo = jnp.clip(t - w // 2, 0, SEQ)
        hi = jnp.clip(t - w // 2 + w, 0, SEQ)
        cnt = (hi - lo).astype(F32)
        pg = win / cnt - um_ref[r0:r0 + rc, c0:c1]
        mixed = jnp.dot(pg.astype(BF16), pw_ref[g], preferred_element_type=F32)
        return (mixed * ps_ref[:, c0:c1]).astype(BF16)

    def prenorm(r0, nrows):
        rows = slice(r0, r0 + nrows)
        hm_ref[rows, :] = _rms_modulate(x1_ref[rows, :], nw_ref[...], sh_ref[...],
                                        sc_ref[...]).astype(BF16)

    chunk_starts = list(range(0, tm, rc))
    nchunks = len(chunk_starts)
    pooled = {}
    for s in range(nchunks + 2):
        if s < nchunks:
            pooled[s] = []
        if 1 <= s <= nchunks:
            rows = slice(chunk_starts[s - 1], chunk_starts[s - 1] + rc)
            mix_rows = jnp.concatenate([attn_ref[rows, :]] + pooled.pop(s - 1), axis=1)
        for j in range(ngroups):
            if s < nchunks:
                pooled[s].append(pool_group(chunk_starts[s], j))
            if 1 <= s <= nchunks:
                cols = slice(j * ncol, (j + 1) * ncol)
                o = jnp.dot(mix_rows, wout_ref[:, cols], preferred_element_type=F32)
                x1_ref[rows, cols] = x_ref[rows, cols] + ga_ref[:, cols] * o
            if s >= 2:
                prenorm(chunk_starts[s - 2] + j * (rc // ngroups), rc // ngroups)


def _mixout(x, attn, u, w_out_bf16, pool_w_bf16, pool_scale, g_a, norm_w, shift, scale):
    tm = MIX_TM
    hb = tm // POOL_HALO
    nhalo = SEQ // POOL_HALO
    row = lambda i: (i, 0)
    fixed = lambda i: (0, 0)
    return pl.pallas_call(
        _mixout_kernel,
        grid=(SEQ // tm,),
        in_specs=[
            pl.BlockSpec((tm, D_MODEL), row),
            pl.BlockSpec((tm, ATTN_WIDTH), row),
            pl.BlockSpec((POOL_HALO, POOL_WIDTH), lambda i: (jnp.maximum(i * hb - 1, 0), 0)),
            pl.BlockSpec((tm, POOL_WIDTH), row),
            pl.BlockSpec((POOL_HALO, POOL_WIDTH), lambda i: (jnp.minimum((i + 1) * hb, nhalo - 1), 0)),
            pl.BlockSpec((D_MODEL, D_MODEL), fixed, pipeline_mode=pl.Buffered(1)),
            pl.BlockSpec((len(POOL_WINDOWS), POOL_GROUP_DIM, POOL_GROUP_DIM), lambda i: (0, 0, 0)),
            pl.BlockSpec((1, POOL_WIDTH), fixed),
            pl.BlockSpec((1, D_MODEL), fixed),
            pl.BlockSpec((1, D_MODEL), fixed),
            pl.BlockSpec((1, D_MODEL), fixed),
            pl.BlockSpec((1, D_MODEL), fixed),
        ],
        out_specs=[pl.BlockSpec((tm, D_MODEL), row), pl.BlockSpec((tm, D_MODEL), row)],
        out_shape=[jax.ShapeDtypeStruct((SEQ, D_MODEL), F32),
                   jax.ShapeDtypeStruct((SEQ, D_MODEL), BF16)],
        scratch_shapes=[pltpu.VMEM((tm + 2 * POOL_HALO, POOL_WIDTH), F32)],
        compiler_params=_params("arbitrary"),
        name="mixout",
    )(x, attn, u, u, u, w_out_bf16, pool_w_bf16, pool_scale, g_a, norm_w, shift, scale)


def _mlp_kernel(hm_ref, x1r_ref, wup_ref, wdn_ref, gm_ref, fw_ref, o_ref):
    f = pl.program_id(1)
    last = MLP_NF - 1
    row_chunks = [slice(r0, r0 + MLP_RC) for r0 in range(0, MLP_TM, MLP_RC)]
    gm = gm_ref[...]

    def branch_out(rows):
        a = jnp.maximum(jnp.dot(hm_ref[rows, :], wup_ref[...], preferred_element_type=F32), 0.0)
        return gm * jnp.dot((a * a).astype(BF16), wdn_ref[...], preferred_element_type=F32)

    def final_norm(rows):
        x2 = o_ref[rows, :]
        y = x2 * lax.rsqrt(jnp.mean(x2 * x2, axis=-1, keepdims=True) + EPS)
        o_ref[rows, :] = y * fw_ref[...]

    @pl.when(f == 0)
    def _():
        for rows in row_chunks:
            o_ref[rows, :] = branch_out(rows)
        o_ref[0:MLP_XR, :] += x1r_ref[...]

    @pl.when(jnp.logical_and(f > 0, f < last))
    def _():
        x_rows = pl.ds(pl.multiple_of(f * MLP_XR, MLP_XR), MLP_XR)
        o_ref[x_rows, :] += x1r_ref[...]
        for rows in row_chunks:
            o_ref[rows, :] += branch_out(rows)

    @pl.when(f == last)
    def _():
        o_ref[last * MLP_XR:(last + 1) * MLP_XR, :] += x1r_ref[...]
        npiece = 4
        cw, rp = D_MODEL // npiece, MLP_RC // npiece
        for ci, rows in enumerate(row_chunks):
            a = jnp.maximum(jnp.dot(hm_ref[rows, :], wup_ref[...], preferred_element_type=F32), 0.0)
            a = (a * a).astype(BF16)
            for j in range(npiece):
                cols = slice(j * cw, (j + 1) * cw)
                o_ref[rows, cols] += gm[:, cols] * jnp.dot(a, wdn_ref[:, cols],
                                                           preferred_element_type=F32)
                if ci > 0:
                    r0 = row_chunks[ci - 1].start + j * rp
                    final_norm(slice(r0, r0 + rp))
        final_norm(row_chunks[-1])


def _mlp(hm, x1, w_up_bf16, w_dn_bf16, g_m, final_w):
    tm, tf = MLP_TM, MLP_TF
    return pl.pallas_call(
        _mlp_kernel,
        grid=(SEQ // tm, MLP_NF),
        in_specs=[
            pl.BlockSpec((tm, D_MODEL), lambda i, f: (i, 0)),
            pl.BlockSpec((MLP_XR, D_MODEL), lambda i, f: (i * MLP_NF + f, 0)),
            pl.BlockSpec((None, D_MODEL, tf), lambda i, f: (f, 0, 0)),
            pl.BlockSpec((tf, D_MODEL), lambda i, f: (f, 0)),
            pl.BlockSpec((1, D_MODEL), lambda i, f: (0, 0)),
            pl.BlockSpec((1, D_MODEL), lambda i, f: (0, 0)),
        ],
        out_specs=pl.BlockSpec((tm, D_MODEL), lambda i, f: (i, 0)),
        out_shape=jax.ShapeDtypeStruct((SEQ, D_MODEL), F32),
        compiler_params=_params("arbitrary", "arbitrary"),
        name="mlp",
    )(hm, x1, w_up_bf16, w_dn_bf16, g_m, final_w)


def kernel(x, c, ctx, c_ctx, norm_attn_w, norm_mlp_w, w_ada, b_ada, w_in, attn_sink,
           pool_w, pool_scale, w_out, w_mlp_up, w_mlp_down, final_norm_w):
    assert x.shape == (1, SEQ, D_MODEL) and ctx.shape == (1, CTX_LEN, D_MODEL)
    assert w_ada.shape[0] == 1, "single layer"
    x2d, ctx2d = x[0], ctx[0]

    cond = jnp.concatenate([c, c_ctx[None, :], jnp.zeros((SUBLANES - 2, D_MODEL), F32)], axis=0)
    b_row = b_ada[0][None, :]
    n_early = 2 * D_MODEL
    mod_a = _ada(cond, w_ada[0], b_row, n_early)
    chunk = lambda m, r, j: m[r:r + 1, j * D_MODEL:(j + 1) * D_MODEL]
    sh_a, sc_a = chunk(mod_a, 0, 0), chunk(mod_a, 0, 1)
    csh_a, csc_a = chunk(mod_a, 1, 0), chunk(mod_a, 1, 1)

    nw_a = norm_attn_w[0][None, :]
    tables = tuple(jnp.asarray(t) for t in _rope_tables())
    qt, k, vt, u = _inproj(x2d, nw_a, sh_a, sc_a, w_in[0], tables)
    kc, vtc = _ctx_kv(ctx2d, nw_a, csh_a, csc_a, w_in[0])

    sink_row = jnp.repeat(attn_sink[0].astype(F32), BLOCK)[None, :]
    attn, w_out_b, w_up_b, w_dn_b, mod_b = _attn(
        qt, k, vt, kc, vtc, sink_row, w_out[0], w_mlp_up[0], w_mlp_down[0],
        cond, w_ada[0], b_row, n_early)
    g_a, sh_m, sc_m, g_m = (chunk(mod_b, 0, j) for j in range(N_MOD - 2))

    x1, hm = _mixout(x2d, attn, u, w_out_b, pool_w[0].astype(BF16),
                     pool_scale[0][None, :], g_a, norm_mlp_w[0][None, :], sh_m, sc_m)
    out = _mlp(hm, x1, w_up_b, w_dn_b, g_m, final_norm_w[None, :])
    return out[None]
```

```python
import functools

import numpy as np
import jax
import jax.numpy as jnp
from jax import lax
from jax.experimental import pallas as pl
from jax.experimental.pallas import tpu as pltpu

F32 = jnp.float32
BF16 = jnp.bfloat16

D_MODEL = 2048
SEQ = 8192
CTX_LEN = 256
GRID_W = 64
HEAD_DIM = 64
N_Q_HEADS = 16
N_KV_HEADS = 4
GQA = N_Q_HEADS // N_KV_HEADS
ATTN_WIDTH = N_Q_HEADS * HEAD_DIM
KV_WIDTH = N_KV_HEADS * HEAD_DIM
POOL_WINDOWS = (2, 4, 8, 16)
POOL_WIDTH = D_MODEL - ATTN_WIDTH
POOL_GROUP_DIM = POOL_WIDTH // len(POOL_WINDOWS)
IN_WIDTH = ATTN_WIDTH + 2 * KV_WIDTH + POOL_WIDTH
D_FF = 4 * D_MODEL
BLOCK = 128
ROPE_BASE = 10000.0
N_MOD = 6
EPS = 1e-6
NEG_INF = -1e30
LOG2E = 1.4426950408889634
Q_SCALE = HEAD_DIM ** -0.5 * LOG2E

LANES = 128
SUBLANES = 8
POOL_HALO = 8
LATENT_ROW = slice(0, 1)
CONTEXT_ROW = slice(1, 2)
VMEM_LIMIT = 56 * 1024 * 1024

ADA_TN = 512
INPROJ_TM = 512
ATTN_TQ = 512
CAST_JOB_ELEMS = 64 * 1024
MIX_TM = 512
MIX_CHUNK = 256
MLP_TM = 1024
MLP_TF = 1024
MLP_RC = 512
MLP_NF = D_FF // MLP_TF
MLP_XR = MLP_TM // MLP_NF


def _params(*sem):
    return pltpu.CompilerParams(dimension_semantics=sem, vmem_limit_bytes=VMEM_LIMIT)


def _mod_spec(chunk):
    return pl.BlockSpec((SUBLANES, D_MODEL), lambda *_: (0, chunk))


def _rms_modulate(x, norm_w, shift, scale):
    gain = norm_w * (1.0 + scale)
    y = x * lax.rsqrt(jnp.mean(x * x, axis=-1, keepdims=True) + EPS)
    return y * gain + shift


def _ada_block(c_ref, cctx_ref, w_ref, b_ref):
    cnd = jnp.concatenate([c_ref[...], cctx_ref[...], jnp.zeros((SUBLANES - 2, D_MODEL), F32)], axis=0)
    s = cnd * (1.0 / (1.0 + jnp.exp(-cnd)))
    return jnp.dot(s, w_ref[...], preferred_element_type=F32) + b_ref[...]


def _ada_kernel(c_ref, cctx_ref, w_ref, b_ref, o_ref):
    o_ref[...] = _ada_block(c_ref, cctx_ref, w_ref, b_ref)


def _ada(c, c_ctx, w_ada, b_ada, n):
    return pl.pallas_call(
        _ada_kernel,
        grid=(n // ADA_TN,),
        in_specs=[
            pl.BlockSpec((1, D_MODEL), lambda j: (0, 0)),
            pl.BlockSpec((1, D_MODEL), lambda j: (0, 0)),
            pl.BlockSpec((D_MODEL, ADA_TN), lambda j: (0, j)),
            pl.BlockSpec((1, ADA_TN), lambda j: (0, j)),
        ],
        out_specs=pl.BlockSpec((SUBLANES, ADA_TN), lambda j: (0, j)),
        out_shape=jax.ShapeDtypeStruct((SUBLANES, n), F32),
        compiler_params=_params("arbitrary"),
        name="ada",
    )(c, c_ctx, w_ada, b_ada)


def _rope_tables():
    half = HEAD_DIM // 2
    inv_freq = ROPE_BASE ** (-np.arange(0, half, 2, dtype=np.float64) / half)
    t = np.arange(SEQ)
    row = (t // GRID_W).astype(np.float64)
    col = (t % GRID_W).astype(np.float64)
    ang_r = row[:, None] * inv_freq[None, :]
    ang_c = col[:, None] * inv_freq[None, :]
    cos_t = Q_SCALE * np.concatenate([np.cos(ang_r), np.cos(ang_c)], axis=1).T
    sin_t = Q_SCALE * np.concatenate([np.sin(ang_r), np.sin(ang_c)], axis=1).T
    cos_l = np.concatenate([np.cos(ang_r), np.cos(ang_r), np.cos(ang_c), np.cos(ang_c)], axis=1)
    zeros = np.zeros_like(ang_r)
    s_up = np.concatenate([-np.sin(ang_r), zeros, -np.sin(ang_c), zeros], axis=1)
    s_dn = np.concatenate([zeros, np.sin(ang_r), zeros, np.sin(ang_c)], axis=1)
    tile2 = lambda a: np.concatenate([a, a], axis=1).astype(np.float32)
    return (cos_t.astype(np.float32), sin_t.astype(np.float32),
            tile2(cos_l), tile2(s_up), tile2(s_dn))


def _inproj_kernel(x_ref, nw_ref, sh_ref, sc_ref, w_ref, cos_t_ref, sin_t_ref,
                   cos_l_ref, sup_ref, sdn_ref, qt_ref, k_ref, vt_ref, u_ref, wb_ref):
    @pl.when(pl.program_id(0) == 0)
    def _():
        for c0 in range(0, IN_WIDTH, 2 * LANES):
            wb_ref[:, c0:c0 + 2 * LANES] = w_ref[:, c0:c0 + 2 * LANES].astype(BF16)

    h = _rms_modulate(x_ref[...], nw_ref[...], sh_ref[LATENT_ROW], sc_ref[LATENT_ROW]).astype(BF16)
    p = jnp.dot(h, wb_ref[...], preferred_element_type=F32)

    qt = p[:, :ATTN_WIDTH].T
    cr, cc = cos_t_ref[0:16, :], cos_t_ref[16:32, :]
    sr, sc = sin_t_ref[0:16, :], sin_t_ref[16:32, :]
    pieces = []
    for hd in range(N_Q_HEADS):
        b = hd * HEAD_DIM
        x1r, x2r = qt[b:b + 16, :], qt[b + 16:b + 32, :]
        x1c, x2c = qt[b + 32:b + 48, :], qt[b + 48:b + 64, :]
        pieces += [x1r * cr - x2r * sr, x1r * sr + x2r * cr,
                   x1c * cc - x2c * sc, x1c * sc + x2c * cc]
    qt_ref[...] = jnp.concatenate(pieces, axis=0).astype(BF16)

    k = p[:, ATTN_WIDTH:ATTN_WIDTH + KV_WIDTH]
    cos_l, s_up, s_dn = cos_l_ref[...], sup_ref[...], sdn_ref[...]
    kparts = []
    for g in range(KV_WIDTH // LANES):
        kg = k[:, g * LANES:(g + 1) * LANES]
        kparts.append(kg * cos_l + pltpu.roll(kg, LANES - 16, 1) * s_up
                      + pltpu.roll(kg, 16, 1) * s_dn)
    k_ref[...] = jnp.concatenate(kparts, axis=1).astype(BF16)

    vt_ref[...] = p[:, ATTN_WIDTH + KV_WIDTH:ATTN_WIDTH + 2 * KV_WIDTH].T.astype(BF16)
    u_ref[...] = p[:, ATTN_WIDTH + 2 * KV_WIDTH:]


def _inproj(x, norm_w, mod, w_in, tables):
    cos_t, sin_t, cos_l, s_up, s_dn = tables
    tm = INPROJ_TM
    row = lambda i: (i, 0)
    colb = lambda i: (0, i)
    fixed = lambda i: (0, 0)
    return pl.pallas_call(
        _inproj_kernel,
        grid=(SEQ // tm,),
        in_specs=[
            pl.BlockSpec((tm, D_MODEL), row),
            pl.BlockSpec((1, D_MODEL), fixed),
            _mod_spec(0),
            _mod_spec(1),
            pl.BlockSpec((D_MODEL, IN_WIDTH), fixed, pipeline_mode=pl.Buffered(1)),
            pl.BlockSpec((32, tm), colb),
            pl.BlockSpec((32, tm), colb),
            pl.BlockSpec((tm, LANES), row),
            pl.BlockSpec((tm, LANES), row),
            pl.BlockSpec((tm, LANES), row),
        ],
        out_specs=[
            pl.BlockSpec((ATTN_WIDTH, tm), colb),
            pl.BlockSpec((tm, KV_WIDTH), row),
            pl.BlockSpec((KV_WIDTH, tm), colb),
            pl.BlockSpec((tm, POOL_WIDTH), row),
        ],
        out_shape=[
            jax.ShapeDtypeStruct((ATTN_WIDTH, SEQ), BF16),
            jax.ShapeDtypeStruct((SEQ, KV_WIDTH), BF16),
            jax.ShapeDtypeStruct((KV_WIDTH, SEQ), BF16),
            jax.ShapeDtypeStruct((SEQ, POOL_WIDTH), F32),
        ],
        scratch_shapes=[pltpu.VMEM((D_MODEL, IN_WIDTH), BF16)],
        compiler_params=_params("arbitrary"),
        name="inproj",
    )(x, norm_w, mod, mod, w_in, cos_t, sin_t, cos_l, s_up, s_dn)


def _ctx_kv_kernel(x_ref, nw_ref, sh_ref, sc_ref, w_ref, k_ref, vt_ref):
    h = _rms_modulate(x_ref[...], nw_ref[...], sh_ref[CONTEXT_ROW], sc_ref[CONTEXT_ROW]).astype(BF16)
    p = jnp.dot(h, w_ref[...].astype(BF16), preferred_element_type=F32)
    k_ref[...] = p[:, :KV_WIDTH].astype(BF16)
    vt_ref[...] = p[:, KV_WIDTH:].T.astype(BF16)


def _ctx_kv(ctx, norm_w, mod, w_in):
    fixed = lambda i: (0, 0)
    kv_block = ATTN_WIDTH // (2 * KV_WIDTH)
    return pl.pallas_call(
        _ctx_kv_kernel,
        grid=(1,),
        in_specs=[
            pl.BlockSpec((CTX_LEN, D_MODEL), fixed),
            pl.BlockSpec((1, D_MODEL), fixed),
            _mod_spec(0),
            _mod_spec(1),
            pl.BlockSpec((D_MODEL, 2 * KV_WIDTH), lambda i: (0, kv_block)),
        ],
        out_specs=[
            pl.BlockSpec((CTX_LEN, KV_WIDTH), fixed),
            pl.BlockSpec((KV_WIDTH, CTX_LEN), fixed),
        ],
        out_shape=[
            jax.ShapeDtypeStruct((CTX_LEN, KV_WIDTH), BF16),
            jax.ShapeDtypeStruct((KV_WIDTH, CTX_LEN), BF16),
        ],
        compiler_params=_params("arbitrary"),
        name="ctx_kv",
    )(ctx, norm_w, mod, mod, w_in)


def _attn_kernel(qt_ref, kp_ref, km_ref, kn_ref, vtp_ref, vtm_ref, vtn_ref,
                 kc_ref, vtc_ref, sink_ref, wout_ref, wup_ref, wdn_ref,
                 c_ref, cctx_ref, wada_ref, bada_ref,
                 o_ref, wout_b_ref, wup_b_ref, wdn_b_ref, mod_ref):
    mod_ref[...] = _ada_block(c_ref, cctx_ref, wada_ref, bada_ref)

    i = pl.program_id(0)
    nsub = ATTN_TQ // BLOCK
    nblk = SEQ // BLOCK
    gw = GQA * BLOCK
    pw = 2 * BLOCK
    nkeys = 3 * BLOCK + CTX_LEN

    key_j = lax.broadcasted_iota(jnp.int32, (BLOCK, gw), 0)
    qry_i = lax.broadcasted_iota(jnp.int32, (BLOCK, gw), 1) % BLOCK
    neg = jnp.full((BLOCK, gw), NEG_INF, F32)
    zero = jnp.zeros((BLOCK, gw), F32)
    band_prev = jnp.where(key_j >= qry_i, zero, neg)
    band_next = jnp.where(key_j <= qry_i, zero, neg)

    kc = kc_ref[...]
    vtc = vtc_ref[...]
    ones_rows = jnp.ones((2 * SUBLANES, nkeys), BF16)

    windows = []
    for sb in range(nsub):
        n = i * nsub + sb
        lo, hi = sb * BLOCK, (sb + 1) * BLOCK
        k_prev = kp_ref[...] if sb == 0 else km_ref[lo - BLOCK:lo, :]
        k_next = kn_ref[...] if sb == nsub - 1 else km_ref[hi:hi + BLOCK, :]
        vt_prev = vtp_ref[...] if sb == 0 else vtm_ref[:, lo - BLOCK:lo]
        vt_next = vtn_ref[...] if sb == nsub - 1 else vtm_ref[:, hi:hi + BLOCK]
        windows.append(dict(
            kwin=jnp.concatenate([k_prev, km_ref[lo:hi, :], k_next, kc], axis=0),
            vtwin=jnp.concatenate([vt_prev, vtm_ref[:, lo:hi], vt_next, vtc], axis=1),
            bias_prev=band_prev + jnp.where(n > 0, 0.0, NEG_INF),
            bias_next=band_next + jnp.where(n < nblk - 1, 0.0, NEG_INF)))

    units = [(sb, h, half) for sb in range(nsub) for h in range(N_KV_HEADS)
             for half in range(GQA // 2)]

    def scores(sb, h):
        win = windows[sb]
        lo, hi = sb * BLOCK, (sb + 1) * BLOCK
        qh = jnp.concatenate(
            [qt_ref[(GQA * h + g) * HEAD_DIM:(GQA * h + g + 1) * HEAD_DIM, lo:hi]
             for g in range(GQA)], axis=1)
        blocks = []
        if h > 0:
            blocks.append(jnp.zeros((h * HEAD_DIM, gw), BF16))
        blocks.append(qh)
        if h < N_KV_HEADS - 1:
            blocks.append(jnp.zeros(((N_KV_HEADS - 1 - h) * HEAD_DIM, gw), BF16))
        s = jnp.dot(win["kwin"], jnp.concatenate(blocks, axis=0),
                    preferred_element_type=F32)
        s = jnp.concatenate([s[0:BLOCK] + win["bias_prev"], s[BLOCK:2 * BLOCK],
                             s[2 * BLOCK:3 * BLOCK] + win["bias_next"], s[3 * BLOCK:]], axis=0)
        return s, jnp.max(s, axis=0, keepdims=True)

    def probs(unit, s, smax):
        sb, h, half = unit
        hd0 = GQA * h + 2 * half
        cols = slice(half * pw, (half + 1) * pw)
        sink = jnp.concatenate([jnp.full((1, BLOCK), sink_ref[hd0 + g] * LOG2E, F32)
                                for g in range(2)], axis=1)
        m = jnp.maximum(smax[:, cols], sink)
        return jnp.exp2(s[:, cols] - m).astype(BF16), jnp.exp2(sink - m)

    def finish(unit, e, e_sink):
        sb, h, half = unit
        lo, hi = sb * BLOCK, (sb + 1) * BLOCK
        hd0 = GQA * h + 2 * half
        vt_ones = jnp.concatenate(
            [windows[sb]["vtwin"][h * HEAD_DIM:(h + 1) * HEAD_DIM, :], ones_rows], axis=0)
        o = jnp.dot(vt_ones, e, preferred_element_type=F32)
        denom = o[HEAD_DIM:HEAD_DIM + 1, :] + e_sink
        o = o[:HEAD_DIM, :] * (1.0 / denom)
        blk = jnp.concatenate([o[:, 0:BLOCK], o[:, BLOCK:2 * BLOCK]], axis=0)
        o_ref[lo:hi, hd0 * HEAD_DIM:(hd0 + 2) * HEAD_DIM] = blk.T.astype(BF16)

    cast_jobs = []
    for src, dst in ((wout_ref, wout_b_ref), (wup_ref, wup_b_ref), (wdn_ref, wdn_b_ref)):
        wc = max(LANES, CAST_JOB_ELEMS // src.shape[0])
        cast_jobs += [(src, dst, c0) + (wc,) for c0 in range(0, src.shape[1], wc)]

    def cast_piece(idx, n):
        for src, dst, c0, wc in cast_jobs[idx * len(cast_jobs) // n:(idx + 1) * len(cast_jobs) // n]:
            val = src[:, c0:c0 + wc].astype(BF16)
            if len(dst.shape) == 3:
                tf = dst.shape[2]
                dst[c0 // tf, :, c0 % tf:c0 % tf + wc] = val
            else:
                dst[:, c0:c0 + wc] = val

    nu = len(units)
    s_vals, p_vals = {}, {}
    for step in range(nu + 2):
        if step < nu and units[step][2] == 0:
            s_vals[units[step][:2]] = scores(*units[step][:2])
        if 1 <= step <= nu:
            unit = units[step - 1]
            p_vals[step - 1] = probs(unit, *s_vals[unit[:2]])
            if unit[2] == GQA // 2 - 1:
                del s_vals[unit[:2]]
            cast_piece(step - 1, nu)
        if step >= 2:
            finish(units[step - 2], *p_vals.pop(step - 2))


def _attn(qt, k, vt, kc, vtc, sink, w_out, w_up, w_dn, c, c_ctx, w_ada, b_ada, ada_col0):
    tq = ATTN_TQ
    r = tq // BLOCK
    nblk = SEQ // BLOCK
    nsteps = SEQ // tq
    ada_tn = (w_ada.shape[1] - ada_col0) // nsteps
    ada_b0 = ada_col0 // ada_tn
    fixed = lambda i: (0, 0)
    row = lambda i: (i, 0)
    slab = lambda w: pl.BlockSpec((w.shape[0] // nsteps, w.shape[1]), row)
    bf16_like = lambda w: jax.ShapeDtypeStruct(w.shape, BF16)
    prev = lambda i: jnp.maximum(i * r - 1, 0)
    nxt = lambda i: jnp.minimum((i + 1) * r, nblk - 1)
    return pl.pallas_call(
        _attn_kernel,
        grid=(SEQ // tq,),
        in_specs=[
            pl.BlockSpec((ATTN_WIDTH, tq), lambda i: (0, i)),
            pl.BlockSpec((BLOCK, KV_WIDTH), lambda i: (prev(i), 0)),
            pl.BlockSpec((tq, KV_WIDTH), lambda i: (i, 0)),
            pl.BlockSpec((BLOCK, KV_WIDTH), lambda i: (nxt(i), 0)),
            pl.BlockSpec((KV_WIDTH, BLOCK), lambda i: (0, prev(i))),
            pl.BlockSpec((KV_WIDTH, tq), lambda i: (0, i)),
            pl.BlockSpec((KV_WIDTH, BLOCK), lambda i: (0, nxt(i))),
            pl.BlockSpec((CTX_LEN, KV_WIDTH), fixed),
            pl.BlockSpec((KV_WIDTH, CTX_LEN), fixed),
            pl.BlockSpec(memory_space=pltpu.SMEM),
            slab(w_out), slab(w_up), slab(w_dn),
            pl.BlockSpec((1, D_MODEL), fixed),
            pl.BlockSpec((1, D_MODEL), fixed),
            pl.BlockSpec((D_MODEL, ada_tn), lambda i: (0, ada_b0 + i)),
            pl.BlockSpec((1, ada_tn), lambda i: (0, ada_b0 + i)),
        ],
        out_specs=[pl.BlockSpec((tq, ATTN_WIDTH), row), slab(w_out),
                   pl.BlockSpec((MLP_NF, w_up.shape[0] // nsteps, MLP_TF), lambda i: (0, i, 0)),
                   slab(w_dn),
                   pl.BlockSpec((SUBLANES, ada_tn), lambda i: (0, i))],
        out_shape=[jax.ShapeDtypeStruct((SEQ, ATTN_WIDTH), BF16),
                   bf16_like(w_out),
                   jax.ShapeDtypeStruct((MLP_NF, w_up.shape[0], MLP_TF), BF16),
                   bf16_like(w_dn),
                   jax.ShapeDtypeStruct((SUBLANES, w_ada.shape[1] - ada_col0), F32)],
        compiler_params=_params("arbitrary"),
        name="attn",
    )(qt, k, k, k, vt, vt, vt, kc, vtc, sink, w_out, w_up, w_dn, c, c_ctx, w_ada, b_ada)


def _mixout_kernel(x_ref, attn_ref, up_ref, um_ref, un_ref, wout_ref, pw_ref, ps_ref,
                   ga_ref, nw_ref, sh_ref, sc_ref, x1_ref, hm_ref, ubuf):
    i = pl.program_id(0)
    tm = MIX_TM
    nsteps = SEQ // tm
    ubuf[0:POOL_HALO, :] = jnp.where(i > 0, up_ref[...], 0.0)
    ubuf[POOL_HALO:POOL_HALO + tm, :] = um_ref[...]
    ubuf[POOL_HALO + tm:2 * POOL_HALO + tm, :] = jnp.where(i < nsteps - 1, un_ref[...], 0.0)

    rc = MIX_CHUNK
    n = rc + 2 * POOL_HALO
    c = POOL_HALO
    ngroups = len(POOL_WINDOWS)
    ncol = D_MODEL // ngroups

    def pool_group(r0, g):
        w = POOL_WINDOWS[g]
        c0, c1 = g * POOL_GROUP_DIM, (g + 1) * POOL_GROUP_DIM
        e = ubuf[r0:r0 + n, c0:c1]
        if w == 2:
            win = e[c - 1:c - 1 + rc] + e[c:c + rc]
        else:
            f2 = e[0:n - 1] + e[1:n]
            if w == 4:
                win = f2[c - 2:c - 2 + rc] + f2[c:c + rc]
            else:
                f4 = f2[0:n - 3] + f2[2:n - 1]
                if w == 8:
                    win = f4[c - 4:c - 4 + rc] + f4[c:c + rc]
                else:
                    f8 = f4[0:n - 7] + f4[4:n - 3]
                    win = f8[c - 8:c - 8 + rc] + f8[c:c + rc]
        t = i * tm + r0 + lax.broadcasted_iota(jnp.int32, (rc, 1), 0)
        lo = jnp.clip(t - w // 2, 0, SEQ)
        hi = jnp.clip(t - w // 2 + w, 0, SEQ)
        cnt = (hi - lo).astype(F32)
        pg = win / cnt - um_ref[r0:r0 + rc, c0:c1]
        mixed = jnp.dot(pg.astype(BF16), pw_ref[g].astype(BF16), preferred_element_type=F32)
        return (mixed * ps_ref[:, c0:c1]).astype(BF16)

    def prenorm(r0, nrows):
        rows = slice(r0, r0 + nrows)
        hm_ref[rows, :] = _rms_modulate(x1_ref[rows, :], nw_ref[...], sh_ref[LATENT_ROW],
                                        sc_ref[LATENT_ROW]).astype(BF16)

    chunk_starts = list(range(0, tm, rc))
    for ci, r0 in enumerate(chunk_starts):
        rows = slice(r0, r0 + rc)
        attn_rows = attn_ref[rows, :]
        pooled = []
        for j in range(ngroups):
            cols = slice(j * ncol, (j + 1) * ncol)
            o = jnp.dot(attn_rows, wout_ref[0:ATTN_WIDTH, cols], preferred_element_type=F32)
            x1_ref[rows, cols] = x_ref[rows, cols] + ga_ref[LATENT_ROW, cols] * o
            pooled.append(pool_group(r0, j))
        pool_rows = jnp.concatenate(pooled, axis=1)
        for j in range(ngroups):
            cols = slice(j * ncol, (j + 1) * ncol)
            o = jnp.dot(pool_rows, wout_ref[ATTN_WIDTH:, cols], preferred_element_type=F32)
            x1_ref[rows, cols] += ga_ref[LATENT_ROW, cols] * o
            if ci > 0:
                prenorm(chunk_starts[ci - 1] + j * (rc // ngroups), rc // ngroups)
    for j in range(ngroups):
        prenorm(chunk_starts[-1] + j * (rc // ngroups), rc // ngroups)


def _mixout(x, attn, u, w_out_bf16, pool_w, pool_scale, mod, norm_w):
    tm = MIX_TM
    hb = tm // POOL_HALO
    nhalo = SEQ // POOL_HALO
    row = lambda i: (i, 0)
    fixed = lambda i: (0, 0)
    return pl.pallas_call(
        _mixout_kernel,
        grid=(SEQ // tm,),
        in_specs=[
            pl.BlockSpec((tm, D_MODEL), row),
            pl.BlockSpec((tm, ATTN_WIDTH), row),
            pl.BlockSpec((POOL_HALO, POOL_WIDTH), lambda i: (jnp.maximum(i * hb - 1, 0), 0)),
            pl.BlockSpec((tm, POOL_WIDTH), row),
            pl.BlockSpec((POOL_HALO, POOL_WIDTH), lambda i: (jnp.minimum((i + 1) * hb, nhalo - 1), 0)),
            pl.BlockSpec((D_MODEL, D_MODEL), fixed),
            pl.BlockSpec((len(POOL_WINDOWS), POOL_GROUP_DIM, POOL_GROUP_DIM), lambda i: (0, 0, 0)),
            pl.BlockSpec((1, POOL_WIDTH), fixed),
            _mod_spec(0),
            pl.BlockSpec((1, D_MODEL), fixed),
            _mod_spec(1),
            _mod_spec(2),
        ],
        out_specs=[pl.BlockSpec((tm, D_MODEL), row), pl.BlockSpec((tm, D_MODEL), row)],
        out_shape=[jax.ShapeDtypeStruct((SEQ, D_MODEL), F32),
                   jax.ShapeDtypeStruct((SEQ, D_MODEL), BF16)],
        scratch_shapes=[pltpu.VMEM((tm + 2 * POOL_HALO, POOL_WIDTH), F32)],
        compiler_params=_params("arbitrary"),
        name="mixout",
    )(x, attn, u, u, u, w_out_bf16, pool_w, pool_scale, mod, norm_w, mod, mod)


def _mlp_kernel(hm_ref, x1r_ref, wup_ref, wdn_ref, gm_ref, fw_ref, o_ref):
    f = pl.program_id(1)
    last = MLP_NF - 1
    row_chunks = [slice(r0, r0 + MLP_RC) for r0 in range(0, MLP_TM, MLP_RC)]
    gm = gm_ref[LATENT_ROW]

    def branch_out(rows):
        a = jnp.maximum(jnp.dot(hm_ref[rows, :], wup_ref[...], preferred_element_type=F32), 0.0)
        return gm * jnp.dot((a * a).astype(BF16), wdn_ref[...], preferred_element_type=F32)

    def final_norm(rows):
        x2 = o_ref[rows, :]
        y = x2 * lax.rsqrt(jnp.mean(x2 * x2, axis=-1, keepdims=True) + EPS)
        o_ref[rows, :] = y * fw_ref[...]

    @pl.when(f == 0)
    def _():
        for rows in row_chunks:
            o_ref[rows, :] = branch_out(rows)
        o_ref[0:MLP_XR, :] += x1r_ref[...]

    @pl.when(jnp.logical_and(f > 0, f < last))
    def _():
        x_rows = pl.ds(pl.multiple_of(f * MLP_XR, MLP_XR), MLP_XR)
        o_ref[x_rows, :] += x1r_ref[...]
        for rows in row_chunks:
            o_ref[rows, :] += branch_out(rows)

    @pl.when(f == last)
    def _():
        o_ref[last * MLP_XR:(last + 1) * MLP_XR, :] += x1r_ref[...]
        npiece = 4
        cw, rp = D_MODEL // npiece, MLP_RC // npiece
        for ci, rows in enumerate(row_chunks):
            a = jnp.maximum(jnp.dot(hm_ref[rows, :], wup_ref[...], preferred_element_type=F32), 0.0)
            a = (a * a).astype(BF16)
            for j in range(npiece):
                cols = slice(j * cw, (j + 1) * cw)
                o_ref[rows, cols] += gm[:, cols] * jnp.dot(a, wdn_ref[:, cols],
                                                           preferred_element_type=F32)
                if ci > 0:
                    r0 = row_chunks[ci - 1].start + j * rp
                    final_norm(slice(r0, r0 + rp))
        final_norm(row_chunks[-1])


def _mlp(hm, x1, w_up_bf16, w_dn_bf16, mod, final_w):
    tm, tf = MLP_TM, MLP_TF
    return pl.pallas_call(
        _mlp_kernel,
        grid=(SEQ // tm, MLP_NF),
        in_specs=[
            pl.BlockSpec((tm, D_MODEL), lambda i, f: (i, 0)),
            pl.BlockSpec((MLP_XR, D_MODEL), lambda i, f: (i * MLP_NF + f, 0)),
            pl.BlockSpec((None, D_MODEL, tf), lambda i, f: (f, 0, 0)),
            pl.BlockSpec((tf, D_MODEL), lambda i, f: (f, 0)),
            _mod_spec(3),
            pl.BlockSpec((1, D_MODEL), lambda i, f: (0, 0)),
        ],
        out_specs=pl.BlockSpec((tm, D_MODEL), lambda i, f: (i, 0)),
        out_shape=jax.ShapeDtypeStruct((SEQ, D_MODEL), F32),
        compiler_params=_params("arbitrary", "arbitrary"),
        name="mlp",
    )(hm, x1, w_up_bf16, w_dn_bf16, mod, final_w)


def kernel(x, c, ctx, c_ctx, norm_attn_w, norm_mlp_w, w_ada, b_ada, w_in, attn_sink,
           pool_w, pool_scale, w_out, w_mlp_up, w_mlp_down, final_norm_w):
    assert x.shape == (1, SEQ, D_MODEL) and ctx.shape == (1, CTX_LEN, D_MODEL)
    assert w_ada.shape[0] == 1, "single layer"
    x2d, ctx2d = x[0], ctx[0]

    c_ctx_row = c_ctx[None, :]
    b_row = b_ada[0][None, :]
    n_early = 2 * D_MODEL
    mod_a = _ada(c, c_ctx_row, w_ada[0], b_row, n_early)

    nw_a = norm_attn_w[0][None, :]
    tables = tuple(jnp.asarray(t) for t in _rope_tables())
    qt, k, vt, u = _inproj(x2d, nw_a, mod_a, w_in[0], tables)
    kc, vtc = _ctx_kv(ctx2d, nw_a, mod_a, w_in[0])

    attn, w_out_b, w_up_b, w_dn_b, mod_b = _attn(
        qt, k, vt, kc, vtc, attn_sink[0].astype(F32), w_out[0], w_mlp_up[0], w_mlp_down[0],
        c, c_ctx_row, w_ada[0], b_row, n_early)

    x1, hm = _mixout(x2d, attn, u, w_out_b, pool_w[0], pool_scale[0][None, :], mod_b,
                     norm_mlp_w[0][None, :])
    out = _mlp(hm, x1, w_up_b, w_dn_b, mod_b, final_norm_w[None, :])
    return out[None]
```

```python
import functools

import numpy as np
import jax
import jax.numpy as jnp
from jax import lax
from jax.experimental import pallas as pl
from jax.experimental.pallas import tpu as pltpu

F32 = jnp.float32
BF16 = jnp.bfloat16

D_MODEL = 2048
SEQ = 8192
CTX_LEN = 256
GRID_W = 64
HEAD_DIM = 64
N_Q_HEADS = 16
N_KV_HEADS = 4
GQA = N_Q_HEADS // N_KV_HEADS
ATTN_WIDTH = N_Q_HEADS * HEAD_DIM
KV_WIDTH = N_KV_HEADS * HEAD_DIM
POOL_WINDOWS = (2, 4, 8, 16)
POOL_WIDTH = D_MODEL - ATTN_WIDTH
POOL_GROUP_DIM = POOL_WIDTH // len(POOL_WINDOWS)
IN_WIDTH = ATTN_WIDTH + 2 * KV_WIDTH + POOL_WIDTH
D_FF = 4 * D_MODEL
BLOCK = 128
ROPE_BASE = 10000.0
N_MOD = 6
EPS = 1e-6
NEG_INF = -1e30
LOG2E = 1.4426950408889634
Q_SCALE = HEAD_DIM ** -0.5 * LOG2E

LANES = 128
SUBLANES = 8
POOL_HALO = 8
LATENT_ROW = slice(0, 1)
CONTEXT_ROW = slice(1, 2)
VMEM_LIMIT = 56 * 1024 * 1024

ADA_TN = 512
INPROJ_TM = 512
ATTN_TQ = 512
CAST_JOB_ELEMS = 64 * 1024
MIX_TM = 512
MIX_CHUNK = 256
MLP_TM = 1024
MLP_TF = 512
MLP_RC = 512
MLP_NC = 1024
MLP_NF = D_FF // MLP_TF


def _params(*sem):
    return pltpu.CompilerParams(dimension_semantics=sem, vmem_limit_bytes=VMEM_LIMIT)


def _mod_spec(chunk):
    return pl.BlockSpec((SUBLANES, D_MODEL), lambda *_: (0, chunk))


def _rms_modulate(x, norm_w, shift, scale):
    gain = norm_w * (1.0 + scale)
    y = x * lax.rsqrt(jnp.mean(x * x, axis=-1, keepdims=True) + EPS)
    return y * gain + shift


def _ada_block(c_ref, cctx_ref, w_ref, b_ref):
    cnd = jnp.concatenate([c_ref[...], cctx_ref[...], jnp.zeros((SUBLANES - 2, D_MODEL), F32)], axis=0)
    s = cnd * (1.0 / (1.0 + jnp.exp(-cnd)))
    return jnp.dot(s, w_ref[...], preferred_element_type=F32) + b_ref[...]


def _ada_kernel(c_ref, cctx_ref, w_ref, b_ref, o_ref):
    o_ref[...] = _ada_block(c_ref, cctx_ref, w_ref, b_ref)


def _ada(c, c_ctx, w_ada, b_ada, n):
    return pl.pallas_call(
        _ada_kernel,
        grid=(n // ADA_TN,),
        in_specs=[
            pl.BlockSpec((1, D_MODEL), lambda j: (0, 0)),
            pl.BlockSpec((1, D_MODEL), lambda j: (0, 0)),
            pl.BlockSpec((D_MODEL, ADA_TN), lambda j: (0, j)),
            pl.BlockSpec((1, ADA_TN), lambda j: (0, j)),
        ],
        out_specs=pl.BlockSpec((SUBLANES, ADA_TN), lambda j: (0, j)),
        out_shape=jax.ShapeDtypeStruct((SUBLANES, n), F32),
        compiler_params=_params("arbitrary"),
        name="ada",
    )(c, c_ctx, w_ada, b_ada)


def _rope_tables():
    half = HEAD_DIM // 2
    inv_freq = ROPE_BASE ** (-np.arange(0, half, 2, dtype=np.float64) / half)
    t = np.arange(SEQ)
    row = (t // GRID_W).astype(np.float64)
    col = (t % GRID_W).astype(np.float64)
    ang_r = row[:, None] * inv_freq[None, :]
    ang_c = col[:, None] * inv_freq[None, :]
    cos_t = Q_SCALE * np.concatenate([np.cos(ang_r), np.cos(ang_c)], axis=1).T
    sin_t = Q_SCALE * np.concatenate([np.sin(ang_r), np.sin(ang_c)], axis=1).T
    cos_l = np.concatenate([np.cos(ang_r), np.cos(ang_r), np.cos(ang_c), np.cos(ang_c)], axis=1)
    zeros = np.zeros_like(ang_r)
    s_up = np.concatenate([-np.sin(ang_r), zeros, -np.sin(ang_c), zeros], axis=1)
    s_dn = np.concatenate([zeros, np.sin(ang_r), zeros, np.sin(ang_c)], axis=1)
    tile2 = lambda a: np.concatenate([a, a], axis=1).astype(np.float32)
    return (cos_t.astype(np.float32), sin_t.astype(np.float32),
            tile2(cos_l), tile2(s_up), tile2(s_dn))


def _inproj_kernel(x_ref, nw_ref, sh_ref, sc_ref, w_ref, cos_t_ref, sin_t_ref,
                   cos_l_ref, sup_ref, sdn_ref, qt_ref, k_ref, vt_ref, u_ref, wb_ref):
    @pl.when(pl.program_id(0) == 0)
    def _():
        for c0 in range(0, IN_WIDTH, 2 * LANES):
            wb_ref[:, c0:c0 + 2 * LANES] = w_ref[:, c0:c0 + 2 * LANES].astype(BF16)

    h = _rms_modulate(x_ref[...], nw_ref[...], sh_ref[LATENT_ROW], sc_ref[LATENT_ROW]).astype(BF16)
    p = jnp.dot(h, wb_ref[...], preferred_element_type=F32)

    qt = p[:, :ATTN_WIDTH].T
    cr, cc = cos_t_ref[0:16, :], cos_t_ref[16:32, :]
    sr, sc = sin_t_ref[0:16, :], sin_t_ref[16:32, :]
    pieces = []
    for hd in range(N_Q_HEADS):
        b = hd * HEAD_DIM
        x1r, x2r = qt[b:b + 16, :], qt[b + 16:b + 32, :]
        x1c, x2c = qt[b + 32:b + 48, :], qt[b + 48:b + 64, :]
        pieces += [x1r * cr - x2r * sr, x1r * sr + x2r * cr,
                   x1c * cc - x2c * sc, x1c * sc + x2c * cc]
    qt_ref[...] = jnp.concatenate(pieces, axis=0).astype(BF16)

    k = p[:, ATTN_WIDTH:ATTN_WIDTH + KV_WIDTH]
    cos_l, s_up, s_dn = cos_l_ref[...], sup_ref[...], sdn_ref[...]
    kparts = []
    for g in range(KV_WIDTH // LANES):
        kg = k[:, g * LANES:(g + 1) * LANES]
        kparts.append(kg * cos_l + pltpu.roll(kg, LANES - 16, 1) * s_up
                      + pltpu.roll(kg, 16, 1) * s_dn)
    k_ref[...] = jnp.concatenate(kparts, axis=1).astype(BF16)

    vt_ref[...] = p[:, ATTN_WIDTH + KV_WIDTH:ATTN_WIDTH + 2 * KV_WIDTH].T.astype(BF16)
    u_ref[...] = p[:, ATTN_WIDTH + 2 * KV_WIDTH:]


def _inproj(x, norm_w, mod, w_in, tables):
    cos_t, sin_t, cos_l, s_up, s_dn = tables
    tm = INPROJ_TM
    row = lambda i: (i, 0)
    colb = lambda i: (0, i)
    fixed = lambda i: (0, 0)
    return pl.pallas_call(
        _inproj_kernel,
        grid=(SEQ // tm,),
        in_specs=[
            pl.BlockSpec((tm, D_MODEL), row),
            pl.BlockSpec((1, D_MODEL), fixed),
            _mod_spec(0),
            _mod_spec(1),
            pl.BlockSpec((D_MODEL, IN_WIDTH), fixed, pipeline_mode=pl.Buffered(1)),
            pl.BlockSpec((32, tm), colb),
            pl.BlockSpec((32, tm), colb),
            pl.BlockSpec((tm, LANES), row),
            pl.BlockSpec((tm, LANES), row),
            pl.BlockSpec((tm, LANES), row),
        ],
        out_specs=[
            pl.BlockSpec((ATTN_WIDTH, tm), colb),
            pl.BlockSpec((tm, KV_WIDTH), row),
            pl.BlockSpec((KV_WIDTH, tm), colb),
            pl.BlockSpec((tm, POOL_WIDTH), row),
        ],
        out_shape=[
            jax.ShapeDtypeStruct((ATTN_WIDTH, SEQ), BF16),
            jax.ShapeDtypeStruct((SEQ, KV_WIDTH), BF16),
            jax.ShapeDtypeStruct((KV_WIDTH, SEQ), BF16),
            jax.ShapeDtypeStruct((SEQ, POOL_WIDTH), F32),
        ],
        scratch_shapes=[pltpu.VMEM((D_MODEL, IN_WIDTH), BF16)],
        compiler_params=_params("arbitrary"),
        name="inproj",
    )(x, norm_w, mod, mod, w_in, cos_t, sin_t, cos_l, s_up, s_dn)


def _ctx_kv_kernel(x_ref, nw_ref, sh_ref, sc_ref, w_ref, k_ref, vt_ref):
    h = _rms_modulate(x_ref[...], nw_ref[...], sh_ref[CONTEXT_ROW], sc_ref[CONTEXT_ROW]).astype(BF16)
    p = jnp.dot(h, w_ref[...].astype(BF16), preferred_element_type=F32)
    k_ref[...] = p[:, :KV_WIDTH].astype(BF16)
    vt_ref[...] = p[:, KV_WIDTH:].T.astype(BF16)


def _ctx_kv(ctx, norm_w, mod, w_in):
    fixed = lambda i: (0, 0)
    kv_block = ATTN_WIDTH // (2 * KV_WIDTH)
    return pl.pallas_call(
        _ctx_kv_kernel,
        grid=(1,),
        in_specs=[
            pl.BlockSpec((CTX_LEN, D_MODEL), fixed),
            pl.BlockSpec((1, D_MODEL), fixed),
            _mod_spec(0),
            _mod_spec(1),
            pl.BlockSpec((D_MODEL, 2 * KV_WIDTH), lambda i: (0, kv_block)),
        ],
        out_specs=[
            pl.BlockSpec((CTX_LEN, KV_WIDTH), fixed),
            pl.BlockSpec((KV_WIDTH, CTX_LEN), fixed),
        ],
        out_shape=[
            jax.ShapeDtypeStruct((CTX_LEN, KV_WIDTH), BF16),
            jax.ShapeDtypeStruct((KV_WIDTH, CTX_LEN), BF16),
        ],
        compiler_params=_params("arbitrary"),
        name="ctx_kv",
    )(ctx, norm_w, mod, mod, w_in)


def _attn_kernel(qt_ref, kp_ref, km_ref, kn_ref, vtp_ref, vtm_ref, vtn_ref,
                 kc_ref, vtc_ref, sink_ref, wout_ref, wup_ref, wdn_ref,
                 c_ref, cctx_ref, wada_ref, bada_ref,
                 o_ref, wout_b_ref, wup_b_ref, wdn_b_ref, mod_ref):
    mod_ref[...] = _ada_block(c_ref, cctx_ref, wada_ref, bada_ref)

    i = pl.program_id(0)
    nsub = ATTN_TQ // BLOCK
    nblk = SEQ // BLOCK
    gw = GQA * BLOCK
    pw = 2 * BLOCK
    nkeys = 3 * BLOCK + CTX_LEN

    key_j = lax.broadcasted_iota(jnp.int32, (BLOCK, gw), 0)
    qry_i = lax.broadcasted_iota(jnp.int32, (BLOCK, gw), 1) % BLOCK
    neg = jnp.full((BLOCK, gw), NEG_INF, F32)
    zero = jnp.zeros((BLOCK, gw), F32)
    band_prev = jnp.where(key_j >= qry_i, zero, neg)
    band_next = jnp.where(key_j <= qry_i, zero, neg)

    kc = kc_ref[...]
    vtc = vtc_ref[...]
    ones_rows = jnp.ones((2 * SUBLANES, nkeys), BF16)
    sink_row = jnp.concatenate([jnp.full((1, BLOCK), sink_ref[hd] * LOG2E, F32)
                                for hd in range(N_Q_HEADS)], axis=1)

    windows = []
    for sb in range(nsub):
        n = i * nsub + sb
        lo, hi = sb * BLOCK, (sb + 1) * BLOCK
        k_prev = kp_ref[...] if sb == 0 else km_ref[lo - BLOCK:lo, :]
        k_next = kn_ref[...] if sb == nsub - 1 else km_ref[hi:hi + BLOCK, :]
        vt_prev = vtp_ref[...] if sb == 0 else vtm_ref[:, lo - BLOCK:lo]
        vt_next = vtn_ref[...] if sb == nsub - 1 else vtm_ref[:, hi:hi + BLOCK]
        windows.append(dict(
            kwin=jnp.concatenate([k_prev, km_ref[lo:hi, :], k_next, kc], axis=0),
            vtwin=jnp.concatenate([vt_prev, vtm_ref[:, lo:hi], vt_next, vtc], axis=1),
            bias_prev=band_prev + jnp.where(n > 0, 0.0, NEG_INF),
            bias_next=band_next + jnp.where(n < nblk - 1, 0.0, NEG_INF)))

    units = [(sb, h, half) for sb in range(nsub) for h in range(N_KV_HEADS)
             for half in range(GQA // 2)]

    def scores(sb, h):
        win = windows[sb]
        lo, hi = sb * BLOCK, (sb + 1) * BLOCK
        qh = jnp.concatenate(
            [qt_ref[(GQA * h + g) * HEAD_DIM:(GQA * h + g + 1) * HEAD_DIM, lo:hi]
             for g in range(GQA)], axis=1)
        blocks = []
        if h > 0:
            blocks.append(jnp.zeros((h * HEAD_DIM, gw), BF16))
        blocks.append(qh)
        if h < N_KV_HEADS - 1:
            blocks.append(jnp.zeros(((N_KV_HEADS - 1 - h) * HEAD_DIM, gw), BF16))
        s = jnp.dot(win["kwin"], jnp.concatenate(blocks, axis=0),
                    preferred_element_type=F32)
        s = jnp.concatenate([s[0:BLOCK] + win["bias_prev"], s[BLOCK:2 * BLOCK],
                             s[2 * BLOCK:3 * BLOCK] + win["bias_next"], s[3 * BLOCK:]], axis=0)
        return s, jnp.max(s, axis=0, keepdims=True)

    def probs(unit, s, smax):
        sb, h, half = unit
        hd0 = GQA * h + 2 * half
        cols = slice(half * pw, (half + 1) * pw)
        sink = sink_row[:, hd0 * BLOCK:(hd0 + 2) * BLOCK]
        m = jnp.maximum(smax[:, cols], sink)
        return jnp.exp2(s[:, cols] - m).astype(BF16), jnp.exp2(sink - m)

    def finish(unit, e, e_sink):
        sb, h, half = unit
        lo, hi = sb * BLOCK, (sb + 1) * BLOCK
        hd0 = GQA * h + 2 * half
        vt_ones = jnp.concatenate(
            [windows[sb]["vtwin"][h * HEAD_DIM:(h + 1) * HEAD_DIM, :], ones_rows], axis=0)
        o = jnp.dot(vt_ones, e, preferred_element_type=F32)
        denom = o[HEAD_DIM:HEAD_DIM + 1, :] + e_sink
        o = o[:HEAD_DIM, :] * (1.0 / denom)
        blk = jnp.concatenate([o[:, 0:BLOCK], o[:, BLOCK:2 * BLOCK]], axis=0)
        o_ref[lo:hi, hd0 * HEAD_DIM:(hd0 + 2) * HEAD_DIM] = blk.T.astype(BF16)

    cast_jobs = []
    for src, dst in ((wout_ref, wout_b_ref), (wup_ref, wup_b_ref), (wdn_ref, wdn_b_ref)):
        wc = max(LANES, CAST_JOB_ELEMS // src.shape[0])
        cast_jobs += [(src, dst, c0) + (wc,) for c0 in range(0, src.shape[1], wc)]

    def cast_piece(idx, n):
        for src, dst, c0, wc in cast_jobs[idx * len(cast_jobs) // n:(idx + 1) * len(cast_jobs) // n]:
            val = src[:, c0:c0 + wc].astype(BF16)
            if len(dst.shape) == 3:
                tf = dst.shape[2]
                dst[c0 // tf, :, c0 % tf:c0 % tf + wc] = val
            else:
                dst[:, c0:c0 + wc] = val

    nu = len(units)
    s_vals, p_vals = {}, {}
    for step in range(nu + 2):
        if step < nu and units[step][2] == 0:
            s_vals[units[step][:2]] = scores(*units[step][:2])
        if 1 <= step <= nu:
            unit = units[step - 1]
            p_vals[step - 1] = probs(unit, *s_vals[unit[:2]])
            if unit[2] == GQA // 2 - 1:
                del s_vals[unit[:2]]
            cast_piece(step - 1, nu)
        if step >= 2:
            finish(units[step - 2], *p_vals.pop(step - 2))


def _attn(qt, k, vt, kc, vtc, sink, w_out, w_up, w_dn, c, c_ctx, w_ada, b_ada, ada_col0):
    tq = ATTN_TQ
    r = tq // BLOCK
    nblk = SEQ // BLOCK
    nsteps = SEQ // tq
    ada_tn = (w_ada.shape[1] - ada_col0) // nsteps
    ada_b0 = ada_col0 // ada_tn
    fixed = lambda i: (0, 0)
    row = lambda i: (i, 0)
    slab = lambda w: pl.BlockSpec((w.shape[0] // nsteps, w.shape[1]), row)
    bf16_like = lambda w: jax.ShapeDtypeStruct(w.shape, BF16)
    prev = lambda i: jnp.maximum(i * r - 1, 0)
    nxt = lambda i: jnp.minimum((i + 1) * r, nblk - 1)
    return pl.pallas_call(
        _attn_kernel,
        grid=(SEQ // tq,),
        in_specs=[
            pl.BlockSpec((ATTN_WIDTH, tq), lambda i: (0, i)),
            pl.BlockSpec((BLOCK, KV_WIDTH), lambda i: (prev(i), 0)),
            pl.BlockSpec((tq, KV_WIDTH), lambda i: (i, 0)),
            pl.BlockSpec((BLOCK, KV_WIDTH), lambda i: (nxt(i), 0)),
            pl.BlockSpec((KV_WIDTH, BLOCK), lambda i: (0, prev(i))),
            pl.BlockSpec((KV_WIDTH, tq), lambda i: (0, i)),
            pl.BlockSpec((KV_WIDTH, BLOCK), lambda i: (0, nxt(i))),
            pl.BlockSpec((CTX_LEN, KV_WIDTH), fixed),
            pl.BlockSpec((KV_WIDTH, CTX_LEN), fixed),
            pl.BlockSpec(memory_space=pltpu.SMEM),
            slab(w_out), slab(w_up), slab(w_dn),
            pl.BlockSpec((1, D_MODEL), fixed),
            pl.BlockSpec((1, D_MODEL), fixed),
            pl.BlockSpec((D_MODEL, ada_tn), lambda i: (0, ada_b0 + i)),
            pl.BlockSpec((1, ada_tn), lambda i: (0, ada_b0 + i)),
        ],
        out_specs=[pl.BlockSpec((tq, ATTN_WIDTH), row), slab(w_out),
                   pl.BlockSpec((MLP_NF, w_up.shape[0] // nsteps, MLP_TF), lambda i: (0, i, 0)),
                   slab(w_dn),
                   pl.BlockSpec((SUBLANES, ada_tn), lambda i: (0, i))],
        out_shape=[jax.ShapeDtypeStruct((SEQ, ATTN_WIDTH), BF16),
                   bf16_like(w_out),
                   jax.ShapeDtypeStruct((MLP_NF, w_up.shape[0], MLP_TF), BF16),
                   bf16_like(w_dn),
                   jax.ShapeDtypeStruct((SUBLANES, w_ada.shape[1] - ada_col0), F32)],
        compiler_params=_params("arbitrary"),
        name="attn",
    )(qt, k, k, k, vt, vt, vt, kc, vtc, sink, w_out, w_up, w_dn, c, c_ctx, w_ada, b_ada)


def _mixout_kernel(x_ref, attn_ref, up_ref, um_ref, un_ref, wout_ref, pw_ref, ps_ref,
                   ga_ref, nw_ref, sh_ref, sc_ref, x1_ref, hm_ref, ubuf):
    i = pl.program_id(0)
    tm = MIX_TM
    nsteps = SEQ // tm
    ubuf[0:POOL_HALO, :] = jnp.where(i > 0, up_ref[...], 0.0)
    ubuf[POOL_HALO:POOL_HALO + tm, :] = um_ref[...]
    ubuf[POOL_HALO + tm:2 * POOL_HALO + tm, :] = jnp.where(i < nsteps - 1, un_ref[...], 0.0)

    rc = MIX_CHUNK
    n = rc + 2 * POOL_HALO
    c = POOL_HALO
    ngroups = len(POOL_WINDOWS)
    ncol = D_MODEL // ngroups

    def pool_group(r0, g):
        w = POOL_WINDOWS[g]
        c0, c1 = g * POOL_GROUP_DIM, (g + 1) * POOL_GROUP_DIM
        e = ubuf[r0:r0 + n, c0:c1]
        if w == 2:
            win = e[c - 1:c - 1 + rc] + e[c:c + rc]
        else:
            f2 = e[0:n - 1] + e[1:n]
            if w == 4:
                win = f2[c - 2:c - 2 + rc] + f2[c:c + rc]
            else:
                f4 = f2[0:n - 3] + f2[2:n - 1]
                if w == 8:
                    win = f4[c - 4:c - 4 + rc] + f4[c:c + rc]
                else:
                    f8 = f4[0:n - 7] + f4[4:n - 3]
                    win = f8[c - 8:c - 8 + rc] + f8[c:c + rc]
        t = i * tm + r0 + lax.broadcasted_iota(jnp.int32, (rc, 1), 0)
        lo = jnp.clip(t - w // 2, 0, SEQ)
        hi = jnp.clip(t - w // 2 + w, 0, SEQ)
        cnt = (hi - lo).astype(F32)
        pg = win / cnt - um_ref[r0:r0 + rc, c0:c1]
        mixed = jnp.dot(pg.astype(BF16), pw_ref[g].astype(BF16), preferred_element_type=F32)
        return (mixed * ps_ref[:, c0:c1]).astype(BF16)

    def prenorm(r0, nrows):
        rows = slice(r0, r0 + nrows)
        hm_ref[rows, :] = _rms_modulate(x1_ref[rows, :], nw_ref[...], sh_ref[LATENT_ROW],
                                        sc_ref[LATENT_ROW]).astype(BF16)

    chunk_starts = list(range(0, tm, rc))
    for ci, r0 in enumerate(chunk_starts):
        rows = slice(r0, r0 + rc)
        attn_rows = attn_ref[rows, :]
        pooled = []
        for j in range(ngroups):
            cols = slice(j * ncol, (j + 1) * ncol)
            o = jnp.dot(attn_rows, wout_ref[0:ATTN_WIDTH, cols], preferred_element_type=F32)
            x1_ref[rows, cols] = x_ref[rows, cols] + ga_ref[LATENT_ROW, cols] * o
            pooled.append(pool_group(r0, j))
        pool_rows = jnp.concatenate(pooled, axis=1)
        for j in range(ngroups):
            cols = slice(j * ncol, (j + 1) * ncol)
            o = jnp.dot(pool_rows, wout_ref[ATTN_WIDTH:, cols], preferred_element_type=F32)
            x1_ref[rows, cols] += ga_ref[LATENT_ROW, cols] * o
            if ci > 0:
                prenorm(chunk_starts[ci - 1] + j * (rc // ngroups), rc // ngroups)
    for j in range(ngroups):
        prenorm(chunk_starts[-1] + j * (rc // ngroups), rc // ngroups)


def _mixout(x, attn, u, w_out_bf16, pool_w, pool_scale, mod, norm_w):
    tm = MIX_TM
    hb = tm // POOL_HALO
    nhalo = SEQ // POOL_HALO
    row = lambda i: (i, 0)
    fixed = lambda i: (0, 0)
    return pl.pallas_call(
        _mixout_kernel,
        grid=(SEQ // tm,),
        in_specs=[
            pl.BlockSpec((tm, D_MODEL), row),
            pl.BlockSpec((tm, ATTN_WIDTH), row),
            pl.BlockSpec((POOL_HALO, POOL_WIDTH), lambda i: (jnp.maximum(i * hb - 1, 0), 0)),
            pl.BlockSpec((tm, POOL_WIDTH), row),
            pl.BlockSpec((POOL_HALO, POOL_WIDTH), lambda i: (jnp.minimum((i + 1) * hb, nhalo - 1), 0)),
            pl.BlockSpec((D_MODEL, D_MODEL), fixed),
            pl.BlockSpec((len(POOL_WINDOWS), POOL_GROUP_DIM, POOL_GROUP_DIM), lambda i: (0, 0, 0)),
            pl.BlockSpec((1, POOL_WIDTH), fixed),
            _mod_spec(0),
            pl.BlockSpec((1, D_MODEL), fixed),
            _mod_spec(1),
            _mod_spec(2),
        ],
        out_specs=[pl.BlockSpec((tm, D_MODEL), row), pl.BlockSpec((tm, D_MODEL), row)],
        out_shape=[jax.ShapeDtypeStruct((SEQ, D_MODEL), F32),
                   jax.ShapeDtypeStruct((SEQ, D_MODEL), BF16)],
        scratch_shapes=[pltpu.VMEM((tm + 2 * POOL_HALO, POOL_WIDTH), F32)],
        compiler_params=_params("arbitrary"),
        name="mixout",
    )(x, attn, u, u, u, w_out_bf16, pool_w, pool_scale, mod, norm_w, mod, mod)


def _mlp_kernel(hm_ref, x1_hbm, wup_hbm, wdn_hbm, gm_ref, fw_ref, o_ref,
                wup_buf, wdn_buf, x1_buf, sems):
    i = pl.program_id(0)
    n_tiles = pl.num_programs(0)
    row_chunks = [slice(r0, r0 + MLP_RC) for r0 in range(0, MLP_TM, MLP_RC)]

    def weight_copies(f, slot):
        row0 = f * MLP_TF if isinstance(f, int) else pl.multiple_of(f * MLP_TF, MLP_TF)
        return (pltpu.make_async_copy(wup_hbm.at[f], wup_buf.at[slot], sems.at[slot]),
                pltpu.make_async_copy(wdn_hbm.at[pl.ds(row0, MLP_TF)],
                                      wdn_buf.at[slot], sems.at[2 + slot]))

    x1_copy = pltpu.make_async_copy(
        x1_hbm.at[pl.ds(pl.multiple_of(i * MLP_TM, MLP_TM), MLP_TM)], x1_buf, sems.at[4])
    x1_copy.start()

    @pl.when(i == 0)
    def _():
        for cp in weight_copies(0, 0):
            cp.start()

    def f_step(f, accumulate):
        slot = f % 2
        for cp in weight_copies(f, slot):
            cp.wait()
        nxt = jnp.where(f + 1 < MLP_NF, f + 1, 0)

        @pl.when(jnp.logical_or(f + 1 < MLP_NF, i + 1 < n_tiles))
        def _():
            for cp in weight_copies(nxt, 1 - slot):
                cp.start()

        for rows in row_chunks:
            a = jnp.maximum(jnp.dot(hm_ref[rows, :], wup_buf[slot], preferred_element_type=F32), 0.0)
            a = (a * a).astype(BF16)
            for c0 in range(0, D_MODEL, MLP_NC):
                part = jnp.dot(a, wdn_buf[slot, :, c0:c0 + MLP_NC], preferred_element_type=F32)
                if accumulate:
                    o_ref[rows, c0:c0 + MLP_NC] += part
                else:
                    o_ref[rows, c0:c0 + MLP_NC] = part

    f_step(0, accumulate=False)
    lax.fori_loop(1, MLP_NF, lambda f, carry: (f_step(f, accumulate=True), carry)[1], 0)

    x1_copy.wait()
    gm = gm_ref[LATENT_ROW]
    for rows in row_chunks:
        x2 = x1_buf[rows, :] + gm * o_ref[rows, :]
        y = x2 * lax.rsqrt(jnp.mean(x2 * x2, axis=-1, keepdims=True) + EPS)
        o_ref[rows, :] = y * fw_ref[...]


def _mlp(hm, x1, w_up_bf16, w_dn_bf16, mod, final_w):
    tm, tf = MLP_TM, MLP_TF
    assert MLP_NF % 2 == 0, "weight slot of tile 0 must be the same for every row tile"
    return pl.pallas_call(
        _mlp_kernel,
        grid=(SEQ // tm,),
        in_specs=[
            pl.BlockSpec((tm, D_MODEL), lambda i: (i, 0)),
            pl.BlockSpec(memory_space=pl.ANY),
            pl.BlockSpec(memory_space=pl.ANY),
            pl.BlockSpec(memory_space=pl.ANY),
            _mod_spec(3),
            pl.BlockSpec((1, D_MODEL), lambda i: (0, 0)),
        ],
        out_specs=pl.BlockSpec((tm, D_MODEL), lambda i: (i, 0)),
        out_shape=jax.ShapeDtypeStruct((SEQ, D_MODEL), F32),
        scratch_shapes=[pltpu.VMEM((2, D_MODEL, tf), BF16),
                        pltpu.VMEM((2, tf, D_MODEL), BF16),
                        pltpu.VMEM((tm, D_MODEL), F32),
                        pltpu.SemaphoreType.DMA((5,))],
        compiler_params=_params("arbitrary"),
        name="mlp",
    )(hm, x1, w_up_bf16, w_dn_bf16, mod, final_w)


def kernel(x, c, ctx, c_ctx, norm_attn_w, norm_mlp_w, w_ada, b_ada, w_in, attn_sink,
           pool_w, pool_scale, w_out, w_mlp_up, w_mlp_down, final_norm_w):
    assert x.shape == (1, SEQ, D_MODEL) and ctx.shape == (1, CTX_LEN, D_MODEL)
    assert w_ada.shape[0] == 1, "single layer"
    x2d, ctx2d = x[0], ctx[0]

    c_ctx_row = c_ctx[None, :]
    b_row = b_ada[0][None, :]
    n_early = 2 * D_MODEL
    mod_a = _ada(c, c_ctx_row, w_ada[0], b_row, n_early)

    nw_a = norm_attn_w[0][None, :]
    tables = tuple(jnp.asarray(t) for t in _rope_tables())
    qt, k, vt, u = _inproj(x2d, nw_a, mod_a, w_in[0], tables)
    kc, vtc = _ctx_kv(ctx2d, nw_a, mod_a, w_in[0])

    attn, w_out_b, w_up_b, w_dn_b, mod_b = _attn(
        qt, k, vt, kc, vtc, attn_sink[0].astype(F32), w_out[0], w_mlp_up[0], w_mlp_down[0],
        c, c_ctx_row, w_ada[0], b_row, n_early)

    x1, hm = _mixout(x2d, attn, u, w_out_b, pool_w[0], pool_scale[0][None, :], mod_b,
                     norm_mlp_w[0][None, :])
    out = _mlp(hm, x1, w_up_b, w_dn_b, mod_b, final_norm_w[None, :])
    return out[None]
```

```python
import functools

import numpy as np
import jax
import jax.numpy as jnp
from jax import lax
from jax.experimental import pallas as pl
from jax.experimental.pallas import tpu as pltpu

F32 = jnp.float32
BF16 = jnp.bfloat16

D_MODEL = 2048
SEQ = 8192
CTX_LEN = 256
GRID_W = 64
HEAD_DIM = 64
N_Q_HEADS = 16
N_KV_HEADS = 4
GQA = N_Q_HEADS // N_KV_HEADS
ATTN_WIDTH = N_Q_HEADS * HEAD_DIM
KV_WIDTH = N_KV_HEADS * HEAD_DIM
POOL_WINDOWS = (2, 4, 8, 16)
POOL_WIDTH = D_MODEL - ATTN_WIDTH
POOL_GROUP_DIM = POOL_WIDTH // len(POOL_WINDOWS)
IN_WIDTH = ATTN_WIDTH + 2 * KV_WIDTH + POOL_WIDTH
D_FF = 4 * D_MODEL
BLOCK = 128
ROPE_BASE = 10000.0
N_MOD = 6
EPS = 1e-6
NEG_INF = -1e30
LOG2E = 1.4426950408889634
Q_SCALE = HEAD_DIM ** -0.5 * LOG2E

LANES = 128
SUBLANES = 8
POOL_HALO = 8
LATENT_ROW = slice(0, 1)
CONTEXT_ROW = slice(1, 2)
VMEM_LIMIT = 56 * 1024 * 1024

ADA_TN = 512
INPROJ_TM = 512
ATTN_TQ = 512
CAST_JOB_ELEMS = 64 * 1024
MIX_TM = 512
MIX_CHUNK = 256
MLP_TM = 1024
MLP_TF = 512
MLP_RC = 512
MLP_NC = 1024
MLP_UNROLL = 5
MLP_NF = D_FF // MLP_TF


def _params(*sem):
    return pltpu.CompilerParams(dimension_semantics=sem, vmem_limit_bytes=VMEM_LIMIT)


def _mod_spec(chunk):
    return pl.BlockSpec((SUBLANES, D_MODEL), lambda *_: (0, chunk))


def _rms_modulate(x, norm_w, shift, scale):
    gain = norm_w * (1.0 + scale)
    y = x * lax.rsqrt(jnp.mean(x * x, axis=-1, keepdims=True) + EPS)
    return y * gain + shift


def _ada_block(c_ref, cctx_ref, w_ref, b_ref):
    cnd = jnp.concatenate([c_ref[...], cctx_ref[...], jnp.zeros((SUBLANES - 2, D_MODEL), F32)], axis=0)
    s = cnd * (1.0 / (1.0 + jnp.exp(-cnd)))
    return jnp.dot(s, w_ref[...], preferred_element_type=F32) + b_ref[...]


def _ada_kernel(c_ref, cctx_ref, w_ref, b_ref, o_ref):
    o_ref[...] = _ada_block(c_ref, cctx_ref, w_ref, b_ref)


def _ada(c, c_ctx, w_ada, b_ada, n):
    return pl.pallas_call(
        _ada_kernel,
        grid=(n // ADA_TN,),
        in_specs=[
            pl.BlockSpec((1, D_MODEL), lambda j: (0, 0)),
            pl.BlockSpec((1, D_MODEL), lambda j: (0, 0)),
            pl.BlockSpec((D_MODEL, ADA_TN), lambda j: (0, j)),
            pl.BlockSpec((1, ADA_TN), lambda j: (0, j)),
        ],
        out_specs=pl.BlockSpec((SUBLANES, ADA_TN), lambda j: (0, j)),
        out_shape=jax.ShapeDtypeStruct((SUBLANES, n), F32),
        compiler_params=_params("arbitrary"),
        name="ada",
    )(c, c_ctx, w_ada, b_ada)


def _rope_tables():
    half = HEAD_DIM // 2
    inv_freq = ROPE_BASE ** (-np.arange(0, half, 2, dtype=np.float64) / half)
    t = np.arange(SEQ)
    row = (t // GRID_W).astype(np.float64)
    col = (t % GRID_W).astype(np.float64)
    ang_r = row[:, None] * inv_freq[None, :]
    ang_c = col[:, None] * inv_freq[None, :]
    cos_t = Q_SCALE * np.concatenate([np.cos(ang_r), np.cos(ang_c)], axis=1).T
    sin_t = Q_SCALE * np.concatenate([np.sin(ang_r), np.sin(ang_c)], axis=1).T
    cos_l = np.concatenate([np.cos(ang_r), np.cos(ang_r), np.cos(ang_c), np.cos(ang_c)], axis=1)
    zeros = np.zeros_like(ang_r)
    s_up = np.concatenate([-np.sin(ang_r), zeros, -np.sin(ang_c), zeros], axis=1)
    s_dn = np.concatenate([zeros, np.sin(ang_r), zeros, np.sin(ang_c)], axis=1)
    tile2 = lambda a: np.concatenate([a, a], axis=1).astype(np.float32)
    return (cos_t.astype(np.float32), sin_t.astype(np.float32),
            tile2(cos_l), tile2(s_up), tile2(s_dn))


def _inproj_kernel(x_ref, nw_ref, sh_ref, sc_ref, w_ref, cos_t_ref, sin_t_ref,
                   cos_l_ref, sup_ref, sdn_ref, qt_ref, k_ref, vt_ref, u_ref, wb_ref):
    @pl.when(pl.program_id(0) == 0)
    def _():
        for c0 in range(0, IN_WIDTH, 2 * LANES):
            wb_ref[:, c0:c0 + 2 * LANES] = w_ref[:, c0:c0 + 2 * LANES].astype(BF16)

    h = _rms_modulate(x_ref[...], nw_ref[...], sh_ref[LATENT_ROW], sc_ref[LATENT_ROW]).astype(BF16)
    p = jnp.dot(h, wb_ref[...], preferred_element_type=F32)

    qt = p[:, :ATTN_WIDTH].T
    cr, cc = cos_t_ref[0:16, :], cos_t_ref[16:32, :]
    sr, sc = sin_t_ref[0:16, :], sin_t_ref[16:32, :]
    pieces = []
    for hd in range(N_Q_HEADS):
        b = hd * HEAD_DIM
        x1r, x2r = qt[b:b + 16, :], qt[b + 16:b + 32, :]
        x1c, x2c = qt[b + 32:b + 48, :], qt[b + 48:b + 64, :]
        pieces += [x1r * cr - x2r * sr, x1r * sr + x2r * cr,
                   x1c * cc - x2c * sc, x1c * sc + x2c * cc]
    qt_ref[...] = jnp.concatenate(pieces, axis=0).astype(BF16)

    k = p[:, ATTN_WIDTH:ATTN_WIDTH + KV_WIDTH]
    cos_l, s_up, s_dn = cos_l_ref[...], sup_ref[...], sdn_ref[...]
    kparts = []
    for g in range(KV_WIDTH // LANES):
        kg = k[:, g * LANES:(g + 1) * LANES]
        kparts.append(kg * cos_l + pltpu.roll(kg, LANES - 16, 1) * s_up
                      + pltpu.roll(kg, 16, 1) * s_dn)
    k_ref[...] = jnp.concatenate(kparts, axis=1).astype(BF16)

    vt_ref[...] = p[:, ATTN_WIDTH + KV_WIDTH:ATTN_WIDTH + 2 * KV_WIDTH].T.astype(BF16)
    u_ref[...] = p[:, ATTN_WIDTH + 2 * KV_WIDTH:]


def _inproj(x, norm_w, mod, w_in, tables):
    cos_t, sin_t, cos_l, s_up, s_dn = tables
    tm = INPROJ_TM
    row = lambda i: (i, 0)
    colb = lambda i: (0, i)
    fixed = lambda i: (0, 0)
    return pl.pallas_call(
        _inproj_kernel,
        grid=(SEQ // tm,),
        in_specs=[
            pl.BlockSpec((tm, D_MODEL), row),
            pl.BlockSpec((1, D_MODEL), fixed),
            _mod_spec(0),
            _mod_spec(1),
            pl.BlockSpec((D_MODEL, IN_WIDTH), fixed, pipeline_mode=pl.Buffered(1)),
            pl.BlockSpec((32, tm), colb),
            pl.BlockSpec((32, tm), colb),
            pl.BlockSpec((tm, LANES), row),
            pl.BlockSpec((tm, LANES), row),
            pl.BlockSpec((tm, LANES), row),
        ],
        out_specs=[
            pl.BlockSpec((ATTN_WIDTH, tm), colb),
            pl.BlockSpec((tm, KV_WIDTH), row),
            pl.BlockSpec((KV_WIDTH, tm), colb),
            pl.BlockSpec((tm, POOL_WIDTH), row),
        ],
        out_shape=[
            jax.ShapeDtypeStruct((ATTN_WIDTH, SEQ), BF16),
            jax.ShapeDtypeStruct((SEQ, KV_WIDTH), BF16),
            jax.ShapeDtypeStruct((KV_WIDTH, SEQ), BF16),
            jax.ShapeDtypeStruct((SEQ, POOL_WIDTH), F32),
        ],
        scratch_shapes=[pltpu.VMEM((D_MODEL, IN_WIDTH), BF16)],
        compiler_params=_params("arbitrary"),
        name="inproj",
    )(x, norm_w, mod, mod, w_in, cos_t, sin_t, cos_l, s_up, s_dn)


def _ctx_kv_kernel(x_ref, nw_ref, sh_ref, sc_ref, w_ref, k_ref, vt_ref):
    h = _rms_modulate(x_ref[...], nw_ref[...], sh_ref[CONTEXT_ROW], sc_ref[CONTEXT_ROW]).astype(BF16)
    p = jnp.dot(h, w_ref[...].astype(BF16), preferred_element_type=F32)
    k_ref[...] = p[:, :KV_WIDTH].astype(BF16)
    vt_ref[...] = p[:, KV_WIDTH:].T.astype(BF16)


def _ctx_kv(ctx, norm_w, mod, w_in):
    fixed = lambda i: (0, 0)
    kv_block = ATTN_WIDTH // (2 * KV_WIDTH)
    return pl.pallas_call(
        _ctx_kv_kernel,
        grid=(1,),
        in_specs=[
            pl.BlockSpec((CTX_LEN, D_MODEL), fixed),
            pl.BlockSpec((1, D_MODEL), fixed),
            _mod_spec(0),
            _mod_spec(1),
            pl.BlockSpec((D_MODEL, 2 * KV_WIDTH), lambda i: (0, kv_block)),
        ],
        out_specs=[
            pl.BlockSpec((CTX_LEN, KV_WIDTH), fixed),
            pl.BlockSpec((KV_WIDTH, CTX_LEN), fixed),
        ],
        out_shape=[
            jax.ShapeDtypeStruct((CTX_LEN, KV_WIDTH), BF16),
            jax.ShapeDtypeStruct((KV_WIDTH, CTX_LEN), BF16),
        ],
        compiler_params=_params("arbitrary"),
        name="ctx_kv",
    )(ctx, norm_w, mod, mod, w_in)


def _attn_kernel(qt_ref, kp_ref, km_ref, kn_ref, vtp_ref, vtm_ref, vtn_ref,
                 kc_ref, vtc_ref, sink_ref, wout_ref, wup_ref, wdn_ref,
                 c_ref, cctx_ref, wada_ref, bada_ref,
                 o_ref, wout_b_ref, wup_b_ref, wdn_b_ref, mod_ref):
    mod_ref[...] = _ada_block(c_ref, cctx_ref, wada_ref, bada_ref)

    i = pl.program_id(0)
    nsub = ATTN_TQ // BLOCK
    nblk = SEQ // BLOCK
    gw = GQA * BLOCK
    pw = 2 * BLOCK
    nkeys = 3 * BLOCK + CTX_LEN

    key_j = lax.broadcasted_iota(jnp.int32, (BLOCK, gw), 0)
    qry_i = lax.broadcasted_iota(jnp.int32, (BLOCK, gw), 1) % BLOCK
    neg = jnp.full((BLOCK, gw), NEG_INF, F32)
    zero = jnp.zeros((BLOCK, gw), F32)
    band_prev = jnp.where(key_j >= qry_i, zero, neg)
    band_next = jnp.where(key_j <= qry_i, zero, neg)

    kc = kc_ref[...]
    vtc = vtc_ref[...]
    ones_rows = jnp.ones((2 * SUBLANES, nkeys), BF16)
    sink_row = jnp.concatenate([jnp.full((1, BLOCK), sink_ref[hd] * LOG2E, F32)
                                for hd in range(N_Q_HEADS)], axis=1)

    windows = []
    for sb in range(nsub):
        n = i * nsub + sb
        lo, hi = sb * BLOCK, (sb + 1) * BLOCK
        k_prev = kp_ref[...] if sb == 0 else km_ref[lo - BLOCK:lo, :]
        k_next = kn_ref[...] if sb == nsub - 1 else km_ref[hi:hi + BLOCK, :]
        vt_prev = vtp_ref[...] if sb == 0 else vtm_ref[:, lo - BLOCK:lo]
        vt_next = vtn_ref[...] if sb == nsub - 1 else vtm_ref[:, hi:hi + BLOCK]
        windows.append(dict(
            kwin=jnp.concatenate([k_prev, km_ref[lo:hi, :], k_next, kc], axis=0),
            vtwin=jnp.concatenate([vt_prev, vtm_ref[:, lo:hi], vt_next, vtc], axis=1),
            bias_prev=band_prev + jnp.where(n > 0, 0.0, NEG_INF),
            bias_next=band_next + jnp.where(n < nblk - 1, 0.0, NEG_INF)))

    units = [(sb, h, half) for sb in range(nsub) for h in range(N_KV_HEADS)
             for half in range(GQA // 2)]

    def scores(sb, h):
        win = windows[sb]
        lo, hi = sb * BLOCK, (sb + 1) * BLOCK
        qh = jnp.concatenate(
            [qt_ref[(GQA * h + g) * HEAD_DIM:(GQA * h + g + 1) * HEAD_DIM, lo:hi]
             for g in range(GQA)], axis=1)
        blocks = []
        if h > 0:
            blocks.append(jnp.zeros((h * HEAD_DIM, gw), BF16))
        blocks.append(qh)
        if h < N_KV_HEADS - 1:
            blocks.append(jnp.zeros(((N_KV_HEADS - 1 - h) * HEAD_DIM, gw), BF16))
        s = jnp.dot(win["kwin"], jnp.concatenate(blocks, axis=0),
                    preferred_element_type=F32)
        s = jnp.concatenate([s[0:BLOCK] + win["bias_prev"], s[BLOCK:2 * BLOCK],
                             s[2 * BLOCK:3 * BLOCK] + win["bias_next"], s[3 * BLOCK:]], axis=0)
        return s, jnp.max(s, axis=0, keepdims=True)

    def probs(unit, s, smax):
        sb, h, half = unit
        hd0 = GQA * h + 2 * half
        cols = slice(half * pw, (half + 1) * pw)
        sink = sink_row[:, hd0 * BLOCK:(hd0 + 2) * BLOCK]
        m = jnp.maximum(smax[:, cols], sink)
        return jnp.exp2(s[:, cols] - m).astype(BF16), jnp.exp2(sink - m)

    def finish(unit, e, e_sink):
        sb, h, half = unit
        lo, hi = sb * BLOCK, (sb + 1) * BLOCK
        hd0 = GQA * h + 2 * half
        vt_ones = jnp.concatenate(
            [windows[sb]["vtwin"][h * HEAD_DIM:(h + 1) * HEAD_DIM, :], ones_rows], axis=0)
        o = jnp.dot(vt_ones, e, preferred_element_type=F32)
        denom = o[HEAD_DIM:HEAD_DIM + 1, :] + e_sink
        o = o[:HEAD_DIM, :] * (1.0 / denom)
        blk = jnp.concatenate([o[:, 0:BLOCK], o[:, BLOCK:2 * BLOCK]], axis=0)
        o_ref[lo:hi, hd0 * HEAD_DIM:(hd0 + 2) * HEAD_DIM] = blk.T.astype(BF16)

    cast_jobs = []
    for src, dst in ((wout_ref, wout_b_ref), (wup_ref, wup_b_ref), (wdn_ref, wdn_b_ref)):
        wc = max(LANES, CAST_JOB_ELEMS // src.shape[0])
        cast_jobs += [(src, dst, c0) + (wc,) for c0 in range(0, src.shape[1], wc)]

    def cast_piece(idx, n):
        for src, dst, c0, wc in cast_jobs[idx * len(cast_jobs) // n:(idx + 1) * len(cast_jobs) // n]:
            val = src[:, c0:c0 + wc].astype(BF16)
            if len(dst.shape) == 3:
                tf = dst.shape[2]
                dst[c0 // tf, :, c0 % tf:c0 % tf + wc] = val
            else:
                dst[:, c0:c0 + wc] = val

    nu = len(units)
    s_vals, p_vals = {}, {}
    for step in range(nu + 2):
        if step < nu and units[step][2] == 0:
            s_vals[units[step][:2]] = scores(*units[step][:2])
        if 1 <= step <= nu:
            unit = units[step - 1]
            p_vals[step - 1] = probs(unit, *s_vals[unit[:2]])
            if unit[2] == GQA // 2 - 1:
                del s_vals[unit[:2]]
            cast_piece(step - 1, nu)
        if step >= 2:
            finish(units[step - 2], *p_vals.pop(step - 2))


def _attn(qt, k, vt, kc, vtc, sink, w_out, w_up, w_dn, c, c_ctx, w_ada, b_ada, ada_col0):
    tq = ATTN_TQ
    r = tq // BLOCK
    nblk = SEQ // BLOCK
    nsteps = SEQ // tq
    ada_tn = (w_ada.shape[1] - ada_col0) // nsteps
    ada_b0 = ada_col0 // ada_tn
    fixed = lambda i: (0, 0)
    row = lambda i: (i, 0)
    slab = lambda w: pl.BlockSpec((w.shape[0] // nsteps, w.shape[1]), row)
    bf16_like = lambda w: jax.ShapeDtypeStruct(w.shape, BF16)
    prev = lambda i: jnp.maximum(i * r - 1, 0)
    nxt = lambda i: jnp.minimum((i + 1) * r, nblk - 1)
    return pl.pallas_call(
        _attn_kernel,
        grid=(SEQ // tq,),
        in_specs=[
            pl.BlockSpec((ATTN_WIDTH, tq), lambda i: (0, i)),
            pl.BlockSpec((BLOCK, KV_WIDTH), lambda i: (prev(i), 0)),
            pl.BlockSpec((tq, KV_WIDTH), lambda i: (i, 0)),
            pl.BlockSpec((BLOCK, KV_WIDTH), lambda i: (nxt(i), 0)),
            pl.BlockSpec((KV_WIDTH, BLOCK), lambda i: (0, prev(i))),
            pl.BlockSpec((KV_WIDTH, tq), lambda i: (0, i)),
            pl.BlockSpec((KV_WIDTH, BLOCK), lambda i: (0, nxt(i))),
            pl.BlockSpec((CTX_LEN, KV_WIDTH), fixed),
            pl.BlockSpec((KV_WIDTH, CTX_LEN), fixed),
            pl.BlockSpec(memory_space=pltpu.SMEM),
            slab(w_out), slab(w_up), slab(w_dn),
            pl.BlockSpec((1, D_MODEL), fixed),
            pl.BlockSpec((1, D_MODEL), fixed),
            pl.BlockSpec((D_MODEL, ada_tn), lambda i: (0, ada_b0 + i)),
            pl.BlockSpec((1, ada_tn), lambda i: (0, ada_b0 + i)),
        ],
        out_specs=[pl.BlockSpec((tq, ATTN_WIDTH), row), slab(w_out),
                   pl.BlockSpec((MLP_NF, w_up.shape[0] // nsteps, MLP_TF), lambda i: (0, i, 0)),
                   slab(w_dn),
                   pl.BlockSpec((SUBLANES, ada_tn), lambda i: (0, i))],
        out_shape=[jax.ShapeDtypeStruct((SEQ, ATTN_WIDTH), BF16),
                   bf16_like(w_out),
                   jax.ShapeDtypeStruct((MLP_NF, w_up.shape[0], MLP_TF), BF16),
                   bf16_like(w_dn),
                   jax.ShapeDtypeStruct((SUBLANES, w_ada.shape[1] - ada_col0), F32)],
        compiler_params=_params("arbitrary"),
        name="attn",
    )(qt, k, k, k, vt, vt, vt, kc, vtc, sink, w_out, w_up, w_dn, c, c_ctx, w_ada, b_ada)


def _mixout_kernel(x_ref, attn_ref, up_ref, um_ref, un_ref, wout_ref, pw_ref, ps_ref,
                   ga_ref, nw_ref, sh_ref, sc_ref, x1_ref, hm_ref, ubuf):
    i = pl.program_id(0)
    tm = MIX_TM
    nsteps = SEQ // tm
    ubuf[0:POOL_HALO, :] = jnp.where(i > 0, up_ref[...], 0.0)
    ubuf[POOL_HALO:POOL_HALO + tm, :] = um_ref[...]
    ubuf[POOL_HALO + tm:2 * POOL_HALO + tm, :] = jnp.where(i < nsteps - 1, un_ref[...], 0.0)

    rc = MIX_CHUNK
    n = rc + 2 * POOL_HALO
    c = POOL_HALO
    ngroups = len(POOL_WINDOWS)
    ncol = D_MODEL // ngroups

    def pool_group(r0, g):
        w = POOL_WINDOWS[g]
        c0, c1 = g * POOL_GROUP_DIM, (g + 1) * POOL_GROUP_DIM
        e = ubuf[r0:r0 + n, c0:c1]
        if w == 2:
            win = e[c - 1:c - 1 + rc] + e[c:c + rc]
        else:
            f2 = e[0:n - 1] + e[1:n]
            if w == 4:
                win = f2[c - 2:c - 2 + rc] + f2[c:c + rc]
            else:
                f4 = f2[0:n - 3] + f2[2:n - 1]
                if w == 8:
                    win = f4[c - 4:c - 4 + rc] + f4[c:c + rc]
                else:
                    f8 = f4[0:n - 7] + f4[4:n - 3]
                    win = f8[c - 8:c - 8 + rc] + f8[c:c + rc]
        t = i * tm + r0 + lax.broadcasted_iota(jnp.int32, (rc, 1), 0)
        lo = jnp.clip(t - w // 2, 0, SEQ)
        hi = jnp.clip(t - w // 2 + w, 0, SEQ)
        cnt = (hi - lo).astype(F32)
        pg = win / cnt - um_ref[r0:r0 + rc, c0:c1]
        mixed = jnp.dot(pg.astype(BF16), pw_ref[g].astype(BF16), preferred_element_type=F32)
        return (mixed * ps_ref[:, c0:c1]).astype(BF16)

    def prenorm(r0, nrows):
        rows = slice(r0, r0 + nrows)
        hm_ref[rows, :] = _rms_modulate(x1_ref[rows, :], nw_ref[...], sh_ref[LATENT_ROW],
                                        sc_ref[LATENT_ROW]).astype(BF16)

    chunk_starts = list(range(0, tm, rc))
    for ci, r0 in enumerate(chunk_starts):
        rows = slice(r0, r0 + rc)
        attn_rows = attn_ref[rows, :]
        pooled = []
        for j in range(ngroups):
            cols = slice(j * ncol, (j + 1) * ncol)
            o = jnp.dot(attn_rows, wout_ref[0:ATTN_WIDTH, cols], preferred_element_type=F32)
            x1_ref[rows, cols] = x_ref[rows, cols] + ga_ref[LATENT_ROW, cols] * o
            pooled.append(pool_group(r0, j))
        pool_rows = jnp.concatenate(pooled, axis=1)
        for j in range(ngroups):
            cols = slice(j * ncol, (j + 1) * ncol)
            o = jnp.dot(pool_rows, wout_ref[ATTN_WIDTH:, cols], preferred_element_type=F32)
            x1_ref[rows, cols] += ga_ref[LATENT_ROW, cols] * o
            if ci > 0:
                prenorm(chunk_starts[ci - 1] + j * (rc // ngroups), rc // ngroups)
    for j in range(ngroups):
        prenorm(chunk_starts[-1] + j * (rc // ngroups), rc // ngroups)


def _mixout(x, attn, u, w_out_bf16, pool_w, pool_scale, mod, norm_w):
    tm = MIX_TM
    hb = tm // POOL_HALO
    nhalo = SEQ // POOL_HALO
    row = lambda i: (i, 0)
    fixed = lambda i: (0, 0)
    return pl.pallas_call(
        _mixout_kernel,
        grid=(SEQ // tm,),
        in_specs=[
            pl.BlockSpec((tm, D_MODEL), row),
            pl.BlockSpec((tm, ATTN_WIDTH), row),
            pl.BlockSpec((POOL_HALO, POOL_WIDTH), lambda i: (jnp.maximum(i * hb - 1, 0), 0)),
            pl.BlockSpec((tm, POOL_WIDTH), row),
            pl.BlockSpec((POOL_HALO, POOL_WIDTH), lambda i: (jnp.minimum((i + 1) * hb, nhalo - 1), 0)),
            pl.BlockSpec((D_MODEL, D_MODEL), fixed),
            pl.BlockSpec((len(POOL_WINDOWS), POOL_GROUP_DIM, POOL_GROUP_DIM), lambda i: (0, 0, 0)),
            pl.BlockSpec((1, POOL_WIDTH), fixed),
            _mod_spec(0),
            pl.BlockSpec((1, D_MODEL), fixed),
            _mod_spec(1),
            _mod_spec(2),
        ],
        out_specs=[pl.BlockSpec((tm, D_MODEL), row), pl.BlockSpec((tm, D_MODEL), row)],
        out_shape=[jax.ShapeDtypeStruct((SEQ, D_MODEL), F32),
                   jax.ShapeDtypeStruct((SEQ, D_MODEL), BF16)],
        scratch_shapes=[pltpu.VMEM((tm + 2 * POOL_HALO, POOL_WIDTH), F32)],
        compiler_params=_params("arbitrary"),
        name="mixout",
    )(x, attn, u, u, u, w_out_bf16, pool_w, pool_scale, mod, norm_w, mod, mod)


def _mlp_kernel(hm_ref, x1_hbm, wup_hbm, wdn_hbm, gm_ref, fw_ref, o_ref,
                wup_buf, wdn_buf, x1_buf, sems):
    i = pl.program_id(0)
    n_tiles = pl.num_programs(0)
    row_chunks = [slice(r0, r0 + MLP_RC) for r0 in range(0, MLP_TM, MLP_RC)]

    def weight_copies(f, slot):
        row0 = f * MLP_TF if isinstance(f, int) else pl.multiple_of(f * MLP_TF, MLP_TF)
        return (pltpu.make_async_copy(wup_hbm.at[f], wup_buf.at[slot], sems.at[slot]),
                pltpu.make_async_copy(wdn_hbm.at[pl.ds(row0, MLP_TF)],
                                      wdn_buf.at[slot], sems.at[2 + slot]))

    x1_copy = pltpu.make_async_copy(
        x1_hbm.at[pl.ds(pl.multiple_of(i * MLP_TM, MLP_TM), MLP_TM)], x1_buf, sems.at[4])
    x1_copy.start()

    @pl.when(i == 0)
    def _():
        for cp in weight_copies(0, 0):
            cp.start()

    def f_step(f, accumulate):
        slot = f % 2
        for cp in weight_copies(f, slot):
            cp.wait()
        nxt = jnp.where(f + 1 < MLP_NF, f + 1, 0)
        for ci, rows in enumerate(row_chunks):
            a = jnp.maximum(jnp.dot(hm_ref[rows, :], wup_buf[slot], preferred_element_type=F32), 0.0)
            a = (a * a).astype(BF16)
            if ci == 0:
                for cp in weight_copies(nxt, 1 - slot):
                    cp.start()
            for c0 in range(0, D_MODEL, MLP_NC):
                part = jnp.dot(a, wdn_buf[slot, :, c0:c0 + MLP_NC], preferred_element_type=F32)
                if accumulate:
                    o_ref[rows, c0:c0 + MLP_NC] += part
                else:
                    o_ref[rows, c0:c0 + MLP_NC] = part

    f_step(0, accumulate=False)
    lax.fori_loop(1, MLP_NF, lambda f, carry: (f_step(f, accumulate=True), carry)[1], 0,
                  unroll=MLP_UNROLL)

    @pl.when(i == n_tiles - 1)
    def _():
        for cp in weight_copies(0, 0):
            cp.wait()

    x1_copy.wait()
    gm = gm_ref[LATENT_ROW]
    for rows in row_chunks:
        x2 = x1_buf[rows, :] + gm * o_ref[rows, :]
        y = x2 * lax.rsqrt(jnp.mean(x2 * x2, axis=-1, keepdims=True) + EPS)
        o_ref[rows, :] = y * fw_ref[...]


def _mlp(hm, x1, w_up_bf16, w_dn_bf16, mod, final_w):
    tm, tf = MLP_TM, MLP_TF
    assert MLP_NF % 2 == 0, "weight slot of tile 0 must be the same for every row tile"
    return pl.pallas_call(
        _mlp_kernel,
        grid=(SEQ // tm,),
        in_specs=[
            pl.BlockSpec((tm, D_MODEL), lambda i: (i, 0)),
            pl.BlockSpec(memory_space=pl.ANY),
            pl.BlockSpec(memory_space=pl.ANY),
            pl.BlockSpec(memory_space=pl.ANY),
            _mod_spec(3),
            pl.BlockSpec((1, D_MODEL), lambda i: (0, 0)),
        ],
        out_specs=pl.BlockSpec((tm, D_MODEL), lambda i: (i, 0)),
        out_shape=jax.ShapeDtypeStruct((SEQ, D_MODEL), F32),
        scratch_shapes=[pltpu.VMEM((2, D_MODEL, tf), BF16),
                        pltpu.VMEM((2, tf, D_MODEL), BF16),
                        pltpu.VMEM((tm, D_MODEL), F32),
                        pltpu.SemaphoreType.DMA((5,))],
        compiler_params=_params("arbitrary"),
        name="mlp",
    )(hm, x1, w_up_bf16, w_dn_bf16, mod, final_w)


def kernel(x, c, ctx, c_ctx, norm_attn_w, norm_mlp_w, w_ada, b_ada, w_in, attn_sink,
           pool_w, pool_scale, w_out, w_mlp_up, w_mlp_down, final_norm_w):
    assert x.shape == (1, SEQ, D_MODEL) and ctx.shape == (1, CTX_LEN, D_MODEL)
    assert w_ada.shape[0] == 1, "single layer"
    x2d, ctx2d = x[0], ctx[0]

    c_ctx_row = c_ctx[None, :]
    b_row = b_ada[0][None, :]
    n_early = 2 * D_MODEL
    mod_a = _ada(c, c_ctx_row, w_ada[0], b_row, n_early)

    nw_a = norm_attn_w[0][None, :]
    tables = tuple(jnp.asarray(t) for t in _rope_tables())
    qt, k, vt, u = _inproj(x2d, nw_a, mod_a, w_in[0], tables)
    kc, vtc = _ctx_kv(ctx2d, nw_a, mod_a, w_in[0])

    attn, w_out_b, w_up_b, w_dn_b, mod_b = _attn(
        qt, k, vt, kc, vtc, attn_sink[0].astype(F32), w_out[0], w_mlp_up[0], w_mlp_down[0],
        c, c_ctx_row, w_ada[0], b_row, n_early)

    x1, hm = _mixout(x2d, attn, u, w_out_b, pool_w[0], pool_scale[0][None, :], mod_b,
                     norm_mlp_w[0][None, :])
    out = _mlp(hm, x1, w_up_b, w_dn_b, mod_b, final_norm_w[None, :])
    return out[None]
```

```python
import functools

import numpy as np
import jax
import jax.numpy as jnp
from jax import lax
from jax.experimental import pallas as pl
from jax.experimental.pallas import tpu as pltpu

F32 = jnp.float32
BF16 = jnp.bfloat16

D_MODEL = 2048
SEQ = 8192
CTX_LEN = 256
GRID_W = 64
HEAD_DIM = 64
N_Q_HEADS = 16
N_KV_HEADS = 4
GQA = N_Q_HEADS // N_KV_HEADS
ATTN_WIDTH = N_Q_HEADS * HEAD_DIM
KV_WIDTH = N_KV_HEADS * HEAD_DIM
POOL_WINDOWS = (2, 4, 8, 16)
POOL_WIDTH = D_MODEL - ATTN_WIDTH
POOL_GROUP_DIM = POOL_WIDTH // len(POOL_WINDOWS)
IN_WIDTH = ATTN_WIDTH + 2 * KV_WIDTH + POOL_WIDTH
D_FF = 4 * D_MODEL
BLOCK = 128
ROPE_BASE = 10000.0
N_MOD = 6
EPS = 1e-6
NEG_INF = -1e30
LOG2E = 1.4426950408889634
Q_SCALE = HEAD_DIM ** -0.5 * LOG2E

LANES = 128
SUBLANES = 8
POOL_HALO = 8
LATENT_ROW = slice(0, 1)
CONTEXT_ROW = slice(1, 2)
VMEM_LIMIT = 56 * 1024 * 1024

ADA_TN = 512
INPROJ_TM = 512
ATTN_TQ = 512
CAST_JOB_ELEMS = 64 * 1024
MIX_TM = 512
MIX_CHUNK = 256
MLP_TM = 1024
MLP_TF = 1024
MLP_RC = 512
MLP_NF = D_FF // MLP_TF
MLP_XR = MLP_TM // MLP_NF


def _params(*sem):
    return pltpu.CompilerParams(dimension_semantics=sem, vmem_limit_bytes=VMEM_LIMIT)


def _mod_spec(chunk):
    return pl.BlockSpec((SUBLANES, D_MODEL), lambda *_: (0, chunk))


def _rms_modulate(x, norm_w, shift, scale):
    gain = norm_w * (1.0 + scale)
    y = x * lax.rsqrt(jnp.mean(x * x, axis=-1, keepdims=True) + EPS)
    return y * gain + shift


def _ada_block(c_ref, cctx_ref, w_ref, b_ref):
    cnd = jnp.concatenate([c_ref[...], cctx_ref[...], jnp.zeros((SUBLANES - 2, D_MODEL), F32)], axis=0)
    s = cnd * (1.0 / (1.0 + jnp.exp(-cnd)))
    return jnp.dot(s, w_ref[...], preferred_element_type=F32) + b_ref[...]


def _ada_kernel(c_ref, cctx_ref, w_ref, b_ref, o_ref):
    o_ref[...] = _ada_block(c_ref, cctx_ref, w_ref, b_ref)


def _ada(c, c_ctx, w_ada, b_ada, n):
    return pl.pallas_call(
        _ada_kernel,
        grid=(n // ADA_TN,),
        in_specs=[
            pl.BlockSpec((1, D_MODEL), lambda j: (0, 0)),
            pl.BlockSpec((1, D_MODEL), lambda j: (0, 0)),
            pl.BlockSpec((D_MODEL, ADA_TN), lambda j: (0, j)),
            pl.BlockSpec((1, ADA_TN), lambda j: (0, j)),
        ],
        out_specs=pl.BlockSpec((SUBLANES, ADA_TN), lambda j: (0, j)),
        out_shape=jax.ShapeDtypeStruct((SUBLANES, n), F32),
        compiler_params=_params("arbitrary"),
        name="ada",
    )(c, c_ctx, w_ada, b_ada)


def _rope_tables():
    half = HEAD_DIM // 2
    inv_freq = ROPE_BASE ** (-np.arange(0, half, 2, dtype=np.float64) / half)
    t = np.arange(SEQ)
    row = (t // GRID_W).astype(np.float64)
    col = (t % GRID_W).astype(np.float64)
    ang_r = row[:, None] * inv_freq[None, :]
    ang_c = col[:, None] * inv_freq[None, :]
    cos_t = Q_SCALE * np.concatenate([np.cos(ang_r), np.cos(ang_c)], axis=1).T
    sin_t = Q_SCALE * np.concatenate([np.sin(ang_r), np.sin(ang_c)], axis=1).T
    cos_l = np.concatenate([np.cos(ang_r), np.cos(ang_r), np.cos(ang_c), np.cos(ang_c)], axis=1)
    zeros = np.zeros_like(ang_r)
    s_up = np.concatenate([-np.sin(ang_r), zeros, -np.sin(ang_c), zeros], axis=1)
    s_dn = np.concatenate([zeros, np.sin(ang_r), zeros, np.sin(ang_c)], axis=1)
    tile2 = lambda a: np.concatenate([a, a], axis=1).astype(np.float32)
    return (cos_t.astype(np.float32), sin_t.astype(np.float32),
            tile2(cos_l), tile2(s_up), tile2(s_dn))


def _inproj_kernel(x_ref, nw_ref, sh_ref, sc_ref, w_ref, cos_t_ref, sin_t_ref,
                   cos_l_ref, sup_ref, sdn_ref, qt_ref, k_ref, vt_ref, u_ref, wb_ref):
    @pl.when(pl.program_id(0) == 0)
    def _():
        for c0 in range(0, IN_WIDTH, 2 * LANES):
            wb_ref[:, c0:c0 + 2 * LANES] = w_ref[:, c0:c0 + 2 * LANES].astype(BF16)

    h = _rms_modulate(x_ref[...], nw_ref[...], sh_ref[LATENT_ROW], sc_ref[LATENT_ROW]).astype(BF16)
    p = jnp.dot(h, wb_ref[...], preferred_element_type=F32)

    qt = p[:, :ATTN_WIDTH].T
    cr, cc = cos_t_ref[0:16, :], cos_t_ref[16:32, :]
    sr, sc = sin_t_ref[0:16, :], sin_t_ref[16:32, :]
    pieces = []
    for hd in range(N_Q_HEADS):
        b = hd * HEAD_DIM
        x1r, x2r = qt[b:b + 16, :], qt[b + 16:b + 32, :]
        x1c, x2c = qt[b + 32:b + 48, :], qt[b + 48:b + 64, :]
        pieces += [x1r * cr - x2r * sr, x1r * sr + x2r * cr,
                   x1c * cc - x2c * sc, x1c * sc + x2c * cc]
    qt_ref[...] = jnp.concatenate(pieces, axis=0).astype(BF16)

    k = p[:, ATTN_WIDTH:ATTN_WIDTH + KV_WIDTH]
    cos_l, s_up, s_dn = cos_l_ref[...], sup_ref[...], sdn_ref[...]
    kparts = []
    for g in range(KV_WIDTH // LANES):
        kg = k[:, g * LANES:(g + 1) * LANES]
        kparts.append(kg * cos_l + pltpu.roll(kg, LANES - 16, 1) * s_up
                      + pltpu.roll(kg, 16, 1) * s_dn)
    k_ref[...] = jnp.concatenate(kparts, axis=1).astype(BF16)

    vt_ref[...] = p[:, ATTN_WIDTH + KV_WIDTH:ATTN_WIDTH + 2 * KV_WIDTH].T.astype(BF16)
    u_ref[...] = p[:, ATTN_WIDTH + 2 * KV_WIDTH:]


def _inproj(x, norm_w, mod, w_in, tables):
    cos_t, sin_t, cos_l, s_up, s_dn = tables
    tm = INPROJ_TM
    row = lambda i: (i, 0)
    colb = lambda i: (0, i)
    fixed = lambda i: (0, 0)
    return pl.pallas_call(
        _inproj_kernel,
        grid=(SEQ // tm,),
        in_specs=[
            pl.BlockSpec((tm, D_MODEL), row),
            pl.BlockSpec((1, D_MODEL), fixed),
            _mod_spec(0),
            _mod_spec(1),
            pl.BlockSpec((D_MODEL, IN_WIDTH), fixed, pipeline_mode=pl.Buffered(1)),
            pl.BlockSpec((32, tm), colb),
            pl.BlockSpec((32, tm), colb),
            pl.BlockSpec((tm, LANES), row),
            pl.BlockSpec((tm, LANES), row),
            pl.BlockSpec((tm, LANES), row),
        ],
        out_specs=[
            pl.BlockSpec((ATTN_WIDTH, tm), colb),
            pl.BlockSpec((tm, KV_WIDTH), row),
            pl.BlockSpec((KV_WIDTH, tm), colb),
            pl.BlockSpec((tm, POOL_WIDTH), row),
        ],
        out_shape=[
            jax.ShapeDtypeStruct((ATTN_WIDTH, SEQ), BF16),
            jax.ShapeDtypeStruct((SEQ, KV_WIDTH), BF16),
            jax.ShapeDtypeStruct((KV_WIDTH, SEQ), BF16),
            jax.ShapeDtypeStruct((SEQ, POOL_WIDTH), F32),
        ],
        scratch_shapes=[pltpu.VMEM((D_MODEL, IN_WIDTH), BF16)],
        compiler_params=_params("arbitrary"),
        name="inproj",
    )(x, norm_w, mod, mod, w_in, cos_t, sin_t, cos_l, s_up, s_dn)


def _ctx_kv_kernel(x_ref, nw_ref, sh_ref, sc_ref, w_ref, k_ref, vt_ref):
    h = _rms_modulate(x_ref[...], nw_ref[...], sh_ref[CONTEXT_ROW], sc_ref[CONTEXT_ROW]).astype(BF16)
    p = jnp.dot(h, w_ref[...].astype(BF16), preferred_element_type=F32)
    k_ref[...] = p[:, :KV_WIDTH].astype(BF16)
    vt_ref[...] = p[:, KV_WIDTH:].T.astype(BF16)


def _ctx_kv(ctx, norm_w, mod, w_in):
    fixed = lambda i: (0, 0)
    kv_block = ATTN_WIDTH // (2 * KV_WIDTH)
    return pl.pallas_call(
        _ctx_kv_kernel,
        grid=(1,),
        in_specs=[
            pl.BlockSpec((CTX_LEN, D_MODEL), fixed),
            pl.BlockSpec((1, D_MODEL), fixed),
            _mod_spec(0),
            _mod_spec(1),
            pl.BlockSpec((D_MODEL, 2 * KV_WIDTH), lambda i: (0, kv_block)),
        ],
        out_specs=[
            pl.BlockSpec((CTX_LEN, KV_WIDTH), fixed),
            pl.BlockSpec((KV_WIDTH, CTX_LEN), fixed),
        ],
        out_shape=[
            jax.ShapeDtypeStruct((CTX_LEN, KV_WIDTH), BF16),
            jax.ShapeDtypeStruct((KV_WIDTH, CTX_LEN), BF16),
        ],
        compiler_params=_params("arbitrary"),
        name="ctx_kv",
    )(ctx, norm_w, mod, mod, w_in)


def _attn_kernel(qt_ref, kp_ref, km_ref, kn_ref, vtp_ref, vtm_ref, vtn_ref,
                 kc_ref, vtc_ref, sink_ref, wout_ref, wup_ref, wdn_ref,
                 c_ref, cctx_ref, wada_ref, bada_ref,
                 o_ref, wout_b_ref, wup_b_ref, wdn_b_ref, mod_ref):
    mod_ref[...] = _ada_block(c_ref, cctx_ref, wada_ref, bada_ref)

    i = pl.program_id(0)
    nsub = ATTN_TQ // BLOCK
    nblk = SEQ // BLOCK
    gw = GQA * BLOCK
    pw = 2 * BLOCK
    nkeys = 3 * BLOCK + CTX_LEN

    key_j = lax.broadcasted_iota(jnp.int32, (BLOCK, gw), 0)
    qry_i = lax.broadcasted_iota(jnp.int32, (BLOCK, gw), 1) % BLOCK
    neg = jnp.full((BLOCK, gw), NEG_INF, F32)
    zero = jnp.zeros((BLOCK, gw), F32)
    band_prev = jnp.where(key_j >= qry_i, zero, neg)
    band_next = jnp.where(key_j <= qry_i, zero, neg)

    kc = kc_ref[...]
    vtc = vtc_ref[...]
    ones_rows = jnp.ones((2 * SUBLANES, nkeys), BF16)
    sink_row = jnp.concatenate([jnp.full((1, BLOCK), sink_ref[hd] * LOG2E, F32)
                                for hd in range(N_Q_HEADS)], axis=1)

    windows = []
    for sb in range(nsub):
        n = i * nsub + sb
        lo, hi = sb * BLOCK, (sb + 1) * BLOCK
        k_prev = kp_ref[...] if sb == 0 else km_ref[lo - BLOCK:lo, :]
        k_next = kn_ref[...] if sb == nsub - 1 else km_ref[hi:hi + BLOCK, :]
        vt_prev = vtp_ref[...] if sb == 0 else vtm_ref[:, lo - BLOCK:lo]
        vt_next = vtn_ref[...] if sb == nsub - 1 else vtm_ref[:, hi:hi + BLOCK]
        windows.append(dict(
            kwin=jnp.concatenate([k_prev, km_ref[lo:hi, :], k_next, kc], axis=0),
            vtwin=jnp.concatenate([vt_prev, vtm_ref[:, lo:hi], vt_next, vtc], axis=1),
            bias_prev=band_prev + jnp.where(n > 0, 0.0, NEG_INF),
            bias_next=band_next + jnp.where(n < nblk - 1, 0.0, NEG_INF)))

    units = [(sb, h, half) for sb in range(nsub) for h in range(N_KV_HEADS)
             for half in range(GQA // 2)]

    def scores(sb, h):
        win = windows[sb]
        lo, hi = sb * BLOCK, (sb + 1) * BLOCK
        qh = jnp.concatenate(
            [qt_ref[(GQA * h + g) * HEAD_DIM:(GQA * h + g + 1) * HEAD_DIM, lo:hi]
             for g in range(GQA)], axis=1)
        blocks = []
        if h > 0:
            blocks.append(jnp.zeros((h * HEAD_DIM, gw), BF16))
        blocks.append(qh)
        if h < N_KV_HEADS - 1:
            blocks.append(jnp.zeros(((N_KV_HEADS - 1 - h) * HEAD_DIM, gw), BF16))
        s = jnp.dot(win["kwin"], jnp.concatenate(blocks, axis=0),
                    preferred_element_type=F32)
        s = jnp.concatenate([s[0:BLOCK] + win["bias_prev"], s[BLOCK:2 * BLOCK],
                             s[2 * BLOCK:3 * BLOCK] + win["bias_next"], s[3 * BLOCK:]], axis=0)
        return s, jnp.max(s, axis=0, keepdims=True)

    def probs(unit, s, smax):
        sb, h, half = unit
        hd0 = GQA * h + 2 * half
        cols = slice(half * pw, (half + 1) * pw)
        sink = sink_row[:, hd0 * BLOCK:(hd0 + 2) * BLOCK]
        m = jnp.maximum(smax[:, cols], sink)
        return jnp.exp2(s[:, cols] - m).astype(BF16), jnp.exp2(sink - m)

    def finish(unit, e, e_sink):
        sb, h, half = unit
        lo, hi = sb * BLOCK, (sb + 1) * BLOCK
        hd0 = GQA * h + 2 * half
        vt_ones = jnp.concatenate(
            [windows[sb]["vtwin"][h * HEAD_DIM:(h + 1) * HEAD_DIM, :], ones_rows], axis=0)
        o = jnp.dot(vt_ones, e, preferred_element_type=F32)
        denom = o[HEAD_DIM:HEAD_DIM + 1, :] + e_sink
        o = o[:HEAD_DIM, :] * (1.0 / denom)
        blk = jnp.concatenate([o[:, 0:BLOCK], o[:, BLOCK:2 * BLOCK]], axis=0)
        o_ref[lo:hi, hd0 * HEAD_DIM:(hd0 + 2) * HEAD_DIM] = blk.T.astype(BF16)

    cast_jobs = []
    for src, dst in ((wout_ref, wout_b_ref), (wup_ref, wup_b_ref), (wdn_ref, wdn_b_ref)):
        wc = max(LANES, CAST_JOB_ELEMS // src.shape[0])
        cast_jobs += [(src, dst, c0) + (wc,) for c0 in range(0, src.shape[1], wc)]

    def cast_piece(idx, n):
        for src, dst, c0, wc in cast_jobs[idx * len(cast_jobs) // n:(idx + 1) * len(cast_jobs) // n]:
            val = src[:, c0:c0 + wc].astype(BF16)
            if len(dst.shape) == 3:
                tf = dst.shape[2]
                dst[c0 // tf, :, c0 % tf:c0 % tf + wc] = val
            else:
                dst[:, c0:c0 + wc] = val

    nu = len(units)
    s_vals, p_vals = {}, {}
    for step in range(nu + 2):
        if step < nu and units[step][2] == 0:
            s_vals[units[step][:2]] = scores(*units[step][:2])
        if 1 <= step <= nu:
            unit = units[step - 1]
            p_vals[step - 1] = probs(unit, *s_vals[unit[:2]])
            if unit[2] == GQA // 2 - 1:
                del s_vals[unit[:2]]
            cast_piece(step - 1, nu)
        if step >= 2:
            finish(units[step - 2], *p_vals.pop(step - 2))


def _attn(qt, k, vt, kc, vtc, sink, w_out, w_up, w_dn, c, c_ctx, w_ada, b_ada, ada_col0):
    tq = ATTN_TQ
    r = tq // BLOCK
    nblk = SEQ // BLOCK
    nsteps = SEQ // tq
    ada_tn = (w_ada.shape[1] - ada_col0) // nsteps
    ada_b0 = ada_col0 // ada_tn
    fixed = lambda i: (0, 0)
    row = lambda i: (i, 0)
    slab = lambda w: pl.BlockSpec((w.shape[0] // nsteps, w.shape[1]), row)
    bf16_like = lambda w: jax.ShapeDtypeStruct(w.shape, BF16)
    prev = lambda i: jnp.maximum(i * r - 1, 0)
    nxt = lambda i: jnp.minimum((i + 1) * r, nblk - 1)
    return pl.pallas_call(
        _attn_kernel,
        grid=(SEQ // tq,),
        in_specs=[
            pl.BlockSpec((ATTN_WIDTH, tq), lambda i: (0, i)),
            pl.BlockSpec((BLOCK, KV_WIDTH), lambda i: (prev(i), 0)),
            pl.BlockSpec((tq, KV_WIDTH), lambda i: (i, 0)),
            pl.BlockSpec((BLOCK, KV_WIDTH), lambda i: (nxt(i), 0)),
            pl.BlockSpec((KV_WIDTH, BLOCK), lambda i: (0, prev(i))),
            pl.BlockSpec((KV_WIDTH, tq), lambda i: (0, i)),
            pl.BlockSpec((KV_WIDTH, BLOCK), lambda i: (0, nxt(i))),
            pl.BlockSpec((CTX_LEN, KV_WIDTH), fixed),
            pl.BlockSpec((KV_WIDTH, CTX_LEN), fixed),
            pl.BlockSpec(memory_space=pltpu.SMEM),
            slab(w_out), slab(w_up), slab(w_dn),
            pl.BlockSpec((1, D_MODEL), fixed),
            pl.BlockSpec((1, D_MODEL), fixed),
            pl.BlockSpec((D_MODEL, ada_tn), lambda i: (0, ada_b0 + i)),
            pl.BlockSpec((1, ada_tn), lambda i: (0, ada_b0 + i)),
        ],
        out_specs=[pl.BlockSpec((tq, ATTN_WIDTH), row), slab(w_out),
                   pl.BlockSpec((MLP_NF, w_up.shape[0] // nsteps, MLP_TF), lambda i: (0, i, 0)),
                   slab(w_dn),
                   pl.BlockSpec((SUBLANES, ada_tn), lambda i: (0, i))],
        out_shape=[jax.ShapeDtypeStruct((SEQ, ATTN_WIDTH), BF16),
                   bf16_like(w_out),
                   jax.ShapeDtypeStruct((MLP_NF, w_up.shape[0], MLP_TF), BF16),
                   bf16_like(w_dn),
                   jax.ShapeDtypeStruct((SUBLANES, w_ada.shape[1] - ada_col0), F32)],
        compiler_params=_params("arbitrary"),
        name="attn",
    )(qt, k, k, k, vt, vt, vt, kc, vtc, sink, w_out, w_up, w_dn, c, c_ctx, w_ada, b_ada)


def _mixout_kernel(x_ref, attn_ref, up_ref, um_ref, un_ref, wout_ref, pw_ref, ps_ref,
                   ga_ref, nw_ref, sh_ref, sc_ref, x1_ref, hm_ref, ubuf):
    i = pl.program_id(0)
    tm = MIX_TM
    nsteps = SEQ // tm
    ubuf[0:POOL_HALO, :] = jnp.where(i > 0, up_ref[...], 0.0)
    ubuf[POOL_HALO:POOL_HALO + tm, :] = um_ref[...]
    ubuf[POOL_HALO + tm:2 * POOL_HALO + tm, :] = jnp.where(i < nsteps - 1, un_ref[...], 0.0)

    rc = MIX_CHUNK
    n = rc + 2 * POOL_HALO
    c = POOL_HALO
    ngroups = len(POOL_WINDOWS)
    ncol = D_MODEL // ngroups

    def pool_group(r0, g):
        w = POOL_WINDOWS[g]
        c0, c1 = g * POOL_GROUP_DIM, (g + 1) * POOL_GROUP_DIM
        e = ubuf[r0:r0 + n, c0:c1]
        if w == 2:
            win = e[c - 1:c - 1 + rc] + e[c:c + rc]
        else:
            f2 = e[0:n - 1] + e[1:n]
            if w == 4:
                win = f2[c - 2:c - 2 + rc] + f2[c:c + rc]
            else:
                f4 = f2[0:n - 3] + f2[2:n - 1]
                if w == 8:
                    win = f4[c - 4:c - 4 + rc] + f4[c:c + rc]
                else:
                    f8 = f4[0:n - 7] + f4[4:n - 3]
                    win = f8[c - 8:c - 8 + rc] + f8[c:c + rc]
        t = i * tm + r0 + lax.broadcasted_iota(jnp.int32, (rc, 1), 0)
        lo = jnp.clip(t - w // 2, 0, SEQ)
        hi = jnp.clip(t - w // 2 + w, 0, SEQ)
        cnt = (hi - lo).astype(F32)
        pg = win / cnt - um_ref[r0:r0 + rc, c0:c1]
        mixed = jnp.dot(pg.astype(BF16), pw_ref[g].astype(BF16), preferred_element_type=F32)
        return (mixed * ps_ref[:, c0:c1]).astype(BF16)

    def prenorm(r0, nrows):
        rows = slice(r0, r0 + nrows)
        hm_ref[rows, :] = _rms_modulate(x1_ref[rows, :], nw_ref[...], sh_ref[LATENT_ROW],
                                        sc_ref[LATENT_ROW]).astype(BF16)

    chunk_starts = list(range(0, tm, rc))
    for ci, r0 in enumerate(chunk_starts):
        rows = slice(r0, r0 + rc)
        attn_rows = attn_ref[rows, :]
        pooled = []
        for j in range(ngroups):
            cols = slice(j * ncol, (j + 1) * ncol)
            o = jnp.dot(attn_rows, wout_ref[0:ATTN_WIDTH, cols], preferred_element_type=F32)
            x1_ref[rows, cols] = x_ref[rows, cols] + ga_ref[LATENT_ROW, cols] * o
            pooled.append(pool_group(r0, j))
        pool_rows = jnp.concatenate(pooled, axis=1)
        for j in range(ngroups):
            cols = slice(j * ncol, (j + 1) * ncol)
            o = jnp.dot(pool_rows, wout_ref[ATTN_WIDTH:, cols], preferred_element_type=F32)
            x1_ref[rows, cols] += ga_ref[LATENT_ROW, cols] * o
            if ci > 0:
                prenorm(chunk_starts[ci - 1] + j * (rc // ngroups), rc // ngroups)
    for j in range(ngroups):
        prenorm(chunk_starts[-1] + j * (rc // ngroups), rc // ngroups)


def _mixout(x, attn, u, w_out_bf16, pool_w, pool_scale, mod, norm_w):
    tm = MIX_TM
    hb = tm // POOL_HALO
    nhalo = SEQ // POOL_HALO
    row = lambda i: (i, 0)
    fixed = lambda i: (0, 0)
    return pl.pallas_call(
        _mixout_kernel,
        grid=(SEQ // tm,),
        in_specs=[
            pl.BlockSpec((tm, D_MODEL), row),
            pl.BlockSpec((tm, ATTN_WIDTH), row),
            pl.BlockSpec((POOL_HALO, POOL_WIDTH), lambda i: (jnp.maximum(i * hb - 1, 0), 0)),
            pl.BlockSpec((tm, POOL_WIDTH), row),
            pl.BlockSpec((POOL_HALO, POOL_WIDTH), lambda i: (jnp.minimum((i + 1) * hb, nhalo - 1), 0)),
            pl.BlockSpec((D_MODEL, D_MODEL), fixed),
            pl.BlockSpec((len(POOL_WINDOWS), POOL_GROUP_DIM, POOL_GROUP_DIM), lambda i: (0, 0, 0)),
            pl.BlockSpec((1, POOL_WIDTH), fixed),
            _mod_spec(0),
            pl.BlockSpec((1, D_MODEL), fixed),
            _mod_spec(1),
            _mod_spec(2),
        ],
        out_specs=[pl.BlockSpec((tm, D_MODEL), row), pl.BlockSpec((tm, D_MODEL), row)],
        out_shape=[jax.ShapeDtypeStruct((SEQ, D_MODEL), F32),
                   jax.ShapeDtypeStruct((SEQ, D_MODEL), BF16)],
        scratch_shapes=[pltpu.VMEM((tm + 2 * POOL_HALO, POOL_WIDTH), F32)],
        compiler_params=_params("arbitrary"),
        name="mixout",
    )(x, attn, u, u, u, w_out_bf16, pool_w, pool_scale, mod, norm_w, mod, mod)


def _mlp_kernel(hm_ref, x1r_ref, wup_ref, wdn_ref, gm_ref, fw_ref, o_ref):
    f = pl.program_id(1)
    last = MLP_NF - 1
    row_chunks = [slice(r0, r0 + MLP_RC) for r0 in range(0, MLP_TM, MLP_RC)]
    gm = gm_ref[LATENT_ROW]

    def branch_out(rows):
        a = jnp.maximum(jnp.dot(hm_ref[rows, :], wup_ref[...], preferred_element_type=F32), 0.0)
        return gm * jnp.dot((a * a).astype(BF16), wdn_ref[...], preferred_element_type=F32)

    def final_norm(rows):
        x2 = o_ref[rows, :]
        y = x2 * lax.rsqrt(jnp.mean(x2 * x2, axis=-1, keepdims=True) + EPS)
        o_ref[rows, :] = y * fw_ref[...]

    @pl.when(f == 0)
    def _():
        for rows in row_chunks:
            o_ref[rows, :] = branch_out(rows)
        o_ref[0:MLP_XR, :] += x1r_ref[...]

    @pl.when(jnp.logical_and(f > 0, f < last))
    def _():
        x_rows = pl.ds(pl.multiple_of(f * MLP_XR, MLP_XR), MLP_XR)
        o_ref[x_rows, :] += x1r_ref[...]
        for rows in row_chunks:
            o_ref[rows, :] += branch_out(rows)

    @pl.when(f == last)
    def _():
        o_ref[last * MLP_XR:(last + 1) * MLP_XR, :] += x1r_ref[...]
        npiece = 4
        cw, rp = D_MODEL // npiece, MLP_RC // npiece
        for ci, rows in enumerate(row_chunks):
            a = jnp.maximum(jnp.dot(hm_ref[rows, :], wup_ref[...], preferred_element_type=F32), 0.0)
            a = (a * a).astype(BF16)
            for j in range(npiece):
                cols = slice(j * cw, (j + 1) * cw)
                o_ref[rows, cols] += gm[:, cols] * jnp.dot(a, wdn_ref[:, cols],
                                                           preferred_element_type=F32)
                if ci > 0:
                    r0 = row_chunks[ci - 1].start + j * rp
                    final_norm(slice(r0, r0 + rp))
        final_norm(row_chunks[-1])


def _mlp(hm, x1, w_up_bf16, w_dn_bf16, mod, final_w):
    tm, tf = MLP_TM, MLP_TF
    return pl.pallas_call(
        _mlp_kernel,
        grid=(SEQ // tm, MLP_NF),
        in_specs=[
            pl.BlockSpec((tm, D_MODEL), lambda i, f: (i, 0)),
            pl.BlockSpec((MLP_XR, D_MODEL), lambda i, f: (i * MLP_NF + f, 0)),
            pl.BlockSpec((None, D_MODEL, tf), lambda i, f: (f, 0, 0)),
            pl.BlockSpec((tf, D_MODEL), lambda i, f: (f, 0)),
            _mod_spec(3),
            pl.BlockSpec((1, D_MODEL), lambda i, f: (0, 0)),
        ],
        out_specs=pl.BlockSpec((tm, D_MODEL), lambda i, f: (i, 0)),
        out_shape=jax.ShapeDtypeStruct((SEQ, D_MODEL), F32),
        compiler_params=_params("arbitrary", "arbitrary"),
        name="mlp",
    )(hm, x1, w_up_bf16, w_dn_bf16, mod, final_w)


def kernel(x, c, ctx, c_ctx, norm_attn_w, norm_mlp_w, w_ada, b_ada, w_in, attn_sink,
           pool_w, pool_scale, w_out, w_mlp_up, w_mlp_down, final_norm_w):
    assert x.shape == (1, SEQ, D_MODEL) and ctx.shape == (1, CTX_LEN, D_MODEL)
    assert w_ada.shape[0] == 1, "single layer"
    x2d, ctx2d = x[0], ctx[0]

    c_ctx_row = c_ctx[None, :]
    b_row = b_ada[0][None, :]
    n_early = 2 * D_MODEL
    mod_a = _ada(c, c_ctx_row, w_ada[0], b_row, n_early)

    nw_a = norm_attn_w[0][None, :]
    tables = tuple(jnp.asarray(t) for t in _rope_tables())
    qt, k, vt, u = _inproj(x2d, nw_a, mod_a, w_in[0], tables)
    kc, vtc = _ctx_kv(ctx2d, nw_a, mod_a, w_in[0])

    attn, w_out_b, w_up_b, w_dn_b, mod_b = _attn(
        qt, k, vt, kc, vtc, attn_sink[0].astype(F32), w_out[0], w_mlp_up[0], w_mlp_down[0],
        c, c_ctx_row, w_ada[0], b_row, n_early)

    x1, hm = _mixout(x2d, attn, u, w_out_b, pool_w[0], pool_scale[0][None, :], mod_b,
                     norm_mlp_w[0][None, :])
    out = _mlp(hm, x1, w_up_b, w_dn_b, mod_b, final_norm_w[None, :])
    return out[None]
```

```python
import functools

import numpy as np
import jax
import jax.numpy as jnp
from jax import lax
from jax.experimental import pallas as pl
from jax.experimental.pallas import tpu as pltpu

F32 = jnp.float32
BF16 = jnp.bfloat16

D_MODEL = 2048
SEQ = 8192
CTX_LEN = 256
GRID_W = 64
HEAD_DIM = 64
N_Q_HEADS = 16
N_KV_HEADS = 4
GQA = N_Q_HEADS // N_KV_HEADS
ATTN_WIDTH = N_Q_HEADS * HEAD_DIM
KV_WIDTH = N_KV_HEADS * HEAD_DIM
POOL_WINDOWS = (2, 4, 8, 16)
POOL_WIDTH = D_MODEL - ATTN_WIDTH
POOL_GROUP_DIM = POOL_WIDTH // len(POOL_WINDOWS)
IN_WIDTH = ATTN_WIDTH + 2 * KV_WIDTH + POOL_WIDTH
D_FF = 4 * D_MODEL
BLOCK = 128
ROPE_BASE = 10000.0
N_MOD = 6
EPS = 1e-6
NEG_INF = -1e30
LOG2E = 1.4426950408889634
Q_SCALE = HEAD_DIM ** -0.5 * LOG2E

LANES = 128
SUBLANES = 8
POOL_HALO = 8
LATENT_ROW = slice(0, 1)
CONTEXT_ROW = slice(1, 2)
VMEM_LIMIT = 56 * 1024 * 1024

ADA_TN = 512
INPROJ_TM = 512
ATTN_TQ = 512
CAST_JOB_ELEMS = 64 * 1024
MIX_TM = 512
MIX_CHUNK = 256
MLP_TM = 1024
MLP_TF = 1024
MLP_RC = 512
MLP_NF = D_FF // MLP_TF
MLP_XR = MLP_TM // MLP_NF


def _params(*sem):
    return pltpu.CompilerParams(dimension_semantics=sem, vmem_limit_bytes=VMEM_LIMIT)


def _mod_spec(chunk):
    return pl.BlockSpec((SUBLANES, D_MODEL), lambda *_: (0, chunk))


def _rms_modulate(x, norm_w, shift, scale):
    gain = norm_w * (1.0 + scale)
    y = x * lax.rsqrt(jnp.mean(x * x, axis=-1, keepdims=True) + EPS)
    return y * gain + shift


def _ada_block(c_ref, cctx_ref, w_ref, b_ref):
    cnd = jnp.concatenate([c_ref[...], cctx_ref[...], jnp.zeros((SUBLANES - 2, D_MODEL), F32)], axis=0)
    s = cnd * (1.0 / (1.0 + jnp.exp(-cnd)))
    return jnp.dot(s, w_ref[...], preferred_element_type=F32) + b_ref[...]


def _ada_kernel(c_ref, cctx_ref, w_ref, b_ref, o_ref):
    o_ref[...] = _ada_block(c_ref, cctx_ref, w_ref, b_ref)


def _ada(c, c_ctx, w_ada, b_ada, n):
    return pl.pallas_call(
        _ada_kernel,
        grid=(n // ADA_TN,),
        in_specs=[
            pl.BlockSpec((1, D_MODEL), lambda j: (0, 0)),
            pl.BlockSpec((1, D_MODEL), lambda j: (0, 0)),
            pl.BlockSpec((D_MODEL, ADA_TN), lambda j: (0, j)),
            pl.BlockSpec((1, ADA_TN), lambda j: (0, j)),
        ],
        out_specs=pl.BlockSpec((SUBLANES, ADA_TN), lambda j: (0, j)),
        out_shape=jax.ShapeDtypeStruct((SUBLANES, n), F32),
        compiler_params=_params("arbitrary"),
        name="ada",
    )(c, c_ctx, w_ada, b_ada)


def _rope_tables():
    half = HEAD_DIM // 2
    inv_freq = ROPE_BASE ** (-np.arange(0, half, 2, dtype=np.float64) / half)
    t = np.arange(SEQ)
    row = (t // GRID_W).astype(np.float64)
    col = (t % GRID_W).astype(np.float64)
    ang_r = row[:, None] * inv_freq[None, :]
    ang_c = col[:, None] * inv_freq[None, :]
    cos_t = Q_SCALE * np.concatenate([np.cos(ang_r), np.cos(ang_c)], axis=1).T
    sin_t = Q_SCALE * np.concatenate([np.sin(ang_r), np.sin(ang_c)], axis=1).T
    cos_l = np.concatenate([np.cos(ang_r), np.cos(ang_r), np.cos(ang_c), np.cos(ang_c)], axis=1)
    zeros = np.zeros_like(ang_r)
    s_up = np.concatenate([-np.sin(ang_r), zeros, -np.sin(ang_c), zeros], axis=1)
    s_dn = np.concatenate([zeros, np.sin(ang_r), zeros, np.sin(ang_c)], axis=1)
    tile2 = lambda a: np.concatenate([a, a], axis=1).astype(np.float32)
    return (cos_t.astype(np.float32), sin_t.astype(np.float32),
            tile2(cos_l), tile2(s_up), tile2(s_dn))


def _inproj_kernel(x_ref, nw_ref, sh_ref, sc_ref, w_ref, cos_t_ref, sin_t_ref,
                   cos_l_ref, sup_ref, sdn_ref, qt_ref, k_ref, vt_ref, u_ref, wb_ref):
    @pl.when(pl.program_id(0) == 0)
    def _():
        for c0 in range(0, IN_WIDTH, 2 * LANES):
            wb_ref[:, c0:c0 + 2 * LANES] = w_ref[:, c0:c0 + 2 * LANES].astype(BF16)

    h = _rms_modulate(x_ref[...], nw_ref[...], sh_ref[LATENT_ROW], sc_ref[LATENT_ROW]).astype(BF16)
    p = jnp.dot(h, wb_ref[...], preferred_element_type=F32)

    qt = p[:, :ATTN_WIDTH].T
    cr, cc = cos_t_ref[0:16, :], cos_t_ref[16:32, :]
    sr, sc = sin_t_ref[0:16, :], sin_t_ref[16:32, :]
    pieces = []
    for hd in range(N_Q_HEADS):
        b = hd * HEAD_DIM
        x1r, x2r = qt[b:b + 16, :], qt[b + 16:b + 32, :]
        x1c, x2c = qt[b + 32:b + 48, :], qt[b + 48:b + 64, :]
        pieces += [x1r * cr - x2r * sr, x1r * sr + x2r * cr,
                   x1c * cc - x2c * sc, x1c * sc + x2c * cc]
    qt_ref[...] = jnp.concatenate(pieces, axis=0).astype(BF16)

    k = p[:, ATTN_WIDTH:ATTN_WIDTH + KV_WIDTH]
    cos_l, s_up, s_dn = cos_l_ref[...], sup_ref[...], sdn_ref[...]
    kparts = []
    for g in range(KV_WIDTH // LANES):
        kg = k[:, g * LANES:(g + 1) * LANES]
        kparts.append(kg * cos_l + pltpu.roll(kg, LANES - 16, 1) * s_up
                      + pltpu.roll(kg, 16, 1) * s_dn)
    k_ref[...] = jnp.concatenate(kparts, axis=1).astype(BF16)

    vt_ref[...] = p[:, ATTN_WIDTH + KV_WIDTH:ATTN_WIDTH + 2 * KV_WIDTH].T.astype(BF16)
    u_ref[...] = p[:, ATTN_WIDTH + 2 * KV_WIDTH:]


def _inproj(x, norm_w, mod, w_in, tables):
    cos_t, sin_t, cos_l, s_up, s_dn = tables
    tm = INPROJ_TM
    row = lambda i: (i, 0)
    colb = lambda i: (0, i)
    fixed = lambda i: (0, 0)
    return pl.pallas_call(
        _inproj_kernel,
        grid=(SEQ // tm,),
        in_specs=[
            pl.BlockSpec((tm, D_MODEL), row),
            pl.BlockSpec((1, D_MODEL), fixed),
            _mod_spec(0),
            _mod_spec(1),
            pl.BlockSpec((D_MODEL, IN_WIDTH), fixed, pipeline_mode=pl.Buffered(1)),
            pl.BlockSpec((32, tm), colb),
            pl.BlockSpec((32, tm), colb),
            pl.BlockSpec((tm, LANES), row),
            pl.BlockSpec((tm, LANES), row),
            pl.BlockSpec((tm, LANES), row),
        ],
        out_specs=[
            pl.BlockSpec((ATTN_WIDTH, tm), colb),
            pl.BlockSpec((tm, KV_WIDTH), row),
            pl.BlockSpec((KV_WIDTH, tm), colb),
            pl.BlockSpec((tm, POOL_WIDTH), row),
        ],
        out_shape=[
            jax.ShapeDtypeStruct((ATTN_WIDTH, SEQ), BF16),
            jax.ShapeDtypeStruct((SEQ, KV_WIDTH), BF16),
            jax.ShapeDtypeStruct((KV_WIDTH, SEQ), BF16),
            jax.ShapeDtypeStruct((SEQ, POOL_WIDTH), F32),
        ],
        scratch_shapes=[pltpu.VMEM((D_MODEL, IN_WIDTH), BF16)],
        compiler_params=_params("arbitrary"),
        name="inproj",
    )(x, norm_w, mod, mod, w_in, cos_t, sin_t, cos_l, s_up, s_dn)


def _ctx_kv_kernel(x_ref, nw_ref, sh_ref, sc_ref, w_ref, k_ref, vt_ref):
    h = _rms_modulate(x_ref[...], nw_ref[...], sh_ref[CONTEXT_ROW], sc_ref[CONTEXT_ROW]).astype(BF16)
    p = jnp.dot(h, w_ref[...].astype(BF16), preferred_element_type=F32)
    k_ref[...] = p[:, :KV_WIDTH].astype(BF16)
    vt_ref[...] = p[:, KV_WIDTH:].T.astype(BF16)


def _ctx_kv(ctx, norm_w, mod, w_in):
    fixed = lambda i: (0, 0)
    kv_block = ATTN_WIDTH // (2 * KV_WIDTH)
    return pl.pallas_call(
        _ctx_kv_kernel,
        grid=(1,),
        in_specs=[
            pl.BlockSpec((CTX_LEN, D_MODEL), fixed),
            pl.BlockSpec((1, D_MODEL), fixed),
            _mod_spec(0),
            _mod_spec(1),
            pl.BlockSpec((D_MODEL, 2 * KV_WIDTH), lambda i: (0, kv_block)),
        ],
        out_specs=[
            pl.BlockSpec((CTX_LEN, KV_WIDTH), fixed),
            pl.BlockSpec((KV_WIDTH, CTX_LEN), fixed),
        ],
        out_shape=[
            jax.ShapeDtypeStruct((CTX_LEN, KV_WIDTH), BF16),
            jax.ShapeDtypeStruct((KV_WIDTH, CTX_LEN), BF16),
        ],
        compiler_params=_params("arbitrary"),
        name="ctx_kv",
    )(ctx, norm_w, mod, mod, w_in)


def _attn_kernel(qt_ref, kp_ref, km_ref, kn_ref, vtp_ref, vtm_ref, vtn_ref,
                 kc_ref, vtc_ref, sink_ref, wout_ref, wup_ref, wdn_ref,
                 c_ref, cctx_ref, wada_ref, bada_ref,
                 o_ref, wout_b_ref, wup_b_ref, wdn_b_ref, mod_ref):
    mod_ref[...] = _ada_block(c_ref, cctx_ref, wada_ref, bada_ref)

    i = pl.program_id(0)
    nsub = ATTN_TQ // BLOCK
    nblk = SEQ // BLOCK
    gw = GQA * BLOCK
    pw = 2 * BLOCK
    nkeys = 3 * BLOCK + CTX_LEN

    key_j = lax.broadcasted_iota(jnp.int32, (BLOCK, gw), 0)
    qry_i = lax.broadcasted_iota(jnp.int32, (BLOCK, gw), 1) % BLOCK
    neg = jnp.full((BLOCK, gw), NEG_INF, F32)
    zero = jnp.zeros((BLOCK, gw), F32)
    band_prev = jnp.where(key_j >= qry_i, zero, neg)
    band_next = jnp.where(key_j <= qry_i, zero, neg)

    kc = kc_ref[...]
    vtc = vtc_ref[...]
    ones_rows = jnp.ones((2 * SUBLANES, nkeys), BF16)
    sink_row = sink_ref[...] * LOG2E

    windows = []
    for sb in range(nsub):
        n = i * nsub + sb
        lo, hi = sb * BLOCK, (sb + 1) * BLOCK
        k_prev = kp_ref[...] if sb == 0 else km_ref[lo - BLOCK:lo, :]
        k_next = kn_ref[...] if sb == nsub - 1 else km_ref[hi:hi + BLOCK, :]
        vt_prev = vtp_ref[...] if sb == 0 else vtm_ref[:, lo - BLOCK:lo]
        vt_next = vtn_ref[...] if sb == nsub - 1 else vtm_ref[:, hi:hi + BLOCK]
        windows.append(dict(
            kwin=jnp.concatenate([k_prev, km_ref[lo:hi, :], k_next, kc], axis=0),
            vtwin=jnp.concatenate([vt_prev, vtm_ref[:, lo:hi], vt_next, vtc], axis=1),
            bias_prev=band_prev + jnp.where(n > 0, 0.0, NEG_INF),
            bias_next=band_next + jnp.where(n < nblk - 1, 0.0, NEG_INF)))

    units = [(sb, h, half) for sb in range(nsub) for h in range(N_KV_HEADS)
             for half in range(GQA // 2)]

    def scores(sb, h):
        win = windows[sb]
        lo, hi = sb * BLOCK, (sb + 1) * BLOCK
        qh = jnp.concatenate(
            [qt_ref[(GQA * h + g) * HEAD_DIM:(GQA * h + g + 1) * HEAD_DIM, lo:hi]
             for g in range(GQA)], axis=1)
        blocks = []
        if h > 0:
            blocks.append(jnp.zeros((h * HEAD_DIM, gw), BF16))
        blocks.append(qh)
        if h < N_KV_HEADS - 1:
            blocks.append(jnp.zeros(((N_KV_HEADS - 1 - h) * HEAD_DIM, gw), BF16))
        s = jnp.dot(win["kwin"], jnp.concatenate(blocks, axis=0),
                    preferred_element_type=F32)
        s = jnp.concatenate([s[0:BLOCK] + win["bias_prev"], s[BLOCK:2 * BLOCK],
                             s[2 * BLOCK:3 * BLOCK] + win["bias_next"], s[3 * BLOCK:]], axis=0)
        return s, jnp.max(s, axis=0, keepdims=True)

    def probs(unit, s, smax):
        sb, h, half = unit
        hd0 = GQA * h + 2 * half
        cols = slice(half * pw, (half + 1) * pw)
        sink = sink_row[:, hd0 * BLOCK:(hd0 + 2) * BLOCK]
        m = jnp.maximum(smax[:, cols], sink)
        return jnp.exp2(s[:, cols] - m).astype(BF16), jnp.exp2(sink - m)

    def finish(unit, e, e_sink):
        sb, h, half = unit
        lo, hi = sb * BLOCK, (sb + 1) * BLOCK
        hd0 = GQA * h + 2 * half
        vt_ones = jnp.concatenate(
            [windows[sb]["vtwin"][h * HEAD_DIM:(h + 1) * HEAD_DIM, :], ones_rows], axis=0)
        o = jnp.dot(vt_ones, e, preferred_element_type=F32)
        denom = o[HEAD_DIM:HEAD_DIM + 1, :] + e_sink
        o = o[:HEAD_DIM, :] * (1.0 / denom)
        blk = jnp.concatenate([o[:, 0:BLOCK], o[:, BLOCK:2 * BLOCK]], axis=0)
        o_ref[lo:hi, hd0 * HEAD_DIM:(hd0 + 2) * HEAD_DIM] = blk.T.astype(BF16)

    cast_jobs = []
    for src, dst in ((wout_ref, wout_b_ref), (wup_ref, wup_b_ref), (wdn_ref, wdn_b_ref)):
        wc = max(LANES, CAST_JOB_ELEMS // src.shape[0])
        cast_jobs += [(src, dst, c0) + (wc,) for c0 in range(0, src.shape[1], wc)]

    def cast_piece(idx, n):
        for src, dst, c0, wc in cast_jobs[idx * len(cast_jobs) // n:(idx + 1) * len(cast_jobs) // n]:
            val = src[:, c0:c0 + wc].astype(BF16)
            if len(dst.shape) == 3:
                tf = dst.shape[2]
                dst[c0 // tf, :, c0 % tf:c0 % tf + wc] = val
            else:
                dst[:, c0:c0 + wc] = val

    nu = len(units)
    s_vals, p_vals = {}, {}
    for step in range(nu + 2):
        if step < nu and units[step][2] == 0:
            s_vals[units[step][:2]] = scores(*units[step][:2])
        if 1 <= step <= nu:
            unit = units[step - 1]
            p_vals[step - 1] = probs(unit, *s_vals[unit[:2]])
            if unit[2] == GQA // 2 - 1:
                del s_vals[unit[:2]]
            cast_piece(step - 1, nu)
        if step >= 2:
            finish(units[step - 2], *p_vals.pop(step - 2))


def _attn(qt, k, vt, kc, vtc, sink, w_out, w_up, w_dn, c, c_ctx, w_ada, b_ada, ada_col0):
    tq = ATTN_TQ
    r = tq // BLOCK
    nblk = SEQ // BLOCK
    nsteps = SEQ // tq
    ada_tn = (w_ada.shape[1] - ada_col0) // nsteps
    ada_b0 = ada_col0 // ada_tn
    fixed = lambda i: (0, 0)
    row = lambda i: (i, 0)
    slab = lambda w: pl.BlockSpec((w.shape[0] // nsteps, w.shape[1]), row)
    bf16_like = lambda w: jax.ShapeDtypeStruct(w.shape, BF16)
    prev = lambda i: jnp.maximum(i * r - 1, 0)
    nxt = lambda i: jnp.minimum((i + 1) * r, nblk - 1)
    return pl.pallas_call(
        _attn_kernel,
        grid=(SEQ // tq,),
        in_specs=[
            pl.BlockSpec((ATTN_WIDTH, tq), lambda i: (0, i)),
            pl.BlockSpec((BLOCK, KV_WIDTH), lambda i: (prev(i), 0)),
            pl.BlockSpec((tq, KV_WIDTH), lambda i: (i, 0)),
            pl.BlockSpec((BLOCK, KV_WIDTH), lambda i: (nxt(i), 0)),
            pl.BlockSpec((KV_WIDTH, BLOCK), lambda i: (0, prev(i))),
            pl.BlockSpec((KV_WIDTH, tq), lambda i: (0, i)),
            pl.BlockSpec((KV_WIDTH, BLOCK), lambda i: (0, nxt(i))),
            pl.BlockSpec((CTX_LEN, KV_WIDTH), fixed),
            pl.BlockSpec((KV_WIDTH, CTX_LEN), fixed),
            pl.BlockSpec((1, N_Q_HEADS * BLOCK), fixed),
            slab(w_out), slab(w_up), slab(w_dn),
            pl.BlockSpec((1, D_MODEL), fixed),
            pl.BlockSpec((1, D_MODEL), fixed),
            pl.BlockSpec((D_MODEL, ada_tn), lambda i: (0, ada_b0 + i)),
            pl.BlockSpec((1, ada_tn), lambda i: (0, ada_b0 + i)),
        ],
        out_specs=[pl.BlockSpec((tq, ATTN_WIDTH), row), slab(w_out),
                   pl.BlockSpec((MLP_NF, w_up.shape[0] // nsteps, MLP_TF), lambda i: (0, i, 0)),
                   slab(w_dn),
                   pl.BlockSpec((SUBLANES, ada_tn), lambda i: (0, i))],
        out_shape=[jax.ShapeDtypeStruct((SEQ, ATTN_WIDTH), BF16),
                   bf16_like(w_out),
                   jax.ShapeDtypeStruct((MLP_NF, w_up.shape[0], MLP_TF), BF16),
                   bf16_like(w_dn),
                   jax.ShapeDtypeStruct((SUBLANES, w_ada.shape[1] - ada_col0), F32)],
        compiler_params=_params("arbitrary"),
        name="attn",
    )(qt, k, k, k, vt, vt, vt, kc, vtc, sink, w_out, w_up, w_dn, c, c_ctx, w_ada, b_ada)


def _mixout_kernel(x_ref, attn_ref, up_ref, um_ref, un_ref, wout_ref, pw_ref, ps_ref,
                   ga_ref, nw_ref, sh_ref, sc_ref, x1_ref, hm_ref, ubuf):
    i = pl.program_id(0)
    tm = MIX_TM
    nsteps = SEQ // tm
    ubuf[0:POOL_HALO, :] = jnp.where(i > 0, up_ref[...], 0.0)
    ubuf[POOL_HALO:POOL_HALO + tm, :] = um_ref[...]
    ubuf[POOL_HALO + tm:2 * POOL_HALO + tm, :] = jnp.where(i < nsteps - 1, un_ref[...], 0.0)

    rc = MIX_CHUNK
    n = rc + 2 * POOL_HALO
    c = POOL_HALO
    ngroups = len(POOL_WINDOWS)
    ncol = D_MODEL // ngroups

    def pool_group(r0, g):
        w = POOL_WINDOWS[g]
        c0, c1 = g * POOL_GROUP_DIM, (g + 1) * POOL_GROUP_DIM
        e = ubuf[r0:r0 + n, c0:c1]
        if w == 2:
            win = e[c - 1:c - 1 + rc] + e[c:c + rc]
        else:
            f2 = e[0:n - 1] + e[1:n]
            if w == 4:
                win = f2[c - 2:c - 2 + rc] + f2[c:c + rc]
            else:
                f4 = f2[0:n - 3] + f2[2:n - 1]
                if w == 8:
                    win = f4[c - 4:c - 4 + rc] + f4[c:c + rc]
                else:
                    f8 = f4[0:n - 7] + f4[4:n - 3]
                    win = f8[c - 8:c - 8 + rc] + f8[c:c + rc]
        t = i * tm + r0 + lax.broadcasted_iota(jnp.int32, (rc, 1), 0)
        lo = jnp.clip(t - w // 2, 0, SEQ)
        hi = jnp.clip(t - w // 2 + w, 0, SEQ)
        cnt = (hi - lo).astype(F32)
        pg = win / cnt - um_ref[r0:r0 + rc, c0:c1]
        mixed = jnp.dot(pg.astype(BF16), pw_ref[g].astype(BF16), preferred_element_type=F32)
        return (mixed * ps_ref[:, c0:c1]).astype(BF16)

    def prenorm(r0, nrows):
        rows = slice(r0, r0 + nrows)
        hm_ref[rows, :] = _rms_modulate(x1_ref[rows, :], nw_ref[...], sh_ref[LATENT_ROW],
                                        sc_ref[LATENT_ROW]).astype(BF16)

    chunk_starts = list(range(0, tm, rc))
    for ci, r0 in enumerate(chunk_starts):
        rows = slice(r0, r0 + rc)
        attn_rows = attn_ref[rows, :]
        pooled = []
        for j in range(ngroups):
            cols = slice(j * ncol, (j + 1) * ncol)
            o = jnp.dot(attn_rows, wout_ref[0:ATTN_WIDTH, cols], preferred_element_type=F32)
            x1_ref[rows, cols] = x_ref[rows, cols] + ga_ref[LATENT_ROW, cols] * o
            pooled.append(pool_group(r0, j))
        pool_rows = jnp.concatenate(pooled, axis=1)
        for j in range(ngroups):
            cols = slice(j * ncol, (j + 1) * ncol)
            o = jnp.dot(pool_rows, wout_ref[ATTN_WIDTH:, cols], preferred_element_type=F32)
            x1_ref[rows, cols] += ga_ref[LATENT_ROW, cols] * o
            if ci > 0:
                prenorm(chunk_starts[ci - 1] + j * (rc // ngroups), rc // ngroups)
    for j in range(ngroups):
        prenorm(chunk_starts[-1] + j * (rc // ngroups), rc // ngroups)


def _mixout(x, attn, u, w_out_bf16, pool_w, pool_scale, mod, norm_w):
    tm = MIX_TM
    hb = tm // POOL_HALO
    nhalo = SEQ // POOL_HALO
    row = lambda i: (i, 0)
    fixed = lambda i: (0, 0)
    return pl.pallas_call(
        _mixout_kernel,
        grid=(SEQ // tm,),
        in_specs=[
            pl.BlockSpec((tm, D_MODEL), row),
            pl.BlockSpec((tm, ATTN_WIDTH), row),
            pl.BlockSpec((POOL_HALO, POOL_WIDTH), lambda i: (jnp.maximum(i * hb - 1, 0), 0)),
            pl.BlockSpec((tm, POOL_WIDTH), row),
            pl.BlockSpec((POOL_HALO, POOL_WIDTH), lambda i: (jnp.minimum((i + 1) * hb, nhalo - 1), 0)),
            pl.BlockSpec((D_MODEL, D_MODEL), fixed),
            pl.BlockSpec((len(POOL_WINDOWS), POOL_GROUP_DIM, POOL_GROUP_DIM), lambda i: (0, 0, 0)),
            pl.BlockSpec((1, POOL_WIDTH), fixed),
            _mod_spec(0),
            pl.BlockSpec((1, D_MODEL), fixed),
            _mod_spec(1),
            _mod_spec(2),
        ],
        out_specs=[pl.BlockSpec((tm, D_MODEL), row), pl.BlockSpec((tm, D_MODEL), row)],
        out_shape=[jax.ShapeDtypeStruct((SEQ, D_MODEL), F32),
                   jax.ShapeDtypeStruct((SEQ, D_MODEL), BF16)],
        scratch_shapes=[pltpu.VMEM((tm + 2 * POOL_HALO, POOL_WIDTH), F32)],
        compiler_params=_params("arbitrary"),
        name="mixout",
    )(x, attn, u, u, u, w_out_bf16, pool_w, pool_scale, mod, norm_w, mod, mod)


def _mlp_kernel(hm_ref, x1r_ref, wup_ref, wdn_ref, gm_ref, fw_ref, o_ref):
    f = pl.program_id(1)
    last = MLP_NF - 1
    row_chunks = [slice(r0, r0 + MLP_RC) for r0 in range(0, MLP_TM, MLP_RC)]
    gm = gm_ref[LATENT_ROW]

    def branch_out(rows):
        a = jnp.maximum(jnp.dot(hm_ref[rows, :], wup_ref[...], preferred_element_type=F32), 0.0)
        return gm * jnp.dot((a * a).astype(BF16), wdn_ref[...], preferred_element_type=F32)

    def final_norm(rows):
        x2 = o_ref[rows, :]
        y = x2 * lax.rsqrt(jnp.mean(x2 * x2, axis=-1, keepdims=True) + EPS)
        o_ref[rows, :] = y * fw_ref[...]

    @pl.when(f == 0)
    def _():
        for rows in row_chunks:
            o_ref[rows, :] = branch_out(rows)
        o_ref[0:MLP_XR, :] += x1r_ref[...]

    @pl.when(jnp.logical_and(f > 0, f < last))
    def _():
        x_rows = pl.ds(pl.multiple_of(f * MLP_XR, MLP_XR), MLP_XR)
        o_ref[x_rows, :] += x1r_ref[...]
        for rows in row_chunks:
            o_ref[rows, :] += branch_out(rows)

    @pl.when(f == last)
    def _():
        o_ref[last * MLP_XR:(last + 1) * MLP_XR, :] += x1r_ref[...]
        npiece = 4
        cw, rp = D_MODEL // npiece, MLP_RC // npiece
        for ci, rows in enumerate(row_chunks):
            a = jnp.maximum(jnp.dot(hm_ref[rows, :], wup_ref[...], preferred_element_type=F32), 0.0)
            a = (a * a).astype(BF16)
            for j in range(npiece):
                cols = slice(j * cw, (j + 1) * cw)
                o_ref[rows, cols] += gm[:, cols] * jnp.dot(a, wdn_ref[:, cols],
                                                           preferred_element_type=F32)
                if ci > 0:
                    r0 = row_chunks[ci - 1].start + j * rp
                    final_norm(slice(r0, r0 + rp))
        final_norm(row_chunks[-1])


def _mlp(hm, x1, w_up_bf16, w_dn_bf16, mod, final_w):
    tm, tf = MLP_TM, MLP_TF
    return pl.pallas_call(
        _mlp_kernel,
        grid=(SEQ // tm, MLP_NF),
        in_specs=[
            pl.BlockSpec((tm, D_MODEL), lambda i, f: (i, 0)),
            pl.BlockSpec((MLP_XR, D_MODEL), lambda i, f: (i * MLP_NF + f, 0)),
            pl.BlockSpec((None, D_MODEL, tf), lambda i, f: (f, 0, 0)),
            pl.BlockSpec((tf, D_MODEL), lambda i, f: (f, 0)),
            _mod_spec(3),
            pl.BlockSpec((1, D_MODEL), lambda i, f: (0, 0)),
        ],
        out_specs=pl.BlockSpec((tm, D_MODEL), lambda i, f: (i, 0)),
        out_shape=jax.ShapeDtypeStruct((SEQ, D_MODEL), F32),
        compiler_params=_params("arbitrary", "arbitrary"),
        name="mlp",
    )(hm, x1, w_up_bf16, w_dn_bf16, mod, final_w)


def kernel(x, c, ctx, c_ctx, norm_attn_w, norm_mlp_w, w_ada, b_ada, w_in, attn_sink,
           pool_w, pool_scale, w_out, w_mlp_up, w_mlp_down, final_norm_w):
    assert x.shape == (1, SEQ, D_MODEL) and ctx.shape == (1, CTX_LEN, D_MODEL)
    assert w_ada.shape[0] == 1, "single layer"
    x2d, ctx2d = x[0], ctx[0]

    c_ctx_row = c_ctx[None, :]
    b_row = b_ada[0][None, :]
    n_early = 2 * D_MODEL
    mod_a = _ada(c, c_ctx_row, w_ada[0], b_row, n_early)

    nw_a = norm_attn_w[0][None, :]
    tables = tuple(jnp.asarray(t) for t in _rope_tables())
    qt, k, vt, u = _inproj(x2d, nw_a, mod_a, w_in[0], tables)
    kc, vtc = _ctx_kv(ctx2d, nw_a, mod_a, w_in[0])

    sink_row = jnp.repeat(attn_sink[0].astype(F32), BLOCK)[None, :]
    attn, w_out_b, w_up_b, w_dn_b, mod_b = _attn(
        qt, k, vt, kc, vtc, sink_row, w_out[0], w_mlp_up[0], w_mlp_down[0],
        c, c_ctx_row, w_ada[0], b_row, n_early)

    x1, hm = _mixout(x2d, attn, u, w_out_b, pool_w[0], pool_scale[0][None, :], mod_b,
                     norm_mlp_w[0][None, :])
    out = _mlp(hm, x1, w_up_b, w_dn_b, mod_b, final_norm_w[None, :])
    return out[None]
```

```python
import functools

import numpy as np
import jax
import jax.numpy as jnp
from jax import lax
from jax.experimental import pallas as pl
from jax.experimental.pallas import tpu as pltpu

F32 = jnp.float32
BF16 = jnp.bfloat16

D_MODEL = 2048
SEQ = 8192
CTX_LEN = 256
GRID_W = 64
HEAD_DIM = 64
N_Q_HEADS = 16
N_KV_HEADS = 4
GQA = N_Q_HEADS // N_KV_HEADS
ATTN_WIDTH = N_Q_HEADS * HEAD_DIM
KV_WIDTH = N_KV_HEADS * HEAD_DIM
POOL_WINDOWS = (2, 4, 8, 16)
POOL_WIDTH = D_MODEL - ATTN_WIDTH
POOL_GROUP_DIM = POOL_WIDTH // len(POOL_WINDOWS)
IN_WIDTH = ATTN_WIDTH + 2 * KV_WIDTH + POOL_WIDTH
D_FF = 4 * D_MODEL
BLOCK = 128
ROPE_BASE = 10000.0
N_MOD = 6
EPS = 1e-6
NEG_INF = -1e30
LOG2E = 1.4426950408889634
Q_SCALE = HEAD_DIM ** -0.5 * LOG2E

LANES = 128
SUBLANES = 8
POOL_HALO = 8
LATENT_ROW = slice(0, 1)
CONTEXT_ROW = slice(1, 2)
VMEM_LIMIT = 56 * 1024 * 1024

ADA_TN = 512
INPROJ_TM = 512
ATTN_TQ = 512
CAST_JOB_ELEMS = 64 * 1024
MIX_TM = 512
MIX_CHUNK = 256
MLP_TM = 1024
MLP_TF = 1024
MLP_RC = 512
MLP_NF = D_FF // MLP_TF
MLP_XR = MLP_TM // MLP_NF


def _params(*sem):
    return pltpu.CompilerParams(dimension_semantics=sem, vmem_limit_bytes=VMEM_LIMIT)


def _mod_spec(chunk):
    return pl.BlockSpec((SUBLANES, D_MODEL), lambda *_: (0, chunk))


def _rms_modulate(x, norm_w, shift, scale):
    gain = norm_w * (1.0 + scale)
    y = x * lax.rsqrt(jnp.mean(x * x, axis=-1, keepdims=True) + EPS)
    return y * gain + shift


def _ada_block(c_ref, cctx_ref, w_ref, b_ref):
    cnd = jnp.concatenate([c_ref[...], cctx_ref[...], jnp.zeros((SUBLANES - 2, D_MODEL), F32)], axis=0)
    s = cnd * (1.0 / (1.0 + jnp.exp(-cnd)))
    return jnp.dot(s, w_ref[...], preferred_element_type=F32) + b_ref[...]


def _ada_kernel(c_ref, cctx_ref, w_ref, b_ref, o_ref):
    o_ref[...] = _ada_block(c_ref, cctx_ref, w_ref, b_ref)


def _ada(c, c_ctx, w_ada, b_ada, n):
    return pl.pallas_call(
        _ada_kernel,
        grid=(n // ADA_TN,),
        in_specs=[
            pl.BlockSpec((1, D_MODEL), lambda j: (0, 0)),
            pl.BlockSpec((1, D_MODEL), lambda j: (0, 0)),
            pl.BlockSpec((D_MODEL, ADA_TN), lambda j: (0, j)),
            pl.BlockSpec((1, ADA_TN), lambda j: (0, j)),
        ],
        out_specs=pl.BlockSpec((SUBLANES, ADA_TN), lambda j: (0, j)),
        out_shape=jax.ShapeDtypeStruct((SUBLANES, n), F32),
        compiler_params=_params("arbitrary"),
        name="ada",
    )(c, c_ctx, w_ada, b_ada)


def _rope_tables():
    half = HEAD_DIM // 2
    inv_freq = ROPE_BASE ** (-np.arange(0, half, 2, dtype=np.float64) / half)
    t = np.arange(SEQ)
    row = (t // GRID_W).astype(np.float64)
    col = (t % GRID_W).astype(np.float64)
    ang_r = row[:, None] * inv_freq[None, :]
    ang_c = col[:, None] * inv_freq[None, :]
    cos_t = Q_SCALE * np.concatenate([np.cos(ang_r), np.cos(ang_c)], axis=1).T
    sin_t = Q_SCALE * np.concatenate([np.sin(ang_r), np.sin(ang_c)], axis=1).T
    cos_l = np.concatenate([np.cos(ang_r), np.cos(ang_r), np.cos(ang_c), np.cos(ang_c)], axis=1)
    zeros = np.zeros_like(ang_r)
    s_up = np.concatenate([-np.sin(ang_r), zeros, -np.sin(ang_c), zeros], axis=1)
    s_dn = np.concatenate([zeros, np.sin(ang_r), zeros, np.sin(ang_c)], axis=1)
    tile2 = lambda a: np.concatenate([a, a], axis=1).astype(np.float32)
    return (cos_t.astype(np.float32), sin_t.astype(np.float32),
            tile2(cos_l), tile2(s_up), tile2(s_dn))


def _inproj_kernel(x_ref, nw_ref, sh_ref, sc_ref, w_ref, cos_t_ref, sin_t_ref,
                   cos_l_ref, sup_ref, sdn_ref, qt_ref, k_ref, vt_ref, u_ref, wb_ref):
    @pl.when(pl.program_id(0) == 0)
    def _():
        for c0 in range(0, IN_WIDTH, 2 * LANES):
            wb_ref[:, c0:c0 + 2 * LANES] = w_ref[:, c0:c0 + 2 * LANES].astype(BF16)

    h = _rms_modulate(x_ref[...], nw_ref[...], sh_ref[LATENT_ROW], sc_ref[LATENT_ROW]).astype(BF16)
    p = jnp.dot(h, wb_ref[...], preferred_element_type=F32)

    qt = p[:, :ATTN_WIDTH].T
    cr, cc = cos_t_ref[0:16, :], cos_t_ref[16:32, :]
    sr, sc = sin_t_ref[0:16, :], sin_t_ref[16:32, :]
    pieces = []
    for hd in range(N_Q_HEADS):
        b = hd * HEAD_DIM
        x1r, x2r = qt[b:b + 16, :], qt[b + 16:b + 32, :]
        x1c, x2c = qt[b + 32:b + 48, :], qt[b + 48:b + 64, :]
        pieces += [x1r * cr - x2r * sr, x1r * sr + x2r * cr,
                   x1c * cc - x2c * sc, x1c * sc + x2c * cc]
    qt_ref[...] = jnp.concatenate(pieces, axis=0).astype(BF16)

    k = p[:, ATTN_WIDTH:ATTN_WIDTH + KV_WIDTH]
    cos_l, s_up, s_dn = cos_l_ref[...], sup_ref[...], sdn_ref[...]
    kparts = []
    for g in range(KV_WIDTH // LANES):
        kg = k[:, g * LANES:(g + 1) * LANES]
        kparts.append(kg * cos_l + pltpu.roll(kg, LANES - 16, 1) * s_up
                      + pltpu.roll(kg, 16, 1) * s_dn)
    k_ref[...] = jnp.concatenate(kparts, axis=1).astype(BF16)

    vt_ref[...] = p[:, ATTN_WIDTH + KV_WIDTH:ATTN_WIDTH + 2 * KV_WIDTH].T.astype(BF16)
    u_ref[...] = p[:, ATTN_WIDTH + 2 * KV_WIDTH:]


def _inproj(x, norm_w, mod, w_in, tables):
    cos_t, sin_t, cos_l, s_up, s_dn = tables
    tm = INPROJ_TM
    row = lambda i: (i, 0)
    colb = lambda i: (0, i)
    fixed = lambda i: (0, 0)
    return pl.pallas_call(
        _inproj_kernel,
        grid=(SEQ // tm,),
        in_specs=[
            pl.BlockSpec((tm, D_MODEL), row),
            pl.BlockSpec((1, D_MODEL), fixed),
            _mod_spec(0),
            _mod_spec(1),
            pl.BlockSpec((D_MODEL, IN_WIDTH), fixed, pipeline_mode=pl.Buffered(1)),
            pl.BlockSpec((32, tm), colb),
            pl.BlockSpec((32, tm), colb),
            pl.BlockSpec((tm, LANES), row),
            pl.BlockSpec((tm, LANES), row),
            pl.BlockSpec((tm, LANES), row),
        ],
        out_specs=[
            pl.BlockSpec((ATTN_WIDTH, tm), colb),
            pl.BlockSpec((tm, KV_WIDTH), row),
            pl.BlockSpec((KV_WIDTH, tm), colb),
            pl.BlockSpec((tm, POOL_WIDTH), row),
        ],
        out_shape=[
            jax.ShapeDtypeStruct((ATTN_WIDTH, SEQ), BF16),
            jax.ShapeDtypeStruct((SEQ, KV_WIDTH), BF16),
            jax.ShapeDtypeStruct((KV_WIDTH, SEQ), BF16),
            jax.ShapeDtypeStruct((SEQ, POOL_WIDTH), F32),
        ],
        scratch_shapes=[pltpu.VMEM((D_MODEL, IN_WIDTH), BF16)],
        compiler_params=_params("arbitrary"),
        name="inproj",
    )(x, norm_w, mod, mod, w_in, cos_t, sin_t, cos_l, s_up, s_dn)


def _ctx_kv_kernel(x_ref, nw_ref, sh_ref, sc_ref, w_ref, k_ref, vt_ref):
    h = _rms_modulate(x_ref[...], nw_ref[...], sh_ref[CONTEXT_ROW], sc_ref[CONTEXT_ROW]).astype(BF16)
    p = jnp.dot(h, w_ref[...].astype(BF16), preferred_element_type=F32)
    k_ref[...] = p[:, :KV_WIDTH].astype(BF16)
    vt_ref[...] = p[:, KV_WIDTH:].T.astype(BF16)


def _ctx_kv(ctx, norm_w, mod, w_in):
    fixed = lambda i: (0, 0)
    kv_block = ATTN_WIDTH // (2 * KV_WIDTH)
    return pl.pallas_call(
        _ctx_kv_kernel,
        grid=(1,),
        in_specs=[
            pl.BlockSpec((CTX_LEN, D_MODEL), fixed),
            pl.BlockSpec((1, D_MODEL), fixed),
            _mod_spec(0),
            _mod_spec(1),
            pl.BlockSpec((D_MODEL, 2 * KV_WIDTH), lambda i: (0, kv_block)),
        ],
        out_specs=[
            pl.BlockSpec((CTX_LEN, KV_WIDTH), fixed),
            pl.BlockSpec((KV_WIDTH, CTX_LEN), fixed),
        ],
        out_shape=[
            jax.ShapeDtypeStruct((CTX_LEN, KV_WIDTH), BF16),
            jax.ShapeDtypeStruct((KV_WIDTH, CTX_LEN), BF16),
        ],
        compiler_params=_params("arbitrary"),
        name="ctx_kv",
    )(ctx, norm_w, mod, mod, w_in)


def _attn_kernel(qt_ref, kp_ref, km_ref, kn_ref, vtp_ref, vtm_ref, vtn_ref,
                 kc_ref, vtc_ref, sink_ref, wout_ref, wup_ref, wdn_ref,
                 c_ref, cctx_ref, wada_ref, bada_ref,
                 o_ref, wout_b_ref, wup_b_ref, wdn_b_ref, mod_ref):
    mod_ref[...] = _ada_block(c_ref, cctx_ref, wada_ref, bada_ref)

    i = pl.program_id(0)
    nsub = ATTN_TQ // BLOCK
    nblk = SEQ // BLOCK
    pw = 2 * BLOCK
    nkeys = 3 * BLOCK + CTX_LEN

    key_j = lax.broadcasted_iota(jnp.int32, (BLOCK, pw), 0)
    qry_i = lax.broadcasted_iota(jnp.int32, (BLOCK, pw), 1) % BLOCK
    neg = jnp.full((BLOCK, pw), NEG_INF, F32)
    zero = jnp.zeros((BLOCK, pw), F32)
    band_prev = jnp.where(key_j >= qry_i, zero, neg)
    band_next = jnp.where(key_j <= qry_i, zero, neg)

    kc = kc_ref[...]
    vtc = vtc_ref[...]
    ones_rows = jnp.ones((2 * SUBLANES, nkeys), BF16)
    sink_row = sink_ref[...] * LOG2E

    windows = []
    for sb in range(nsub):
        n = i * nsub + sb
        lo, hi = sb * BLOCK, (sb + 1) * BLOCK
        k_prev = kp_ref[...] if sb == 0 else km_ref[lo - BLOCK:lo, :]
        k_next = kn_ref[...] if sb == nsub - 1 else km_ref[hi:hi + BLOCK, :]
        vt_prev = vtp_ref[...] if sb == 0 else vtm_ref[:, lo - BLOCK:lo]
        vt_next = vtn_ref[...] if sb == nsub - 1 else vtm_ref[:, hi:hi + BLOCK]
        windows.append(dict(
            kwin=jnp.concatenate([k_prev, km_ref[lo:hi, :], k_next, kc], axis=0),
            vtwin=jnp.concatenate([vt_prev, vtm_ref[:, lo:hi], vt_next, vtc], axis=1),
            bias_prev=band_prev + jnp.where(n > 0, 0.0, NEG_INF),
            bias_next=band_next + jnp.where(n < nblk - 1, 0.0, NEG_INF)))

    units = [(sb, h, half) for sb in range(nsub) for h in range(N_KV_HEADS)
             for half in range(GQA // 2)]

    def scores(unit):
        sb, h, half = unit
        win = windows[sb]
        lo, hi = sb * BLOCK, (sb + 1) * BLOCK
        hd0 = GQA * h + 2 * half
        qh = jnp.concatenate([qt_ref[(hd0 + g) * HEAD_DIM:(hd0 + g + 1) * HEAD_DIM, lo:hi]
                              for g in range(2)], axis=1)
        blocks = []
        if h > 0:
            blocks.append(jnp.zeros((h * HEAD_DIM, pw), BF16))
        blocks.append(qh)
        if h < N_KV_HEADS - 1:
            blocks.append(jnp.zeros(((N_KV_HEADS - 1 - h) * HEAD_DIM, pw), BF16))
        s = jnp.dot(win["kwin"], jnp.concatenate(blocks, axis=0),
                    preferred_element_type=F32)
        s = jnp.concatenate([s[0:BLOCK] + win["bias_prev"], s[BLOCK:2 * BLOCK],
                             s[2 * BLOCK:3 * BLOCK] + win["bias_next"], s[3 * BLOCK:]], axis=0)
        return s, jnp.max(s, axis=0, keepdims=True)

    def probs(unit, s, smax):
        sb, h, half = unit
        hd0 = GQA * h + 2 * half
        sink = sink_row[:, hd0 * BLOCK:(hd0 + 2) * BLOCK]
        m = jnp.maximum(smax, sink)
        return jnp.exp2(s - m).astype(BF16), jnp.exp2(sink - m)

    def finish(unit, e, e_sink):
        sb, h, half = unit
        lo, hi = sb * BLOCK, (sb + 1) * BLOCK
        hd0 = GQA * h + 2 * half
        vt_ones = jnp.concatenate(
            [windows[sb]["vtwin"][h * HEAD_DIM:(h + 1) * HEAD_DIM, :], ones_rows], axis=0)
        o = jnp.dot(vt_ones, e, preferred_element_type=F32)
        denom = o[HEAD_DIM:HEAD_DIM + 1, :] + e_sink
        o = o[:HEAD_DIM, :] * (1.0 / denom)
        blk = jnp.concatenate([o[:, 0:BLOCK], o[:, BLOCK:2 * BLOCK]], axis=0)
        o_ref[lo:hi, hd0 * HEAD_DIM:(hd0 + 2) * HEAD_DIM] = blk.T.astype(BF16)

    cast_jobs = []
    for src, dst in ((wout_ref, wout_b_ref), (wup_ref, wup_b_ref), (wdn_ref, wdn_b_ref)):
        wc = max(LANES, CAST_JOB_ELEMS // src.shape[0])
        cast_jobs += [(src, dst, c0) + (wc,) for c0 in range(0, src.shape[1], wc)]

    def cast_piece(idx, n):
        for src, dst, c0, wc in cast_jobs[idx * len(cast_jobs) // n:(idx + 1) * len(cast_jobs) // n]:
            val = src[:, c0:c0 + wc].astype(BF16)
            if len(dst.shape) == 3:
                tf = dst.shape[2]
                dst[c0 // tf, :, c0 % tf:c0 % tf + wc] = val
            else:
                dst[:, c0:c0 + wc] = val

    nu = len(units)
    s_vals, p_vals = {}, {}
    for step in range(nu + 2):
        if step < nu:
            s_vals[step] = scores(units[step])
        if 1 <= step <= nu:
            p_vals[step - 1] = probs(units[step - 1], *s_vals.pop(step - 1))
            cast_piece(step - 1, nu)
        if step >= 2:
            finish(units[step - 2], *p_vals.pop(step - 2))


def _attn(qt, k, vt, kc, vtc, sink, w_out, w_up, w_dn, c, c_ctx, w_ada, b_ada, ada_col0):
    tq = ATTN_TQ
    r = tq // BLOCK
    nblk = SEQ // BLOCK
    nsteps = SEQ // tq
    ada_tn = (w_ada.shape[1] - ada_col0) // nsteps
    ada_b0 = ada_col0 // ada_tn
    fixed = lambda i: (0, 0)
    row = lambda i: (i, 0)
    slab = lambda w: pl.BlockSpec((w.shape[0] // nsteps, w.shape[1]), row)
    bf16_like = lambda w: jax.ShapeDtypeStruct(w.shape, BF16)
    prev = lambda i: jnp.maximum(i * r - 1, 0)
    nxt = lambda i: jnp.minimum((i + 1) * r, nblk - 1)
    return pl.pallas_call(
        _attn_kernel,
        grid=(SEQ // tq,),
        in_specs=[
            pl.BlockSpec((ATTN_WIDTH, tq), lambda i: (0, i)),
            pl.BlockSpec((BLOCK, KV_WIDTH), lambda i: (prev(i), 0)),
            pl.BlockSpec((tq, KV_WIDTH), lambda i: (i, 0)),
            pl.BlockSpec((BLOCK, KV_WIDTH), lambda i: (nxt(i), 0)),
            pl.BlockSpec((KV_WIDTH, BLOCK), lambda i: (0, prev(i))),
            pl.BlockSpec((KV_WIDTH, tq), lambda i: (0, i)),
            pl.BlockSpec((KV_WIDTH, BLOCK), lambda i: (0, nxt(i))),
            pl.BlockSpec((CTX_LEN, KV_WIDTH), fixed),
            pl.BlockSpec((KV_WIDTH, CTX_LEN), fixed),
            pl.BlockSpec((1, N_Q_HEADS * BLOCK), fixed),
            slab(w_out), slab(w_up), slab(w_dn),
            pl.BlockSpec((1, D_MODEL), fixed),
            pl.BlockSpec((1, D_MODEL), fixed),
            pl.BlockSpec((D_MODEL, ada_tn), lambda i: (0, ada_b0 + i)),
            pl.BlockSpec((1, ada_tn), lambda i: (0, ada_b0 + i)),
        ],
        out_specs=[pl.BlockSpec((tq, ATTN_WIDTH), row), slab(w_out),
                   pl.BlockSpec((MLP_NF, w_up.shape[0] // nsteps, MLP_TF), lambda i: (0, i, 0)),
                   slab(w_dn),
                   pl.BlockSpec((SUBLANES, ada_tn), lambda i: (0, i))],
        out_shape=[jax.ShapeDtypeStruct((SEQ, ATTN_WIDTH), BF16),
                   bf16_like(w_out),
                   jax.ShapeDtypeStruct((MLP_NF, w_up.shape[0], MLP_TF), BF16),
                   bf16_like(w_dn),
                   jax.ShapeDtypeStruct((SUBLANES, w_ada.shape[1] - ada_col0), F32)],
        compiler_params=_params("arbitrary"),
        name="attn",
    )(qt, k, k, k, vt, vt, vt, kc, vtc, sink, w_out, w_up, w_dn, c, c_ctx, w_ada, b_ada)


def _mixout_kernel(x_ref, attn_ref, up_ref, um_ref, un_ref, wout_ref, pw_ref, ps_ref,
                   ga_ref, nw_ref, sh_ref, sc_ref, x1_ref, hm_ref, ubuf):
    i = pl.program_id(0)
    tm = MIX_TM
    nsteps = SEQ // tm
    ubuf[0:POOL_HALO, :] = jnp.where(i > 0, up_ref[...], 0.0)
    ubuf[POOL_HALO:POOL_HALO + tm, :] = um_ref[...]
    ubuf[POOL_HALO + tm:2 * POOL_HALO + tm, :] = jnp.where(i < nsteps - 1, un_ref[...], 0.0)

    rc = MIX_CHUNK
    n = rc + 2 * POOL_HALO
    c = POOL_HALO
    ngroups = len(POOL_WINDOWS)
    ncol = D_MODEL // ngroups

    def pool_group(r0, g):
        w = POOL_WINDOWS[g]
        c0, c1 = g * POOL_GROUP_DIM, (g + 1) * POOL_GROUP_DIM
        e = ubuf[r0:r0 + n, c0:c1]
        if w == 2:
            win = e[c - 1:c - 1 + rc] + e[c:c + rc]
        else:
            f2 = e[0:n - 1] + e[1:n]
            if w == 4:
                win = f2[c - 2:c - 2 + rc] + f2[c:c + rc]
            else:
                f4 = f2[0:n - 3] + f2[2:n - 1]
                if w == 8:
                    win = f4[c - 4:c - 4 + rc] + f4[c:c + rc]
                else:
                    f8 = f4[0:n - 7] + f4[4:n - 3]
                    win = f8[c - 8:c - 8 + rc] + f8[c:c + rc]
        t = i * tm + r0 + lax.broadcasted_iota(jnp.int32, (rc, 1), 0)
        lo = jnp.clip(t - w // 2, 0, SEQ)
        hi = jnp.clip(t - w // 2 + w, 0, SEQ)
        cnt = (hi - lo).astype(F32)
        pg = win / cnt - um_ref[r0:r0 + rc, c0:c1]
        mixed = jnp.dot(pg.astype(BF16), pw_ref[g].astype(BF16), preferred_element_type=F32)
        return (mixed * ps_ref[:, c0:c1]).astype(BF16)

    def prenorm(r0, nrows):
        rows = slice(r0, r0 + nrows)
        hm_ref[rows, :] = _rms_modulate(x1_ref[rows, :], nw_ref[...], sh_ref[LATENT_ROW],
                                        sc_ref[LATENT_ROW]).astype(BF16)

    chunk_starts = list(range(0, tm, rc))
    for ci, r0 in enumerate(chunk_starts):
        rows = slice(r0, r0 + rc)
        attn_rows = attn_ref[rows, :]
        pooled = []
        for j in range(ngroups):
            cols = slice(j * ncol, (j + 1) * ncol)
            o = jnp.dot(attn_rows, wout_ref[0:ATTN_WIDTH, cols], preferred_element_type=F32)
            x1_ref[rows, cols] = x_ref[rows, cols] + ga_ref[LATENT_ROW, cols] * o
            pooled.append(pool_group(r0, j))
        pool_rows = jnp.concatenate(pooled, axis=1)
        for j in range(ngroups):
            cols = slice(j * ncol, (j + 1) * ncol)
            o = jnp.dot(pool_rows, wout_ref[ATTN_WIDTH:, cols], preferred_element_type=F32)
            x1_ref[rows, cols] += ga_ref[LATENT_ROW, cols] * o
            if ci > 0:
                prenorm(chunk_starts[ci - 1] + j * (rc // ngroups), rc // ngroups)
    for j in range(ngroups):
        prenorm(chunk_starts[-1] + j * (rc // ngroups), rc // ngroups)


def _mixout(x, attn, u, w_out_bf16, pool_w, pool_scale, mod, norm_w):
    tm = MIX_TM
    hb = tm // POOL_HALO
    nhalo = SEQ // POOL_HALO
    row = lambda i: (i, 0)
    fixed = lambda i: (0, 0)
    return pl.pallas_call(
        _mixout_kernel,
        grid=(SEQ // tm,),
        in_specs=[
            pl.BlockSpec((tm, D_MODEL), row),
            pl.BlockSpec((tm, ATTN_WIDTH), row),
            pl.BlockSpec((POOL_HALO, POOL_WIDTH), lambda i: (jnp.maximum(i * hb - 1, 0), 0)),
            pl.BlockSpec((tm, POOL_WIDTH), row),
            pl.BlockSpec((POOL_HALO, POOL_WIDTH), lambda i: (jnp.minimum((i + 1) * hb, nhalo - 1), 0)),
            pl.BlockSpec((D_MODEL, D_MODEL), fixed),
            pl.BlockSpec((len(POOL_WINDOWS), POOL_GROUP_DIM, POOL_GROUP_DIM), lambda i: (0, 0, 0)),
            pl.BlockSpec((1, POOL_WIDTH), fixed),
            _mod_spec(0),
            pl.BlockSpec((1, D_MODEL), fixed),
            _mod_spec(1),
            _mod_spec(2),
        ],
        out_specs=[pl.BlockSpec((tm, D_MODEL), row), pl.BlockSpec((tm, D_MODEL), row)],
        out_shape=[jax.ShapeDtypeStruct((SEQ, D_MODEL), F32),
                   jax.ShapeDtypeStruct((SEQ, D_MODEL), BF16)],
        scratch_shapes=[pltpu.VMEM((tm + 2 * POOL_HALO, POOL_WIDTH), F32)],
        compiler_params=_params("arbitrary"),
        name="mixout",
    )(x, attn, u, u, u, w_out_bf16, pool_w, pool_scale, mod, norm_w, mod, mod)


def _mlp_kernel(hm_ref, x1r_ref, wup_ref, wdn_ref, gm_ref, fw_ref, o_ref):
    f = pl.program_id(1)
    last = MLP_NF - 1
    row_chunks = [slice(r0, r0 + MLP_RC) for r0 in range(0, MLP_TM, MLP_RC)]
    gm = gm_ref[LATENT_ROW]

    def branch_out(rows):
        a = jnp.maximum(jnp.dot(hm_ref[rows, :], wup_ref[...], preferred_element_type=F32), 0.0)
        return gm * jnp.dot((a * a).astype(BF16), wdn_ref[...], preferred_element_type=F32)

    def final_norm(rows):
        x2 = o_ref[rows, :]
        y = x2 * lax.rsqrt(jnp.mean(x2 * x2, axis=-1, keepdims=True) + EPS)
        o_ref[rows, :] = y * fw_ref[...]

    @pl.when(f == 0)
    def _():
        for rows in row_chunks:
            o_ref[rows, :] = branch_out(rows)
        o_ref[0:MLP_XR, :] += x1r_ref[...]

    @pl.when(jnp.logical_and(f > 0, f < last))
    def _():
        x_rows = pl.ds(pl.multiple_of(f * MLP_XR, MLP_XR), MLP_XR)
        o_ref[x_rows, :] += x1r_ref[...]
        for rows in row_chunks:
            o_ref[rows, :] += branch_out(rows)

    @pl.when(f == last)
    def _():
        o_ref[last * MLP_XR:(last + 1) * MLP_XR, :] += x1r_ref[...]
        npiece = 4
        cw, rp = D_MODEL // npiece, MLP_RC // npiece
        for ci, rows in enumerate(row_chunks):
            a = jnp.maximum(jnp.dot(hm_ref[rows, :], wup_ref[...], preferred_element_type=F32), 0.0)
            a = (a * a).astype(BF16)
            for j in range(npiece):
                cols = slice(j * cw, (j + 1) * cw)
                o_ref[rows, cols] += gm[:, cols] * jnp.dot(a, wdn_ref[:, cols],
                                                           preferred_element_type=F32)
                if ci > 0:
                    r0 = row_chunks[ci - 1].start + j * rp
                    final_norm(slice(r0, r0 + rp))
        final_norm(row_chunks[-1])


def _mlp(hm, x1, w_up_bf16, w_dn_bf16, mod, final_w):
    tm, tf = MLP_TM, MLP_TF
    return pl.pallas_call(
        _mlp_kernel,
        grid=(SEQ // tm, MLP_NF),
        in_specs=[
            pl.BlockSpec((tm, D_MODEL), lambda i, f: (i, 0)),
            pl.BlockSpec((MLP_XR, D_MODEL), lambda i, f: (i * MLP_NF + f, 0)),
            pl.BlockSpec((None, D_MODEL, tf), lambda i, f: (f, 0, 0)),
            pl.BlockSpec((tf, D_MODEL), lambda i, f: (f, 0)),
            _mod_spec(3),
            pl.BlockSpec((1, D_MODEL), lambda i, f: (0, 0)),
        ],
        out_specs=pl.BlockSpec((tm, D_MODEL), lambda i, f: (i, 0)),
        out_shape=jax.ShapeDtypeStruct((SEQ, D_MODEL), F32),
        compiler_params=_params("arbitrary", "arbitrary"),
        name="mlp",
    )(hm, x1, w_up_bf16, w_dn_bf16, mod, final_w)


def kernel(x, c, ctx, c_ctx, norm_attn_w, norm_mlp_w, w_ada, b_ada, w_in, attn_sink,
           pool_w, pool_scale, w_out, w_mlp_up, w_mlp_down, final_norm_w):
    assert x.shape == (1, SEQ, D_MODEL) and ctx.shape == (1, CTX_LEN, D_MODEL)
    assert w_ada.shape[0] == 1, "single layer"
    x2d, ctx2d = x[0], ctx[0]

    c_ctx_row = c_ctx[None, :]
    b_row = b_ada[0][None, :]
    n_early = 2 * D_MODEL
    mod_a = _ada(c, c_ctx_row, w_ada[0], b_row, n_early)

    nw_a = norm_attn_w[0][None, :]
    tables = tuple(jnp.asarray(t) for t in _rope_tables())
    qt, k, vt, u = _inproj(x2d, nw_a, mod_a, w_in[0], tables)
    kc, vtc = _ctx_kv(ctx2d, nw_a, mod_a, w_in[0])

    sink_row = jnp.repeat(attn_sink[0].astype(F32), BLOCK)[None, :]
    attn, w_out_b, w_up_b, w_dn_b, mod_b = _attn(
        qt, k, vt, kc, vtc, sink_row, w_out[0], w_mlp_up[0], w_mlp_down[0],
        c, c_ctx_row, w_ada[0], b_row, n_early)

    x1, hm = _mixout(x2d, attn, u, w_out_b, pool_w[0], pool_scale[0][None, :], mod_b,
                     norm_mlp_w[0][None, :])
    out = _mlp(hm, x1, w_up_b, w_dn_b, mod_b, final_norm_w[None, :])
    return out[None]
```

```python
import numpy as np
import jax
import jax.numpy as jnp
from jax import lax
from jax.experimental import pallas as pl
from jax.experimental.pallas import tpu as pltpu

F32 = jnp.float32
BF16 = jnp.bfloat16

D_MODEL = 2048
SEQ = 8192
CTX_LEN = 256
GRID_W = 64
HEAD_DIM = 64
N_Q_HEADS = 16
N_KV_HEADS = 4
GQA = N_Q_HEADS // N_KV_HEADS
ATTN_WIDTH = N_Q_HEADS * HEAD_DIM
KV_WIDTH = N_KV_HEADS * HEAD_DIM
POOL_WINDOWS = (2, 4, 8, 16)
POOL_WIDTH = D_MODEL - ATTN_WIDTH
POOL_GROUP_DIM = POOL_WIDTH // len(POOL_WINDOWS)
IN_WIDTH = ATTN_WIDTH + 2 * KV_WIDTH + POOL_WIDTH
D_FF = 4 * D_MODEL
BLOCK = 128
ROPE_BASE = 10000.0
N_MOD = 6
EPS = 1e-6
NEG_INF = -1e30
LOG2E = 1.4426950408889634
Q_SCALE = HEAD_DIM ** -0.5 * LOG2E

LANES = 128
SUBLANES = 8
POOL_HALO = 8
LATENT_ROW = slice(0, 1)
CONTEXT_ROW = slice(1, 2)
VMEM_LIMIT = 56 * 1024 * 1024

ADA_TN = 512
INPROJ_TM = 512
ATTN_TQ = 512
CAST_JOB_ELEMS = 64 * 1024
MIX_TM = 512
MIX_CHUNK = 256
MLP_TM = 1024
MLP_TF = 1024
MLP_RC = 512
MLP_NF = D_FF // MLP_TF
MLP_XR = MLP_TM // MLP_NF


def _params(*sem):
    return pltpu.CompilerParams(dimension_semantics=sem, vmem_limit_bytes=VMEM_LIMIT)


def _mod_spec(chunk):
    return pl.BlockSpec((SUBLANES, D_MODEL), lambda *_: (0, chunk))


def _rms_modulate(x, norm_w, shift, scale):
    gain = norm_w * (1.0 + scale)
    y = x * lax.rsqrt(jnp.mean(x * x, axis=-1, keepdims=True) + EPS)
    return y * gain + shift


def _silu_rows(c_ref, cctx_ref):
    cnd = jnp.concatenate([c_ref[...], cctx_ref[...], jnp.zeros((SUBLANES - 2, D_MODEL), F32)], axis=0)
    return cnd * (1.0 / (1.0 + jnp.exp(-cnd)))


def _ada_cols(s, w_ref, b_ref, cols=slice(None)):
    return jnp.dot(s, w_ref[:, cols], preferred_element_type=F32) + b_ref[:, cols]


def _ada_kernel(c_ref, cctx_ref, w_ref, b_ref, o_ref):
    o_ref[...] = _ada_cols(_silu_rows(c_ref, cctx_ref), w_ref, b_ref)


def _ada(c, c_ctx, w_ada, b_ada, n):
    return pl.pallas_call(
        _ada_kernel,
        grid=(n // ADA_TN,),
        in_specs=[
            pl.BlockSpec((1, D_MODEL), lambda j: (0, 0)),
            pl.BlockSpec((1, D_MODEL), lambda j: (0, 0)),
            pl.BlockSpec((D_MODEL, ADA_TN), lambda j: (0, j)),
            pl.BlockSpec((1, ADA_TN), lambda j: (0, j)),
        ],
        out_specs=pl.BlockSpec((SUBLANES, ADA_TN), lambda j: (0, j)),
        out_shape=jax.ShapeDtypeStruct((SUBLANES, n), F32),
        compiler_params=_params("arbitrary"),
        name="ada",
    )(c, c_ctx, w_ada, b_ada)


def _rope_tables():
    half = HEAD_DIM // 2
    inv_freq = ROPE_BASE ** (-np.arange(0, half, 2, dtype=np.float64) / half)
    t = np.arange(SEQ)
    row = (t // GRID_W).astype(np.float64)
    col = (t % GRID_W).astype(np.float64)
    ang_r = row[:, None] * inv_freq[None, :]
    ang_c = col[:, None] * inv_freq[None, :]
    cos_t = Q_SCALE * np.concatenate([np.cos(ang_r), np.cos(ang_c)], axis=1).T
    sin_t = Q_SCALE * np.concatenate([np.sin(ang_r), np.sin(ang_c)], axis=1).T
    cos_l = np.concatenate([np.cos(ang_r), np.cos(ang_r), np.cos(ang_c), np.cos(ang_c)], axis=1)
    zeros = np.zeros_like(ang_r)
    s_up = np.concatenate([-np.sin(ang_r), zeros, -np.sin(ang_c), zeros], axis=1)
    s_dn = np.concatenate([zeros, np.sin(ang_r), zeros, np.sin(ang_c)], axis=1)
    tile2 = lambda a: np.concatenate([a, a], axis=1)
    q_table = np.concatenate([cos_t, sin_t], axis=0).astype(np.float32)
    k_table = np.concatenate([tile2(cos_l), tile2(s_up), tile2(s_dn)], axis=1).astype(np.float32)
    return q_table, k_table


def _inproj_kernel(x_ref, nw_ref, sh_ref, sc_ref, w_ref, qtab_ref, ktab_ref,
                   qt_ref, k_ref, vt_ref, u_ref, wb_ref):
    @pl.when(pl.program_id(0) == 0)
    def _():
        for c0 in range(0, IN_WIDTH, 2 * LANES):
            wb_ref[:, c0:c0 + 2 * LANES] = w_ref[:, c0:c0 + 2 * LANES].astype(BF16)

    h = _rms_modulate(x_ref[...], nw_ref[...], sh_ref[LATENT_ROW], sc_ref[LATENT_ROW]).astype(BF16)
    p = jnp.dot(h, wb_ref[...], preferred_element_type=F32)

    qt = p[:, :ATTN_WIDTH].T
    cr, cc = qtab_ref[0:16, :], qtab_ref[16:32, :]
    sr, sc = qtab_ref[32:48, :], qtab_ref[48:64, :]
    pieces = []
    for hd in range(N_Q_HEADS):
        b = hd * HEAD_DIM
        x1r, x2r = qt[b:b + 16, :], qt[b + 16:b + 32, :]
        x1c, x2c = qt[b + 32:b + 48, :], qt[b + 48:b + 64, :]
        pieces += [x1r * cr - x2r * sr, x1r * sr + x2r * cr,
                   x1c * cc - x2c * sc, x1c * sc + x2c * cc]
    qt_ref[...] = jnp.concatenate(pieces, axis=0).astype(BF16)

    k = p[:, ATTN_WIDTH:ATTN_WIDTH + KV_WIDTH]
    cos_l, s_up, s_dn = (ktab_ref[:, j * LANES:(j + 1) * LANES] for j in range(3))
    kparts = []
    for g in range(KV_WIDTH // LANES):
        kg = k[:, g * LANES:(g + 1) * LANES]
        kparts.append(kg * cos_l + pltpu.roll(kg, LANES - 16, 1) * s_up
                      + pltpu.roll(kg, 16, 1) * s_dn)
    k_ref[...] = jnp.concatenate(kparts, axis=1).astype(BF16)

    vt_ref[...] = p[:, ATTN_WIDTH + KV_WIDTH:ATTN_WIDTH + 2 * KV_WIDTH].T.astype(BF16)
    u_ref[...] = p[:, ATTN_WIDTH + 2 * KV_WIDTH:]


def _inproj(x, norm_w, mod, w_in, tables):
    q_table, k_table = tables
    tm = INPROJ_TM
    row = lambda i: (i, 0)
    colb = lambda i: (0, i)
    fixed = lambda i: (0, 0)
    return pl.pallas_call(
        _inproj_kernel,
        grid=(SEQ // tm,),
        in_specs=[
            pl.BlockSpec((tm, D_MODEL), row),
            pl.BlockSpec((1, D_MODEL), fixed),
            _mod_spec(0),
            _mod_spec(1),
            pl.BlockSpec((D_MODEL, IN_WIDTH), fixed, pipeline_mode=pl.Buffered(1)),
            pl.BlockSpec((q_table.shape[0], tm), colb),
            pl.BlockSpec((tm, k_table.shape[1]), row),
        ],
        out_specs=[
            pl.BlockSpec((ATTN_WIDTH, tm), colb),
            pl.BlockSpec((tm, KV_WIDTH), row),
            pl.BlockSpec((KV_WIDTH, tm), colb),
            pl.BlockSpec((tm, POOL_WIDTH), row),
        ],
        out_shape=[
            jax.ShapeDtypeStruct((ATTN_WIDTH, SEQ), BF16),
            jax.ShapeDtypeStruct((SEQ, KV_WIDTH), BF16),
            jax.ShapeDtypeStruct((KV_WIDTH, SEQ), BF16),
            jax.ShapeDtypeStruct((SEQ, POOL_WIDTH), F32),
        ],
        scratch_shapes=[pltpu.VMEM((D_MODEL, IN_WIDTH), BF16)],
        compiler_params=_params("arbitrary"),
        name="inproj",
    )(x, norm_w, mod, mod, w_in, q_table, k_table)


def _ctx_kv_kernel(x_ref, nw_ref, sh_ref, sc_ref, w_ref, k_ref, vt_ref):
    h = _rms_modulate(x_ref[...], nw_ref[...], sh_ref[CONTEXT_ROW], sc_ref[CONTEXT_ROW]).astype(BF16)
    p = jnp.dot(h, w_ref[...].astype(BF16), preferred_element_type=F32)
    k_ref[...] = p[:, :KV_WIDTH].astype(BF16)
    vt_ref[...] = p[:, KV_WIDTH:].T.astype(BF16)


def _ctx_kv(ctx, norm_w, mod, w_in):
    fixed = lambda i: (0, 0)
    kv_block = ATTN_WIDTH // (2 * KV_WIDTH)
    return pl.pallas_call(
        _ctx_kv_kernel,
        grid=(1,),
        in_specs=[
            pl.BlockSpec((CTX_LEN, D_MODEL), fixed),
            pl.BlockSpec((1, D_MODEL), fixed),
            _mod_spec(0),
            _mod_spec(1),
            pl.BlockSpec((D_MODEL, 2 * KV_WIDTH), lambda i: (0, kv_block)),
        ],
        out_specs=[
            pl.BlockSpec((CTX_LEN, KV_WIDTH), fixed),
            pl.BlockSpec((KV_WIDTH, CTX_LEN), fixed),
        ],
        out_shape=[
            jax.ShapeDtypeStruct((CTX_LEN, KV_WIDTH), BF16),
            jax.ShapeDtypeStruct((KV_WIDTH, CTX_LEN), BF16),
        ],
        compiler_params=_params("arbitrary"),
        name="ctx_kv",
    )(ctx, norm_w, mod, mod, w_in)


def _attn_kernel(qt_ref, kp_ref, km_ref, kn_ref, vtp_ref, vtm_ref, vtn_ref,
                 kc_ref, vtc_ref, sink_ref, wout_ref, wup_ref, wdn_ref,
                 c_ref, cctx_ref, wada_ref, bada_ref,
                 o_ref, wout_b_ref, wup_b_ref, wdn_b_ref, mod_ref):
    cond_act = _silu_rows(c_ref, cctx_ref)
    ada_cols = [slice(c0, c0 + 2 * LANES) for c0 in range(0, mod_ref.shape[1], 2 * LANES)]

    def ada_piece(idx, n):
        for cols in ada_cols[idx * len(ada_cols) // n:(idx + 1) * len(ada_cols) // n]:
            mod_ref[:, cols] = _ada_cols(cond_act, wada_ref, bada_ref, cols)

    i = pl.program_id(0)
    nsub = ATTN_TQ // BLOCK
    nblk = SEQ // BLOCK
    pw = 2 * BLOCK
    nkeys = 3 * BLOCK + CTX_LEN

    key_j = lax.broadcasted_iota(jnp.int32, (BLOCK, pw), 0)
    qry_i = lax.broadcasted_iota(jnp.int32, (BLOCK, pw), 1) % BLOCK
    neg = jnp.full((BLOCK, pw), NEG_INF, F32)
    zero = jnp.zeros((BLOCK, pw), F32)
    band_prev = jnp.where(key_j >= qry_i, zero, neg)
    band_next = jnp.where(key_j <= qry_i, zero, neg)

    kc = kc_ref[...]
    vtc = vtc_ref[...]
    ones_rows = jnp.ones((2 * SUBLANES, nkeys), BF16)
    sink_row = sink_ref[...] * LOG2E

    windows = []
    for sb in range(nsub):
        n = i * nsub + sb
        lo, hi = sb * BLOCK, (sb + 1) * BLOCK
        k_prev = kp_ref[...] if sb == 0 else km_ref[lo - BLOCK:lo, :]
        k_next = kn_ref[...] if sb == nsub - 1 else km_ref[hi:hi + BLOCK, :]
        vt_prev = vtp_ref[...] if sb == 0 else vtm_ref[:, lo - BLOCK:lo]
        vt_next = vtn_ref[...] if sb == nsub - 1 else vtm_ref[:, hi:hi + BLOCK]
        windows.append(dict(
            kwin=jnp.concatenate([k_prev, km_ref[lo:hi, :], k_next, kc], axis=0),
            vtwin=jnp.concatenate([vt_prev, vtm_ref[:, lo:hi], vt_next, vtc], axis=1),
            bias_prev=band_prev + jnp.where(n > 0, 0.0, NEG_INF),
            bias_next=band_next + jnp.where(n < nblk - 1, 0.0, NEG_INF)))

    units = [(sb, h, half) for sb in range(nsub) for h in range(N_KV_HEADS)
             for half in range(GQA // 2)]

    def scores(unit):
        sb, h, half = unit
        win = windows[sb]
        lo, hi = sb * BLOCK, (sb + 1) * BLOCK
        hd0 = GQA * h + 2 * half
        qh = jnp.concatenate([qt_ref[(hd0 + g) * HEAD_DIM:(hd0 + g + 1) * HEAD_DIM, lo:hi]
                              for g in range(2)], axis=1)
        blocks = []
        if h > 0:
            blocks.append(jnp.zeros((h * HEAD_DIM, pw), BF16))
        blocks.append(qh)
        if h < N_KV_HEADS - 1:
            blocks.append(jnp.zeros(((N_KV_HEADS - 1 - h) * HEAD_DIM, pw), BF16))
        s = jnp.dot(win["kwin"], jnp.concatenate(blocks, axis=0),
                    preferred_element_type=F32)
        s = jnp.concatenate([s[0:BLOCK] + win["bias_prev"], s[BLOCK:2 * BLOCK],
                             s[2 * BLOCK:3 * BLOCK] + win["bias_next"], s[3 * BLOCK:]], axis=0)
        return s, jnp.max(s, axis=0, keepdims=True)

    def probs(unit, s, smax):
        sb, h, half = unit
        hd0 = GQA * h + 2 * half
        sink = sink_row[:, hd0 * BLOCK:(hd0 + 2) * BLOCK]
        m = jnp.maximum(smax, sink)
        return jnp.exp2(s - m).astype(BF16), jnp.exp2(sink - m)

    def finish(unit, e, e_sink):
        sb, h, half = unit
        lo, hi = sb * BLOCK, (sb + 1) * BLOCK
        hd0 = GQA * h + 2 * half
        vt_ones = jnp.concatenate(
            [windows[sb]["vtwin"][h * HEAD_DIM:(h + 1) * HEAD_DIM, :], ones_rows], axis=0)
        o = jnp.dot(vt_ones, e, preferred_element_type=F32)
        denom = o[HEAD_DIM:HEAD_DIM + 1, :] + e_sink
        o = o[:HEAD_DIM, :] * (1.0 / denom)
        blk = jnp.concatenate([o[:, 0:BLOCK], o[:, BLOCK:2 * BLOCK]], axis=0)
        o_ref[lo:hi, hd0 * HEAD_DIM:(hd0 + 2) * HEAD_DIM] = blk.T.astype(BF16)

    cast_jobs = []
    for src, dst in ((wout_ref, wout_b_ref), (wup_ref, wup_b_ref), (wdn_ref, wdn_b_ref)):
        wc = max(LANES, CAST_JOB_ELEMS // src.shape[0])
        cast_jobs += [(src, dst, c0) + (wc,) for c0 in range(0, src.shape[1], wc)]

    def cast_piece(idx, n):
        for src, dst, c0, wc in cast_jobs[idx * len(cast_jobs) // n:(idx + 1) * len(cast_jobs) // n]:
            val = src[:, c0:c0 + wc].astype(BF16)
            if len(dst.shape) == 3:
                tf = dst.shape[2]
                dst[c0 // tf, :, c0 % tf:c0 % tf + wc] = val
            else:
                dst[:, c0:c0 + wc] = val

    nu = len(units)
    s_vals, p_vals = {}, {}
    for step in range(nu + 2):
        if step < nu:
            s_vals[step] = scores(units[step])
        if 1 <= step <= nu:
            p_vals[step - 1] = probs(units[step - 1], *s_vals.pop(step - 1))
            cast_piece(step - 1, nu)
            ada_piece(step - 1, nu)
        if step >= 2:
            finish(units[step - 2], *p_vals.pop(step - 2))


def _attn(qt, k, vt, kc, vtc, sink, w_out, w_up, w_dn, c, c_ctx, w_ada, b_ada, ada_col0):
    tq = ATTN_TQ
    r = tq // BLOCK
    nblk = SEQ // BLOCK
    nsteps = SEQ // tq
    ada_tn = (w_ada.shape[1] - ada_col0) // nsteps
    ada_b0 = ada_col0 // ada_tn
    fixed = lambda i: (0, 0)
    row = lambda i: (i, 0)
    slab = lambda w: pl.BlockSpec((w.shape[0] // nsteps, w.shape[1]), row)
    bf16_like = lambda w: jax.ShapeDtypeStruct(w.shape, BF16)
    prev = lambda i: jnp.maximum(i * r - 1, 0)
    nxt = lambda i: jnp.minimum((i + 1) * r, nblk - 1)
    return pl.pallas_call(
        _attn_kernel,
        grid=(SEQ // tq,),
        in_specs=[
            pl.BlockSpec((ATTN_WIDTH, tq), lambda i: (0, i)),
            pl.BlockSpec((BLOCK, KV_WIDTH), lambda i: (prev(i), 0)),
            pl.BlockSpec((tq, KV_WIDTH), lambda i: (i, 0)),
            pl.BlockSpec((BLOCK, KV_WIDTH), lambda i: (nxt(i), 0)),
            pl.BlockSpec((KV_WIDTH, BLOCK), lambda i: (0, prev(i))),
            pl.BlockSpec((KV_WIDTH, tq), lambda i: (0, i)),
            pl.BlockSpec((KV_WIDTH, BLOCK), lambda i: (0, nxt(i))),
            pl.BlockSpec((CTX_LEN, KV_WIDTH), fixed),
            pl.BlockSpec((KV_WIDTH, CTX_LEN), fixed),
            pl.BlockSpec((1, N_Q_HEADS * BLOCK), fixed),
            slab(w_out), slab(w_up), slab(w_dn),
            pl.BlockSpec((1, D_MODEL), fixed),
            pl.BlockSpec((1, D_MODEL), fixed),
            pl.BlockSpec((D_MODEL, ada_tn), lambda i: (0, ada_b0 + i)),
            pl.BlockSpec((1, ada_tn), lambda i: (0, ada_b0 + i)),
        ],
        out_specs=[pl.BlockSpec((tq, ATTN_WIDTH), row), slab(w_out),
                   pl.BlockSpec((MLP_NF, w_up.shape[0] // nsteps, MLP_TF), lambda i: (0, i, 0)),
                   slab(w_dn),
                   pl.BlockSpec((SUBLANES, ada_tn), lambda i: (0, i))],
        out_shape=[jax.ShapeDtypeStruct((SEQ, ATTN_WIDTH), BF16),
                   bf16_like(w_out),
                   jax.ShapeDtypeStruct((MLP_NF, w_up.shape[0], MLP_TF), BF16),
                   bf16_like(w_dn),
                   jax.ShapeDtypeStruct((SUBLANES, w_ada.shape[1] - ada_col0), F32)],
        compiler_params=_params("arbitrary"),
        name="attn",
    )(qt, k, k, k, vt, vt, vt, kc, vtc, sink, w_out, w_up, w_dn, c, c_ctx, w_ada, b_ada)


def _mixout_kernel(x_ref, attn_ref, up_ref, um_ref, un_ref, wout_ref, pw_ref, ps_ref,
                   ga_ref, nw_ref, sh_ref, sc_ref, x1_ref, hm_ref, ubuf):
    i = pl.program_id(0)
    tm = MIX_TM
    nsteps = SEQ // tm
    ubuf[0:POOL_HALO, :] = jnp.where(i > 0, up_ref[...], 0.0)
    ubuf[POOL_HALO:POOL_HALO + tm, :] = um_ref[...]
    ubuf[POOL_HALO + tm:2 * POOL_HALO + tm, :] = jnp.where(i < nsteps - 1, un_ref[...], 0.0)

    rc = MIX_CHUNK
    n = rc + 2 * POOL_HALO
    c = POOL_HALO
    ngroups = len(POOL_WINDOWS)
    ncol = D_MODEL // ngroups

    def pool_group(r0, g):
        w = POOL_WINDOWS[g]
        c0, c1 = g * POOL_GROUP_DIM, (g + 1) * POOL_GROUP_DIM
        e = ubuf[r0:r0 + n, c0:c1]
        if w == 2:
            win = e[c - 1:c - 1 + rc] + e[c:c + rc]
        else:
            f2 = e[0:n - 1] + e[1:n]
            if w == 4:
                win = f2[c - 2:c - 2 + rc] + f2[c:c + rc]
            else:
                f4 = f2[0:n - 3] + f2[2:n - 1]
                if w == 8:
                    win = f4[c - 4:c - 4 + rc] + f4[c:c + rc]
                else:
                    f8 = f4[0:n - 7] + f4[4:n - 3]
                    win = f8[c - 8:c - 8 + rc] + f8[c:c + rc]
        t = i * tm + r0 + lax.broadcasted_iota(jnp.int32, (rc, 1), 0)
        lo = jnp.clip(t - w // 2, 0, SEQ)
        hi = jnp.clip(t - w // 2 + w, 0, SEQ)
        cnt = (hi - lo).astype(F32)
        pg = win / cnt - um_ref[r0:r0 + rc, c0:c1]
        mixed = jnp.dot(pg.astype(BF16), pw_ref[g].astype(BF16), preferred_element_type=F32)
        return (mixed * ps_ref[:, c0:c1]).astype(BF16)

    def prenorm(r0, nrows):
        rows = slice(r0, r0 + nrows)
        hm_ref[rows, :] = _rms_modulate(x1_ref[rows, :], nw_ref[...], sh_ref[LATENT_ROW],
                                        sc_ref[LATENT_ROW]).astype(BF16)

    chunk_starts = list(range(0, tm, rc))
    for ci, r0 in enumerate(chunk_starts):
        rows = slice(r0, r0 + rc)
        attn_rows = attn_ref[rows, :]
        pooled = []
        for j in range(ngroups):
            cols = slice(j * ncol, (j + 1) * ncol)
            o = jnp.dot(attn_rows, wout_ref[0:ATTN_WIDTH, cols], preferred_element_type=F32)
            x1_ref[rows, cols] = x_ref[rows, cols] + ga_ref[LATENT_ROW, cols] * o
            pooled.append(pool_group(r0, j))
        pool_rows = jnp.concatenate(pooled, axis=1)
        for j in range(ngroups):
            cols = slice(j * ncol, (j + 1) * ncol)
            o = jnp.dot(pool_rows, wout_ref[ATTN_WIDTH:, cols], preferred_element_type=F32)
            x1_ref[rows, cols] += ga_ref[LATENT_ROW, cols] * o
            if ci > 0:
                prenorm(chunk_starts[ci - 1] + j * (rc // ngroups), rc // ngroups)
    for j in range(ngroups):
        prenorm(chunk_starts[-1] + j * (rc // ngroups), rc // ngroups)


def _mixout(x, attn, u, w_out_bf16, pool_w, pool_scale, mod, norm_w):
    tm = MIX_TM
    hb = tm // POOL_HALO
    nhalo = SEQ // POOL_HALO
    row = lambda i: (i, 0)
    fixed = lambda i: (0, 0)
    return pl.pallas_call(
        _mixout_kernel,
        grid=(SEQ // tm,),
        in_specs=[
            pl.BlockSpec((tm, D_MODEL), row),
            pl.BlockSpec((tm, ATTN_WIDTH), row),
            pl.BlockSpec((POOL_HALO, POOL_WIDTH), lambda i: (jnp.maximum(i * hb - 1, 0), 0)),
            pl.BlockSpec((tm, POOL_WIDTH), row),
            pl.BlockSpec((POOL_HALO, POOL_WIDTH), lambda i: (jnp.minimum((i + 1) * hb, nhalo - 1), 0)),
            pl.BlockSpec((D_MODEL, D_MODEL), fixed),
            pl.BlockSpec((len(POOL_WINDOWS), POOL_GROUP_DIM, POOL_GROUP_DIM), lambda i: (0, 0, 0)),
            pl.BlockSpec((1, POOL_WIDTH), fixed),
            _mod_spec(0),
            pl.BlockSpec((1, D_MODEL), fixed),
            _mod_spec(1),
            _mod_spec(2),
        ],
        out_specs=[pl.BlockSpec((tm, D_MODEL), row), pl.BlockSpec((tm, D_MODEL), row)],
        out_shape=[jax.ShapeDtypeStruct((SEQ, D_MODEL), F32),
                   jax.ShapeDtypeStruct((SEQ, D_MODEL), BF16)],
        scratch_shapes=[pltpu.VMEM((tm + 2 * POOL_HALO, POOL_WIDTH), F32)],
        compiler_params=_params("arbitrary"),
        name="mixout",
    )(x, attn, u, u, u, w_out_bf16, pool_w, pool_scale, mod, norm_w, mod, mod)


def _mlp_kernel(hm_ref, x1r_ref, wup_ref, wdn_ref, gm_ref, fw_ref, o_ref):
    f = pl.program_id(1)
    last = MLP_NF - 1
    row_chunks = [slice(r0, r0 + MLP_RC) for r0 in range(0, MLP_TM, MLP_RC)]
    gm = gm_ref[LATENT_ROW]

    def branch_out(rows):
        a = jnp.maximum(jnp.dot(hm_ref[rows, :], wup_ref[...], preferred_element_type=F32), 0.0)
        return gm * jnp.dot((a * a).astype(BF16), wdn_ref[...], preferred_element_type=F32)

    def final_norm(rows):
        x2 = o_ref[rows, :]
        y = x2 * lax.rsqrt(jnp.mean(x2 * x2, axis=-1, keepdims=True) + EPS)
        o_ref[rows, :] = y * fw_ref[...]

    @pl.when(f == 0)
    def _():
        for rows in row_chunks:
            o_ref[rows, :] = branch_out(rows)
        o_ref[0:MLP_XR, :] += x1r_ref[...]

    @pl.when(jnp.logical_and(f > 0, f < last))
    def _():
        x_rows = pl.ds(pl.multiple_of(f * MLP_XR, MLP_XR), MLP_XR)
        o_ref[x_rows, :] += x1r_ref[...]
        for rows in row_chunks:
            o_ref[rows, :] += branch_out(rows)

    @pl.when(f == last)
    def _():
        o_ref[last * MLP_XR:(last + 1) * MLP_XR, :] += x1r_ref[...]
        npiece = 4
        cw, rp = D_MODEL // npiece, MLP_RC // npiece
        for ci, rows in enumerate(row_chunks):
            a = jnp.maximum(jnp.dot(hm_ref[rows, :], wup_ref[...], preferred_element_type=F32), 0.0)
            a = (a * a).astype(BF16)
            for j in range(npiece):
                cols = slice(j * cw, (j + 1) * cw)
                o_ref[rows, cols] += gm[:, cols] * jnp.dot(a, wdn_ref[:, cols],
                                                           preferred_element_type=F32)
                if ci > 0:
                    r0 = row_chunks[ci - 1].start + j * rp
                    final_norm(slice(r0, r0 + rp))
        final_norm(row_chunks[-1])


def _mlp(hm, x1, w_up_bf16, w_dn_bf16, mod, final_w):
    tm, tf = MLP_TM, MLP_TF
    return pl.pallas_call(
        _mlp_kernel,
        grid=(SEQ // tm, MLP_NF),
        in_specs=[
            pl.BlockSpec((tm, D_MODEL), lambda i, f: (i, 0)),
            pl.BlockSpec((MLP_XR, D_MODEL), lambda i, f: (i * MLP_NF + f, 0)),
            pl.BlockSpec((None, D_MODEL, tf), lambda i, f: (f, 0, 0)),
            pl.BlockSpec((tf, D_MODEL), lambda i, f: (f, 0)),
            _mod_spec(3),
            pl.BlockSpec((1, D_MODEL), lambda i, f: (0, 0)),
        ],
        out_specs=pl.BlockSpec((tm, D_MODEL), lambda i, f: (i, 0)),
        out_shape=jax.ShapeDtypeStruct((SEQ, D_MODEL), F32),
        compiler_params=_params("arbitrary", "arbitrary"),
        name="mlp",
    )(hm, x1, w_up_bf16, w_dn_bf16, mod, final_w)


def kernel(x, c, ctx, c_ctx, norm_attn_w, norm_mlp_w, w_ada, b_ada, w_in, attn_sink,
           pool_w, pool_scale, w_out, w_mlp_up, w_mlp_down, final_norm_w):
    assert x.shape == (1, SEQ, D_MODEL) and ctx.shape == (1, CTX_LEN, D_MODEL)
    assert w_ada.shape[0] == 1, "single layer"
    x2d, ctx2d = x[0], ctx[0]

    c_ctx_row = c_ctx[None, :]
    b_row = b_ada[0][None, :]
    n_early = 2 * D_MODEL
    mod_a = _ada(c, c_ctx_row, w_ada[0], b_row, n_early)

    nw_a = norm_attn_w[0][None, :]
    tables = tuple(jnp.asarray(t) for t in _rope_tables())
    qt, k, vt, u = _inproj(x2d, nw_a, mod_a, w_in[0], tables)
    kc, vtc = _ctx_kv(ctx2d, nw_a, mod_a, w_in[0])

    sink_row = jnp.repeat(attn_sink[0].astype(F32), BLOCK)[None, :]
    attn, w_out_b, w_up_b, w_dn_b, mod_b = _attn(
        qt, k, vt, kc, vtc, sink_row, w_out[0], w_mlp_up[0], w_mlp_down[0],
        c, c_ctx_row, w_ada[0], b_row, n_early)

    x1, hm = _mixout(x2d, attn, u, w_out_b, pool_w[0], pool_scale[0][None, :], mod_b,
                     norm_mlp_w[0][None, :])
    out = _mlp(hm, x1, w_up_b, w_dn_b, mod_b, final_norm_w[None, :])
    return out[None]
```

```python
import numpy as np
import jax
import jax.numpy as jnp
from jax import lax
from jax.experimental import pallas as pl
from jax.experimental.pallas import tpu as pltpu

F32 = jnp.float32
BF16 = jnp.bfloat16

D_MODEL = 2048
SEQ = 8192
CTX_LEN = 256
GRID_W = 64
HEAD_DIM = 64
N_Q_HEADS = 16
N_KV_HEADS = 4
GQA = N_Q_HEADS // N_KV_HEADS
ATTN_WIDTH = N_Q_HEADS * HEAD_DIM
KV_WIDTH = N_KV_HEADS * HEAD_DIM
POOL_WINDOWS = (2, 4, 8, 16)
POOL_WIDTH = D_MODEL - ATTN_WIDTH
POOL_GROUP_DIM = POOL_WIDTH // len(POOL_WINDOWS)
IN_WIDTH = ATTN_WIDTH + 2 * KV_WIDTH + POOL_WIDTH
D_FF = 4 * D_MODEL
BLOCK = 128
ROPE_BASE = 10000.0
N_MOD = 6
EPS = 1e-6
NEG_INF = -1e30
LOG2E = 1.4426950408889634
Q_SCALE = HEAD_DIM ** -0.5 * LOG2E

LANES = 128
SUBLANES = 8
POOL_HALO = 8
LATENT_ROW = slice(0, 1)
CONTEXT_ROW = slice(1, 2)
VMEM_LIMIT = 56 * 1024 * 1024

ADA_TN = 512
INPROJ_TM = 512
ATTN_TQ = 512
CAST_JOB_ELEMS = 64 * 1024
ATTN_STAGE_LAG = (2, 4)
MIX_TM = 512
MIX_CHUNK = 256
MLP_TM = 1024
MLP_TF = 1024
MLP_RC = 512
MLP_NF = D_FF // MLP_TF
MLP_XR = MLP_TM // MLP_NF


def _params(*sem):
    return pltpu.CompilerParams(dimension_semantics=sem, vmem_limit_bytes=VMEM_LIMIT)


def _mod_spec(chunk):
    return pl.BlockSpec((SUBLANES, D_MODEL), lambda *_: (0, chunk))


def _rms_modulate(x, norm_w, shift, scale):
    gain = norm_w * (1.0 + scale)
    y = x * lax.rsqrt(jnp.mean(x * x, axis=-1, keepdims=True) + EPS)
    return y * gain + shift


def _silu_rows(c_ref, cctx_ref):
    cnd = jnp.concatenate([c_ref[...], cctx_ref[...], jnp.zeros((SUBLANES - 2, D_MODEL), F32)], axis=0)
    return cnd * (1.0 / (1.0 + jnp.exp(-cnd)))


def _ada_cols(s, w_ref, b_ref, cols=slice(None)):
    return jnp.dot(s, w_ref[:, cols], preferred_element_type=F32) + b_ref[:, cols]


def _ada_kernel(c_ref, cctx_ref, w_ref, b_ref, o_ref):
    o_ref[...] = _ada_cols(_silu_rows(c_ref, cctx_ref), w_ref, b_ref)


def _ada(c, c_ctx, w_ada, b_ada, n):
    return pl.pallas_call(
        _ada_kernel,
        grid=(n // ADA_TN,),
        in_specs=[
            pl.BlockSpec((1, D_MODEL), lambda j: (0, 0)),
            pl.BlockSpec((1, D_MODEL), lambda j: (0, 0)),
            pl.BlockSpec((D_MODEL, ADA_TN), lambda j: (0, j)),
            pl.BlockSpec((1, ADA_TN), lambda j: (0, j)),
        ],
        out_specs=pl.BlockSpec((SUBLANES, ADA_TN), lambda j: (0, j)),
        out_shape=jax.ShapeDtypeStruct((SUBLANES, n), F32),
        compiler_params=_params("arbitrary"),
        name="ada",
    )(c, c_ctx, w_ada, b_ada)


def _rope_tables():
    half = HEAD_DIM // 2
    inv_freq = ROPE_BASE ** (-np.arange(0, half, 2, dtype=np.float64) / half)
    t = np.arange(SEQ)
    row = (t // GRID_W).astype(np.float64)
    col = (t % GRID_W).astype(np.float64)
    ang_r = row[:, None] * inv_freq[None, :]
    ang_c = col[:, None] * inv_freq[None, :]
    cos_t = Q_SCALE * np.concatenate([np.cos(ang_r), np.cos(ang_c)], axis=1).T
    sin_t = Q_SCALE * np.concatenate([np.sin(ang_r), np.sin(ang_c)], axis=1).T
    cos_l = np.concatenate([np.cos(ang_r), np.cos(ang_r), np.cos(ang_c), np.cos(ang_c)], axis=1)
    zeros = np.zeros_like(ang_r)
    s_up = np.concatenate([-np.sin(ang_r), zeros, -np.sin(ang_c), zeros], axis=1)
    s_dn = np.concatenate([zeros, np.sin(ang_r), zeros, np.sin(ang_c)], axis=1)
    tile2 = lambda a: np.concatenate([a, a], axis=1)
    q_table = np.concatenate([cos_t, sin_t], axis=0).astype(np.float32)
    k_table = np.concatenate([tile2(cos_l), tile2(s_up), tile2(s_dn)], axis=1).astype(np.float32)
    return q_table, k_table


def _inproj_kernel(x_ref, nw_ref, sh_ref, sc_ref, w_ref, qtab_ref, ktab_ref,
                   qt_ref, k_ref, vt_ref, u_ref, wb_ref):
    @pl.when(pl.program_id(0) == 0)
    def _():
        for c0 in range(0, IN_WIDTH, 2 * LANES):
            wb_ref[:, c0:c0 + 2 * LANES] = w_ref[:, c0:c0 + 2 * LANES].astype(BF16)

    h = _rms_modulate(x_ref[...], nw_ref[...], sh_ref[LATENT_ROW], sc_ref[LATENT_ROW]).astype(BF16)
    p = jnp.dot(h, wb_ref[...], preferred_element_type=F32)

    qt = p[:, :ATTN_WIDTH].T
    cr, cc = qtab_ref[0:16, :], qtab_ref[16:32, :]
    sr, sc = qtab_ref[32:48, :], qtab_ref[48:64, :]
    pieces = []
    for hd in range(N_Q_HEADS):
        b = hd * HEAD_DIM
        x1r, x2r = qt[b:b + 16, :], qt[b + 16:b + 32, :]
        x1c, x2c = qt[b + 32:b + 48, :], qt[b + 48:b + 64, :]
        pieces += [x1r * cr - x2r * sr, x1r * sr + x2r * cr,
                   x1c * cc - x2c * sc, x1c * sc + x2c * cc]
    qt_ref[...] = jnp.concatenate(pieces, axis=0).astype(BF16)

    k = p[:, ATTN_WIDTH:ATTN_WIDTH + KV_WIDTH]
    cos_l, s_up, s_dn = (ktab_ref[:, j * LANES:(j + 1) * LANES] for j in range(3))
    kparts = []
    for g in range(KV_WIDTH // LANES):
        kg = k[:, g * LANES:(g + 1) * LANES]
        kparts.append(kg * cos_l + pltpu.roll(kg, LANES - 16, 1) * s_up
                      + pltpu.roll(kg, 16, 1) * s_dn)
    k_ref[...] = jnp.concatenate(kparts, axis=1).astype(BF16)

    vt_ref[...] = p[:, ATTN_WIDTH + KV_WIDTH:ATTN_WIDTH + 2 * KV_WIDTH].T.astype(BF16)
    u_ref[...] = p[:, ATTN_WIDTH + 2 * KV_WIDTH:]


def _inproj(x, norm_w, mod, w_in, tables):
    q_table, k_table = tables
    tm = INPROJ_TM
    row = lambda i: (i, 0)
    colb = lambda i: (0, i)
    fixed = lambda i: (0, 0)
    return pl.pallas_call(
        _inproj_kernel,
        grid=(SEQ // tm,),
        in_specs=[
            pl.BlockSpec((tm, D_MODEL), row),
            pl.BlockSpec((1, D_MODEL), fixed),
            _mod_spec(0),
            _mod_spec(1),
            pl.BlockSpec((D_MODEL, IN_WIDTH), fixed, pipeline_mode=pl.Buffered(1)),
            pl.BlockSpec((q_table.shape[0], tm), colb),
            pl.BlockSpec((tm, k_table.shape[1]), row),
        ],
        out_specs=[
            pl.BlockSpec((ATTN_WIDTH, tm), colb),
            pl.BlockSpec((tm, KV_WIDTH), row),
            pl.BlockSpec((KV_WIDTH, tm), colb),
            pl.BlockSpec((tm, POOL_WIDTH), row),
        ],
        out_shape=[
            jax.ShapeDtypeStruct((ATTN_WIDTH, SEQ), BF16),
            jax.ShapeDtypeStruct((SEQ, KV_WIDTH), BF16),
            jax.ShapeDtypeStruct((KV_WIDTH, SEQ), BF16),
            jax.ShapeDtypeStruct((SEQ, POOL_WIDTH), F32),
        ],
        scratch_shapes=[pltpu.VMEM((D_MODEL, IN_WIDTH), BF16)],
        compiler_params=_params("arbitrary"),
        name="inproj",
    )(x, norm_w, mod, mod, w_in, q_table, k_table)


def _ctx_kv_kernel(x_ref, nw_ref, sh_ref, sc_ref, w_ref, k_ref, vt_ref):
    h = _rms_modulate(x_ref[...], nw_ref[...], sh_ref[CONTEXT_ROW], sc_ref[CONTEXT_ROW]).astype(BF16)
    p = jnp.dot(h, w_ref[...].astype(BF16), preferred_element_type=F32)
    k_ref[...] = p[:, :KV_WIDTH].astype(BF16)
    vt_ref[...] = p[:, KV_WIDTH:].T.astype(BF16)


def _ctx_kv(ctx, norm_w, mod, w_in):
    fixed = lambda i: (0, 0)
    kv_block = ATTN_WIDTH // (2 * KV_WIDTH)
    return pl.pallas_call(
        _ctx_kv_kernel,
        grid=(1,),
        in_specs=[
            pl.BlockSpec((CTX_LEN, D_MODEL), fixed),
            pl.BlockSpec((1, D_MODEL), fixed),
            _mod_spec(0),
            _mod_spec(1),
            pl.BlockSpec((D_MODEL, 2 * KV_WIDTH), lambda i: (0, kv_block)),
        ],
        out_specs=[
            pl.BlockSpec((CTX_LEN, KV_WIDTH), fixed),
            pl.BlockSpec((KV_WIDTH, CTX_LEN), fixed),
        ],
        out_shape=[
            jax.ShapeDtypeStruct((CTX_LEN, KV_WIDTH), BF16),
            jax.ShapeDtypeStruct((KV_WIDTH, CTX_LEN), BF16),
        ],
        compiler_params=_params("arbitrary"),
        name="ctx_kv",
    )(ctx, norm_w, mod, mod, w_in)


def _attn_kernel(qt_ref, kp_ref, km_ref, kn_ref, vtp_ref, vtm_ref, vtn_ref,
                 kc_ref, vtc_ref, sink_ref, wout_ref, wup_ref, wdn_ref,
                 c_ref, cctx_ref, wada_ref, bada_ref,
                 o_ref, wout_b_ref, wup_b_ref, wdn_b_ref, mod_ref):
    cond_act = _silu_rows(c_ref, cctx_ref)
    ada_cols = [slice(c0, c0 + 2 * LANES) for c0 in range(0, mod_ref.shape[1], 2 * LANES)]

    def ada_piece(idx, n):
        for cols in ada_cols[idx * len(ada_cols) // n:(idx + 1) * len(ada_cols) // n]:
            mod_ref[:, cols] = _ada_cols(cond_act, wada_ref, bada_ref, cols)

    i = pl.program_id(0)
    nsub = ATTN_TQ // BLOCK
    nblk = SEQ // BLOCK
    pw = 2 * BLOCK
    nkeys = 3 * BLOCK + CTX_LEN

    key_j = lax.broadcasted_iota(jnp.int32, (BLOCK, pw), 0)
    qry_i = lax.broadcasted_iota(jnp.int32, (BLOCK, pw), 1) % BLOCK
    neg = jnp.full((BLOCK, pw), NEG_INF, F32)
    zero = jnp.zeros((BLOCK, pw), F32)
    band_prev = jnp.where(key_j >= qry_i, zero, neg)
    band_next = jnp.where(key_j <= qry_i, zero, neg)

    kc = kc_ref[...]
    vtc = vtc_ref[...]
    ones_rows = jnp.ones((2 * SUBLANES, nkeys), BF16)
    sink_row = sink_ref[...] * LOG2E

    windows = []
    for sb in range(nsub):
        n = i * nsub + sb
        lo, hi = sb * BLOCK, (sb + 1) * BLOCK
        k_prev = kp_ref[...] if sb == 0 else km_ref[lo - BLOCK:lo, :]
        k_next = kn_ref[...] if sb == nsub - 1 else km_ref[hi:hi + BLOCK, :]
        vt_prev = vtp_ref[...] if sb == 0 else vtm_ref[:, lo - BLOCK:lo]
        vt_next = vtn_ref[...] if sb == nsub - 1 else vtm_ref[:, hi:hi + BLOCK]
        windows.append(dict(
            kwin=jnp.concatenate([k_prev, km_ref[lo:hi, :], k_next, kc], axis=0),
            vtwin=jnp.concatenate([vt_prev, vtm_ref[:, lo:hi], vt_next, vtc], axis=1),
            bias_prev=band_prev + jnp.where(n > 0, 0.0, NEG_INF),
            bias_next=band_next + jnp.where(n < nblk - 1, 0.0, NEG_INF)))

    units = [(sb, h, half) for sb in range(nsub) for h in range(N_KV_HEADS)
             for half in range(GQA // 2)]

    def scores(unit):
        sb, h, half = unit
        win = windows[sb]
        lo, hi = sb * BLOCK, (sb + 1) * BLOCK
        hd0 = GQA * h + 2 * half
        qh = jnp.concatenate([qt_ref[(hd0 + g) * HEAD_DIM:(hd0 + g + 1) * HEAD_DIM, lo:hi]
                              for g in range(2)], axis=1)
        blocks = []
        if h > 0:
            blocks.append(jnp.zeros((h * HEAD_DIM, pw), BF16))
        blocks.append(qh)
        if h < N_KV_HEADS - 1:
            blocks.append(jnp.zeros(((N_KV_HEADS - 1 - h) * HEAD_DIM, pw), BF16))
        s = jnp.dot(win["kwin"], jnp.concatenate(blocks, axis=0),
                    preferred_element_type=F32)
        s = jnp.concatenate([s[0:BLOCK] + win["bias_prev"], s[BLOCK:2 * BLOCK],
                             s[2 * BLOCK:3 * BLOCK] + win["bias_next"], s[3 * BLOCK:]], axis=0)
        return s, jnp.max(s, axis=0, keepdims=True)

    def probs(unit, s, smax):
        sb, h, half = unit
        hd0 = GQA * h + 2 * half
        sink = sink_row[:, hd0 * BLOCK:(hd0 + 2) * BLOCK]
        m = jnp.maximum(smax, sink)
        return jnp.exp2(s - m).astype(BF16), jnp.exp2(sink - m)

    def finish(unit, e, e_sink):
        sb, h, half = unit
        lo, hi = sb * BLOCK, (sb + 1) * BLOCK
        hd0 = GQA * h + 2 * half
        vt_ones = jnp.concatenate(
            [windows[sb]["vtwin"][h * HEAD_DIM:(h + 1) * HEAD_DIM, :], ones_rows], axis=0)
        o = jnp.dot(vt_ones, e, preferred_element_type=F32)
        denom = o[HEAD_DIM:HEAD_DIM + 1, :] + e_sink
        o = o[:HEAD_DIM, :] * (1.0 / denom)
        blk = jnp.concatenate([o[:, 0:BLOCK], o[:, BLOCK:2 * BLOCK]], axis=0)
        o_ref[lo:hi, hd0 * HEAD_DIM:(hd0 + 2) * HEAD_DIM] = blk.T.astype(BF16)

    cast_jobs = []
    for src, dst in ((wout_ref, wout_b_ref), (wup_ref, wup_b_ref), (wdn_ref, wdn_b_ref)):
        wc = max(LANES, CAST_JOB_ELEMS // src.shape[0])
        cast_jobs += [(src, dst, c0) + (wc,) for c0 in range(0, src.shape[1], wc)]

    def cast_piece(idx, n):
        for src, dst, c0, wc in cast_jobs[idx * len(cast_jobs) // n:(idx + 1) * len(cast_jobs) // n]:
            val = src[:, c0:c0 + wc].astype(BF16)
            if len(dst.shape) == 3:
                tf = dst.shape[2]
                dst[c0 // tf, :, c0 % tf:c0 % tf + wc] = val
            else:
                dst[:, c0:c0 + wc] = val

    nu = len(units)
    s_vals, p_vals = {}, {}
    d_probs, d_finish = ATTN_STAGE_LAG
    for step in range(nu + d_finish):
        if step < nu:
            s_vals[step] = scores(units[step])
        if d_probs <= step < nu + d_probs:
            u = step - d_probs
            p_vals[u] = probs(units[u], *s_vals.pop(u))
            cast_piece(u, nu)
            ada_piece(u, nu)
        if step >= d_finish:
            u = step - d_finish
            finish(units[u], *p_vals.pop(u))


def _attn(qt, k, vt, kc, vtc, sink, w_out, w_up, w_dn, c, c_ctx, w_ada, b_ada, ada_col0):
    tq = ATTN_TQ
    r = tq // BLOCK
    nblk = SEQ // BLOCK
    nsteps = SEQ // tq
    ada_tn = (w_ada.shape[1] - ada_col0) // nsteps
    ada_b0 = ada_col0 // ada_tn
    fixed = lambda i: (0, 0)
    row = lambda i: (i, 0)
    slab = lambda w: pl.BlockSpec((w.shape[0] // nsteps, w.shape[1]), row)
    bf16_like = lambda w: jax.ShapeDtypeStruct(w.shape, BF16)
    prev = lambda i: jnp.maximum(i * r - 1, 0)
    nxt = lambda i: jnp.minimum((i + 1) * r, nblk - 1)
    return pl.pallas_call(
        _attn_kernel,
        grid=(SEQ // tq,),
        in_specs=[
            pl.BlockSpec((ATTN_WIDTH, tq), lambda i: (0, i)),
            pl.BlockSpec((BLOCK, KV_WIDTH), lambda i: (prev(i), 0)),
            pl.BlockSpec((tq, KV_WIDTH), lambda i: (i, 0)),
            pl.BlockSpec((BLOCK, KV_WIDTH), lambda i: (nxt(i), 0)),
            pl.BlockSpec((KV_WIDTH, BLOCK), lambda i: (0, prev(i))),
            pl.BlockSpec((KV_WIDTH, tq), lambda i: (0, i)),
            pl.BlockSpec((KV_WIDTH, BLOCK), lambda i: (0, nxt(i))),
            pl.BlockSpec((CTX_LEN, KV_WIDTH), fixed),
            pl.BlockSpec((KV_WIDTH, CTX_LEN), fixed),
            pl.BlockSpec((1, N_Q_HEADS * BLOCK), fixed),
            slab(w_out), slab(w_up), slab(w_dn),
            pl.BlockSpec((1, D_MODEL), fixed),
            pl.BlockSpec((1, D_MODEL), fixed),
            pl.BlockSpec((D_MODEL, ada_tn), lambda i: (0, ada_b0 + i)),
            pl.BlockSpec((1, ada_tn), lambda i: (0, ada_b0 + i)),
        ],
        out_specs=[pl.BlockSpec((tq, ATTN_WIDTH), row), slab(w_out),
                   pl.BlockSpec((MLP_NF, w_up.shape[0] // nsteps, MLP_TF), lambda i: (0, i, 0)),
                   slab(w_dn),
                   pl.BlockSpec((SUBLANES, ada_tn), lambda i: (0, i))],
        out_shape=[jax.ShapeDtypeStruct((SEQ, ATTN_WIDTH), BF16),
                   bf16_like(w_out),
                   jax.ShapeDtypeStruct((MLP_NF, w_up.shape[0], MLP_TF), BF16),
                   bf16_like(w_dn),
                   jax.ShapeDtypeStruct((SUBLANES, w_ada.shape[1] - ada_col0), F32)],
        compiler_params=_params("arbitrary"),
        name="attn",
    )(qt, k, k, k, vt, vt, vt, kc, vtc, sink, w_out, w_up, w_dn, c, c_ctx, w_ada, b_ada)


def _mixout_kernel(x_ref, attn_ref, up_ref, um_ref, un_ref, wout_ref, pw_ref, ps_ref,
                   ga_ref, nw_ref, sh_ref, sc_ref, x1_ref, hm_ref, ubuf):
    i = pl.program_id(0)
    tm = MIX_TM
    nsteps = SEQ // tm
    ubuf[0:POOL_HALO, :] = jnp.where(i > 0, up_ref[...], 0.0)
    ubuf[POOL_HALO:POOL_HALO + tm, :] = um_ref[...]
    ubuf[POOL_HALO + tm:2 * POOL_HALO + tm, :] = jnp.where(i < nsteps - 1, un_ref[...], 0.0)

    rc = MIX_CHUNK
    n = rc + 2 * POOL_HALO
    c = POOL_HALO
    ngroups = len(POOL_WINDOWS)
    ncol = D_MODEL // ngroups

    def pool_group(r0, g):
        w = POOL_WINDOWS[g]
        c0, c1 = g * POOL_GROUP_DIM, (g + 1) * POOL_GROUP_DIM
        e = ubuf[r0:r0 + n, c0:c1]
        if w == 2:
            win = e[c - 1:c - 1 + rc] + e[c:c + rc]
        else:
            f2 = e[0:n - 1] + e[1:n]
            if w == 4:
                win = f2[c - 2:c - 2 + rc] + f2[c:c + rc]
            else:
                f4 = f2[0:n - 3] + f2[2:n - 1]
                if w == 8:
                    win = f4[c - 4:c - 4 + rc] + f4[c:c + rc]
                else:
                    f8 = f4[0:n - 7] + f4[4:n - 3]
                    win = f8[c - 8:c - 8 + rc] + f8[c:c + rc]
        t = i * tm + r0 + lax.broadcasted_iota(jnp.int32, (rc, 1), 0)
        lo = jnp.clip(t - w // 2, 0, SEQ)
        hi = jnp.clip(t - w // 2 + w, 0, SEQ)
        cnt = (hi - lo).astype(F32)
        pg = win / cnt - um_ref[r0:r0 + rc, c0:c1]
        mixed = jnp.dot(pg.astype(BF16), pw_ref[g].astype(BF16), preferred_element_type=F32)
        return (mixed * ps_ref[:, c0:c1]).astype(BF16)

    def prenorm(r0, nrows):
        rows = slice(r0, r0 + nrows)
        hm_ref[rows, :] = _rms_modulate(x1_ref[rows, :], nw_ref[...], sh_ref[LATENT_ROW],
                                        sc_ref[LATENT_ROW]).astype(BF16)

    chunk_starts = list(range(0, tm, rc))
    for ci, r0 in enumerate(chunk_starts):
        rows = slice(r0, r0 + rc)
        attn_rows = attn_ref[rows, :]
        pooled = []
        for j in range(ngroups):
            cols = slice(j * ncol, (j + 1) * ncol)
            o = jnp.dot(attn_rows, wout_ref[0:ATTN_WIDTH, cols], preferred_element_type=F32)
            x1_ref[rows, cols] = x_ref[rows, cols] + ga_ref[LATENT_ROW, cols] * o
            pooled.append(pool_group(r0, j))
        pool_rows = jnp.concatenate(pooled, axis=1)
        for j in range(ngroups):
            cols = slice(j * ncol, (j + 1) * ncol)
            o = jnp.dot(pool_rows, wout_ref[ATTN_WIDTH:, cols], preferred_element_type=F32)
            x1_ref[rows, cols] += ga_ref[LATENT_ROW, cols] * o
            if ci > 0:
                prenorm(chunk_starts[ci - 1] + j * (rc // ngroups), rc // ngroups)
    for j in range(ngroups):
        prenorm(chunk_starts[-1] + j * (rc // ngroups), rc // ngroups)


def _mixout(x, attn, u, w_out_bf16, pool_w, pool_scale, mod, norm_w):
    tm = MIX_TM
    hb = tm // POOL_HALO
    nhalo = SEQ // POOL_HALO
    row = lambda i: (i, 0)
    fixed = lambda i: (0, 0)
    return pl.pallas_call(
        _mixout_kernel,
        grid=(SEQ // tm,),
        in_specs=[
            pl.BlockSpec((tm, D_MODEL), row),
            pl.BlockSpec((tm, ATTN_WIDTH), row),
            pl.BlockSpec((POOL_HALO, POOL_WIDTH), lambda i: (jnp.maximum(i * hb - 1, 0), 0)),
            pl.BlockSpec((tm, POOL_WIDTH), row),
            pl.BlockSpec((POOL_HALO, POOL_WIDTH), lambda i: (jnp.minimum((i + 1) * hb, nhalo - 1), 0)),
            pl.BlockSpec((D_MODEL, D_MODEL), fixed),
            pl.BlockSpec((len(POOL_WINDOWS), POOL_GROUP_DIM, POOL_GROUP_DIM), lambda i: (0, 0, 0)),
            pl.BlockSpec((1, POOL_WIDTH), fixed),
            _mod_spec(0),
            pl.BlockSpec((1, D_MODEL), fixed),
            _mod_spec(1),
            _mod_spec(2),
        ],
        out_specs=[pl.BlockSpec((tm, D_MODEL), row), pl.BlockSpec((tm, D_MODEL), row)],
        out_shape=[jax.ShapeDtypeStruct((SEQ, D_MODEL), F32),
                   jax.ShapeDtypeStruct((SEQ, D_MODEL), BF16)],
        scratch_shapes=[pltpu.VMEM((tm + 2 * POOL_HALO, POOL_WIDTH), F32)],
        compiler_params=_params("arbitrary"),
        name="mixout",
    )(x, attn, u, u, u, w_out_bf16, pool_w, pool_scale, mod, norm_w, mod, mod)


def _mlp_kernel(hm_ref, x1r_ref, wup_ref, wdn_ref, gm_ref, fw_ref, o_ref):
    f = pl.program_id(1)
    last = MLP_NF - 1
    row_chunks = [slice(r0, r0 + MLP_RC) for r0 in range(0, MLP_TM, MLP_RC)]
    gm = gm_ref[LATENT_ROW]

    def branch_out(rows):
        a = jnp.maximum(jnp.dot(hm_ref[rows, :], wup_ref[...], preferred_element_type=F32), 0.0)
        return gm * jnp.dot((a * a).astype(BF16), wdn_ref[...], preferred_element_type=F32)

    def final_norm(rows):
        x2 = o_ref[rows, :]
        y = x2 * lax.rsqrt(jnp.mean(x2 * x2, axis=-1, keepdims=True) + EPS)
        o_ref[rows, :] = y * fw_ref[...]

    @pl.when(f == 0)
    def _():
        for rows in row_chunks:
            o_ref[rows, :] = branch_out(rows)
        o_ref[0:MLP_XR, :] += x1r_ref[...]

    @pl.when(jnp.logical_and(f > 0, f < last))
    def _():
        x_rows = pl.ds(pl.multiple_of(f * MLP_XR, MLP_XR), MLP_XR)
        o_ref[x_rows, :] += x1r_ref[...]
        for rows in row_chunks:
            o_ref[rows, :] += branch_out(rows)

    @pl.when(f == last)
    def _():
        o_ref[last * MLP_XR:(last + 1) * MLP_XR, :] += x1r_ref[...]
        npiece = 4
        cw, rp = D_MODEL // npiece, MLP_RC // npiece
        for ci, rows in enumerate(row_chunks):
            a = jnp.maximum(jnp.dot(hm_ref[rows, :], wup_ref[...], preferred_element_type=F32), 0.0)
            a = (a * a).astype(BF16)
            for j in range(npiece):
                cols = slice(j * cw, (j + 1) * cw)
                o_ref[rows, cols] += gm[:, cols] * jnp.dot(a, wdn_ref[:, cols],
                                                           preferred_element_type=F32)
                if ci > 0:
                    r0 = row_chunks[ci - 1].start + j * rp
                    final_norm(slice(r0, r0 + rp))
        final_norm(row_chunks[-1])


def _mlp(hm, x1, w_up_bf16, w_dn_bf16, mod, final_w):
    tm, tf = MLP_TM, MLP_TF
    return pl.pallas_call(
        _mlp_kernel,
        grid=(SEQ // tm, MLP_NF),
        in_specs=[
            pl.BlockSpec((tm, D_MODEL), lambda i, f: (i, 0)),
            pl.BlockSpec((MLP_XR, D_MODEL), lambda i, f: (i * MLP_NF + f, 0)),
            pl.BlockSpec((None, D_MODEL, tf), lambda i, f: (f, 0, 0)),
            pl.BlockSpec((tf, D_MODEL), lambda i, f: (f, 0)),
            _mod_spec(3),
            pl.BlockSpec((1, D_MODEL), lambda i, f: (0, 0)),
        ],
        out_specs=pl.BlockSpec((tm, D_MODEL), lambda i, f: (i, 0)),
        out_shape=jax.ShapeDtypeStruct((SEQ, D_MODEL), F32),
        compiler_params=_params("arbitrary", "arbitrary"),
        name="mlp",
    )(hm, x1, w_up_bf16, w_dn_bf16, mod, final_w)


def kernel(x, c, ctx, c_ctx, norm_attn_w, norm_mlp_w, w_ada, b_ada, w_in, attn_sink,
           pool_w, pool_scale, w_out, w_mlp_up, w_mlp_down, final_norm_w):
    assert x.shape == (1, SEQ, D_MODEL) and ctx.shape == (1, CTX_LEN, D_MODEL)
    assert w_ada.shape[0] == 1, "single layer"
    x2d, ctx2d = x[0], ctx[0]

    c_ctx_row = c_ctx[None, :]
    b_row = b_ada[0][None, :]
    n_early = 2 * D_MODEL
    mod_a = _ada(c, c_ctx_row, w_ada[0], b_row, n_early)

    nw_a = norm_attn_w[0][None, :]
    tables = tuple(jnp.asarray(t) for t in _rope_tables())
    qt, k, vt, u = _inproj(x2d, nw_a, mod_a, w_in[0], tables)
    kc, vtc = _ctx_kv(ctx2d, nw_a, mod_a, w_in[0])

    sink_row = jnp.repeat(attn_sink[0].astype(F32), BLOCK)[None, :]
    attn, w_out_b, w_up_b, w_dn_b, mod_b = _attn(
        qt, k, vt, kc, vtc, sink_row, w_out[0], w_mlp_up[0], w_mlp_down[0],
        c, c_ctx_row, w_ada[0], b_row, n_early)

    x1, hm = _mixout(x2d, attn, u, w_out_b, pool_w[0], pool_scale[0][None, :], mod_b,
                     norm_mlp_w[0][None, :])
    out = _mlp(hm, x1, w_up_b, w_dn_b, mod_b, final_norm_w[None, :])
    return out[None]
```

```python
import numpy as np
import jax
import jax.numpy as jnp
from jax import lax
from jax.experimental import pallas as pl
from jax.experimental.pallas import tpu as pltpu

F32 = jnp.float32
BF16 = jnp.bfloat16

D_MODEL = 2048
SEQ = 8192
CTX_LEN = 256
GRID_W = 64
HEAD_DIM = 64
N_Q_HEADS = 16
N_KV_HEADS = 4
GQA = N_Q_HEADS // N_KV_HEADS
ATTN_WIDTH = N_Q_HEADS * HEAD_DIM
KV_WIDTH = N_KV_HEADS * HEAD_DIM
POOL_WINDOWS = (2, 4, 8, 16)
POOL_WIDTH = D_MODEL - ATTN_WIDTH
POOL_GROUP_DIM = POOL_WIDTH // len(POOL_WINDOWS)
IN_WIDTH = ATTN_WIDTH + 2 * KV_WIDTH + POOL_WIDTH
D_FF = 4 * D_MODEL
BLOCK = 128
ROPE_BASE = 10000.0
N_MOD = 6
EPS = 1e-6
NEG_INF = -1e30
LOG2E = 1.4426950408889634
Q_SCALE = HEAD_DIM ** -0.5 * LOG2E

LANES = 128
SUBLANES = 8
POOL_HALO = 8
LATENT_ROW = slice(0, 1)
CONTEXT_ROW = slice(1, 2)
VMEM_LIMIT = 56 * 1024 * 1024

ADA_TN = 512
INPROJ_TM = 512
ATTN_TQ = 512
CAST_JOB_ELEMS = 64 * 1024
ATTN_STAGE_LAG = (2, 5)
MIX_TM = 512
MIX_CHUNK = 256
MLP_TM = 1024
MLP_TF = 1024
MLP_RC = 512
MLP_NF = D_FF // MLP_TF
MLP_XR = MLP_TM // MLP_NF


def _params(*sem):
    return pltpu.CompilerParams(dimension_semantics=sem, vmem_limit_bytes=VMEM_LIMIT)


def _mod_spec(chunk):
    return pl.BlockSpec((SUBLANES, D_MODEL), lambda *_: (0, chunk))


def _rms_modulate(x, norm_w, shift, scale):
    gain = norm_w * (1.0 + scale)
    y = x * lax.rsqrt(jnp.mean(x * x, axis=-1, keepdims=True) + EPS)
    return y * gain + shift


def _silu_rows(c_ref, cctx_ref):
    cnd = jnp.concatenate([c_ref[...], cctx_ref[...], jnp.zeros((SUBLANES - 2, D_MODEL), F32)], axis=0)
    return cnd * (1.0 / (1.0 + jnp.exp(-cnd)))


def _ada_cols(s, w_ref, b_ref, cols=slice(None)):
    return jnp.dot(s, w_ref[:, cols], preferred_element_type=F32) + b_ref[:, cols]


def _ada_kernel(c_ref, cctx_ref, w_ref, b_ref, o_ref):
    o_ref[...] = _ada_cols(_silu_rows(c_ref, cctx_ref), w_ref, b_ref)


def _ada(c, c_ctx, w_ada, b_ada, n):
    return pl.pallas_call(
        _ada_kernel,
        grid=(n // ADA_TN,),
        in_specs=[
            pl.BlockSpec((1, D_MODEL), lambda j: (0, 0)),
            pl.BlockSpec((1, D_MODEL), lambda j: (0, 0)),
            pl.BlockSpec((D_MODEL, ADA_TN), lambda j: (0, j)),
            pl.BlockSpec((1, ADA_TN), lambda j: (0, j)),
        ],
        out_specs=pl.BlockSpec((SUBLANES, ADA_TN), lambda j: (0, j)),
        out_shape=jax.ShapeDtypeStruct((SUBLANES, n), F32),
        compiler_params=_params("arbitrary"),
        name="ada",
    )(c, c_ctx, w_ada, b_ada)


def _rope_tables():
    half = HEAD_DIM // 2
    inv_freq = ROPE_BASE ** (-np.arange(0, half, 2, dtype=np.float64) / half)
    t = np.arange(SEQ)
    row = (t // GRID_W).astype(np.float64)
    col = (t % GRID_W).astype(np.float64)
    ang_r = row[:, None] * inv_freq[None, :]
    ang_c = col[:, None] * inv_freq[None, :]
    cos_t = Q_SCALE * np.concatenate([np.cos(ang_r), np.cos(ang_c)], axis=1).T
    sin_t = Q_SCALE * np.concatenate([np.sin(ang_r), np.sin(ang_c)], axis=1).T
    cos_l = np.concatenate([np.cos(ang_r), np.cos(ang_r), np.cos(ang_c), np.cos(ang_c)], axis=1)
    zeros = np.zeros_like(ang_r)
    s_up = np.concatenate([-np.sin(ang_r), zeros, -np.sin(ang_c), zeros], axis=1)
    s_dn = np.concatenate([zeros, np.sin(ang_r), zeros, np.sin(ang_c)], axis=1)
    tile2 = lambda a: np.concatenate([a, a], axis=1)
    q_table = np.concatenate([cos_t, sin_t], axis=0).astype(np.float32)
    k_table = np.concatenate([tile2(cos_l), tile2(s_up), tile2(s_dn)], axis=1).astype(np.float32)
    return q_table, k_table


def _inproj_kernel(x_ref, nw_ref, sh_ref, sc_ref, w_ref, qtab_ref, ktab_ref,
                   qt_ref, k_ref, vt_ref, u_ref, wb_ref):
    @pl.when(pl.program_id(0) == 0)
    def _():
        for c0 in range(0, IN_WIDTH, 2 * LANES):
            wb_ref[:, c0:c0 + 2 * LANES] = w_ref[:, c0:c0 + 2 * LANES].astype(BF16)

    h = _rms_modulate(x_ref[...], nw_ref[...], sh_ref[LATENT_ROW], sc_ref[LATENT_ROW]).astype(BF16)
    p = jnp.dot(h, wb_ref[...], preferred_element_type=F32)

    qt = p[:, :ATTN_WIDTH].T
    cr, cc = qtab_ref[0:16, :], qtab_ref[16:32, :]
    sr, sc = qtab_ref[32:48, :], qtab_ref[48:64, :]
    pieces = []
    for hd in range(N_Q_HEADS):
        b = hd * HEAD_DIM
        x1r, x2r = qt[b:b + 16, :], qt[b + 16:b + 32, :]
        x1c, x2c = qt[b + 32:b + 48, :], qt[b + 48:b + 64, :]
        pieces += [x1r * cr - x2r * sr, x1r * sr + x2r * cr,
                   x1c * cc - x2c * sc, x1c * sc + x2c * cc]
    qt_ref[...] = jnp.concatenate(pieces, axis=0).astype(BF16)

    k = p[:, ATTN_WIDTH:ATTN_WIDTH + KV_WIDTH]
    cos_l, s_up, s_dn = (ktab_ref[:, j * LANES:(j + 1) * LANES] for j in range(3))
    kparts = []
    for g in range(KV_WIDTH // LANES):
        kg = k[:, g * LANES:(g + 1) * LANES]
        kparts.append(kg * cos_l + pltpu.roll(kg, LANES - 16, 1) * s_up
                      + pltpu.roll(kg, 16, 1) * s_dn)
    k_ref[...] = jnp.concatenate(kparts, axis=1).astype(BF16)

    vt_ref[...] = p[:, ATTN_WIDTH + KV_WIDTH:ATTN_WIDTH + 2 * KV_WIDTH].T.astype(BF16)
    u_ref[...] = p[:, ATTN_WIDTH + 2 * KV_WIDTH:]


def _inproj(x, norm_w, mod, w_in, tables):
    q_table, k_table = tables
    tm = INPROJ_TM
    row = lambda i: (i, 0)
    colb = lambda i: (0, i)
    fixed = lambda i: (0, 0)
    return pl.pallas_call(
        _inproj_kernel,
        grid=(SEQ // tm,),
        in_specs=[
            pl.BlockSpec((tm, D_MODEL), row),
            pl.BlockSpec((1, D_MODEL), fixed),
            _mod_spec(0),
            _mod_spec(1),
            pl.BlockSpec((D_MODEL, IN_WIDTH), fixed, pipeline_mode=pl.Buffered(1)),
            pl.BlockSpec((q_table.shape[0], tm), colb),
            pl.BlockSpec((tm, k_table.shape[1]), row),
        ],
        out_specs=[
            pl.BlockSpec((ATTN_WIDTH, tm), colb),
            pl.BlockSpec((tm, KV_WIDTH), row),
            pl.BlockSpec((KV_WIDTH, tm), colb),
            pl.BlockSpec((tm, POOL_WIDTH), row),
        ],
        out_shape=[
            jax.ShapeDtypeStruct((ATTN_WIDTH, SEQ), BF16),
            jax.ShapeDtypeStruct((SEQ, KV_WIDTH), BF16),
            jax.ShapeDtypeStruct((KV_WIDTH, SEQ), BF16),
            jax.ShapeDtypeStruct((SEQ, POOL_WIDTH), F32),
        ],
        scratch_shapes=[pltpu.VMEM((D_MODEL, IN_WIDTH), BF16)],
        compiler_params=_params("arbitrary"),
        name="inproj",
    )(x, norm_w, mod, mod, w_in, q_table, k_table)


def _ctx_kv_kernel(x_ref, nw_ref, sh_ref, sc_ref, w_ref, k_ref, vt_ref):
    h = _rms_modulate(x_ref[...], nw_ref[...], sh_ref[CONTEXT_ROW], sc_ref[CONTEXT_ROW]).astype(BF16)
    p = jnp.dot(h, w_ref[...].astype(BF16), preferred_element_type=F32)
    k_ref[...] = p[:, :KV_WIDTH].astype(BF16)
    vt_ref[...] = p[:, KV_WIDTH:].T.astype(BF16)


def _ctx_kv(ctx, norm_w, mod, w_in):
    fixed = lambda i: (0, 0)
    kv_block = ATTN_WIDTH // (2 * KV_WIDTH)
    return pl.pallas_call(
        _ctx_kv_kernel,
        grid=(1,),
        in_specs=[
            pl.BlockSpec((CTX_LEN, D_MODEL), fixed),
            pl.BlockSpec((1, D_MODEL), fixed),
            _mod_spec(0),
            _mod_spec(1),
            pl.BlockSpec((D_MODEL, 2 * KV_WIDTH), lambda i: (0, kv_block)),
        ],
        out_specs=[
            pl.BlockSpec((CTX_LEN, KV_WIDTH), fixed),
            pl.BlockSpec((KV_WIDTH, CTX_LEN), fixed),
        ],
        out_shape=[
            jax.ShapeDtypeStruct((CTX_LEN, KV_WIDTH), BF16),
            jax.ShapeDtypeStruct((KV_WIDTH, CTX_LEN), BF16),
        ],
        compiler_params=_params("arbitrary"),
        name="ctx_kv",
    )(ctx, norm_w, mod, mod, w_in)


def _attn_kernel(qt_ref, kp_ref, km_ref, kn_ref, vtp_ref, vtm_ref, vtn_ref,
                 kc_ref, vtc_ref, sink_ref, wout_ref, wup_ref, wdn_ref,
                 c_ref, cctx_ref, wada_ref, bada_ref,
                 o_ref, wout_b_ref, wup_b_ref, wdn_b_ref, mod_ref):
    cond_act = _silu_rows(c_ref, cctx_ref)
    ada_cols = [slice(c0, c0 + 2 * LANES) for c0 in range(0, mod_ref.shape[1], 2 * LANES)]

    def ada_piece(idx, n):
        for cols in ada_cols[idx * len(ada_cols) // n:(idx + 1) * len(ada_cols) // n]:
            mod_ref[:, cols] = _ada_cols(cond_act, wada_ref, bada_ref, cols)

    i = pl.program_id(0)
    nsub = ATTN_TQ // BLOCK
    nblk = SEQ // BLOCK
    pw = 2 * BLOCK
    nkeys = 3 * BLOCK + CTX_LEN

    key_j = lax.broadcasted_iota(jnp.int32, (BLOCK, pw), 0)
    qry_i = lax.broadcasted_iota(jnp.int32, (BLOCK, pw), 1) % BLOCK
    neg = jnp.full((BLOCK, pw), NEG_INF, F32)
    zero = jnp.zeros((BLOCK, pw), F32)
    band_prev = jnp.where(key_j >= qry_i, zero, neg)
    band_next = jnp.where(key_j <= qry_i, zero, neg)

    kc = kc_ref[...]
    vtc = vtc_ref[...]
    ones_rows = jnp.ones((2 * SUBLANES, nkeys), BF16)
    sink_row = sink_ref[...] * LOG2E

    windows = []
    for sb in range(nsub):
        n = i * nsub + sb
        lo, hi = sb * BLOCK, (sb + 1) * BLOCK
        k_prev = kp_ref[...] if sb == 0 else km_ref[lo - BLOCK:lo, :]
        k_next = kn_ref[...] if sb == nsub - 1 else km_ref[hi:hi + BLOCK, :]
        vt_prev = vtp_ref[...] if sb == 0 else vtm_ref[:, lo - BLOCK:lo]
        vt_next = vtn_ref[...] if sb == nsub - 1 else vtm_ref[:, hi:hi + BLOCK]
        windows.append(dict(
            kwin=jnp.concatenate([k_prev, km_ref[lo:hi, :], k_next, kc], axis=0),
            vtwin=jnp.concatenate([vt_prev, vtm_ref[:, lo:hi], vt_next, vtc], axis=1),
            bias_prev=band_prev + jnp.where(n > 0, 0.0, NEG_INF),
            bias_next=band_next + jnp.where(n < nblk - 1, 0.0, NEG_INF)))

    units = [(sb, h, half) for sb in range(nsub) for h in range(N_KV_HEADS)
             for half in range(GQA // 2)]

    def scores(unit):
        sb, h, half = unit
        win = windows[sb]
        lo, hi = sb * BLOCK, (sb + 1) * BLOCK
        hd0 = GQA * h + 2 * half
        qh = jnp.concatenate([qt_ref[(hd0 + g) * HEAD_DIM:(hd0 + g + 1) * HEAD_DIM, lo:hi]
                              for g in range(2)], axis=1)
        blocks = []
        if h > 0:
            blocks.append(jnp.zeros((h * HEAD_DIM, pw), BF16))
        blocks.append(qh)
        if h < N_KV_HEADS - 1:
            blocks.append(jnp.zeros(((N_KV_HEADS - 1 - h) * HEAD_DIM, pw), BF16))
        s = jnp.dot(win["kwin"], jnp.concatenate(blocks, axis=0),
                    preferred_element_type=F32)
        s = jnp.concatenate([s[0:BLOCK] + win["bias_prev"], s[BLOCK:2 * BLOCK],
                             s[2 * BLOCK:3 * BLOCK] + win["bias_next"], s[3 * BLOCK:]], axis=0)
        return s, jnp.max(s, axis=0, keepdims=True)

    def probs(unit, s, smax):
        sb, h, half = unit
        hd0 = GQA * h + 2 * half
        sink = sink_row[:, hd0 * BLOCK:(hd0 + 2) * BLOCK]
        m = jnp.maximum(smax, sink)
        return jnp.exp2(s - m).astype(BF16), jnp.exp2(sink - m)

    def finish(unit, e, e_sink):
        sb, h, half = unit
        lo, hi = sb * BLOCK, (sb + 1) * BLOCK
        hd0 = GQA * h + 2 * half
        vt_ones = jnp.concatenate(
            [windows[sb]["vtwin"][h * HEAD_DIM:(h + 1) * HEAD_DIM, :], ones_rows], axis=0)
        o = jnp.dot(vt_ones, e, preferred_element_type=F32)
        denom = o[HEAD_DIM:HEAD_DIM + 1, :] + e_sink
        o = o[:HEAD_DIM, :] * (1.0 / denom)
        blk = jnp.concatenate([o[:, 0:BLOCK], o[:, BLOCK:2 * BLOCK]], axis=0)
        o_ref[lo:hi, hd0 * HEAD_DIM:(hd0 + 2) * HEAD_DIM] = blk.T.astype(BF16)

    cast_jobs = []
    for src, dst in ((wout_ref, wout_b_ref), (wup_ref, wup_b_ref), (wdn_ref, wdn_b_ref)):
        wc = max(LANES, CAST_JOB_ELEMS // src.shape[0])
        cast_jobs += [(src, dst, c0) + (wc,) for c0 in range(0, src.shape[1], wc)]

    def cast_piece(idx, n):
        for src, dst, c0, wc in cast_jobs[idx * len(cast_jobs) // n:(idx + 1) * len(cast_jobs) // n]:
            val = src[:, c0:c0 + wc].astype(BF16)
            if len(dst.shape) == 3:
                tf = dst.shape[2]
                dst[c0 // tf, :, c0 % tf:c0 % tf + wc] = val
            else:
                dst[:, c0:c0 + wc] = val

    nu = len(units)
    s_vals, p_vals = {}, {}
    d_probs, d_finish = ATTN_STAGE_LAG
    for step in range(nu + d_finish):
        if step < nu:
            s_vals[step] = scores(units[step])
        if d_probs <= step < nu + d_probs:
            u = step - d_probs
            p_vals[u] = probs(units[u], *s_vals.pop(u))
            cast_piece(u, nu)
            ada_piece(u, nu)
        if step >= d_finish:
            u = step - d_finish
            finish(units[u], *p_vals.pop(u))


def _attn(qt, k, vt, kc, vtc, sink, w_out, w_up, w_dn, c, c_ctx, w_ada, b_ada, ada_col0):
    tq = ATTN_TQ
    r = tq // BLOCK
    nblk = SEQ // BLOCK
    nsteps = SEQ // tq
    ada_tn = (w_ada.shape[1] - ada_col0) // nsteps
    ada_b0 = ada_col0 // ada_tn
    fixed = lambda i: (0, 0)
    row = lambda i: (i, 0)
    slab = lambda w: pl.BlockSpec((w.shape[0] // nsteps, w.shape[1]), row)
    bf16_like = lambda w: jax.ShapeDtypeStruct(w.shape, BF16)
    prev = lambda i: jnp.maximum(i * r - 1, 0)
    nxt = lambda i: jnp.minimum((i + 1) * r, nblk - 1)
    return pl.pallas_call(
        _attn_kernel,
        grid=(SEQ // tq,),
        in_specs=[
            pl.BlockSpec((ATTN_WIDTH, tq), lambda i: (0, i)),
            pl.BlockSpec((BLOCK, KV_WIDTH), lambda i: (prev(i), 0)),
            pl.BlockSpec((tq, KV_WIDTH), lambda i: (i, 0)),
            pl.BlockSpec((BLOCK, KV_WIDTH), lambda i: (nxt(i), 0)),
            pl.BlockSpec((KV_WIDTH, BLOCK), lambda i: (0, prev(i))),
            pl.BlockSpec((KV_WIDTH, tq), lambda i: (0, i)),
            pl.BlockSpec((KV_WIDTH, BLOCK), lambda i: (0, nxt(i))),
            pl.BlockSpec((CTX_LEN, KV_WIDTH), fixed),
            pl.BlockSpec((KV_WIDTH, CTX_LEN), fixed),
            pl.BlockSpec((1, N_Q_HEADS * BLOCK), fixed),
            slab(w_out), slab(w_up), slab(w_dn),
            pl.BlockSpec((1, D_MODEL), fixed),
            pl.BlockSpec((1, D_MODEL), fixed),
            pl.BlockSpec((D_MODEL, ada_tn), lambda i: (0, ada_b0 + i)),
            pl.BlockSpec((1, ada_tn), lambda i: (0, ada_b0 + i)),
        ],
        out_specs=[pl.BlockSpec((tq, ATTN_WIDTH), row), slab(w_out),
                   pl.BlockSpec((MLP_NF, w_up.shape[0] // nsteps, MLP_TF), lambda i: (0, i, 0)),
                   slab(w_dn),
                   pl.BlockSpec((SUBLANES, ada_tn), lambda i: (0, i))],
        out_shape=[jax.ShapeDtypeStruct((SEQ, ATTN_WIDTH), BF16),
                   bf16_like(w_out),
                   jax.ShapeDtypeStruct((MLP_NF, w_up.shape[0], MLP_TF), BF16),
                   bf16_like(w_dn),
                   jax.ShapeDtypeStruct((SUBLANES, w_ada.shape[1] - ada_col0), F32)],
        compiler_params=_params("arbitrary"),
        name="attn",
    )(qt, k, k, k, vt, vt, vt, kc, vtc, sink, w_out, w_up, w_dn, c, c_ctx, w_ada, b_ada)


def _mixout_kernel(x_ref, attn_ref, up_ref, um_ref, un_ref, wout_ref, pw_ref, ps_ref,
                   ga_ref, nw_ref, sh_ref, sc_ref, x1_ref, hm_ref, ubuf):
    i = pl.program_id(0)
    tm = MIX_TM
    nsteps = SEQ // tm
    ubuf[0:POOL_HALO, :] = jnp.where(i > 0, up_ref[...], 0.0)
    ubuf[POOL_HALO:POOL_HALO + tm, :] = um_ref[...]
    ubuf[POOL_HALO + tm:2 * POOL_HALO + tm, :] = jnp.where(i < nsteps - 1, un_ref[...], 0.0)

    rc = MIX_CHUNK
    n = rc + 2 * POOL_HALO
    c = POOL_HALO
    ngroups = len(POOL_WINDOWS)
    ncol = D_MODEL // ngroups

    def pool_group(r0, g):
        w = POOL_WINDOWS[g]
        c0, c1 = g * POOL_GROUP_DIM, (g + 1) * POOL_GROUP_DIM
        e = ubuf[r0:r0 + n, c0:c1]
        if w == 2:
            win = e[c - 1:c - 1 + rc] + e[c:c + rc]
        else:
            f2 = e[0:n - 1] + e[1:n]
            if w == 4:
                win = f2[c - 2:c - 2 + rc] + f2[c:c + rc]
            else:
                f4 = f2[0:n - 3] + f2[2:n - 1]
                if w == 8:
                    win = f4[c - 4:c - 4 + rc] + f4[c:c + rc]
                else:
                    f8 = f4[0:n - 7] + f4[4:n - 3]
                    win = f8[c - 8:c - 8 + rc] + f8[c:c + rc]
        t = i * tm + r0 + lax.broadcasted_iota(jnp.int32, (rc, 1), 0)
        lo = jnp.clip(t - w // 2, 0, SEQ)
        hi = jnp.clip(t - w // 2 + w, 0, SEQ)
        cnt = (hi - lo).astype(F32)
        pg = win / cnt - um_ref[r0:r0 + rc, c0:c1]
        mixed = jnp.dot(pg.astype(BF16), pw_ref[g].astype(BF16), preferred_element_type=F32)
        return (mixed * ps_ref[:, c0:c1]).astype(BF16)

    def prenorm(r0, nrows):
        rows = slice(r0, r0 + nrows)
        hm_ref[rows, :] = _rms_modulate(x1_ref[rows, :], nw_ref[...], sh_ref[LATENT_ROW],
                                        sc_ref[LATENT_ROW]).astype(BF16)

    chunk_starts = list(range(0, tm, rc))
    for ci, r0 in enumerate(chunk_starts):
        rows = slice(r0, r0 + rc)
        attn_rows = attn_ref[rows, :]
        pooled = []
        for j in range(ngroups):
            cols = slice(j * ncol, (j + 1) * ncol)
            o = jnp.dot(attn_rows, wout_ref[0:ATTN_WIDTH, cols], preferred_element_type=F32)
            x1_ref[rows, cols] = x_ref[rows, cols] + ga_ref[LATENT_ROW, cols] * o
            pooled.append(pool_group(r0, j))
        pool_rows = jnp.concatenate(pooled, axis=1)
        for j in range(ngroups):
            cols = slice(j * ncol, (j + 1) * ncol)
            o = jnp.dot(pool_rows, wout_ref[ATTN_WIDTH:, cols], preferred_element_type=F32)
            x1_ref[rows, cols] += ga_ref[LATENT_ROW, cols] * o
            if ci > 0:
                prenorm(chunk_starts[ci - 1] + j * (rc // ngroups), rc // ngroups)
    for j in range(ngroups):
        prenorm(chunk_starts[-1] + j * (rc // ngroups), rc // ngroups)


def _mixout(x, attn, u, w_out_bf16, pool_w, pool_scale, mod, norm_w):
    tm = MIX_TM
    hb = tm // POOL_HALO
    nhalo = SEQ // POOL_HALO
    row = lambda i: (i, 0)
    fixed = lambda i: (0, 0)
    return pl.pallas_call(
        _mixout_kernel,
        grid=(SEQ // tm,),
        in_specs=[
            pl.BlockSpec((tm, D_MODEL), row),
            pl.BlockSpec((tm, ATTN_WIDTH), row),
            pl.BlockSpec((POOL_HALO, POOL_WIDTH), lambda i: (jnp.maximum(i * hb - 1, 0), 0)),
            pl.BlockSpec((tm, POOL_WIDTH), row),
            pl.BlockSpec((POOL_HALO, POOL_WIDTH), lambda i: (jnp.minimum((i + 1) * hb, nhalo - 1), 0)),
            pl.BlockSpec((D_MODEL, D_MODEL), fixed),
            pl.BlockSpec((len(POOL_WINDOWS), POOL_GROUP_DIM, POOL_GROUP_DIM), lambda i: (0, 0, 0)),
            pl.BlockSpec((1, POOL_WIDTH), fixed),
            _mod_spec(0),
            pl.BlockSpec((1, D_MODEL), fixed),
            _mod_spec(1),
            _mod_spec(2),
        ],
        out_specs=[pl.BlockSpec((tm, D_MODEL), row), pl.BlockSpec((tm, D_MODEL), row)],
        out_shape=[jax.ShapeDtypeStruct((SEQ, D_MODEL), F32),
                   jax.ShapeDtypeStruct((SEQ, D_MODEL), BF16)],
        scratch_shapes=[pltpu.VMEM((tm + 2 * POOL_HALO, POOL_WIDTH), F32)],
        compiler_params=_params("arbitrary"),
        name="mixout",
    )(x, attn, u, u, u, w_out_bf16, pool_w, pool_scale, mod, norm_w, mod, mod)


def _mlp_kernel(hm_ref, x1r_ref, wup_ref, wdn_ref, gm_ref, fw_ref, o_ref):
    f = pl.program_id(1)
    last = MLP_NF - 1
    row_chunks = [slice(r0, r0 + MLP_RC) for r0 in range(0, MLP_TM, MLP_RC)]
    gm = gm_ref[LATENT_ROW]

    def branch_out(rows):
        a = jnp.maximum(jnp.dot(hm_ref[rows, :], wup_ref[...], preferred_element_type=F32), 0.0)
        return gm * jnp.dot((a * a).astype(BF16), wdn_ref[...], preferred_element_type=F32)

    def final_norm(rows):
        x2 = o_ref[rows, :]
        y = x2 * lax.rsqrt(jnp.mean(x2 * x2, axis=-1, keepdims=True) + EPS)
        o_ref[rows, :] = y * fw_ref[...]

    @pl.when(f == 0)
    def _():
        for rows in row_chunks:
            o_ref[rows, :] = branch_out(rows)
        o_ref[0:MLP_XR, :] += x1r_ref[...]

    @pl.when(jnp.logical_and(f > 0, f < last))
    def _():
        x_rows = pl.ds(pl.multiple_of(f * MLP_XR, MLP_XR), MLP_XR)
        o_ref[x_rows, :] += x1r_ref[...]
        for rows in row_chunks:
            o_ref[rows, :] += branch_out(rows)

    @pl.when(f == last)
    def _():
        o_ref[last * MLP_XR:(last + 1) * MLP_XR, :] += x1r_ref[...]
        npiece = 4
        cw, rp = D_MODEL // npiece, MLP_RC // npiece
        for ci, rows in enumerate(row_chunks):
            a = jnp.maximum(jnp.dot(hm_ref[rows, :], wup_ref[...], preferred_element_type=F32), 0.0)
            a = (a * a).astype(BF16)
            for j in range(npiece):
                cols = slice(j * cw, (j + 1) * cw)
                o_ref[rows, cols] += gm[:, cols] * jnp.dot(a, wdn_ref[:, cols],
                                                           preferred_element_type=F32)
                if ci > 0:
                    r0 = row_chunks[ci - 1].start + j * rp
                    final_norm(slice(r0, r0 + rp))
        final_norm(row_chunks[-1])


def _mlp(hm, x1, w_up_bf16, w_dn_bf16, mod, final_w):
    tm, tf = MLP_TM, MLP_TF
    return pl.pallas_call(
        _mlp_kernel,
        grid=(SEQ // tm, MLP_NF),
        in_specs=[
            pl.BlockSpec((tm, D_MODEL), lambda i, f: (i, 0)),
            pl.BlockSpec((MLP_XR, D_MODEL), lambda i, f: (i * MLP_NF + f, 0)),
            pl.BlockSpec((None, D_MODEL, tf), lambda i, f: (f, 0, 0)),
            pl.BlockSpec((tf, D_MODEL), lambda i, f: (f, 0)),
            _mod_spec(3),
            pl.BlockSpec((1, D_MODEL), lambda i, f: (0, 0)),
        ],
        out_specs=pl.BlockSpec((tm, D_MODEL), lambda i, f: (i, 0)),
        out_shape=jax.ShapeDtypeStruct((SEQ, D_MODEL), F32),
        compiler_params=_params("arbitrary", "arbitrary"),
        name="mlp",
    )(hm, x1, w_up_bf16, w_dn_bf16, mod, final_w)


def kernel(x, c, ctx, c_ctx, norm_attn_w, norm_mlp_w, w_ada, b_ada, w_in, attn_sink,
           pool_w, pool_scale, w_out, w_mlp_up, w_mlp_down, final_norm_w):
    assert x.shape == (1, SEQ, D_MODEL) and ctx.shape == (1, CTX_LEN, D_MODEL)
    assert w_ada.shape[0] == 1, "single layer"
    x2d, ctx2d = x[0], ctx[0]

    c_ctx_row = c_ctx[None, :]
    b_row = b_ada[0][None, :]
    n_early = 2 * D_MODEL
    mod_a = _ada(c, c_ctx_row, w_ada[0], b_row, n_early)

    nw_a = norm_attn_w[0][None, :]
    tables = tuple(jnp.asarray(t) for t in _rope_tables())
    qt, k, vt, u = _inproj(x2d, nw_a, mod_a, w_in[0], tables)
    kc, vtc = _ctx_kv(ctx2d, nw_a, mod_a, w_in[0])

    sink_row = jnp.repeat(attn_sink[0].astype(F32), BLOCK)[None, :]
    attn, w_out_b, w_up_b, w_dn_b, mod_b = _attn(
        qt, k, vt, kc, vtc, sink_row, w_out[0], w_mlp_up[0], w_mlp_down[0],
        c, c_ctx_row, w_ada[0], b_row, n_early)

    x1, hm = _mixout(x2d, attn, u, w_out_b, pool_w[0], pool_scale[0][None, :], mod_b,
                     norm_mlp_w[0][None, :])
    out = _mlp(hm, x1, w_up_b, w_dn_b, mod_b, final_norm_w[None, :])
    return out[None]
```

```python
import numpy as np
import jax
import jax.numpy as jnp
from jax import lax
from jax.experimental import pallas as pl
from jax.experimental.pallas import tpu as pltpu

F32 = jnp.float32
BF16 = jnp.bfloat16

D_MODEL = 2048
SEQ = 8192
CTX_LEN = 256
GRID_W = 64
HEAD_DIM = 64
N_Q_HEADS = 16
N_KV_HEADS = 4
GQA = N_Q_HEADS // N_KV_HEADS
ATTN_WIDTH = N_Q_HEADS * HEAD_DIM
KV_WIDTH = N_KV_HEADS * HEAD_DIM
POOL_WINDOWS = (2, 4, 8, 16)
POOL_WIDTH = D_MODEL - ATTN_WIDTH
POOL_GROUP_DIM = POOL_WIDTH // len(POOL_WINDOWS)
IN_WIDTH = ATTN_WIDTH + 2 * KV_WIDTH + POOL_WIDTH
D_FF = 4 * D_MODEL
BLOCK = 128
ROPE_BASE = 10000.0
N_MOD = 6
EPS = 1e-6
NEG_INF = -1e30
LOG2E = 1.4426950408889634
Q_SCALE = HEAD_DIM ** -0.5 * LOG2E

LANES = 128
SUBLANES = 8
POOL_HALO = 8
LATENT_ROW = slice(0, 1)
CONTEXT_ROW = slice(1, 2)
VMEM_LIMIT = 56 * 1024 * 1024

ADA_TN = 512
INPROJ_TM = 512
ATTN_TQ = 512
CAST_JOB_ELEMS = 64 * 1024
ATTN_STAGE_LAG = (2, 5)
MIX_TM = 512
MIX_CHUNK = 256
MIX_STAGE_LAG = (1, 2)
MIX_PIECES = 4
MLP_TM = 1024
MLP_TF = 1024
MLP_RC = 512
MLP_NF = D_FF // MLP_TF
MLP_XR = MLP_TM // MLP_NF


def _params(*sem):
    return pltpu.CompilerParams(dimension_semantics=sem, vmem_limit_bytes=VMEM_LIMIT)


def _mod_spec(chunk):
    return pl.BlockSpec((SUBLANES, D_MODEL), lambda *_: (0, chunk))


def _rms_modulate(x, norm_w, shift, scale):
    gain = norm_w * (1.0 + scale)
    y = x * lax.rsqrt(jnp.mean(x * x, axis=-1, keepdims=True) + EPS)
    return y * gain + shift


def _silu_rows(c_ref, cctx_ref):
    cnd = jnp.concatenate([c_ref[...], cctx_ref[...], jnp.zeros((SUBLANES - 2, D_MODEL), F32)], axis=0)
    return cnd * (1.0 / (1.0 + jnp.exp(-cnd)))


def _ada_cols(s, w_ref, b_ref, cols=slice(None)):
    return jnp.dot(s, w_ref[:, cols], preferred_element_type=F32) + b_ref[:, cols]


def _ada_kernel(c_ref, cctx_ref, w_ref, b_ref, o_ref):
    o_ref[...] = _ada_cols(_silu_rows(c_ref, cctx_ref), w_ref, b_ref)


def _ada(c, c_ctx, w_ada, b_ada, n):
    return pl.pallas_call(
        _ada_kernel,
        grid=(n // ADA_TN,),
        in_specs=[
            pl.BlockSpec((1, D_MODEL), lambda j: (0, 0)),
            pl.BlockSpec((1, D_MODEL), lambda j: (0, 0)),
            pl.BlockSpec((D_MODEL, ADA_TN), lambda j: (0, j)),
            pl.BlockSpec((1, ADA_TN), lambda j: (0, j)),
        ],
        out_specs=pl.BlockSpec((SUBLANES, ADA_TN), lambda j: (0, j)),
        out_shape=jax.ShapeDtypeStruct((SUBLANES, n), F32),
        compiler_params=_params("arbitrary"),
        name="ada",
    )(c, c_ctx, w_ada, b_ada)


def _rope_tables():
    half = HEAD_DIM // 2
    inv_freq = ROPE_BASE ** (-np.arange(0, half, 2, dtype=np.float64) / half)
    t = np.arange(SEQ)
    row = (t // GRID_W).astype(np.float64)
    col = (t % GRID_W).astype(np.float64)
    ang_r = row[:, None] * inv_freq[None, :]
    ang_c = col[:, None] * inv_freq[None, :]
    cos_t = Q_SCALE * np.concatenate([np.cos(ang_r), np.cos(ang_c)], axis=1).T
    sin_t = Q_SCALE * np.concatenate([np.sin(ang_r), np.sin(ang_c)], axis=1).T
    cos_l = np.concatenate([np.cos(ang_r), np.cos(ang_r), np.cos(ang_c), np.cos(ang_c)], axis=1)
    zeros = np.zeros_like(ang_r)
    s_up = np.concatenate([-np.sin(ang_r), zeros, -np.sin(ang_c), zeros], axis=1)
    s_dn = np.concatenate([zeros, np.sin(ang_r), zeros, np.sin(ang_c)], axis=1)
    tile2 = lambda a: np.concatenate([a, a], axis=1)
    q_table = np.concatenate([cos_t, sin_t], axis=0).astype(np.float32)
    k_table = np.concatenate([tile2(cos_l), tile2(s_up), tile2(s_dn)], axis=1).astype(np.float32)
    return q_table, k_table


def _inproj_kernel(x_ref, nw_ref, sh_ref, sc_ref, w_ref, qtab_ref, ktab_ref,
                   qt_ref, k_ref, vt_ref, u_ref, wb_ref):
    @pl.when(pl.program_id(0) == 0)
    def _():
        for c0 in range(0, IN_WIDTH, 2 * LANES):
            wb_ref[:, c0:c0 + 2 * LANES] = w_ref[:, c0:c0 + 2 * LANES].astype(BF16)

    h = _rms_modulate(x_ref[...], nw_ref[...], sh_ref[LATENT_ROW], sc_ref[LATENT_ROW]).astype(BF16)
    p = jnp.dot(h, wb_ref[...], preferred_element_type=F32)

    qt = p[:, :ATTN_WIDTH].T
    cr, cc = qtab_ref[0:16, :], qtab_ref[16:32, :]
    sr, sc = qtab_ref[32:48, :], qtab_ref[48:64, :]
    pieces = []
    for hd in range(N_Q_HEADS):
        b = hd * HEAD_DIM
        x1r, x2r = qt[b:b + 16, :], qt[b + 16:b + 32, :]
        x1c, x2c = qt[b + 32:b + 48, :], qt[b + 48:b + 64, :]
        pieces += [x1r * cr - x2r * sr, x1r * sr + x2r * cr,
                   x1c * cc - x2c * sc, x1c * sc + x2c * cc]
    qt_ref[...] = jnp.concatenate(pieces, axis=0).astype(BF16)

    k = p[:, ATTN_WIDTH:ATTN_WIDTH + KV_WIDTH]
    cos_l, s_up, s_dn = (ktab_ref[:, j * LANES:(j + 1) * LANES] for j in range(3))
    kparts = []
    for g in range(KV_WIDTH // LANES):
        kg = k[:, g * LANES:(g + 1) * LANES]
        kparts.append(kg * cos_l + pltpu.roll(kg, LANES - 16, 1) * s_up
                      + pltpu.roll(kg, 16, 1) * s_dn)
    k_ref[...] = jnp.concatenate(kparts, axis=1).astype(BF16)

    vt_ref[...] = p[:, ATTN_WIDTH + KV_WIDTH:ATTN_WIDTH + 2 * KV_WIDTH].T.astype(BF16)
    u_ref[...] = p[:, ATTN_WIDTH + 2 * KV_WIDTH:]


def _inproj(x, norm_w, mod, w_in, tables):
    q_table, k_table = tables
    tm = INPROJ_TM
    row = lambda i: (i, 0)
    colb = lambda i: (0, i)
    fixed = lambda i: (0, 0)
    return pl.pallas_call(
        _inproj_kernel,
        grid=(SEQ // tm,),
        in_specs=[
            pl.BlockSpec((tm, D_MODEL), row),
            pl.BlockSpec((1, D_MODEL), fixed),
            _mod_spec(0),
            _mod_spec(1),
            pl.BlockSpec((D_MODEL, IN_WIDTH), fixed, pipeline_mode=pl.Buffered(1)),
            pl.BlockSpec((q_table.shape[0], tm), colb),
            pl.BlockSpec((tm, k_table.shape[1]), row),
        ],
        out_specs=[
            pl.BlockSpec((ATTN_WIDTH, tm), colb),
            pl.BlockSpec((tm, KV_WIDTH), row),
            pl.BlockSpec((KV_WIDTH, tm), colb),
            pl.BlockSpec((tm, POOL_WIDTH), row),
        ],
        out_shape=[
            jax.ShapeDtypeStruct((ATTN_WIDTH, SEQ), BF16),
            jax.ShapeDtypeStruct((SEQ, KV_WIDTH), BF16),
            jax.ShapeDtypeStruct((KV_WIDTH, SEQ), BF16),
            jax.ShapeDtypeStruct((SEQ, POOL_WIDTH), F32),
        ],
        scratch_shapes=[pltpu.VMEM((D_MODEL, IN_WIDTH), BF16)],
        compiler_params=_params("arbitrary"),
        name="inproj",
    )(x, norm_w, mod, mod, w_in, q_table, k_table)


def _ctx_kv_kernel(x_ref, nw_ref, sh_ref, sc_ref, w_ref, k_ref, vt_ref):
    h = _rms_modulate(x_ref[...], nw_ref[...], sh_ref[CONTEXT_ROW], sc_ref[CONTEXT_ROW]).astype(BF16)
    p = jnp.dot(h, w_ref[...].astype(BF16), preferred_element_type=F32)
    k_ref[...] = p[:, :KV_WIDTH].astype(BF16)
    vt_ref[...] = p[:, KV_WIDTH:].T.astype(BF16)


def _ctx_kv(ctx, norm_w, mod, w_in):
    fixed = lambda i: (0, 0)
    kv_block = ATTN_WIDTH // (2 * KV_WIDTH)
    return pl.pallas_call(
        _ctx_kv_kernel,
        grid=(1,),
        in_specs=[
            pl.BlockSpec((CTX_LEN, D_MODEL), fixed),
            pl.BlockSpec((1, D_MODEL), fixed),
            _mod_spec(0),
            _mod_spec(1),
            pl.BlockSpec((D_MODEL, 2 * KV_WIDTH), lambda i: (0, kv_block)),
        ],
        out_specs=[
            pl.BlockSpec((CTX_LEN, KV_WIDTH), fixed),
            pl.BlockSpec((KV_WIDTH, CTX_LEN), fixed),
        ],
        out_shape=[
            jax.ShapeDtypeStruct((CTX_LEN, KV_WIDTH), BF16),
            jax.ShapeDtypeStruct((KV_WIDTH, CTX_LEN), BF16),
        ],
        compiler_params=_params("arbitrary"),
        name="ctx_kv",
    )(ctx, norm_w, mod, mod, w_in)


def _attn_kernel(qt_ref, kp_ref, km_ref, kn_ref, vtp_ref, vtm_ref, vtn_ref,
                 kc_ref, vtc_ref, sink_ref, wout_ref, wup_ref, wdn_ref,
                 c_ref, cctx_ref, wada_ref, bada_ref,
                 o_ref, wout_b_ref, wup_b_ref, wdn_b_ref, mod_ref):
    cond_act = _silu_rows(c_ref, cctx_ref)
    ada_cols = [slice(c0, c0 + 2 * LANES) for c0 in range(0, mod_ref.shape[1], 2 * LANES)]

    def ada_piece(idx, n):
        for cols in ada_cols[idx * len(ada_cols) // n:(idx + 1) * len(ada_cols) // n]:
            mod_ref[:, cols] = _ada_cols(cond_act, wada_ref, bada_ref, cols)

    i = pl.program_id(0)
    nsub = ATTN_TQ // BLOCK
    nblk = SEQ // BLOCK
    pw = 2 * BLOCK
    nkeys = 3 * BLOCK + CTX_LEN

    key_j = lax.broadcasted_iota(jnp.int32, (BLOCK, pw), 0)
    qry_i = lax.broadcasted_iota(jnp.int32, (BLOCK, pw), 1) % BLOCK
    neg = jnp.full((BLOCK, pw), NEG_INF, F32)
    zero = jnp.zeros((BLOCK, pw), F32)
    band_prev = jnp.where(key_j >= qry_i, zero, neg)
    band_next = jnp.where(key_j <= qry_i, zero, neg)

    kc = kc_ref[...]
    vtc = vtc_ref[...]
    ones_rows = jnp.ones((2 * SUBLANES, nkeys), BF16)
    sink_row = sink_ref[...] * LOG2E

    windows = []
    for sb in range(nsub):
        n = i * nsub + sb
        lo, hi = sb * BLOCK, (sb + 1) * BLOCK
        k_prev = kp_ref[...] if sb == 0 else km_ref[lo - BLOCK:lo, :]
        k_next = kn_ref[...] if sb == nsub - 1 else km_ref[hi:hi + BLOCK, :]
        vt_prev = vtp_ref[...] if sb == 0 else vtm_ref[:, lo - BLOCK:lo]
        vt_next = vtn_ref[...] if sb == nsub - 1 else vtm_ref[:, hi:hi + BLOCK]
        windows.append(dict(
            kwin=jnp.concatenate([k_prev, km_ref[lo:hi, :], k_next, kc], axis=0),
            vtwin=jnp.concatenate([vt_prev, vtm_ref[:, lo:hi], vt_next, vtc], axis=1),
            bias_prev=band_prev + jnp.where(n > 0, 0.0, NEG_INF),
            bias_next=band_next + jnp.where(n < nblk - 1, 0.0, NEG_INF)))

    units = [(sb, h, half) for sb in range(nsub) for h in range(N_KV_HEADS)
             for half in range(GQA // 2)]

    def scores(unit):
        sb, h, half = unit
        win = windows[sb]
        lo, hi = sb * BLOCK, (sb + 1) * BLOCK
        hd0 = GQA * h + 2 * half
        qh = jnp.concatenate([qt_ref[(hd0 + g) * HEAD_DIM:(hd0 + g + 1) * HEAD_DIM, lo:hi]
                              for g in range(2)], axis=1)
        blocks = []
        if h > 0:
            blocks.append(jnp.zeros((h * HEAD_DIM, pw), BF16))
        blocks.append(qh)
        if h < N_KV_HEADS - 1:
            blocks.append(jnp.zeros(((N_KV_HEADS - 1 - h) * HEAD_DIM, pw), BF16))
        s = jnp.dot(win["kwin"], jnp.concatenate(blocks, axis=0),
                    preferred_element_type=F32)
        s = jnp.concatenate([s[0:BLOCK] + win["bias_prev"], s[BLOCK:2 * BLOCK],
                             s[2 * BLOCK:3 * BLOCK] + win["bias_next"], s[3 * BLOCK:]], axis=0)
        return s, jnp.max(s, axis=0, keepdims=True)

    def probs(unit, s, smax):
        sb, h, half = unit
        hd0 = GQA * h + 2 * half
        sink = sink_row[:, hd0 * BLOCK:(hd0 + 2) * BLOCK]
        m = jnp.maximum(smax, sink)
        return jnp.exp2(s - m).astype(BF16), jnp.exp2(sink - m)

    def finish(unit, e, e_sink):
        sb, h, half = unit
        lo, hi = sb * BLOCK, (sb + 1) * BLOCK
        hd0 = GQA * h + 2 * half
        vt_ones = jnp.concatenate(
            [windows[sb]["vtwin"][h * HEAD_DIM:(h + 1) * HEAD_DIM, :], ones_rows], axis=0)
        o = jnp.dot(vt_ones, e, preferred_element_type=F32)
        denom = o[HEAD_DIM:HEAD_DIM + 1, :] + e_sink
        o = o[:HEAD_DIM, :] * (1.0 / denom)
        blk = jnp.concatenate([o[:, 0:BLOCK], o[:, BLOCK:2 * BLOCK]], axis=0)
        o_ref[lo:hi, hd0 * HEAD_DIM:(hd0 + 2) * HEAD_DIM] = blk.T.astype(BF16)

    cast_jobs = []
    for src, dst in ((wout_ref, wout_b_ref), (wup_ref, wup_b_ref), (wdn_ref, wdn_b_ref)):
        wc = max(LANES, CAST_JOB_ELEMS // src.shape[0])
        cast_jobs += [(src, dst, c0) + (wc,) for c0 in range(0, src.shape[1], wc)]

    def cast_piece(idx, n):
        for src, dst, c0, wc in cast_jobs[idx * len(cast_jobs) // n:(idx + 1) * len(cast_jobs) // n]:
            val = src[:, c0:c0 + wc].astype(BF16)
            if len(dst.shape) == 3:
                tf = dst.shape[2]
                dst[c0 // tf, :, c0 % tf:c0 % tf + wc] = val
            else:
                dst[:, c0:c0 + wc] = val

    nu = len(units)
    s_vals, p_vals = {}, {}
    d_probs, d_finish = ATTN_STAGE_LAG
    for step in range(nu + d_finish):
        if step < nu:
            s_vals[step] = scores(units[step])
        if d_probs <= step < nu + d_probs:
            u = step - d_probs
            p_vals[u] = probs(units[u], *s_vals.pop(u))
            cast_piece(u, nu)
            ada_piece(u, nu)
        if step >= d_finish:
            u = step - d_finish
            finish(units[u], *p_vals.pop(u))


def _attn(qt, k, vt, kc, vtc, sink, w_out, w_up, w_dn, c, c_ctx, w_ada, b_ada, ada_col0):
    tq = ATTN_TQ
    r = tq // BLOCK
    nblk = SEQ // BLOCK
    nsteps = SEQ // tq
    ada_tn = (w_ada.shape[1] - ada_col0) // nsteps
    ada_b0 = ada_col0 // ada_tn
    fixed = lambda i: (0, 0)
    row = lambda i: (i, 0)
    slab = lambda w: pl.BlockSpec((w.shape[0] // nsteps, w.shape[1]), row)
    bf16_like = lambda w: jax.ShapeDtypeStruct(w.shape, BF16)
    prev = lambda i: jnp.maximum(i * r - 1, 0)
    nxt = lambda i: jnp.minimum((i + 1) * r, nblk - 1)
    return pl.pallas_call(
        _attn_kernel,
        grid=(SEQ // tq,),
        in_specs=[
            pl.BlockSpec((ATTN_WIDTH, tq), lambda i: (0, i)),
            pl.BlockSpec((BLOCK, KV_WIDTH), lambda i: (prev(i), 0)),
            pl.BlockSpec((tq, KV_WIDTH), lambda i: (i, 0)),
            pl.BlockSpec((BLOCK, KV_WIDTH), lambda i: (nxt(i), 0)),
            pl.BlockSpec((KV_WIDTH, BLOCK), lambda i: (0, prev(i))),
            pl.BlockSpec((KV_WIDTH, tq), lambda i: (0, i)),
            pl.BlockSpec((KV_WIDTH, BLOCK), lambda i: (0, nxt(i))),
            pl.BlockSpec((CTX_LEN, KV_WIDTH), fixed),
            pl.BlockSpec((KV_WIDTH, CTX_LEN), fixed),
            pl.BlockSpec((1, N_Q_HEADS * BLOCK), fixed),
            slab(w_out), slab(w_up), slab(w_dn),
            pl.BlockSpec((1, D_MODEL), fixed),
            pl.BlockSpec((1, D_MODEL), fixed),
            pl.BlockSpec((D_MODEL, ada_tn), lambda i: (0, ada_b0 + i)),
            pl.BlockSpec((1, ada_tn), lambda i: (0, ada_b0 + i)),
        ],
        out_specs=[pl.BlockSpec((tq, ATTN_WIDTH), row), slab(w_out),
                   pl.BlockSpec((MLP_NF, w_up.shape[0] // nsteps, MLP_TF), lambda i: (0, i, 0)),
                   slab(w_dn),
                   pl.BlockSpec((SUBLANES, ada_tn), lambda i: (0, i))],
        out_shape=[jax.ShapeDtypeStruct((SEQ, ATTN_WIDTH), BF16),
                   bf16_like(w_out),
                   jax.ShapeDtypeStruct((MLP_NF, w_up.shape[0], MLP_TF), BF16),
                   bf16_like(w_dn),
                   jax.ShapeDtypeStruct((SUBLANES, w_ada.shape[1] - ada_col0), F32)],
        compiler_params=_params("arbitrary"),
        name="attn",
    )(qt, k, k, k, vt, vt, vt, kc, vtc, sink, w_out, w_up, w_dn, c, c_ctx, w_ada, b_ada)


def _mixout_kernel(x_ref, attn_ref, up_ref, um_ref, un_ref, wout_ref, pw_ref, ps_ref,
                   ga_ref, nw_ref, sh_ref, sc_ref, x1_ref, hm_ref, ubuf):
    i = pl.program_id(0)
    tm = MIX_TM
    nsteps = SEQ // tm
    ubuf[0:POOL_HALO, :] = jnp.where(i > 0, up_ref[...], 0.0)
    ubuf[POOL_HALO:POOL_HALO + tm, :] = um_ref[...]
    ubuf[POOL_HALO + tm:2 * POOL_HALO + tm, :] = jnp.where(i < nsteps - 1, un_ref[...], 0.0)

    rc = MIX_CHUNK
    n = rc + 2 * POOL_HALO
    c = POOL_HALO
    ngroups = len(POOL_WINDOWS)

    def pool_group(r0, g):
        w = POOL_WINDOWS[g]
        c0, c1 = g * POOL_GROUP_DIM, (g + 1) * POOL_GROUP_DIM
        e = ubuf[r0:r0 + n, c0:c1]
        if w == 2:
            win = e[c - 1:c - 1 + rc] + e[c:c + rc]
        else:
            f2 = e[0:n - 1] + e[1:n]
            if w == 4:
                win = f2[c - 2:c - 2 + rc] + f2[c:c + rc]
            else:
                f4 = f2[0:n - 3] + f2[2:n - 1]
                if w == 8:
                    win = f4[c - 4:c - 4 + rc] + f4[c:c + rc]
                else:
                    f8 = f4[0:n - 7] + f4[4:n - 3]
                    win = f8[c - 8:c - 8 + rc] + f8[c:c + rc]
        t = i * tm + r0 + lax.broadcasted_iota(jnp.int32, (rc, 1), 0)
        lo = jnp.clip(t - w // 2, 0, SEQ)
        hi = jnp.clip(t - w // 2 + w, 0, SEQ)
        cnt = (hi - lo).astype(F32)
        pg = win / cnt - um_ref[r0:r0 + rc, c0:c1]
        mixed = jnp.dot(pg.astype(BF16), pw_ref[g].astype(BF16), preferred_element_type=F32)
        return (mixed * ps_ref[:, c0:c1]).astype(BF16)

    def prenorm(r0, nrows):
        rows = slice(r0, r0 + nrows)
        hm_ref[rows, :] = _rms_modulate(x1_ref[rows, :], nw_ref[...], sh_ref[LATENT_ROW],
                                        sc_ref[LATENT_ROW]).astype(BF16)

    chunk_starts = list(range(0, tm, rc))
    nchunks = len(chunk_starts)
    d_pool, d_norm = MIX_STAGE_LAG
    pooled = {}
    npiece = MIX_PIECES
    ncol, gpp, nrow = D_MODEL // npiece, ngroups // npiece, rc // npiece
    for s in range(nchunks + d_norm):
        for j in range(npiece):
            cols = slice(j * ncol, (j + 1) * ncol)
            if s < nchunks:
                r0 = chunk_starts[s]
                rows = slice(r0, r0 + rc)
                o = jnp.dot(attn_ref[rows, :], wout_ref[0:ATTN_WIDTH, cols],
                            preferred_element_type=F32)
                x1_ref[rows, cols] = x_ref[rows, cols] + ga_ref[LATENT_ROW, cols] * o
                pooled.setdefault(s, []).extend(pool_group(r0, g)
                                                for g in range(j * gpp, (j + 1) * gpp))
            if d_pool <= s < nchunks + d_pool:
                cq = s - d_pool
                rows = slice(chunk_starts[cq], chunk_starts[cq] + rc)
                if j == 0:
                    pooled[cq] = jnp.concatenate(pooled[cq], axis=1)
                o = jnp.dot(pooled[cq], wout_ref[ATTN_WIDTH:, cols], preferred_element_type=F32)
                x1_ref[rows, cols] += ga_ref[LATENT_ROW, cols] * o
            if s >= d_norm:
                prenorm(chunk_starts[s - d_norm] + j * nrow, nrow)


def _mixout(x, attn, u, w_out_bf16, pool_w, pool_scale, mod, norm_w):
    tm = MIX_TM
    hb = tm // POOL_HALO
    nhalo = SEQ // POOL_HALO
    row = lambda i: (i, 0)
    fixed = lambda i: (0, 0)
    return pl.pallas_call(
        _mixout_kernel,
        grid=(SEQ // tm,),
        in_specs=[
            pl.BlockSpec((tm, D_MODEL), row),
            pl.BlockSpec((tm, ATTN_WIDTH), row),
            pl.BlockSpec((POOL_HALO, POOL_WIDTH), lambda i: (jnp.maximum(i * hb - 1, 0), 0)),
            pl.BlockSpec((tm, POOL_WIDTH), row),
            pl.BlockSpec((POOL_HALO, POOL_WIDTH), lambda i: (jnp.minimum((i + 1) * hb, nhalo - 1), 0)),
            pl.BlockSpec((D_MODEL, D_MODEL), fixed),
            pl.BlockSpec((len(POOL_WINDOWS), POOL_GROUP_DIM, POOL_GROUP_DIM), lambda i: (0, 0, 0)),
            pl.BlockSpec((1, POOL_WIDTH), fixed),
            _mod_spec(0),
            pl.BlockSpec((1, D_MODEL), fixed),
            _mod_spec(1),
            _mod_spec(2),
        ],
        out_specs=[pl.BlockSpec((tm, D_MODEL), row), pl.BlockSpec((tm, D_MODEL), row)],
        out_shape=[jax.ShapeDtypeStruct((SEQ, D_MODEL), F32),
                   jax.ShapeDtypeStruct((SEQ, D_MODEL), BF16)],
        scratch_shapes=[pltpu.VMEM((tm + 2 * POOL_HALO, POOL_WIDTH), F32)],
        compiler_params=_params("arbitrary"),
        name="mixout",
    )(x, attn, u, u, u, w_out_bf16, pool_w, pool_scale, mod, norm_w, mod, mod)


def _mlp_kernel(hm_ref, x1r_ref, wup_ref, wdn_ref, gm_ref, fw_ref, o_ref):
    f = pl.program_id(1)
    last = MLP_NF - 1
    row_chunks = [slice(r0, r0 + MLP_RC) for r0 in range(0, MLP_TM, MLP_RC)]
    gm = gm_ref[LATENT_ROW]

    def branch_out(rows):
        a = jnp.maximum(jnp.dot(hm_ref[rows, :], wup_ref[...], preferred_element_type=F32), 0.0)
        return gm * jnp.dot((a * a).astype(BF16), wdn_ref[...], preferred_element_type=F32)

    def final_norm(rows):
        x2 = o_ref[rows, :]
        y = x2 * lax.rsqrt(jnp.mean(x2 * x2, axis=-1, keepdims=True) + EPS)
        o_ref[rows, :] = y * fw_ref[...]

    @pl.when(f == 0)
    def _():
        for rows in row_chunks:
            o_ref[rows, :] = branch_out(rows)
        o_ref[0:MLP_XR, :] += x1r_ref[...]

    @pl.when(jnp.logical_and(f > 0, f < last))
    def _():
        x_rows = pl.ds(pl.multiple_of(f * MLP_XR, MLP_XR), MLP_XR)
        o_ref[x_rows, :] += x1r_ref[...]
        for rows in row_chunks:
            o_ref[rows, :] += branch_out(rows)

    @pl.when(f == last)
    def _():
        o_ref[last * MLP_XR:(last + 1) * MLP_XR, :] += x1r_ref[...]
        npiece = 4
        cw, rp = D_MODEL // npiece, MLP_RC // npiece
        for ci, rows in enumerate(row_chunks):
            a = jnp.maximum(jnp.dot(hm_ref[rows, :], wup_ref[...], preferred_element_type=F32), 0.0)
            a = (a * a).astype(BF16)
            for j in range(npiece):
                cols = slice(j * cw, (j + 1) * cw)
                o_ref[rows, cols] += gm[:, cols] * jnp.dot(a, wdn_ref[:, cols],
                                                           preferred_element_type=F32)
                if ci > 0:
                    r0 = row_chunks[ci - 1].start + j * rp
                    final_norm(slice(r0, r0 + rp))
        final_norm(row_chunks[-1])


def _mlp(hm, x1, w_up_bf16, w_dn_bf16, mod, final_w):
    tm, tf = MLP_TM, MLP_TF
    return pl.pallas_call(
        _mlp_kernel,
        grid=(SEQ // tm, MLP_NF),
        in_specs=[
            pl.BlockSpec((tm, D_MODEL), lambda i, f: (i, 0)),
            pl.BlockSpec((MLP_XR, D_MODEL), lambda i, f: (i * MLP_NF + f, 0)),
            pl.BlockSpec((None, D_MODEL, tf), lambda i, f: (f, 0, 0)),
            pl.BlockSpec((tf, D_MODEL), lambda i, f: (f, 0)),
            _mod_spec(3),
            pl.BlockSpec((1, D_MODEL), lambda i, f: (0, 0)),
        ],
        out_specs=pl.BlockSpec((tm, D_MODEL), lambda i, f: (i, 0)),
        out_shape=jax.ShapeDtypeStruct((SEQ, D_MODEL), F32),
        compiler_params=_params("arbitrary", "arbitrary"),
        name="mlp",
    )(hm, x1, w_up_bf16, w_dn_bf16, mod, final_w)


def kernel(x, c, ctx, c_ctx, norm_attn_w, norm_mlp_w, w_ada, b_ada, w_in, attn_sink,
           pool_w, pool_scale, w_out, w_mlp_up, w_mlp_down, final_norm_w):
    assert x.shape == (1, SEQ, D_MODEL) and ctx.shape == (1, CTX_LEN, D_MODEL)
    assert w_ada.shape[0] == 1, "single layer"
    x2d, ctx2d = x[0], ctx[0]

    c_ctx_row = c_ctx[None, :]
    b_row = b_ada[0][None, :]
    n_early = 2 * D_MODEL
    mod_a = _ada(c, c_ctx_row, w_ada[0], b_row, n_early)

    nw_a = norm_attn_w[0][None, :]
    tables = tuple(jnp.asarray(t) for t in _rope_tables())
    qt, k, vt, u = _inproj(x2d, nw_a, mod_a, w_in[0], tables)
    kc, vtc = _ctx_kv(ctx2d, nw_a, mod_a, w_in[0])

    sink_row = jnp.repeat(attn_sink[0].astype(F32), BLOCK)[None, :]
    attn, w_out_b, w_up_b, w_dn_b, mod_b = _attn(
        qt, k, vt, kc, vtc, sink_row, w_out[0], w_mlp_up[0], w_mlp_down[0],
        c, c_ctx_row, w_ada[0], b_row, n_early)

    x1, hm = _mixout(x2d, attn, u, w_out_b, pool_w[0], pool_scale[0][None, :], mod_b,
                     norm_mlp_w[0][None, :])
    out = _mlp(hm, x1, w_up_b, w_dn_b, mod_b, final_norm_w[None, :])
    return out[None]
```

```python
import numpy as np
import jax
import jax.numpy as jnp
from jax import lax
from jax.experimental import pallas as pl
from jax.experimental.pallas import tpu as pltpu

F32 = jnp.float32
BF16 = jnp.bfloat16

D_MODEL = 2048
SEQ = 8192
CTX_LEN = 256
GRID_W = 64
HEAD_DIM = 64
N_Q_HEADS = 16
N_KV_HEADS = 4
GQA = N_Q_HEADS // N_KV_HEADS
ATTN_WIDTH = N_Q_HEADS * HEAD_DIM
KV_WIDTH = N_KV_HEADS * HEAD_DIM
POOL_WINDOWS = (2, 4, 8, 16)
POOL_WIDTH = D_MODEL - ATTN_WIDTH
POOL_GROUP_DIM = POOL_WIDTH // len(POOL_WINDOWS)
IN_WIDTH = ATTN_WIDTH + 2 * KV_WIDTH + POOL_WIDTH
D_FF = 4 * D_MODEL
BLOCK = 128
ROPE_BASE = 10000.0
N_MOD = 6
EPS = 1e-6
NEG_INF = -1e30
LOG2E = 1.4426950408889634
Q_SCALE = HEAD_DIM ** -0.5 * LOG2E

LANES = 128
SUBLANES = 8
POOL_HALO = 8
LATENT_ROW = slice(0, 1)
CONTEXT_ROW = slice(1, 2)
VMEM_LIMIT = 56 * 1024 * 1024

ADA_TN = 512
INPROJ_TM = 512
ATTN_TQ = 512
CAST_JOB_ELEMS = 64 * 1024
ATTN_STAGE_LAG = (2, 5)
MIX_TM = 512
MIX_CHUNK = 256
MIX_STAGE_LAG = (1, 2)
MIX_PIECES = 4
MLP_TM = 1024
MLP_TF = 1024
MLP_RC = 512
MLP_NF = D_FF // MLP_TF
MLP_XR = MLP_TM // MLP_NF


def _params(*sem):
    return pltpu.CompilerParams(dimension_semantics=sem, vmem_limit_bytes=VMEM_LIMIT)


def _mod_spec(chunk):
    return pl.BlockSpec((SUBLANES, D_MODEL), lambda *_: (0, chunk))


def _rms_modulate(x, norm_w, shift, scale):
    gain = norm_w * (1.0 + scale)
    y = x * lax.rsqrt(jnp.mean(x * x, axis=-1, keepdims=True) + EPS)
    return y * gain + shift


def _silu_rows(c_ref, cctx_ref):
    cnd = jnp.concatenate([c_ref[...], cctx_ref[...], jnp.zeros((SUBLANES - 2, D_MODEL), F32)], axis=0)
    return cnd * (1.0 / (1.0 + jnp.exp(-cnd)))


def _ada_cols(s, w_ref, b_ref, cols=slice(None)):
    return jnp.dot(s, w_ref[:, cols], preferred_element_type=F32) + b_ref[:, cols]


def _ada_kernel(c_ref, cctx_ref, w_ref, b_ref, o_ref):
    o_ref[...] = _ada_cols(_silu_rows(c_ref, cctx_ref), w_ref, b_ref)


def _ada(c, c_ctx, w_ada, b_ada, n):
    return pl.pallas_call(
        _ada_kernel,
        grid=(n // ADA_TN,),
        in_specs=[
            pl.BlockSpec((1, D_MODEL), lambda j: (0, 0)),
            pl.BlockSpec((1, D_MODEL), lambda j: (0, 0)),
            pl.BlockSpec((D_MODEL, ADA_TN), lambda j: (0, j)),
            pl.BlockSpec((1, ADA_TN), lambda j: (0, j)),
        ],
        out_specs=pl.BlockSpec((SUBLANES, ADA_TN), lambda j: (0, j)),
        out_shape=jax.ShapeDtypeStruct((SUBLANES, n), F32),
        compiler_params=_params("arbitrary"),
        name="ada",
    )(c, c_ctx, w_ada, b_ada)


def _rope_tables():
    half = HEAD_DIM // 2
    inv_freq = ROPE_BASE ** (-np.arange(0, half, 2, dtype=np.float64) / half)
    t = np.arange(SEQ)
    row = (t // GRID_W).astype(np.float64)
    col = (t % GRID_W).astype(np.float64)
    ang_r = row[:, None] * inv_freq[None, :]
    ang_c = col[:, None] * inv_freq[None, :]
    cos_t = Q_SCALE * np.concatenate([np.cos(ang_r), np.cos(ang_c)], axis=1).T
    sin_t = Q_SCALE * np.concatenate([np.sin(ang_r), np.sin(ang_c)], axis=1).T
    cos_l = np.concatenate([np.cos(ang_r), np.cos(ang_r), np.cos(ang_c), np.cos(ang_c)], axis=1)
    zeros = np.zeros_like(ang_r)
    s_up = np.concatenate([-np.sin(ang_r), zeros, -np.sin(ang_c), zeros], axis=1)
    s_dn = np.concatenate([zeros, np.sin(ang_r), zeros, np.sin(ang_c)], axis=1)
    tile2 = lambda a: np.concatenate([a, a], axis=1)
    q_table = np.concatenate([cos_t, sin_t], axis=0).astype(np.float32)
    k_table = np.concatenate([tile2(cos_l), tile2(s_up), tile2(s_dn)], axis=1).astype(np.float32)
    return q_table, k_table


def _inproj_kernel(x_ref, nw_ref, sh_ref, sc_ref, w_ref, qtab_ref, ktab_ref,
                   qt_ref, k_ref, vt_ref, u_ref, wb_ref):
    @pl.when(pl.program_id(0) == 0)
    def _():
        for c0 in range(0, IN_WIDTH, 2 * LANES):
            wb_ref[:, c0:c0 + 2 * LANES] = w_ref[:, c0:c0 + 2 * LANES].astype(BF16)

    h = _rms_modulate(x_ref[...], nw_ref[...], sh_ref[LATENT_ROW], sc_ref[LATENT_ROW]).astype(BF16)
    p = jnp.dot(h, wb_ref[...], preferred_element_type=F32)

    qt = p[:, :ATTN_WIDTH].T
    cr, cc = qtab_ref[0:16, :], qtab_ref[16:32, :]
    sr, sc = qtab_ref[32:48, :], qtab_ref[48:64, :]
    pieces = []
    for hd in range(N_Q_HEADS):
        b = hd * HEAD_DIM
        x1r, x2r = qt[b:b + 16, :], qt[b + 16:b + 32, :]
        x1c, x2c = qt[b + 32:b + 48, :], qt[b + 48:b + 64, :]
        pieces += [x1r * cr - x2r * sr, x1r * sr + x2r * cr,
                   x1c * cc - x2c * sc, x1c * sc + x2c * cc]
    qt_ref[...] = jnp.concatenate(pieces, axis=0).astype(BF16)

    k = p[:, ATTN_WIDTH:ATTN_WIDTH + KV_WIDTH]
    cos_l, s_up, s_dn = (ktab_ref[:, j * LANES:(j + 1) * LANES] for j in range(3))
    kparts = []
    for g in range(KV_WIDTH // LANES):
        kg = k[:, g * LANES:(g + 1) * LANES]
        kparts.append(kg * cos_l + pltpu.roll(kg, LANES - 16, 1) * s_up
                      + pltpu.roll(kg, 16, 1) * s_dn)
    k_ref[...] = jnp.concatenate(kparts, axis=1).astype(BF16)

    vt_ref[...] = p[:, ATTN_WIDTH + KV_WIDTH:ATTN_WIDTH + 2 * KV_WIDTH].T.astype(BF16)
    u_ref[...] = p[:, ATTN_WIDTH + 2 * KV_WIDTH:]


def _inproj(x, norm_w, mod, w_in, tables):
    q_table, k_table = tables
    tm = INPROJ_TM
    row = lambda i: (i, 0)
    colb = lambda i: (0, i)
    fixed = lambda i: (0, 0)
    return pl.pallas_call(
        _inproj_kernel,
        grid=(SEQ // tm,),
        in_specs=[
            pl.BlockSpec((tm, D_MODEL), row),
            pl.BlockSpec((1, D_MODEL), fixed),
            _mod_spec(0),
            _mod_spec(1),
            pl.BlockSpec((D_MODEL, IN_WIDTH), fixed, pipeline_mode=pl.Buffered(1)),
            pl.BlockSpec((q_table.shape[0], tm), colb),
            pl.BlockSpec((tm, k_table.shape[1]), row),
        ],
        out_specs=[
            pl.BlockSpec((ATTN_WIDTH, tm), colb),
            pl.BlockSpec((tm, KV_WIDTH), row),
            pl.BlockSpec((KV_WIDTH, tm), colb),
            pl.BlockSpec((tm, POOL_WIDTH), row),
        ],
        out_shape=[
            jax.ShapeDtypeStruct((ATTN_WIDTH, SEQ), BF16),
            jax.ShapeDtypeStruct((SEQ, KV_WIDTH), BF16),
            jax.ShapeDtypeStruct((KV_WIDTH, SEQ), BF16),
            jax.ShapeDtypeStruct((SEQ, POOL_WIDTH), F32),
        ],
        scratch_shapes=[pltpu.VMEM((D_MODEL, IN_WIDTH), BF16)],
        compiler_params=_params("arbitrary"),
        name="inproj",
    )(x, norm_w, mod, mod, w_in, q_table, k_table)


def _ctx_kv_kernel(x_ref, nw_ref, sh_ref, sc_ref, w_ref, k_ref, vt_ref):
    h = _rms_modulate(x_ref[...], nw_ref[...], sh_ref[CONTEXT_ROW], sc_ref[CONTEXT_ROW]).astype(BF16)
    p = jnp.dot(h, w_ref[...].astype(BF16), preferred_element_type=F32)
    k_ref[...] = p[:, :KV_WIDTH].astype(BF16)
    vt_ref[...] = p[:, KV_WIDTH:].T.astype(BF16)


def _ctx_kv(ctx, norm_w, mod, w_in):
    fixed = lambda i: (0, 0)
    kv_block = ATTN_WIDTH // (2 * KV_WIDTH)
    return pl.pallas_call(
        _ctx_kv_kernel,
        grid=(1,),
        in_specs=[
            pl.BlockSpec((CTX_LEN, D_MODEL), fixed),
            pl.BlockSpec((1, D_MODEL), fixed),
            _mod_spec(0),
            _mod_spec(1),
            pl.BlockSpec((D_MODEL, 2 * KV_WIDTH), lambda i: (0, kv_block)),
        ],
        out_specs=[
            pl.BlockSpec((CTX_LEN, KV_WIDTH), fixed),
            pl.BlockSpec((KV_WIDTH, CTX_LEN), fixed),
        ],
        out_shape=[
            jax.ShapeDtypeStruct((CTX_LEN, KV_WIDTH), BF16),
            jax.ShapeDtypeStruct((KV_WIDTH, CTX_LEN), BF16),
        ],
        compiler_params=_params("arbitrary"),
        name="ctx_kv",
    )(ctx, norm_w, mod, mod, w_in)


def _attn_kernel(qt_ref, kp_ref, km_ref, kn_ref, vtp_ref, vtm_ref, vtn_ref,
                 kc_ref, vtc_ref, sink_ref, wout_ref, wup_ref, wdn_ref,
                 c_ref, cctx_ref, wada_ref, bada_ref,
                 o_ref, wout_b_ref, wup_b_ref, wdn_b_ref, mod_ref):
    cond_act = _silu_rows(c_ref, cctx_ref)
    ada_cols = [slice(c0, c0 + 2 * LANES) for c0 in range(0, mod_ref.shape[1], 2 * LANES)]

    def ada_piece(idx, n):
        for cols in ada_cols[idx * len(ada_cols) // n:(idx + 1) * len(ada_cols) // n]:
            mod_ref[:, cols] = _ada_cols(cond_act, wada_ref, bada_ref, cols)

    i = pl.program_id(0)
    nsub = ATTN_TQ // BLOCK
    nblk = SEQ // BLOCK
    pw = 2 * BLOCK
    nkeys = 3 * BLOCK + CTX_LEN

    key_j = lax.broadcasted_iota(jnp.int32, (BLOCK, pw), 0)
    qry_i = lax.broadcasted_iota(jnp.int32, (BLOCK, pw), 1) % BLOCK
    neg = jnp.full((BLOCK, pw), NEG_INF, F32)
    zero = jnp.zeros((BLOCK, pw), F32)
    band_prev = jnp.where(key_j >= qry_i, zero, neg)
    band_next = jnp.where(key_j <= qry_i, zero, neg)

    kc = kc_ref[...]
    vtc = vtc_ref[...]
    ones_rows = jnp.ones((2 * SUBLANES, nkeys), BF16)
    sink_row = sink_ref[...] * LOG2E

    windows = []
    for sb in range(nsub):
        n = i * nsub + sb
        lo, hi = sb * BLOCK, (sb + 1) * BLOCK
        k_prev = kp_ref[...] if sb == 0 else km_ref[lo - BLOCK:lo, :]
        k_next = kn_ref[...] if sb == nsub - 1 else km_ref[hi:hi + BLOCK, :]
        vt_prev = vtp_ref[...] if sb == 0 else vtm_ref[:, lo - BLOCK:lo]
        vt_next = vtn_ref[...] if sb == nsub - 1 else vtm_ref[:, hi:hi + BLOCK]
        windows.append(dict(
            kwin=jnp.concatenate([k_prev, km_ref[lo:hi, :], k_next, kc], axis=0),
            vtwin=jnp.concatenate([vt_prev, vtm_ref[:, lo:hi], vt_next, vtc], axis=1),
            bias_prev=band_prev + jnp.where(n > 0, 0.0, NEG_INF),
            bias_next=band_next + jnp.where(n < nblk - 1, 0.0, NEG_INF)))

    units = [(sb, h, half) for sb in range(nsub) for h in range(N_KV_HEADS)
             for half in range(GQA // 2)]

    def scores(unit):
        sb, h, half = unit
        win = windows[sb]
        lo, hi = sb * BLOCK, (sb + 1) * BLOCK
        hd0 = GQA * h + 2 * half
        qh = jnp.concatenate([qt_ref[(hd0 + g) * HEAD_DIM:(hd0 + g + 1) * HEAD_DIM, lo:hi]
                              for g in range(2)], axis=1)
        blocks = []
        if h > 0:
            blocks.append(jnp.zeros((h * HEAD_DIM, pw), BF16))
        blocks.append(qh)
        if h < N_KV_HEADS - 1:
            blocks.append(jnp.zeros(((N_KV_HEADS - 1 - h) * HEAD_DIM, pw), BF16))
        s = jnp.dot(win["kwin"], jnp.concatenate(blocks, axis=0),
                    preferred_element_type=F32)
        s = jnp.concatenate([s[0:BLOCK] + win["bias_prev"], s[BLOCK:2 * BLOCK],
                             s[2 * BLOCK:3 * BLOCK] + win["bias_next"], s[3 * BLOCK:]], axis=0)
        return s, jnp.max(s, axis=0, keepdims=True)

    def probs(unit, s, smax):
        sb, h, half = unit
        hd0 = GQA * h + 2 * half
        sink = sink_row[:, hd0 * BLOCK:(hd0 + 2) * BLOCK]
        m = jnp.maximum(smax, sink)
        return jnp.exp2(s - m).astype(BF16), jnp.exp2(sink - m)

    def finish(unit, e, e_sink):
        sb, h, half = unit
        lo, hi = sb * BLOCK, (sb + 1) * BLOCK
        hd0 = GQA * h + 2 * half
        vt_ones = jnp.concatenate(
            [windows[sb]["vtwin"][h * HEAD_DIM:(h + 1) * HEAD_DIM, :], ones_rows], axis=0)
        o = jnp.dot(vt_ones, e, preferred_element_type=F32)
        denom = o[HEAD_DIM:HEAD_DIM + 1, :] + e_sink
        o = o[:HEAD_DIM, :] * (1.0 / denom)
        blk = jnp.concatenate([o[:, 0:BLOCK], o[:, BLOCK:2 * BLOCK]], axis=0)
        o_ref[lo:hi, hd0 * HEAD_DIM:(hd0 + 2) * HEAD_DIM] = blk.T.astype(BF16)

    cast_jobs = []
    for src, dst in ((wout_ref, wout_b_ref), (wup_ref, wup_b_ref), (wdn_ref, wdn_b_ref)):
        wc = max(LANES, CAST_JOB_ELEMS // src.shape[0])
        cast_jobs += [(src, dst, c0) + (wc,) for c0 in range(0, src.shape[1], wc)]

    def cast_piece(idx, n):
        for src, dst, c0, wc in cast_jobs[idx * len(cast_jobs) // n:(idx + 1) * len(cast_jobs) // n]:
            val = src[:, c0:c0 + wc].astype(BF16)
            if len(dst.shape) == 3:
                tf = dst.shape[2]
                dst[c0 // tf, :, c0 % tf:c0 % tf + wc] = val
            else:
                dst[:, c0:c0 + wc] = val

    nu = len(units)
    s_vals, p_vals = {}, {}
    d_probs, d_finish = ATTN_STAGE_LAG
    for step in range(nu + d_finish):
        if step < nu:
            s_vals[step] = scores(units[step])
        if d_probs <= step < nu + d_probs:
            u = step - d_probs
            p_vals[u] = probs(units[u], *s_vals.pop(u))
            cast_piece(u, nu)
            ada_piece(u, nu)
        if step >= d_finish:
            u = step - d_finish
            finish(units[u], *p_vals.pop(u))


def _attn(qt, k, vt, kc, vtc, sink, w_out, w_up, w_dn, c, c_ctx, w_ada, b_ada, ada_col0):
    tq = ATTN_TQ
    r = tq // BLOCK
    nblk = SEQ // BLOCK
    nsteps = SEQ // tq
    ada_tn = (w_ada.shape[1] - ada_col0) // nsteps
    ada_b0 = ada_col0 // ada_tn
    fixed = lambda i: (0, 0)
    row = lambda i: (i, 0)
    slab = lambda w: pl.BlockSpec((w.shape[0] // nsteps, w.shape[1]), row)
    bf16_like = lambda w: jax.ShapeDtypeStruct(w.shape, BF16)
    prev = lambda i: jnp.maximum(i * r - 1, 0)
    nxt = lambda i: jnp.minimum((i + 1) * r, nblk - 1)
    return pl.pallas_call(
        _attn_kernel,
        grid=(SEQ // tq,),
        in_specs=[
            pl.BlockSpec((ATTN_WIDTH, tq), lambda i: (0, i)),
            pl.BlockSpec((BLOCK, KV_WIDTH), lambda i: (prev(i), 0)),
            pl.BlockSpec((tq, KV_WIDTH), lambda i: (i, 0)),
            pl.BlockSpec((BLOCK, KV_WIDTH), lambda i: (nxt(i), 0)),
            pl.BlockSpec((KV_WIDTH, BLOCK), lambda i: (0, prev(i))),
            pl.BlockSpec((KV_WIDTH, tq), lambda i: (0, i)),
            pl.BlockSpec((KV_WIDTH, BLOCK), lambda i: (0, nxt(i))),
            pl.BlockSpec((CTX_LEN, KV_WIDTH), fixed),
            pl.BlockSpec((KV_WIDTH, CTX_LEN), fixed),
            pl.BlockSpec((1, N_Q_HEADS * BLOCK), fixed),
            slab(w_out), slab(w_up), slab(w_dn),
            pl.BlockSpec((1, D_MODEL), fixed),
            pl.BlockSpec((1, D_MODEL), fixed),
            pl.BlockSpec((D_MODEL, ada_tn), lambda i: (0, ada_b0 + i)),
            pl.BlockSpec((1, ada_tn), lambda i: (0, ada_b0 + i)),
        ],
        out_specs=[pl.BlockSpec((tq, ATTN_WIDTH), row), slab(w_out),
                   pl.BlockSpec((MLP_NF, w_up.shape[0] // nsteps, MLP_TF), lambda i: (0, i, 0)),
                   slab(w_dn),
                   pl.BlockSpec((SUBLANES, ada_tn), lambda i: (0, i))],
        out_shape=[jax.ShapeDtypeStruct((SEQ, ATTN_WIDTH), BF16),
                   bf16_like(w_out),
                   jax.ShapeDtypeStruct((MLP_NF, w_up.shape[0], MLP_TF), BF16),
                   bf16_like(w_dn),
                   jax.ShapeDtypeStruct((SUBLANES, w_ada.shape[1] - ada_col0), F32)],
        compiler_params=_params("arbitrary"),
        name="attn",
    )(qt, k, k, k, vt, vt, vt, kc, vtc, sink, w_out, w_up, w_dn, c, c_ctx, w_ada, b_ada)


def _mixout_kernel(x_ref, attn_ref, up_ref, um_ref, un_ref, wout_ref, pw_ref, ps_ref,
                   ga_ref, nw_ref, sh_ref, sc_ref, x1_ref, hm_ref, ubuf):
    i = pl.program_id(0)
    tm = MIX_TM
    nsteps = SEQ // tm
    ubuf[0:POOL_HALO, :] = jnp.where(i > 0, up_ref[...], 0.0)
    ubuf[POOL_HALO:POOL_HALO + tm, :] = um_ref[...]
    ubuf[POOL_HALO + tm:2 * POOL_HALO + tm, :] = jnp.where(i < nsteps - 1, un_ref[...], 0.0)

    rc = MIX_CHUNK
    n = rc + 2 * POOL_HALO
    c = POOL_HALO
    ngroups = len(POOL_WINDOWS)

    def pool_group(r0, g):
        w = POOL_WINDOWS[g]
        c0, c1 = g * POOL_GROUP_DIM, (g + 1) * POOL_GROUP_DIM
        e = ubuf[r0:r0 + n, c0:c1]
        if w == 2:
            win = e[c - 1:c - 1 + rc] + e[c:c + rc]
        else:
            f2 = e[0:n - 1] + e[1:n]
            if w == 4:
                win = f2[c - 2:c - 2 + rc] + f2[c:c + rc]
            else:
                f4 = f2[0:n - 3] + f2[2:n - 1]
                if w == 8:
                    win = f4[c - 4:c - 4 + rc] + f4[c:c + rc]
                else:
                    f8 = f4[0:n - 7] + f4[4:n - 3]
                    win = f8[c - 8:c - 8 + rc] + f8[c:c + rc]
        t = i * tm + r0 + lax.broadcasted_iota(jnp.int32, (rc, 1), 0)
        lo = jnp.clip(t - w // 2, 0, SEQ)
        hi = jnp.clip(t - w // 2 + w, 0, SEQ)
        cnt = (hi - lo).astype(F32)
        pg = win / cnt - um_ref[r0:r0 + rc, c0:c1]
        mixed = jnp.dot(pg.astype(BF16), pw_ref[g].astype(BF16), preferred_element_type=F32)
        return (mixed * ps_ref[:, c0:c1]).astype(BF16)

    def prenorm(r0, nrows):
        rows = slice(r0, r0 + nrows)
        hm_ref[rows, :] = _rms_modulate(x1_ref[rows, :], nw_ref[...], sh_ref[LATENT_ROW],
                                        sc_ref[LATENT_ROW]).astype(BF16)

    chunk_starts = list(range(0, tm, rc))
    nchunks = len(chunk_starts)
    d_pool, d_norm = MIX_STAGE_LAG
    pooled = {}
    npiece = MIX_PIECES
    ncol, gpp, nrow = D_MODEL // npiece, ngroups // npiece, rc // npiece
    for s in range(nchunks + d_norm):
        for j in range(npiece):
            cols = slice(j * ncol, (j + 1) * ncol)
            if s < nchunks:
                r0 = chunk_starts[s]
                rows = slice(r0, r0 + rc)
                o = jnp.dot(attn_ref[rows, :], wout_ref[0:ATTN_WIDTH, cols],
                            preferred_element_type=F32)
                x1_ref[rows, cols] = x_ref[rows, cols] + ga_ref[LATENT_ROW, cols] * o
                pooled.setdefault(s, []).extend(pool_group(r0, g)
                                                for g in range(j * gpp, (j + 1) * gpp))
            if d_pool <= s < nchunks + d_pool:
                cq = s - d_pool
                rows = slice(chunk_starts[cq], chunk_starts[cq] + rc)
                if j == 0:
                    pooled[cq] = jnp.concatenate(pooled[cq], axis=1)
                o = jnp.dot(pooled[cq], wout_ref[ATTN_WIDTH:, cols], preferred_element_type=F32)
                x1_ref[rows, cols] += ga_ref[LATENT_ROW, cols] * o
            if s >= d_norm:
                prenorm(chunk_starts[s - d_norm] + j * nrow, nrow)


def _mixout(x, attn, u, w_out_bf16, pool_w, pool_scale, mod, norm_w):
    tm = MIX_TM
    hb = tm // POOL_HALO
    nhalo = SEQ // POOL_HALO
    row = lambda i: (i, 0)
    fixed = lambda i: (0, 0)
    return pl.pallas_call(
        _mixout_kernel,
        grid=(SEQ // tm,),
        in_specs=[
            pl.BlockSpec((tm, D_MODEL), row),
            pl.BlockSpec((tm, ATTN_WIDTH), row),
            pl.BlockSpec((POOL_HALO, POOL_WIDTH), lambda i: (jnp.maximum(i * hb - 1, 0), 0)),
            pl.BlockSpec((tm, POOL_WIDTH), row),
            pl.BlockSpec((POOL_HALO, POOL_WIDTH), lambda i: (jnp.minimum((i + 1) * hb, nhalo - 1), 0)),
            pl.BlockSpec((D_MODEL, D_MODEL), fixed),
            pl.BlockSpec((len(POOL_WINDOWS), POOL_GROUP_DIM, POOL_GROUP_DIM), lambda i: (0, 0, 0)),
            pl.BlockSpec((1, POOL_WIDTH), fixed),
            _mod_spec(0),
            pl.BlockSpec((1, D_MODEL), fixed),
            _mod_spec(1),
            _mod_spec(2),
        ],
        out_specs=[pl.BlockSpec((tm, D_MODEL), row), pl.BlockSpec((tm, D_MODEL), row)],
        out_shape=[jax.ShapeDtypeStruct((SEQ, D_MODEL), F32),
                   jax.ShapeDtypeStruct((SEQ, D_MODEL), BF16)],
        scratch_shapes=[pltpu.VMEM((tm + 2 * POOL_HALO, POOL_WIDTH), F32)],
        compiler_params=_params("arbitrary"),
        name="mixout",
    )(x, attn, u, u, u, w_out_bf16, pool_w, pool_scale, mod, norm_w, mod, mod)


def _mlp_kernel(hm_ref, x1r_ref, wup_ref, wdn_ref, gm_ref, fw_ref, o_ref):
    f = pl.program_id(1)
    last = MLP_NF - 1
    row_chunks = [slice(r0, r0 + MLP_RC) for r0 in range(0, MLP_TM, MLP_RC)]
    gm = gm_ref[LATENT_ROW]

    def up_act(rows):
        a = jnp.maximum(jnp.dot(hm_ref[rows, :], wup_ref[...], preferred_element_type=F32), 0.0)
        return (a * a).astype(BF16)

    def down(a):
        return gm * jnp.dot(a, wdn_ref[...], preferred_element_type=F32)

    def lagged_chunks():
        acts = {0: up_act(row_chunks[0])}
        for ci, rows in enumerate(row_chunks):
            if ci + 1 < len(row_chunks):
                acts[ci + 1] = up_act(row_chunks[ci + 1])
            yield ci, rows, acts.pop(ci)

    def final_norm(rows):
        x2 = o_ref[rows, :]
        y = x2 * lax.rsqrt(jnp.mean(x2 * x2, axis=-1, keepdims=True) + EPS)
        o_ref[rows, :] = y * fw_ref[...]

    @pl.when(f == 0)
    def _():
        for _, rows, a in lagged_chunks():
            o_ref[rows, :] = down(a)
        o_ref[0:MLP_XR, :] += x1r_ref[...]

    @pl.when(jnp.logical_and(f > 0, f < last))
    def _():
        x_rows = pl.ds(pl.multiple_of(f * MLP_XR, MLP_XR), MLP_XR)
        o_ref[x_rows, :] += x1r_ref[...]
        for _, rows, a in lagged_chunks():
            o_ref[rows, :] += down(a)

    @pl.when(f == last)
    def _():
        o_ref[last * MLP_XR:(last + 1) * MLP_XR, :] += x1r_ref[...]
        npiece = 4
        cw, rp = D_MODEL // npiece, MLP_RC // npiece
        for ci, rows, a in lagged_chunks():
            for j in range(npiece):
                cols = slice(j * cw, (j + 1) * cw)
                o_ref[rows, cols] += gm[:, cols] * jnp.dot(a, wdn_ref[:, cols],
                                                           preferred_element_type=F32)
                if ci > 0:
                    r0 = row_chunks[ci - 1].start + j * rp
                    final_norm(slice(r0, r0 + rp))
        final_norm(row_chunks[-1])


def _mlp(hm, x1, w_up_bf16, w_dn_bf16, mod, final_w):
    tm, tf = MLP_TM, MLP_TF
    return pl.pallas_call(
        _mlp_kernel,
        grid=(SEQ // tm, MLP_NF),
        in_specs=[
            pl.BlockSpec((tm, D_MODEL), lambda i, f: (i, 0)),
            pl.BlockSpec((MLP_XR, D_MODEL), lambda i, f: (i * MLP_NF + f, 0)),
            pl.BlockSpec((None, D_MODEL, tf), lambda i, f: (f, 0, 0)),
            pl.BlockSpec((tf, D_MODEL), lambda i, f: (f, 0)),
            _mod_spec(3),
            pl.BlockSpec((1, D_MODEL), lambda i, f: (0, 0)),
        ],
        out_specs=pl.BlockSpec((tm, D_MODEL), lambda i, f: (i, 0)),
        out_shape=jax.ShapeDtypeStruct((SEQ, D_MODEL), F32),
        compiler_params=_params("arbitrary", "arbitrary"),
        name="mlp",
    )(hm, x1, w_up_bf16, w_dn_bf16, mod, final_w)


def kernel(x, c, ctx, c_ctx, norm_attn_w, norm_mlp_w, w_ada, b_ada, w_in, attn_sink,
           pool_w, pool_scale, w_out, w_mlp_up, w_mlp_down, final_norm_w):
    assert x.shape == (1, SEQ, D_MODEL) and ctx.shape == (1, CTX_LEN, D_MODEL)
    assert w_ada.shape[0] == 1, "single layer"
    x2d, ctx2d = x[0], ctx[0]

    c_ctx_row = c_ctx[None, :]
    b_row = b_ada[0][None, :]
    n_early = 2 * D_MODEL
    mod_a = _ada(c, c_ctx_row, w_ada[0], b_row, n_early)

    nw_a = norm_attn_w[0][None, :]
    tables = tuple(jnp.asarray(t) for t in _rope_tables())
    qt, k, vt, u = _inproj(x2d, nw_a, mod_a, w_in[0], tables)
    kc, vtc = _ctx_kv(ctx2d, nw_a, mod_a, w_in[0])

    sink_row = jnp.repeat(attn_sink[0].astype(F32), BLOCK)[None, :]
    attn, w_out_b, w_up_b, w_dn_b, mod_b = _attn(
        qt, k, vt, kc, vtc, sink_row, w_out[0], w_mlp_up[0], w_mlp_down[0],
        c, c_ctx_row, w_ada[0], b_row, n_early)

    x1, hm = _mixout(x2d, attn, u, w_out_b, pool_w[0], pool_scale[0][None, :], mod_b,
                     norm_mlp_w[0][None, :])
    out = _mlp(hm, x1, w_up_b, w_dn_b, mod_b, final_norm_w[None, :])
    return out[None]
```

```python
import numpy as np
import jax
import jax.numpy as jnp
from jax import lax
from jax.experimental import pallas as pl
from jax.experimental.pallas import tpu as pltpu

F32 = jnp.float32
BF16 = jnp.bfloat16

D_MODEL = 2048
SEQ = 8192
CTX_LEN = 256
GRID_W = 64
HEAD_DIM = 64
N_Q_HEADS = 16
N_KV_HEADS = 4
GQA = N_Q_HEADS // N_KV_HEADS
ATTN_WIDTH = N_Q_HEADS * HEAD_DIM
KV_WIDTH = N_KV_HEADS * HEAD_DIM
POOL_WINDOWS = (2, 4, 8, 16)
POOL_WIDTH = D_MODEL - ATTN_WIDTH
POOL_GROUP_DIM = POOL_WIDTH // len(POOL_WINDOWS)
IN_WIDTH = ATTN_WIDTH + 2 * KV_WIDTH + POOL_WIDTH
D_FF = 4 * D_MODEL
BLOCK = 128
ROPE_BASE = 10000.0
N_MOD = 6
EPS = 1e-6
NEG_INF = -1e30
LOG2E = 1.4426950408889634
Q_SCALE = HEAD_DIM ** -0.5 * LOG2E

LANES = 128
SUBLANES = 8
POOL_HALO = 8
LATENT_ROW = slice(0, 1)
CONTEXT_ROW = slice(1, 2)
VMEM_LIMIT = 56 * 1024 * 1024

ADA_TN = 512
INPROJ_TM = 512
ATTN_TQ = 512
CAST_JOB_ELEMS = 64 * 1024
ATTN_STAGE_LAG = (2, 5)
MIX_TM = 512
MIX_CHUNK = 256
MLP_TM = 1024
MLP_TF = 1024
MLP_RC = 512
MLP_NF = D_FF // MLP_TF
MLP_XR = MLP_TM // MLP_NF


def _params(*sem):
    return pltpu.CompilerParams(dimension_semantics=sem, vmem_limit_bytes=VMEM_LIMIT)


def _mod_spec(chunk):
    return pl.BlockSpec((SUBLANES, D_MODEL), lambda *_: (0, chunk))


def _rms_modulate(x, norm_w, shift, scale):
    gain = norm_w * (1.0 + scale)
    y = x * lax.rsqrt(jnp.mean(x * x, axis=-1, keepdims=True) + EPS)
    return y * gain + shift


def _silu_rows(c_ref, cctx_ref):
    cnd = jnp.concatenate([c_ref[...], cctx_ref[...], jnp.zeros((SUBLANES - 2, D_MODEL), F32)], axis=0)
    return cnd * (1.0 / (1.0 + jnp.exp(-cnd)))


def _ada_cols(s, w_ref, b_ref, cols=slice(None)):
    return jnp.dot(s, w_ref[:, cols], preferred_element_type=F32) + b_ref[:, cols]


def _ada_kernel(c_ref, cctx_ref, w_ref, b_ref, o_ref):
    o_ref[...] = _ada_cols(_silu_rows(c_ref, cctx_ref), w_ref, b_ref)


def _ada(c, c_ctx, w_ada, b_ada, n):
    return pl.pallas_call(
        _ada_kernel,
        grid=(n // ADA_TN,),
        in_specs=[
            pl.BlockSpec((1, D_MODEL), lambda j: (0, 0)),
            pl.BlockSpec((1, D_MODEL), lambda j: (0, 0)),
            pl.BlockSpec((D_MODEL, ADA_TN), lambda j: (0, j)),
            pl.BlockSpec((1, ADA_TN), lambda j: (0, j)),
        ],
        out_specs=pl.BlockSpec((SUBLANES, ADA_TN), lambda j: (0, j)),
        out_shape=jax.ShapeDtypeStruct((SUBLANES, n), F32),
        compiler_params=_params("arbitrary"),
        name="ada",
    )(c, c_ctx, w_ada, b_ada)


def _rope_tables():
    half = HEAD_DIM // 2
    inv_freq = ROPE_BASE ** (-np.arange(0, half, 2, dtype=np.float64) / half)
    t = np.arange(SEQ)
    row = (t // GRID_W).astype(np.float64)
    col = (t % GRID_W).astype(np.float64)
    ang_r = row[:, None] * inv_freq[None, :]
    ang_c = col[:, None] * inv_freq[None, :]
    cos_t = Q_SCALE * np.concatenate([np.cos(ang_r), np.cos(ang_c)], axis=1).T
    sin_t = Q_SCALE * np.concatenate([np.sin(ang_r), np.sin(ang_c)], axis=1).T
    cos_l = np.concatenate([np.cos(ang_r), np.cos(ang_r), np.cos(ang_c), np.cos(ang_c)], axis=1)
    zeros = np.zeros_like(ang_r)
    s_up = np.concatenate([-np.sin(ang_r), zeros, -np.sin(ang_c), zeros], axis=1)
    s_dn = np.concatenate([zeros, np.sin(ang_r), zeros, np.sin(ang_c)], axis=1)
    tile2 = lambda a: np.concatenate([a, a], axis=1)
    q_table = np.concatenate([cos_t, sin_t], axis=0).astype(np.float32)
    k_table = np.concatenate([tile2(cos_l), tile2(s_up), tile2(s_dn)], axis=1).astype(np.float32)
    return q_table, k_table


def _inproj_kernel(x_ref, nw_ref, sh_ref, sc_ref, w_ref, qtab_ref, ktab_ref,
                   qt_ref, k_ref, vt_ref, u_ref, wb_ref):
    @pl.when(pl.program_id(0) == 0)
    def _():
        for c0 in range(0, IN_WIDTH, 2 * LANES):
            wb_ref[:, c0:c0 + 2 * LANES] = w_ref[:, c0:c0 + 2 * LANES].astype(BF16)

    h = _rms_modulate(x_ref[...], nw_ref[...], sh_ref[LATENT_ROW], sc_ref[LATENT_ROW]).astype(BF16)
    p = jnp.dot(h, wb_ref[...], preferred_element_type=F32)

    qt = p[:, :ATTN_WIDTH].T
    cr, cc = qtab_ref[0:16, :], qtab_ref[16:32, :]
    sr, sc = qtab_ref[32:48, :], qtab_ref[48:64, :]
    pieces = []
    for hd in range(N_Q_HEADS):
        b = hd * HEAD_DIM
        x1r, x2r = qt[b:b + 16, :], qt[b + 16:b + 32, :]
        x1c, x2c = qt[b + 32:b + 48, :], qt[b + 48:b + 64, :]
        pieces += [x1r * cr - x2r * sr, x1r * sr + x2r * cr,
                   x1c * cc - x2c * sc, x1c * sc + x2c * cc]
    qt_ref[...] = jnp.concatenate(pieces, axis=0).astype(BF16)

    k = p[:, ATTN_WIDTH:ATTN_WIDTH + KV_WIDTH]
    cos_l, s_up, s_dn = (ktab_ref[:, j * LANES:(j + 1) * LANES] for j in range(3))
    kparts = []
    for g in range(KV_WIDTH // LANES):
        kg = k[:, g * LANES:(g + 1) * LANES]
        kparts.append(kg * cos_l + pltpu.roll(kg, LANES - 16, 1) * s_up
                      + pltpu.roll(kg, 16, 1) * s_dn)
    k_ref[...] = jnp.concatenate(kparts, axis=1).astype(BF16)

    vt_ref[...] = p[:, ATTN_WIDTH + KV_WIDTH:ATTN_WIDTH + 2 * KV_WIDTH].T.astype(BF16)
    u_ref[...] = p[:, ATTN_WIDTH + 2 * KV_WIDTH:]


def _inproj(x, norm_w, mod, w_in, tables):
    q_table, k_table = tables
    tm = INPROJ_TM
    row = lambda i: (i, 0)
    colb = lambda i: (0, i)
    fixed = lambda i: (0, 0)
    return pl.pallas_call(
        _inproj_kernel,
        grid=(SEQ // tm,),
        in_specs=[
            pl.BlockSpec((tm, D_MODEL), row),
            pl.BlockSpec((1, D_MODEL), fixed),
            _mod_spec(0),
            _mod_spec(1),
            pl.BlockSpec((D_MODEL, IN_WIDTH), fixed, pipeline_mode=pl.Buffered(1)),
            pl.BlockSpec((q_table.shape[0], tm), colb),
            pl.BlockSpec((tm, k_table.shape[1]), row),
        ],
        out_specs=[
            pl.BlockSpec((ATTN_WIDTH, tm), colb),
            pl.BlockSpec((tm, KV_WIDTH), row),
            pl.BlockSpec((KV_WIDTH, tm), colb),
            pl.BlockSpec((tm, POOL_WIDTH), row),
        ],
        out_shape=[
            jax.ShapeDtypeStruct((ATTN_WIDTH, SEQ), BF16),
            jax.ShapeDtypeStruct((SEQ, KV_WIDTH), BF16),
            jax.ShapeDtypeStruct((KV_WIDTH, SEQ), BF16),
            jax.ShapeDtypeStruct((SEQ, POOL_WIDTH), F32),
        ],
        scratch_shapes=[pltpu.VMEM((D_MODEL, IN_WIDTH), BF16)],
        compiler_params=_params("arbitrary"),
        name="inproj",
    )(x, norm_w, mod, mod, w_in, q_table, k_table)


def _ctx_kv_kernel(x_ref, nw_ref, sh_ref, sc_ref, w_ref, k_ref, vt_ref):
    h = _rms_modulate(x_ref[...], nw_ref[...], sh_ref[CONTEXT_ROW], sc_ref[CONTEXT_ROW]).astype(BF16)
    p = jnp.dot(h, w_ref[...].astype(BF16), preferred_element_type=F32)
    k_ref[...] = p[:, :KV_WIDTH].astype(BF16)
    vt_ref[...] = p[:, KV_WIDTH:].T.astype(BF16)


def _ctx_kv(ctx, norm_w, mod, w_in):
    fixed = lambda i: (0, 0)
    kv_block = ATTN_WIDTH // (2 * KV_WIDTH)
    return pl.pallas_call(
        _ctx_kv_kernel,
        grid=(1,),
        in_specs=[
            pl.BlockSpec((CTX_LEN, D_MODEL), fixed),
            pl.BlockSpec((1, D_MODEL), fixed),
            _mod_spec(0),
            _mod_spec(1),
            pl.BlockSpec((D_MODEL, 2 * KV_WIDTH), lambda i: (0, kv_block)),
        ],
        out_specs=[
            pl.BlockSpec((CTX_LEN, KV_WIDTH), fixed),
            pl.BlockSpec((KV_WIDTH, CTX_LEN), fixed),
        ],
        out_shape=[
            jax.ShapeDtypeStruct((CTX_LEN, KV_WIDTH), BF16),
            jax.ShapeDtypeStruct((KV_WIDTH, CTX_LEN), BF16),
        ],
        compiler_params=_params("arbitrary"),
        name="ctx_kv",
    )(ctx, norm_w, mod, mod, w_in)


def _attn_kernel(qt_ref, kp_ref, km_ref, kn_ref, vtp_ref, vtm_ref, vtn_ref,
                 kc_ref, vtc_ref, sink_ref, wout_ref, wup_ref, wdn_ref,
                 c_ref, cctx_ref, wada_ref, bada_ref,
                 o_ref, wout_b_ref, wup_b_ref, wdn_b_ref, mod_ref):
    cond_act = _silu_rows(c_ref, cctx_ref)
    ada_cols = [slice(c0, c0 + 2 * LANES) for c0 in range(0, mod_ref.shape[1], 2 * LANES)]

    def ada_piece(idx, n):
        for cols in ada_cols[idx * len(ada_cols) // n:(idx + 1) * len(ada_cols) // n]:
            mod_ref[:, cols] = _ada_cols(cond_act, wada_ref, bada_ref, cols)

    i = pl.program_id(0)
    nsub = ATTN_TQ // BLOCK
    nblk = SEQ // BLOCK
    pw = 2 * BLOCK
    nkeys = 3 * BLOCK + CTX_LEN

    key_j = lax.broadcasted_iota(jnp.int32, (BLOCK, pw), 0)
    qry_i = lax.broadcasted_iota(jnp.int32, (BLOCK, pw), 1) % BLOCK
    neg = jnp.full((BLOCK, pw), NEG_INF, F32)
    zero = jnp.zeros((BLOCK, pw), F32)
    band_prev = jnp.where(key_j >= qry_i, zero, neg)
    band_next = jnp.where(key_j <= qry_i, zero, neg)

    kc = kc_ref[...]
    vtc = vtc_ref[...]
    ones_rows = jnp.ones((2 * SUBLANES, nkeys), BF16)
    sink_row = sink_ref[...] * LOG2E

    windows = []
    for sb in range(nsub):
        n = i * nsub + sb
        lo, hi = sb * BLOCK, (sb + 1) * BLOCK
        k_prev = kp_ref[...] if sb == 0 else km_ref[lo - BLOCK:lo, :]
        k_next = kn_ref[...] if sb == nsub - 1 else km_ref[hi:hi + BLOCK, :]
        vt_prev = vtp_ref[...] if sb == 0 else vtm_ref[:, lo - BLOCK:lo]
        vt_next = vtn_ref[...] if sb == nsub - 1 else vtm_ref[:, hi:hi + BLOCK]
        windows.append(dict(
            kwin=jnp.concatenate([k_prev, km_ref[lo:hi, :], k_next, kc], axis=0),
            vtwin=jnp.concatenate([vt_prev, vtm_ref[:, lo:hi], vt_next, vtc], axis=1),
            bias_prev=band_prev + jnp.where(n > 0, 0.0, NEG_INF),
            bias_next=band_next + jnp.where(n < nblk - 1, 0.0, NEG_INF)))

    units = [(sb, h, half) for sb in range(nsub) for h in range(N_KV_HEADS)
             for half in range(GQA // 2)]

    def scores(unit):
        sb, h, half = unit
        win = windows[sb]
        lo, hi = sb * BLOCK, (sb + 1) * BLOCK
        hd0 = GQA * h + 2 * half
        qh = jnp.concatenate([qt_ref[(hd0 + g) * HEAD_DIM:(hd0 + g + 1) * HEAD_DIM, lo:hi]
                              for g in range(2)], axis=1)
        blocks = []
        if h > 0:
            blocks.append(jnp.zeros((h * HEAD_DIM, pw), BF16))
        blocks.append(qh)
        if h < N_KV_HEADS - 1:
            blocks.append(jnp.zeros(((N_KV_HEADS - 1 - h) * HEAD_DIM, pw), BF16))
        s = jnp.dot(win["kwin"], jnp.concatenate(blocks, axis=0),
                    preferred_element_type=F32)
        s = jnp.concatenate([s[0:BLOCK] + win["bias_prev"], s[BLOCK:2 * BLOCK],
                             s[2 * BLOCK:3 * BLOCK] + win["bias_next"], s[3 * BLOCK:]], axis=0)
        return s, jnp.max(s, axis=0, keepdims=True)

    def probs(unit, s, smax):
        sb, h, half = unit
        hd0 = GQA * h + 2 * half
        sink = sink_row[:, hd0 * BLOCK:(hd0 + 2) * BLOCK]
        m = jnp.maximum(smax, sink)
        return jnp.exp2(s - m).astype(BF16), jnp.exp2(sink - m)

    def finish(unit, e, e_sink):
        sb, h, half = unit
        lo, hi = sb * BLOCK, (sb + 1) * BLOCK
        hd0 = GQA * h + 2 * half
        vt_ones = jnp.concatenate(
            [windows[sb]["vtwin"][h * HEAD_DIM:(h + 1) * HEAD_DIM, :], ones_rows], axis=0)
        o = jnp.dot(vt_ones, e, preferred_element_type=F32)
        denom = o[HEAD_DIM:HEAD_DIM + 1, :] + e_sink
        o = o[:HEAD_DIM, :] * (1.0 / denom)
        blk = jnp.concatenate([o[:, 0:BLOCK], o[:, BLOCK:2 * BLOCK]], axis=0)
        o_ref[lo:hi, hd0 * HEAD_DIM:(hd0 + 2) * HEAD_DIM] = blk.T.astype(BF16)

    cast_jobs = []
    for src, dst in ((wout_ref, wout_b_ref), (wup_ref, wup_b_ref), (wdn_ref, wdn_b_ref)):
        wc = max(LANES, CAST_JOB_ELEMS // src.shape[0])
        cast_jobs += [(src, dst, c0) + (wc,) for c0 in range(0, src.shape[1], wc)]

    def cast_piece(idx, n):
        for src, dst, c0, wc in cast_jobs[idx * len(cast_jobs) // n:(idx + 1) * len(cast_jobs) // n]:
            val = src[:, c0:c0 + wc].astype(BF16)
            if len(dst.shape) == 3:
                tf = dst.shape[2]
                dst[c0 // tf, :, c0 % tf:c0 % tf + wc] = val
            else:
                dst[:, c0:c0 + wc] = val

    nu = len(units)
    s_vals, p_vals = {}, {}
    d_probs, d_finish = ATTN_STAGE_LAG
    for step in range(nu + d_finish):
        if step < nu:
            s_vals[step] = scores(units[step])
        if d_probs <= step < nu + d_probs:
            u = step - d_probs
            p_vals[u] = probs(units[u], *s_vals.pop(u))
            cast_piece(u, nu)
            ada_piece(u, nu)
        if step >= d_finish:
            u = step - d_finish
            finish(units[u], *p_vals.pop(u))


def _attn(qt, k, vt, kc, vtc, sink, w_out, w_up, w_dn, c, c_ctx, w_ada, b_ada, ada_col0):
    tq = ATTN_TQ
    r = tq // BLOCK
    nblk = SEQ // BLOCK
    nsteps = SEQ // tq
    ada_tn = (w_ada.shape[1] - ada_col0) // nsteps
    ada_b0 = ada_col0 // ada_tn
    fixed = lambda i: (0, 0)
    row = lambda i: (i, 0)
    slab = lambda w: pl.BlockSpec((w.shape[0] // nsteps, w.shape[1]), row)
    bf16_like = lambda w: jax.ShapeDtypeStruct(w.shape, BF16)
    prev = lambda i: jnp.maximum(i * r - 1, 0)
    nxt = lambda i: jnp.minimum((i + 1) * r, nblk - 1)
    return pl.pallas_call(
        _attn_kernel,
        grid=(SEQ // tq,),
        in_specs=[
            pl.BlockSpec((ATTN_WIDTH, tq), lambda i: (0, i)),
            pl.BlockSpec((BLOCK, KV_WIDTH), lambda i: (prev(i), 0)),
            pl.BlockSpec((tq, KV_WIDTH), lambda i: (i, 0)),
            pl.BlockSpec((BLOCK, KV_WIDTH), lambda i: (nxt(i), 0)),
            pl.BlockSpec((KV_WIDTH, BLOCK), lambda i: (0, prev(i))),
            pl.BlockSpec((KV_WIDTH, tq), lambda i: (0, i)),
            pl.BlockSpec((KV_WIDTH, BLOCK), lambda i: (0, nxt(i))),
            pl.BlockSpec((CTX_LEN, KV_WIDTH), fixed),
            pl.BlockSpec((KV_WIDTH, CTX_LEN), fixed),
            pl.BlockSpec((1, N_Q_HEADS * BLOCK), fixed),
            slab(w_out), slab(w_up), slab(w_dn),
            pl.BlockSpec((1, D_MODEL), fixed),
            pl.BlockSpec((1, D_MODEL), fixed),
            pl.BlockSpec((D_MODEL, ada_tn), lambda i: (0, ada_b0 + i)),
            pl.BlockSpec((1, ada_tn), lambda i: (0, ada_b0 + i)),
        ],
        out_specs=[pl.BlockSpec((tq, ATTN_WIDTH), row), slab(w_out),
                   pl.BlockSpec((MLP_NF, w_up.shape[0] // nsteps, MLP_TF), lambda i: (0, i, 0)),
                   slab(w_dn),
                   pl.BlockSpec((SUBLANES, ada_tn), lambda i: (0, i))],
        out_shape=[jax.ShapeDtypeStruct((SEQ, ATTN_WIDTH), BF16),
                   bf16_like(w_out),
                   jax.ShapeDtypeStruct((MLP_NF, w_up.shape[0], MLP_TF), BF16),
                   bf16_like(w_dn),
                   jax.ShapeDtypeStruct((SUBLANES, w_ada.shape[1] - ada_col0), F32)],
        compiler_params=_params("arbitrary"),
        name="attn",
    )(qt, k, k, k, vt, vt, vt, kc, vtc, sink, w_out, w_up, w_dn, c, c_ctx, w_ada, b_ada)


def _mixout_kernel(x_ref, attn_ref, up_ref, um_ref, un_ref, wout_ref, pw_ref, ps_ref,
                   ga_ref, nw_ref, sh_ref, sc_ref, x1_ref, hm_ref, ubuf):
    i = pl.program_id(0)
    tm = MIX_TM
    nsteps = SEQ // tm
    ubuf[0:POOL_HALO, :] = jnp.where(i > 0, up_ref[...], 0.0)
    ubuf[POOL_HALO:POOL_HALO + tm, :] = um_ref[...]
    ubuf[POOL_HALO + tm:2 * POOL_HALO + tm, :] = jnp.where(i < nsteps - 1, un_ref[...], 0.0)

    rc = MIX_CHUNK
    n = rc + 2 * POOL_HALO
    c = POOL_HALO
    ngroups = len(POOL_WINDOWS)
    ncol = D_MODEL // ngroups

    def pool_group(r0, g):
        w = POOL_WINDOWS[g]
        c0, c1 = g * POOL_GROUP_DIM, (g + 1) * POOL_GROUP_DIM
        e = ubuf[r0:r0 + n, c0:c1]
        f, span = e, 1
        while 2 * span < w:
            f = f + pltpu.roll(f, n - span, 0)
            span *= 2
        win = (pltpu.roll(f, span, 0) + f)[c:c + rc]
        t = i * tm + r0 + lax.broadcasted_iota(jnp.int32, (rc, 1), 0)
        lo = jnp.clip(t - w // 2, 0, SEQ)
        hi = jnp.clip(t - w // 2 + w, 0, SEQ)
        cnt = (hi - lo).astype(F32)
        pg = win / cnt - um_ref[r0:r0 + rc, c0:c1]
        mixed = jnp.dot(pg.astype(BF16), pw_ref[g].astype(BF16), preferred_element_type=F32)
        return (mixed * ps_ref[:, c0:c1]).astype(BF16)

    def prenorm(r0, nrows):
        rows = slice(r0, r0 + nrows)
        hm_ref[rows, :] = _rms_modulate(x1_ref[rows, :], nw_ref[...], sh_ref[LATENT_ROW],
                                        sc_ref[LATENT_ROW]).astype(BF16)

    chunk_starts = list(range(0, tm, rc))
    for ci, r0 in enumerate(chunk_starts):
        rows = slice(r0, r0 + rc)
        attn_rows = attn_ref[rows, :]
        pooled = []
        for j in range(ngroups):
            cols = slice(j * ncol, (j + 1) * ncol)
            o = jnp.dot(attn_rows, wout_ref[0:ATTN_WIDTH, cols], preferred_element_type=F32)
            x1_ref[rows, cols] = x_ref[rows, cols] + ga_ref[LATENT_ROW, cols] * o
            pooled.append(pool_group(r0, j))
        pool_rows = jnp.concatenate(pooled, axis=1)
        for j in range(ngroups):
            cols = slice(j * ncol, (j + 1) * ncol)
            o = jnp.dot(pool_rows, wout_ref[ATTN_WIDTH:, cols], preferred_element_type=F32)
            x1_ref[rows, cols] += ga_ref[LATENT_ROW, cols] * o
            if ci > 0:
                prenorm(chunk_starts[ci - 1] + j * (rc // ngroups), rc // ngroups)
    for j in range(ngroups):
        prenorm(chunk_starts[-1] + j * (rc // ngroups), rc // ngroups)


def _mixout(x, attn, u, w_out_bf16, pool_w, pool_scale, mod, norm_w):
    tm = MIX_TM
    hb = tm // POOL_HALO
    nhalo = SEQ // POOL_HALO
    row = lambda i: (i, 0)
    fixed = lambda i: (0, 0)
    return pl.pallas_call(
        _mixout_kernel,
        grid=(SEQ // tm,),
        in_specs=[
            pl.BlockSpec((tm, D_MODEL), row),
            pl.BlockSpec((tm, ATTN_WIDTH), row),
            pl.BlockSpec((POOL_HALO, POOL_WIDTH), lambda i: (jnp.maximum(i * hb - 1, 0), 0)),
            pl.BlockSpec((tm, POOL_WIDTH), row),
            pl.BlockSpec((POOL_HALO, POOL_WIDTH), lambda i: (jnp.minimum((i + 1) * hb, nhalo - 1), 0)),
            pl.BlockSpec((D_MODEL, D_MODEL), fixed),
            pl.BlockSpec((len(POOL_WINDOWS), POOL_GROUP_DIM, POOL_GROUP_DIM), lambda i: (0, 0, 0)),
            pl.BlockSpec((1, POOL_WIDTH), fixed),
            _mod_spec(0),
            pl.BlockSpec((1, D_MODEL), fixed),
            _mod_spec(1),
            _mod_spec(2),
        ],
        out_specs=[pl.BlockSpec((tm, D_MODEL), row), pl.BlockSpec((tm, D_MODEL), row)],
        out_shape=[jax.ShapeDtypeStruct((SEQ, D_MODEL), F32),
                   jax.ShapeDtypeStruct((SEQ, D_MODEL), BF16)],
        scratch_shapes=[pltpu.VMEM((tm + 2 * POOL_HALO, POOL_WIDTH), F32)],
        compiler_params=_params("arbitrary"),
        name="mixout",
    )(x, attn, u, u, u, w_out_bf16, pool_w, pool_scale, mod, norm_w, mod, mod)


def _mlp_kernel(hm_ref, x1r_ref, wup_ref, wdn_ref, gm_ref, fw_ref, o_ref):
    f = pl.program_id(1)
    last = MLP_NF - 1
    row_chunks = [slice(r0, r0 + MLP_RC) for r0 in range(0, MLP_TM, MLP_RC)]
    gm = gm_ref[LATENT_ROW]

    def branch_out(rows):
        a = jnp.maximum(jnp.dot(hm_ref[rows, :], wup_ref[...], preferred_element_type=F32), 0.0)
        return gm * jnp.dot((a * a).astype(BF16), wdn_ref[...], preferred_element_type=F32)

    def final_norm(rows):
        x2 = o_ref[rows, :]
        y = x2 * lax.rsqrt(jnp.mean(x2 * x2, axis=-1, keepdims=True) + EPS)
        o_ref[rows, :] = y * fw_ref[...]

    @pl.when(f == 0)
    def _():
        for rows in row_chunks:
            o_ref[rows, :] = branch_out(rows)
        o_ref[0:MLP_XR, :] += x1r_ref[...]

    @pl.when(jnp.logical_and(f > 0, f < last))
    def _():
        x_rows = pl.ds(pl.multiple_of(f * MLP_XR, MLP_XR), MLP_XR)
        o_ref[x_rows, :] += x1r_ref[...]
        for rows in row_chunks:
            o_ref[rows, :] += branch_out(rows)

    @pl.when(f == last)
    def _():
        o_ref[last * MLP_XR:(last + 1) * MLP_XR, :] += x1r_ref[...]
        npiece = 4
        cw, rp = D_MODEL // npiece, MLP_RC // npiece
        for ci, rows in enumerate(row_chunks):
            a = jnp.maximum(jnp.dot(hm_ref[rows, :], wup_ref[...], preferred_element_type=F32), 0.0)
            a = (a * a).astype(BF16)
            for j in range(npiece):
                cols = slice(j * cw, (j + 1) * cw)
                o_ref[rows, cols] += gm[:, cols] * jnp.dot(a, wdn_ref[:, cols],
                                                           preferred_element_type=F32)
                if ci > 0:
                    r0 = row_chunks[ci - 1].start + j * rp
                    final_norm(slice(r0, r0 + rp))
        final_norm(row_chunks[-1])


def _mlp(hm, x1, w_up_bf16, w_dn_bf16, mod, final_w):
    tm, tf = MLP_TM, MLP_TF
    return pl.pallas_call(
        _mlp_kernel,
        grid=(SEQ // tm, MLP_NF),
        in_specs=[
            pl.BlockSpec((tm, D_MODEL), lambda i, f: (i, 0)),
            pl.BlockSpec((MLP_XR, D_MODEL), lambda i, f: (i * MLP_NF + f, 0)),
            pl.BlockSpec((None, D_MODEL, tf), lambda i, f: (f, 0, 0)),
            pl.BlockSpec((tf, D_MODEL), lambda i, f: (f, 0)),
            _mod_spec(3),
            pl.BlockSpec((1, D_MODEL), lambda i, f: (0, 0)),
        ],
        out_specs=pl.BlockSpec((tm, D_MODEL), lambda i, f: (i, 0)),
        out_shape=jax.ShapeDtypeStruct((SEQ, D_MODEL), F32),
        compiler_params=_params("arbitrary", "arbitrary"),
        name="mlp",
    )(hm, x1, w_up_bf16, w_dn_bf16, mod, final_w)


def kernel(x, c, ctx, c_ctx, norm_attn_w, norm_mlp_w, w_ada, b_ada, w_in, attn_sink,
           pool_w, pool_scale, w_out, w_mlp_up, w_mlp_down, final_norm_w):
    assert x.shape == (1, SEQ, D_MODEL) and ctx.shape == (1, CTX_LEN, D_MODEL)
    assert w_ada.shape[0] == 1, "single layer"
    x2d, ctx2d = x[0], ctx[0]

    c_ctx_row = c_ctx[None, :]
    b_row = b_ada[0][None, :]
    n_early = 2 * D_MODEL
    mod_a = _ada(c, c_ctx_row, w_ada[0], b_row, n_early)

    nw_a = norm_attn_w[0][None, :]
    tables = tuple(jnp.asarray(t) for t in _rope_tables())
    qt, k, vt, u = _inproj(x2d, nw_a, mod_a, w_in[0], tables)
    kc, vtc = _ctx_kv(ctx2d, nw_a, mod_a, w_in[0])

    sink_row = jnp.repeat(attn_sink[0].astype(F32), BLOCK)[None, :]
    attn, w_out_b, w_up_b, w_dn_b, mod_b = _attn(
        qt, k, vt, kc, vtc, sink_row, w_out[0], w_mlp_up[0], w_mlp_down[0],
        c, c_ctx_row, w_ada[0], b_row, n_early)

    x1, hm = _mixout(x2d, attn, u, w_out_b, pool_w[0], pool_scale[0][None, :], mod_b,
                     norm_mlp_w[0][None, :])
    out = _mlp(hm, x1, w_up_b, w_dn_b, mod_b, final_norm_w[None, :])
    return out[None]
```

```python
import numpy as np
import jax
import jax.numpy as jnp
from jax import lax
from jax.experimental import pallas as pl
from jax.experimental.pallas import tpu as pltpu

F32 = jnp.float32
BF16 = jnp.bfloat16

D_MODEL = 2048
SEQ = 8192
CTX_LEN = 256
GRID_W = 64
HEAD_DIM = 64
N_Q_HEADS = 16
N_KV_HEADS = 4
GQA = N_Q_HEADS // N_KV_HEADS
ATTN_WIDTH = N_Q_HEADS * HEAD_DIM
KV_WIDTH = N_KV_HEADS * HEAD_DIM
POOL_WINDOWS = (2, 4, 8, 16)
POOL_WIDTH = D_MODEL - ATTN_WIDTH
POOL_GROUP_DIM = POOL_WIDTH // len(POOL_WINDOWS)
IN_WIDTH = ATTN_WIDTH + 2 * KV_WIDTH + POOL_WIDTH
D_FF = 4 * D_MODEL
BLOCK = 128
ROPE_BASE = 10000.0
N_MOD = 6
EPS = 1e-6
NEG_INF = -1e30
LOG2E = 1.4426950408889634
Q_SCALE = HEAD_DIM ** -0.5 * LOG2E

LANES = 128
SUBLANES = 8
POOL_HALO = 8
LATENT_ROW = slice(0, 1)
CONTEXT_ROW = slice(1, 2)
VMEM_LIMIT = 56 * 1024 * 1024

ADA_TN = 512
INPROJ_TM = 512
ATTN_TQ = 512
CAST_JOB_ELEMS = 64 * 1024
ATTN_STAGE_LAG = (2, 5)
MIX_TM = 512
MIX_CHUNK = 256
MLP_TM = 1024
MLP_TF = 1024
MLP_RC = 512
MLP_NF = D_FF // MLP_TF
MLP_XR = MLP_TM // MLP_NF


def _params(*sem):
    return pltpu.CompilerParams(dimension_semantics=sem, vmem_limit_bytes=VMEM_LIMIT)


def _mod_spec(chunk):
    return pl.BlockSpec((SUBLANES, D_MODEL), lambda *_: (0, chunk))


def _rms_modulate(x, norm_w, shift, scale):
    gain = norm_w * (1.0 + scale)
    y = x * lax.rsqrt(jnp.mean(x * x, axis=-1, keepdims=True) + EPS)
    return y * gain + shift


def _silu_rows(c_ref, cctx_ref):
    cnd = jnp.concatenate([c_ref[...], cctx_ref[...], jnp.zeros((SUBLANES - 2, D_MODEL), F32)], axis=0)
    return cnd * (1.0 / (1.0 + jnp.exp(-cnd)))


def _ada_cols(s, w_ref, b_ref, cols=slice(None)):
    return jnp.dot(s, w_ref[:, cols], preferred_element_type=F32) + b_ref[:, cols]


def _ada_kernel(c_ref, cctx_ref, w_ref, b_ref, o_ref):
    o_ref[...] = _ada_cols(_silu_rows(c_ref, cctx_ref), w_ref, b_ref)


def _ada(c, c_ctx, w_ada, b_ada, n):
    return pl.pallas_call(
        _ada_kernel,
        grid=(n // ADA_TN,),
        in_specs=[
            pl.BlockSpec((1, D_MODEL), lambda j: (0, 0)),
            pl.BlockSpec((1, D_MODEL), lambda j: (0, 0)),
            pl.BlockSpec((D_MODEL, ADA_TN), lambda j: (0, j)),
            pl.BlockSpec((1, ADA_TN), lambda j: (0, j)),
        ],
        out_specs=pl.BlockSpec((SUBLANES, ADA_TN), lambda j: (0, j)),
        out_shape=jax.ShapeDtypeStruct((SUBLANES, n), F32),
        compiler_params=_params("arbitrary"),
        name="ada",
    )(c, c_ctx, w_ada, b_ada)


def _rope_tables():
    half = HEAD_DIM // 2
    inv_freq = ROPE_BASE ** (-np.arange(0, half, 2, dtype=np.float64) / half)
    t = np.arange(SEQ)
    row = (t // GRID_W).astype(np.float64)
    col = (t % GRID_W).astype(np.float64)
    ang_r = row[:, None] * inv_freq[None, :]
    ang_c = col[:, None] * inv_freq[None, :]
    cos_t = Q_SCALE * np.concatenate([np.cos(ang_r), np.cos(ang_c)], axis=1).T
    sin_t = Q_SCALE * np.concatenate([np.sin(ang_r), np.sin(ang_c)], axis=1).T
    cos_l = np.concatenate([np.cos(ang_r), np.cos(ang_r), np.cos(ang_c), np.cos(ang_c)], axis=1)
    zeros = np.zeros_like(ang_r)
    s_up = np.concatenate([-np.sin(ang_r), zeros, -np.sin(ang_c), zeros], axis=1)
    s_dn = np.concatenate([zeros, np.sin(ang_r), zeros, np.sin(ang_c)], axis=1)
    tile2 = lambda a: np.concatenate([a, a], axis=1)
    q_table = np.concatenate([cos_t, sin_t], axis=0).astype(np.float32)
    k_table = np.concatenate([tile2(cos_l), tile2(s_up), tile2(s_dn)], axis=1).astype(np.float32)
    return q_table, k_table


def _inproj_kernel(x_ref, nw_ref, sh_ref, sc_ref, w_ref, qtab_ref, ktab_ref,
                   qt_ref, k_ref, vt_ref, u_ref, wb_ref):
    @pl.when(pl.program_id(0) == 0)
    def _():
        for c0 in range(0, IN_WIDTH, 2 * LANES):
            wb_ref[:, c0:c0 + 2 * LANES] = w_ref[:, c0:c0 + 2 * LANES].astype(BF16)

    h = _rms_modulate(x_ref[...], nw_ref[...], sh_ref[LATENT_ROW], sc_ref[LATENT_ROW]).astype(BF16)
    p = jnp.dot(h, wb_ref[...], preferred_element_type=F32)

    qt = p[:, :ATTN_WIDTH].T
    cr, cc = qtab_ref[0:16, :], qtab_ref[16:32, :]
    sr, sc = qtab_ref[32:48, :], qtab_ref[48:64, :]
    pieces = []
    for hd in range(N_Q_HEADS):
        b = hd * HEAD_DIM
        x1r, x2r = qt[b:b + 16, :], qt[b + 16:b + 32, :]
        x1c, x2c = qt[b + 32:b + 48, :], qt[b + 48:b + 64, :]
        pieces += [x1r * cr - x2r * sr, x1r * sr + x2r * cr,
                   x1c * cc - x2c * sc, x1c * sc + x2c * cc]
    qt_ref[...] = jnp.concatenate(pieces, axis=0).astype(BF16)

    k = p[:, ATTN_WIDTH:ATTN_WIDTH + KV_WIDTH]
    cos_l, s_up, s_dn = (ktab_ref[:, j * LANES:(j + 1) * LANES] for j in range(3))
    kparts = []
    for g in range(KV_WIDTH // LANES):
        kg = k[:, g * LANES:(g + 1) * LANES]
        kparts.append(kg * cos_l + pltpu.roll(kg, LANES - 16, 1) * s_up
                      + pltpu.roll(kg, 16, 1) * s_dn)
    k_ref[...] = jnp.concatenate(kparts, axis=1).astype(BF16)

    vt_ref[...] = p[:, ATTN_WIDTH + KV_WIDTH:ATTN_WIDTH + 2 * KV_WIDTH].T.astype(BF16)
    u_ref[...] = p[:, ATTN_WIDTH + 2 * KV_WIDTH:]


def _inproj(x, norm_w, mod, w_in, tables):
    q_table, k_table = tables
    tm = INPROJ_TM
    row = lambda i: (i, 0)
    colb = lambda i: (0, i)
    fixed = lambda i: (0, 0)
    return pl.pallas_call(
        _inproj_kernel,
        grid=(SEQ // tm,),
        in_specs=[
            pl.BlockSpec((tm, D_MODEL), row),
            pl.BlockSpec((1, D_MODEL), fixed),
            _mod_spec(0),
            _mod_spec(1),
            pl.BlockSpec((D_MODEL, IN_WIDTH), fixed, pipeline_mode=pl.Buffered(1)),
            pl.BlockSpec((q_table.shape[0], tm), colb),
            pl.BlockSpec((tm, k_table.shape[1]), row),
        ],
        out_specs=[
            pl.BlockSpec((ATTN_WIDTH, tm), colb),
            pl.BlockSpec((tm, KV_WIDTH), row),
            pl.BlockSpec((KV_WIDTH, tm), colb),
            pl.BlockSpec((tm, POOL_WIDTH), row),
        ],
        out_shape=[
            jax.ShapeDtypeStruct((ATTN_WIDTH, SEQ), BF16),
            jax.ShapeDtypeStruct((SEQ, KV_WIDTH), BF16),
            jax.ShapeDtypeStruct((KV_WIDTH, SEQ), BF16),
            jax.ShapeDtypeStruct((SEQ, POOL_WIDTH), F32),
        ],
        scratch_shapes=[pltpu.VMEM((D_MODEL, IN_WIDTH), BF16)],
        compiler_params=_params("arbitrary"),
        name="inproj",
    )(x, norm_w, mod, mod, w_in, q_table, k_table)


def _ctx_kv_kernel(x_ref, nw_ref, sh_ref, sc_ref, w_ref, k_ref, vt_ref):
    h = _rms_modulate(x_ref[...], nw_ref[...], sh_ref[CONTEXT_ROW], sc_ref[CONTEXT_ROW]).astype(BF16)
    p = jnp.dot(h, w_ref[...].astype(BF16), preferred_element_type=F32)
    k_ref[...] = p[:, :KV_WIDTH].astype(BF16)
    vt_ref[...] = p[:, KV_WIDTH:].T.astype(BF16)


def _ctx_kv(ctx, norm_w, mod, w_in):
    fixed = lambda i: (0, 0)
    kv_block = ATTN_WIDTH // (2 * KV_WIDTH)
    return pl.pallas_call(
        _ctx_kv_kernel,
        grid=(1,),
        in_specs=[
            pl.BlockSpec((CTX_LEN, D_MODEL), fixed),
            pl.BlockSpec((1, D_MODEL), fixed),
            _mod_spec(0),
            _mod_spec(1),
            pl.BlockSpec((D_MODEL, 2 * KV_WIDTH), lambda i: (0, kv_block)),
        ],
        out_specs=[
            pl.BlockSpec((CTX_LEN, KV_WIDTH), fixed),
            pl.BlockSpec((KV_WIDTH, CTX_LEN), fixed),
        ],
        out_shape=[
            jax.ShapeDtypeStruct((CTX_LEN, KV_WIDTH), BF16),
            jax.ShapeDtypeStruct((KV_WIDTH, CTX_LEN), BF16),
        ],
        compiler_params=_params("arbitrary"),
        name="ctx_kv",
    )(ctx, norm_w, mod, mod, w_in)


def _attn_kernel(qt_ref, kp_ref, km_ref, kn_ref, vtp_ref, vtm_ref, vtn_ref,
                 kc_ref, vtc_ref, sink_ref, wout_ref, wup_ref, wdn_ref,
                 c_ref, cctx_ref, wada_ref, bada_ref,
                 o_ref, wout_b_ref, wup_b_ref, wdn_b_ref, mod_ref):
    cond_act = _silu_rows(c_ref, cctx_ref)
    ada_cols = [slice(c0, c0 + 2 * LANES) for c0 in range(0, mod_ref.shape[1], 2 * LANES)]

    def ada_piece(idx, n):
        for cols in ada_cols[idx * len(ada_cols) // n:(idx + 1) * len(ada_cols) // n]:
            mod_ref[:, cols] = _ada_cols(cond_act, wada_ref, bada_ref, cols)

    i = pl.program_id(0)
    nsub = ATTN_TQ // BLOCK
    nblk = SEQ // BLOCK
    pw = 2 * BLOCK
    nkeys = 3 * BLOCK + CTX_LEN

    key_j = lax.broadcasted_iota(jnp.int32, (BLOCK, pw), 0)
    qry_i = lax.broadcasted_iota(jnp.int32, (BLOCK, pw), 1) % BLOCK
    neg = jnp.full((BLOCK, pw), NEG_INF, F32)
    zero = jnp.zeros((BLOCK, pw), F32)
    band_prev = jnp.where(key_j >= qry_i, zero, neg)
    band_next = jnp.where(key_j <= qry_i, zero, neg)

    kc = kc_ref[...]
    vtc = vtc_ref[...]
    ones_rows = jnp.ones((2 * SUBLANES, nkeys), BF16)
    sink_row = sink_ref[...] * LOG2E

    windows = []
    for sb in range(nsub):
        n = i * nsub + sb
        lo, hi = sb * BLOCK, (sb + 1) * BLOCK
        k_prev = kp_ref[...] if sb == 0 else km_ref[lo - BLOCK:lo, :]
        k_next = kn_ref[...] if sb == nsub - 1 else km_ref[hi:hi + BLOCK, :]
        vt_prev = vtp_ref[...] if sb == 0 else vtm_ref[:, lo - BLOCK:lo]
        vt_next = vtn_ref[...] if sb == nsub - 1 else vtm_ref[:, hi:hi + BLOCK]
        windows.append(dict(
            kwin=jnp.concatenate([k_prev, km_ref[lo:hi, :], k_next, kc], axis=0),
            vtwin=jnp.concatenate([vt_prev, vtm_ref[:, lo:hi], vt_next, vtc], axis=1),
            bias_prev=band_prev + jnp.where(n > 0, 0.0, NEG_INF),
            bias_next=band_next + jnp.where(n < nblk - 1, 0.0, NEG_INF)))

    units = [(sb, h, half) for sb in range(nsub) for h in range(N_KV_HEADS)
             for half in range(GQA // 2)]

    def scores(unit):
        sb, h, half = unit
        win = windows[sb]
        lo, hi = sb * BLOCK, (sb + 1) * BLOCK
        hd0 = GQA * h + 2 * half
        qh = jnp.concatenate([qt_ref[(hd0 + g) * HEAD_DIM:(hd0 + g + 1) * HEAD_DIM, lo:hi]
                              for g in range(2)], axis=1)
        blocks = []
        if h > 0:
            blocks.append(jnp.zeros((h * HEAD_DIM, pw), BF16))
        blocks.append(qh)
        if h < N_KV_HEADS - 1:
            blocks.append(jnp.zeros(((N_KV_HEADS - 1 - h) * HEAD_DIM, pw), BF16))
        s = jnp.dot(win["kwin"], jnp.concatenate(blocks, axis=0),
                    preferred_element_type=F32)
        s = jnp.concatenate([s[0:BLOCK] + win["bias_prev"], s[BLOCK:2 * BLOCK],
                             s[2 * BLOCK:3 * BLOCK] + win["bias_next"], s[3 * BLOCK:]], axis=0)
        return s, jnp.max(s, axis=0, keepdims=True)

    def probs(unit, s, smax):
        sb, h, half = unit
        hd0 = GQA * h + 2 * half
        sink = sink_row[:, hd0 * BLOCK:(hd0 + 2) * BLOCK]
        m = jnp.maximum(smax, sink)
        return jnp.exp2(s - m).astype(BF16), jnp.exp2(sink - m)

    def finish(unit, e, e_sink):
        sb, h, half = unit
        lo, hi = sb * BLOCK, (sb + 1) * BLOCK
        hd0 = GQA * h + 2 * half
        vt_ones = jnp.concatenate(
            [windows[sb]["vtwin"][h * HEAD_DIM:(h + 1) * HEAD_DIM, :], ones_rows], axis=0)
        o = jnp.dot(vt_ones, e, preferred_element_type=F32)
        denom = o[HEAD_DIM:HEAD_DIM + 1, :] + e_sink
        o = o[:HEAD_DIM, :] * (1.0 / denom)
        blk = jnp.concatenate([o[:, 0:BLOCK], o[:, BLOCK:2 * BLOCK]], axis=0)
        o_ref[lo:hi, hd0 * HEAD_DIM:(hd0 + 2) * HEAD_DIM] = blk.T.astype(BF16)

    cast_jobs = []
    for src, dst in ((wout_ref, wout_b_ref), (wup_ref, wup_b_ref), (wdn_ref, wdn_b_ref)):
        wc = max(LANES, CAST_JOB_ELEMS // src.shape[0])
        cast_jobs += [(src, dst, c0) + (wc,) for c0 in range(0, src.shape[1], wc)]

    def cast_piece(idx, n):
        for src, dst, c0, wc in cast_jobs[idx * len(cast_jobs) // n:(idx + 1) * len(cast_jobs) // n]:
            val = src[:, c0:c0 + wc].astype(BF16)
            if len(dst.shape) == 3:
                tf = dst.shape[2]
                dst[c0 // tf, :, c0 % tf:c0 % tf + wc] = val
            else:
                dst[:, c0:c0 + wc] = val

    nu = len(units)
    s_vals, p_vals = {}, {}
    d_probs, d_finish = ATTN_STAGE_LAG
    for step in range(nu + d_finish):
        if step < nu:
            s_vals[step] = scores(units[step])
        if d_probs <= step < nu + d_probs:
            u = step - d_probs
            p_vals[u] = probs(units[u], *s_vals.pop(u))
            cast_piece(u, nu)
            ada_piece(u, nu)
        if step >= d_finish:
            u = step - d_finish
            finish(units[u], *p_vals.pop(u))


def _attn(qt, k, vt, kc, vtc, sink, w_out, w_up, w_dn, c, c_ctx, w_ada, b_ada, ada_col0):
    tq = ATTN_TQ
    r = tq // BLOCK
    nblk = SEQ // BLOCK
    nsteps = SEQ // tq
    ada_tn = (w_ada.shape[1] - ada_col0) // nsteps
    ada_b0 = ada_col0 // ada_tn
    fixed = lambda i: (0, 0)
    row = lambda i: (i, 0)
    slab = lambda w: pl.BlockSpec((w.shape[0] // nsteps, w.shape[1]), row)
    bf16_like = lambda w: jax.ShapeDtypeStruct(w.shape, BF16)
    prev = lambda i: jnp.maximum(i * r - 1, 0)
    nxt = lambda i: jnp.minimum((i + 1) * r, nblk - 1)
    return pl.pallas_call(
        _attn_kernel,
        grid=(SEQ // tq,),
        in_specs=[
            pl.BlockSpec((ATTN_WIDTH, tq), lambda i: (0, i)),
            pl.BlockSpec((BLOCK, KV_WIDTH), lambda i: (prev(i), 0)),
            pl.BlockSpec((tq, KV_WIDTH), lambda i: (i, 0)),
            pl.BlockSpec((BLOCK, KV_WIDTH), lambda i: (nxt(i), 0)),
            pl.BlockSpec((KV_WIDTH, BLOCK), lambda i: (0, prev(i))),
            pl.BlockSpec((KV_WIDTH, tq), lambda i: (0, i)),
            pl.BlockSpec((KV_WIDTH, BLOCK), lambda i: (0, nxt(i))),
            pl.BlockSpec((CTX_LEN, KV_WIDTH), fixed),
            pl.BlockSpec((KV_WIDTH, CTX_LEN), fixed),
            pl.BlockSpec((1, N_Q_HEADS * BLOCK), fixed),
            slab(w_out), slab(w_up), slab(w_dn),
            pl.BlockSpec((1, D_MODEL), fixed),
            pl.BlockSpec((1, D_MODEL), fixed),
            pl.BlockSpec((D_MODEL, ada_tn), lambda i: (0, ada_b0 + i)),
            pl.BlockSpec((1, ada_tn), lambda i: (0, ada_b0 + i)),
        ],
        out_specs=[pl.BlockSpec((tq, ATTN_WIDTH), row), slab(w_out),
                   pl.BlockSpec((MLP_NF, w_up.shape[0] // nsteps, MLP_TF), lambda i: (0, i, 0)),
                   slab(w_dn),
                   pl.BlockSpec((SUBLANES, ada_tn), lambda i: (0, i))],
        out_shape=[jax.ShapeDtypeStruct((SEQ, ATTN_WIDTH), BF16),
                   bf16_like(w_out),
                   jax.ShapeDtypeStruct((MLP_NF, w_up.shape[0], MLP_TF), BF16),
                   bf16_like(w_dn),
                   jax.ShapeDtypeStruct((SUBLANES, w_ada.shape[1] - ada_col0), F32)],
        compiler_params=_params("arbitrary"),
        name="attn",
    )(qt, k, k, k, vt, vt, vt, kc, vtc, sink, w_out, w_up, w_dn, c, c_ctx, w_ada, b_ada)


def _mixout_kernel(x_ref, attn_ref, up_ref, um_ref, un_ref, wout_ref, pw_ref, ps_ref,
                   ga_ref, nw_ref, sh_ref, sc_ref, x1_ref, hm_ref, ubuf):
    i = pl.program_id(0)
    tm = MIX_TM
    nsteps = SEQ // tm
    ubuf[0:POOL_HALO, :] = jnp.where(i > 0, up_ref[...], 0.0)
    ubuf[POOL_HALO:POOL_HALO + tm, :] = um_ref[...]
    ubuf[POOL_HALO + tm:2 * POOL_HALO + tm, :] = jnp.where(i < nsteps - 1, un_ref[...], 0.0)

    rc = MIX_CHUNK
    n = rc + 2 * POOL_HALO
    c = POOL_HALO
    ngroups = len(POOL_WINDOWS)
    ncol = D_MODEL // ngroups

    def pool_group(r0, g):
        w = POOL_WINDOWS[g]
        c0, c1 = g * POOL_GROUP_DIM, (g + 1) * POOL_GROUP_DIM
        e = ubuf[r0:r0 + n, c0:c1]
        f, span = e, 1
        while 2 * span < w:
            f = f + pltpu.roll(f, n - span, 0)
            span *= 2
        win = (pltpu.roll(f, span, 0) + f)[c:c + rc]
        t = i * tm + r0 + lax.broadcasted_iota(jnp.int32, (rc, 1), 0)
        lo = jnp.clip(t - w // 2, 0, SEQ)
        hi = jnp.clip(t - w // 2 + w, 0, SEQ)
        cnt = (hi - lo).astype(F32)
        pg = win / cnt - um_ref[r0:r0 + rc, c0:c1]
        mixed = jnp.dot(pg.astype(BF16), pw_ref[g].astype(BF16), preferred_element_type=F32)
        return (mixed * ps_ref[:, c0:c1]).astype(BF16)

    def prenorm(r0, nrows):
        rows = slice(r0, r0 + nrows)
        hm_ref[rows, :] = _rms_modulate(x1_ref[rows, :], nw_ref[...], sh_ref[LATENT_ROW],
                                        sc_ref[LATENT_ROW]).astype(BF16)

    chunk_starts = list(range(0, tm, rc))
    for ci, r0 in enumerate(chunk_starts):
        rows = slice(r0, r0 + rc)
        attn_rows = attn_ref[rows, :]
        pooled = []
        for j in range(ngroups):
            cols = slice(j * ncol, (j + 1) * ncol)
            x1_ref[rows, cols] = jnp.dot(attn_rows, wout_ref[0:ATTN_WIDTH, cols],
                                         preferred_element_type=F32)
            pooled.append(pool_group(r0, j))
        pool_rows = jnp.concatenate(pooled, axis=1)
        for j in range(ngroups):
            cols = slice(j * ncol, (j + 1) * ncol)
            o = jnp.dot(pool_rows, wout_ref[ATTN_WIDTH:, cols], preferred_element_type=F32)
            x1_ref[rows, cols] = x_ref[rows, cols] + ga_ref[LATENT_ROW, cols] * (x1_ref[rows, cols] + o)
            if ci > 0:
                prenorm(chunk_starts[ci - 1] + j * (rc // ngroups), rc // ngroups)
    for j in range(ngroups):
        prenorm(chunk_starts[-1] + j * (rc // ngroups), rc // ngroups)


def _mixout(x, attn, u, w_out_bf16, pool_w, pool_scale, mod, norm_w):
    tm = MIX_TM
    hb = tm // POOL_HALO
    nhalo = SEQ // POOL_HALO
    row = lambda i: (i, 0)
    fixed = lambda i: (0, 0)
    return pl.pallas_call(
        _mixout_kernel,
        grid=(SEQ // tm,),
        in_specs=[
            pl.BlockSpec((tm, D_MODEL), row),
            pl.BlockSpec((tm, ATTN_WIDTH), row),
            pl.BlockSpec((POOL_HALO, POOL_WIDTH), lambda i: (jnp.maximum(i * hb - 1, 0), 0)),
            pl.BlockSpec((tm, POOL_WIDTH), row),
            pl.BlockSpec((POOL_HALO, POOL_WIDTH), lambda i: (jnp.minimum((i + 1) * hb, nhalo - 1), 0)),
            pl.BlockSpec((D_MODEL, D_MODEL), fixed),
            pl.BlockSpec((len(POOL_WINDOWS), POOL_GROUP_DIM, POOL_GROUP_DIM), lambda i: (0, 0, 0)),
            pl.BlockSpec((1, POOL_WIDTH), fixed),
            _mod_spec(0),
            pl.BlockSpec((1, D_MODEL), fixed),
            _mod_spec(1),
            _mod_spec(2),
        ],
        out_specs=[pl.BlockSpec((tm, D_MODEL), row), pl.BlockSpec((tm, D_MODEL), row)],
        out_shape=[jax.ShapeDtypeStruct((SEQ, D_MODEL), F32),
                   jax.ShapeDtypeStruct((SEQ, D_MODEL), BF16)],
        scratch_shapes=[pltpu.VMEM((tm + 2 * POOL_HALO, POOL_WIDTH), F32)],
        compiler_params=_params("arbitrary"),
        name="mixout",
    )(x, attn, u, u, u, w_out_bf16, pool_w, pool_scale, mod, norm_w, mod, mod)


def _mlp_kernel(hm_ref, x1r_ref, wup_ref, wdn_ref, gm_ref, fw_ref, o_ref):
    f = pl.program_id(1)
    last = MLP_NF - 1
    row_chunks = [slice(r0, r0 + MLP_RC) for r0 in range(0, MLP_TM, MLP_RC)]
    gm = gm_ref[LATENT_ROW]

    def branch_out(rows):
        a = jnp.maximum(jnp.dot(hm_ref[rows, :], wup_ref[...], preferred_element_type=F32), 0.0)
        return gm * jnp.dot((a * a).astype(BF16), wdn_ref[...], preferred_element_type=F32)

    def final_norm(rows):
        x2 = o_ref[rows, :]
        y = x2 * lax.rsqrt(jnp.mean(x2 * x2, axis=-1, keepdims=True) + EPS)
        o_ref[rows, :] = y * fw_ref[...]

    @pl.when(f == 0)
    def _():
        for rows in row_chunks:
            o_ref[rows, :] = branch_out(rows)
        o_ref[0:MLP_XR, :] += x1r_ref[...]

    @pl.when(jnp.logical_and(f > 0, f < last))
    def _():
        x_rows = pl.ds(pl.multiple_of(f * MLP_XR, MLP_XR), MLP_XR)
        o_ref[x_rows, :] += x1r_ref[...]
        for rows in row_chunks:
            o_ref[rows, :] += branch_out(rows)

    @pl.when(f == last)
    def _():
        o_ref[last * MLP_XR:(last + 1) * MLP_XR, :] += x1r_ref[...]
        npiece = 4
        cw, rp = D_MODEL // npiece, MLP_RC // npiece
        for ci, rows in enumerate(row_chunks):
            a = jnp.maximum(jnp.dot(hm_ref[rows, :], wup_ref[...], preferred_element_type=F32), 0.0)
            a = (a * a).astype(BF16)
            for j in range(npiece):
                cols = slice(j * cw, (j + 1) * cw)
                o_ref[rows, cols] += gm[:, cols] * jnp.dot(a, wdn_ref[:, cols],
                                                           preferred_element_type=F32)
                if ci > 0:
                    r0 = row_chunks[ci - 1].start + j * rp
                    final_norm(slice(r0, r0 + rp))
        final_norm(row_chunks[-1])


def _mlp(hm, x1, w_up_bf16, w_dn_bf16, mod, final_w):
    tm, tf = MLP_TM, MLP_TF
    return pl.pallas_call(
        _mlp_kernel,
        grid=(SEQ // tm, MLP_NF),
        in_specs=[
            pl.BlockSpec((tm, D_MODEL), lambda i, f: (i, 0)),
            pl.BlockSpec((MLP_XR, D_MODEL), lambda i, f: (i * MLP_NF + f, 0)),
            pl.BlockSpec((None, D_MODEL, tf), lambda i, f: (f, 0, 0)),
            pl.BlockSpec((tf, D_MODEL), lambda i, f: (f, 0)),
            _mod_spec(3),
            pl.BlockSpec((1, D_MODEL), lambda i, f: (0, 0)),
        ],
        out_specs=pl.BlockSpec((tm, D_MODEL), lambda i, f: (i, 0)),
        out_shape=jax.ShapeDtypeStruct((SEQ, D_MODEL), F32),
        compiler_params=_params("arbitrary", "arbitrary"),
        name="mlp",
    )(hm, x1, w_up_bf16, w_dn_bf16, mod, final_w)


def kernel(x, c, ctx, c_ctx, norm_attn_w, norm_mlp_w, w_ada, b_ada, w_in, attn_sink,
           pool_w, pool_scale, w_out, w_mlp_up, w_mlp_down, final_norm_w):
    assert x.shape == (1, SEQ, D_MODEL) and ctx.shape == (1, CTX_LEN, D_MODEL)
    assert w_ada.shape[0] == 1, "single layer"
    x2d, ctx2d = x[0], ctx[0]

    c_ctx_row = c_ctx[None, :]
    b_row = b_ada[0][None, :]
    n_early = 2 * D_MODEL
    mod_a = _ada(c, c_ctx_row, w_ada[0], b_row, n_early)

    nw_a = norm_attn_w[0][None, :]
    tables = tuple(jnp.asarray(t) for t in _rope_tables())
    qt, k, vt, u = _inproj(x2d, nw_a, mod_a, w_in[0], tables)
    kc, vtc = _ctx_kv(ctx2d, nw_a, mod_a, w_in[0])

    sink_row = jnp.repeat(attn_sink[0].astype(F32), BLOCK)[None, :]
    attn, w_out_b, w_up_b, w_dn_b, mod_b = _attn(
        qt, k, vt, kc, vtc, sink_row, w_out[0], w_mlp_up[0], w_mlp_down[0],
        c, c_ctx_row, w_ada[0], b_row, n_early)

    x1, hm = _mixout(x2d, attn, u, w_out_b, pool_w[0], pool_scale[0][None, :], mod_b,
                     norm_mlp_w[0][None, :])
    out = _mlp(hm, x1, w_up_b, w_dn_b, mod_b, final_norm_w[None, :])
    return out[None]
```

```python
import numpy as np
import jax
import jax.numpy as jnp
from jax import lax
from jax.experimental import pallas as pl
from jax.experimental.pallas import tpu as pltpu

F32 = jnp.float32
BF16 = jnp.bfloat16

D_MODEL = 2048
SEQ = 8192
CTX_LEN = 256
GRID_W = 64
HEAD_DIM = 64
N_Q_HEADS = 16
N_KV_HEADS = 4
GQA = N_Q_HEADS // N_KV_HEADS
ATTN_WIDTH = N_Q_HEADS * HEAD_DIM
KV_WIDTH = N_KV_HEADS * HEAD_DIM
POOL_WINDOWS = (2, 4, 8, 16)
POOL_WIDTH = D_MODEL - ATTN_WIDTH
POOL_GROUP_DIM = POOL_WIDTH // len(POOL_WINDOWS)
IN_WIDTH = ATTN_WIDTH + 2 * KV_WIDTH + POOL_WIDTH
D_FF = 4 * D_MODEL
BLOCK = 128
ROPE_BASE = 10000.0
N_MOD = 6
EPS = 1e-6
NEG_INF = -1e30
LOG2E = 1.4426950408889634
Q_SCALE = HEAD_DIM ** -0.5 * LOG2E

LANES = 128
SUBLANES = 8
POOL_HALO = 8
LATENT_ROW = slice(0, 1)
CONTEXT_ROW = slice(1, 2)
VMEM_LIMIT = 56 * 1024 * 1024

ADA_TN = 512
INPROJ_TM = 512
ATTN_TQ = 512
CAST_JOB_ELEMS = 64 * 1024
ATTN_STAGE_LAG = (2, 5)
MIX_TM = 512
MIX_CHUNK = 256
MLP_TM = 1024
MLP_TF = 1024
MLP_RC = 512
MLP_NF = D_FF // MLP_TF
MLP_XR = MLP_TM // MLP_NF


def _params(*sem):
    return pltpu.CompilerParams(dimension_semantics=sem, vmem_limit_bytes=VMEM_LIMIT)


def _mod_spec(chunk):
    return pl.BlockSpec((SUBLANES, D_MODEL), lambda *_: (0, chunk))


def _rms_modulate(x, norm_w, shift, scale):
    gain = norm_w * (1.0 + scale)
    y = x * lax.rsqrt(jnp.mean(x * x, axis=-1, keepdims=True) + EPS)
    return y * gain + shift


def _silu_rows(c_ref, cctx_ref):
    cnd = jnp.concatenate([c_ref[...], cctx_ref[...], jnp.zeros((SUBLANES - 2, D_MODEL), F32)], axis=0)
    return cnd * (1.0 / (1.0 + jnp.exp(-cnd)))


def _ada_cols(s, w_ref, b_ref, cols=slice(None)):
    return jnp.dot(s, w_ref[:, cols], preferred_element_type=F32) + b_ref[:, cols]


def _ada_kernel(c_ref, cctx_ref, w_ref, b_ref, o_ref):
    o_ref[...] = _ada_cols(_silu_rows(c_ref, cctx_ref), w_ref, b_ref)


def _ada(c, c_ctx, w_ada, b_ada, n):
    return pl.pallas_call(
        _ada_kernel,
        grid=(n // ADA_TN,),
        in_specs=[
            pl.BlockSpec((1, D_MODEL), lambda j: (0, 0)),
            pl.BlockSpec((1, D_MODEL), lambda j: (0, 0)),
            pl.BlockSpec((D_MODEL, ADA_TN), lambda j: (0, j)),
            pl.BlockSpec((1, ADA_TN), lambda j: (0, j)),
        ],
        out_specs=pl.BlockSpec((SUBLANES, ADA_TN), lambda j: (0, j)),
        out_shape=jax.ShapeDtypeStruct((SUBLANES, n), F32),
        compiler_params=_params("arbitrary"),
        name="ada",
    )(c, c_ctx, w_ada, b_ada)


def _rope_tables():
    half = HEAD_DIM // 2
    inv_freq = ROPE_BASE ** (-np.arange(0, half, 2, dtype=np.float64) / half)
    t = np.arange(SEQ)
    row = (t // GRID_W).astype(np.float64)
    col = (t % GRID_W).astype(np.float64)
    ang_r = row[:, None] * inv_freq[None, :]
    ang_c = col[:, None] * inv_freq[None, :]
    cos_t = Q_SCALE * np.concatenate([np.cos(ang_r), np.cos(ang_c)], axis=1).T
    sin_t = Q_SCALE * np.concatenate([np.sin(ang_r), np.sin(ang_c)], axis=1).T
    cos_l = np.concatenate([np.cos(ang_r), np.cos(ang_r), np.cos(ang_c), np.cos(ang_c)], axis=1)
    zeros = np.zeros_like(ang_r)
    s_up = np.concatenate([-np.sin(ang_r), zeros, -np.sin(ang_c), zeros], axis=1)
    s_dn = np.concatenate([zeros, np.sin(ang_r), zeros, np.sin(ang_c)], axis=1)
    tile2 = lambda a: np.concatenate([a, a], axis=1)
    q_table = np.concatenate([cos_t, sin_t], axis=0).astype(np.float32)
    k_table = np.concatenate([tile2(cos_l), tile2(s_up), tile2(s_dn)], axis=1).astype(np.float32)
    return q_table, k_table


def _inproj_kernel(x_ref, nw_ref, sh_ref, sc_ref, w_ref, qtab_ref, ktab_ref,
                   qt_ref, k_ref, vt_ref, u_ref, wb_ref):
    @pl.when(pl.program_id(0) == 0)
    def _():
        for c0 in range(0, IN_WIDTH, 2 * LANES):
            wb_ref[:, c0:c0 + 2 * LANES] = w_ref[:, c0:c0 + 2 * LANES].astype(BF16)

    h = _rms_modulate(x_ref[...], nw_ref[...], sh_ref[LATENT_ROW], sc_ref[LATENT_ROW]).astype(BF16)
    p = jnp.dot(h, wb_ref[...], preferred_element_type=F32)

    qt = p[:, :ATTN_WIDTH].T
    cr, cc = qtab_ref[0:16, :], qtab_ref[16:32, :]
    sr, sc = qtab_ref[32:48, :], qtab_ref[48:64, :]
    pieces = []
    for hd in range(N_Q_HEADS):
        b = hd * HEAD_DIM
        x1r, x2r = qt[b:b + 16, :], qt[b + 16:b + 32, :]
        x1c, x2c = qt[b + 32:b + 48, :], qt[b + 48:b + 64, :]
        pieces += [x1r * cr - x2r * sr, x1r * sr + x2r * cr,
                   x1c * cc - x2c * sc, x1c * sc + x2c * cc]
    qt_ref[...] = jnp.concatenate(pieces, axis=0).astype(BF16)

    k = p[:, ATTN_WIDTH:ATTN_WIDTH + KV_WIDTH]
    cos_l, s_up, s_dn = (ktab_ref[:, j * LANES:(j + 1) * LANES] for j in range(3))
    kparts = []
    for g in range(KV_WIDTH // LANES):
        kg = k[:, g * LANES:(g + 1) * LANES]
        kparts.append(kg * cos_l + pltpu.roll(kg, LANES - 16, 1) * s_up
                      + pltpu.roll(kg, 16, 1) * s_dn)
    k_ref[...] = jnp.concatenate(kparts, axis=1).astype(BF16)

    vt_ref[...] = p[:, ATTN_WIDTH + KV_WIDTH:ATTN_WIDTH + 2 * KV_WIDTH].T.astype(BF16)
    u_ref[...] = p[:, ATTN_WIDTH + 2 * KV_WIDTH:]


def _inproj(x, norm_w, mod, w_in, tables):
    q_table, k_table = tables
    tm = INPROJ_TM
    row = lambda i: (i, 0)
    colb = lambda i: (0, i)
    fixed = lambda i: (0, 0)
    return pl.pallas_call(
        _inproj_kernel,
        grid=(SEQ // tm,),
        in_specs=[
            pl.BlockSpec((tm, D_MODEL), row),
            pl.BlockSpec((1, D_MODEL), fixed),
            _mod_spec(0),
            _mod_spec(1),
            pl.BlockSpec((D_MODEL, IN_WIDTH), fixed, pipeline_mode=pl.Buffered(1)),
            pl.BlockSpec((q_table.shape[0], tm), colb),
            pl.BlockSpec((tm, k_table.shape[1]), row),
        ],
        out_specs=[
            pl.BlockSpec((ATTN_WIDTH, tm), colb),
            pl.BlockSpec((tm, KV_WIDTH), row),
            pl.BlockSpec((KV_WIDTH, tm), colb),
            pl.BlockSpec((tm, POOL_WIDTH), row),
        ],
        out_shape=[
            jax.ShapeDtypeStruct((ATTN_WIDTH, SEQ), BF16),
            jax.ShapeDtypeStruct((SEQ, KV_WIDTH), BF16),
            jax.ShapeDtypeStruct((KV_WIDTH, SEQ), BF16),
            jax.ShapeDtypeStruct((SEQ, POOL_WIDTH), F32),
        ],
        scratch_shapes=[pltpu.VMEM((D_MODEL, IN_WIDTH), BF16)],
        compiler_params=_params("arbitrary"),
        name="inproj",
    )(x, norm_w, mod, mod, w_in, q_table, k_table)


def _ctx_kv_kernel(x_ref, nw_ref, sh_ref, sc_ref, w_ref, k_ref, vt_ref):
    h = _rms_modulate(x_ref[...], nw_ref[...], sh_ref[CONTEXT_ROW], sc_ref[CONTEXT_ROW]).astype(BF16)
    p = jnp.dot(h, w_ref[...].astype(BF16), preferred_element_type=F32)
    k_ref[...] = p[:, :KV_WIDTH].astype(BF16)
    vt_ref[...] = p[:, KV_WIDTH:].T.astype(BF16)


def _ctx_kv(ctx, norm_w, mod, w_in):
    fixed = lambda i: (0, 0)
    kv_block = ATTN_WIDTH // (2 * KV_WIDTH)
    return pl.pallas_call(
        _ctx_kv_kernel,
        grid=(1,),
        in_specs=[
            pl.BlockSpec((CTX_LEN, D_MODEL), fixed),
            pl.BlockSpec((1, D_MODEL), fixed),
            _mod_spec(0),
            _mod_spec(1),
            pl.BlockSpec((D_MODEL, 2 * KV_WIDTH), lambda i: (0, kv_block)),
        ],
        out_specs=[
            pl.BlockSpec((CTX_LEN, KV_WIDTH), fixed),
            pl.BlockSpec((KV_WIDTH, CTX_LEN), fixed),
        ],
        out_shape=[
            jax.ShapeDtypeStruct((CTX_LEN, KV_WIDTH), BF16),
            jax.ShapeDtypeStruct((KV_WIDTH, CTX_LEN), BF16),
        ],
        compiler_params=_params("arbitrary"),
        name="ctx_kv",
    )(ctx, norm_w, mod, mod, w_in)


def _attn_kernel(qt_ref, kp_ref, km_ref, kn_ref, vtp_ref, vtm_ref, vtn_ref,
                 kc_ref, vtc_ref, sink_ref, wout_ref, wup_ref, wdn_ref,
                 c_ref, cctx_ref, wada_ref, bada_ref,
                 o_ref, wout_b_ref, wup_b_ref, wdn_b_ref, mod_ref):
    cond_act = _silu_rows(c_ref, cctx_ref)
    ada_cols = [slice(c0, c0 + 2 * LANES) for c0 in range(0, mod_ref.shape[1], 2 * LANES)]

    def ada_piece(idx, n):
        for cols in ada_cols[idx * len(ada_cols) // n:(idx + 1) * len(ada_cols) // n]:
            mod_ref[:, cols] = _ada_cols(cond_act, wada_ref, bada_ref, cols)

    i = pl.program_id(0)
    nsub = ATTN_TQ // BLOCK
    nblk = SEQ // BLOCK
    pw = 2 * BLOCK
    nkeys = 3 * BLOCK + CTX_LEN

    key_j = lax.broadcasted_iota(jnp.int32, (BLOCK, pw), 0)
    qry_i = lax.broadcasted_iota(jnp.int32, (BLOCK, pw), 1) % BLOCK
    neg = jnp.full((BLOCK, pw), NEG_INF, F32)
    zero = jnp.zeros((BLOCK, pw), F32)
    band_prev = jnp.where(key_j >= qry_i, zero, neg)
    band_next = jnp.where(key_j <= qry_i, zero, neg)

    kc = kc_ref[...]
    vtc = vtc_ref[...]
    ones_rows = jnp.ones((2 * SUBLANES, nkeys), BF16)
    sink_row = sink_ref[...] * LOG2E

    windows = []
    for sb in range(nsub):
        n = i * nsub + sb
        lo, hi = sb * BLOCK, (sb + 1) * BLOCK
        k_prev = kp_ref[...] if sb == 0 else km_ref[lo - BLOCK:lo, :]
        k_next = kn_ref[...] if sb == nsub - 1 else km_ref[hi:hi + BLOCK, :]
        vt_prev = vtp_ref[...] if sb == 0 else vtm_ref[:, lo - BLOCK:lo]
        vt_next = vtn_ref[...] if sb == nsub - 1 else vtm_ref[:, hi:hi + BLOCK]
        windows.append(dict(
            kwin=jnp.concatenate([k_prev, km_ref[lo:hi, :], k_next, kc], axis=0),
            vtwin=jnp.concatenate([vt_prev, vtm_ref[:, lo:hi], vt_next, vtc], axis=1),
            bias_prev=band_prev + jnp.where(n > 0, 0.0, NEG_INF),
            bias_next=band_next + jnp.where(n < nblk - 1, 0.0, NEG_INF)))

    units = [(sb, h, half) for sb in range(nsub) for h in range(N_KV_HEADS)
             for half in range(GQA // 2)]

    def scores(unit):
        sb, h, half = unit
        win = windows[sb]
        lo, hi = sb * BLOCK, (sb + 1) * BLOCK
        hd0 = GQA * h + 2 * half
        qh = jnp.concatenate([qt_ref[(hd0 + g) * HEAD_DIM:(hd0 + g + 1) * HEAD_DIM, lo:hi]
                              for g in range(2)], axis=1)
        blocks = []
        if h > 0:
            blocks.append(jnp.zeros((h * HEAD_DIM, pw), BF16))
        blocks.append(qh)
        if h < N_KV_HEADS - 1:
            blocks.append(jnp.zeros(((N_KV_HEADS - 1 - h) * HEAD_DIM, pw), BF16))
        s = jnp.dot(win["kwin"], jnp.concatenate(blocks, axis=0),
                    preferred_element_type=F32)
        s = jnp.concatenate([s[0:BLOCK] + win["bias_prev"], s[BLOCK:2 * BLOCK],
                             s[2 * BLOCK:3 * BLOCK] + win["bias_next"], s[3 * BLOCK:]], axis=0)
        return s, jnp.max(s, axis=0, keepdims=True)

    def probs(unit, s, smax):
        sb, h, half = unit
        hd0 = GQA * h + 2 * half
        sink = sink_row[:, hd0 * BLOCK:(hd0 + 2) * BLOCK]
        m = jnp.maximum(smax, sink)
        return jnp.exp2(s - m).astype(BF16), jnp.exp2(sink - m)

    def finish(unit, e, e_sink):
        sb, h, half = unit
        lo, hi = sb * BLOCK, (sb + 1) * BLOCK
        hd0 = GQA * h + 2 * half
        vt_ones = jnp.concatenate(
            [windows[sb]["vtwin"][h * HEAD_DIM:(h + 1) * HEAD_DIM, :], ones_rows], axis=0)
        o = jnp.dot(vt_ones, e, preferred_element_type=F32)
        denom = o[HEAD_DIM:HEAD_DIM + 1, :] + e_sink
        o = o[:HEAD_DIM, :] * (1.0 / denom)
        blk = jnp.concatenate([o[:, 0:BLOCK], o[:, BLOCK:2 * BLOCK]], axis=0)
        o_ref[lo:hi, hd0 * HEAD_DIM:(hd0 + 2) * HEAD_DIM] = blk.T.astype(BF16)

    cast_jobs = []
    for src, dst in ((wout_ref, wout_b_ref), (wup_ref, wup_b_ref), (wdn_ref, wdn_b_ref)):
        wc = max(LANES, CAST_JOB_ELEMS // src.shape[0])
        cast_jobs += [(src, dst, c0) + (wc,) for c0 in range(0, src.shape[1], wc)]

    def cast_piece(idx, n):
        for src, dst, c0, wc in cast_jobs[idx * len(cast_jobs) // n:(idx + 1) * len(cast_jobs) // n]:
            val = src[:, c0:c0 + wc].astype(BF16)
            if len(dst.shape) == 3:
                tf = dst.shape[2]
                dst[c0 // tf, :, c0 % tf:c0 % tf + wc] = val
            else:
                dst[:, c0:c0 + wc] = val

    nu = len(units)
    s_vals, p_vals = {}, {}
    d_probs, d_finish = ATTN_STAGE_LAG
    for step in range(nu + d_finish):
        if step < nu:
            s_vals[step] = scores(units[step])
        if d_probs <= step < nu + d_probs:
            u = step - d_probs
            p_vals[u] = probs(units[u], *s_vals.pop(u))
            cast_piece(u, nu)
            ada_piece(u, nu)
        if step >= d_finish:
            u = step - d_finish
            finish(units[u], *p_vals.pop(u))


def _attn(qt, k, vt, kc, vtc, sink, w_out, w_up, w_dn, c, c_ctx, w_ada, b_ada, ada_col0):
    tq = ATTN_TQ
    r = tq // BLOCK
    nblk = SEQ // BLOCK
    nsteps = SEQ // tq
    ada_tn = (w_ada.shape[1] - ada_col0) // nsteps
    ada_b0 = ada_col0 // ada_tn
    fixed = lambda i: (0, 0)
    row = lambda i: (i, 0)
    slab = lambda w: pl.BlockSpec((w.shape[0] // nsteps, w.shape[1]), row)
    bf16_like = lambda w: jax.ShapeDtypeStruct(w.shape, BF16)
    prev = lambda i: jnp.maximum(i * r - 1, 0)
    nxt = lambda i: jnp.minimum((i + 1) * r, nblk - 1)
    return pl.pallas_call(
        _attn_kernel,
        grid=(SEQ // tq,),
        in_specs=[
            pl.BlockSpec((ATTN_WIDTH, tq), lambda i: (0, i)),
            pl.BlockSpec((BLOCK, KV_WIDTH), lambda i: (prev(i), 0)),
            pl.BlockSpec((tq, KV_WIDTH), lambda i: (i, 0)),
            pl.BlockSpec((BLOCK, KV_WIDTH), lambda i: (nxt(i), 0)),
            pl.BlockSpec((KV_WIDTH, BLOCK), lambda i: (0, prev(i))),
            pl.BlockSpec((KV_WIDTH, tq), lambda i: (0, i)),
            pl.BlockSpec((KV_WIDTH, BLOCK), lambda i: (0, nxt(i))),
            pl.BlockSpec((CTX_LEN, KV_WIDTH), fixed),
            pl.BlockSpec((KV_WIDTH, CTX_LEN), fixed),
            pl.BlockSpec((1, N_Q_HEADS * BLOCK), fixed),
            slab(w_out), slab(w_up), slab(w_dn),
            pl.BlockSpec((1, D_MODEL), fixed),
            pl.BlockSpec((1, D_MODEL), fixed),
            pl.BlockSpec((D_MODEL, ada_tn), lambda i: (0, ada_b0 + i)),
            pl.BlockSpec((1, ada_tn), lambda i: (0, ada_b0 + i)),
        ],
        out_specs=[pl.BlockSpec((tq, ATTN_WIDTH), row), slab(w_out),
                   pl.BlockSpec((MLP_NF, w_up.shape[0] // nsteps, MLP_TF), lambda i: (0, i, 0)),
                   slab(w_dn),
                   pl.BlockSpec((SUBLANES, ada_tn), lambda i: (0, i))],
        out_shape=[jax.ShapeDtypeStruct((SEQ, ATTN_WIDTH), BF16),
                   bf16_like(w_out),
                   jax.ShapeDtypeStruct((MLP_NF, w_up.shape[0], MLP_TF), BF16),
                   bf16_like(w_dn),
                   jax.ShapeDtypeStruct((SUBLANES, w_ada.shape[1] - ada_col0), F32)],
        compiler_params=_params("arbitrary"),
        name="attn",
    )(qt, k, k, k, vt, vt, vt, kc, vtc, sink, w_out, w_up, w_dn, c, c_ctx, w_ada, b_ada)


def _mixout_kernel(x_ref, attn_ref, up_ref, um_ref, un_ref, wout_ref, pw_ref, ps_ref,
                   ga_ref, nw_ref, sh_ref, sc_ref, x1_ref, hm_ref):
    i = pl.program_id(0)
    tm = MIX_TM
    nsteps = SEQ // tm

    def u_rows(r0, c0, c1):
        parts = []
        if r0 == 0:
            parts.append(jnp.where(i > 0, up_ref[:, c0:c1], 0.0))
        lo, hi = max(r0 - POOL_HALO, 0), min(r0 + rc + POOL_HALO, tm)
        parts.append(um_ref[lo:hi, c0:c1])
        if r0 + rc == tm:
            parts.append(jnp.where(i < nsteps - 1, un_ref[:, c0:c1], 0.0))
        return jnp.concatenate(parts, axis=0)

    rc = MIX_CHUNK
    n = rc + 2 * POOL_HALO
    c = POOL_HALO
    ngroups = len(POOL_WINDOWS)
    ncol = D_MODEL // ngroups

    def pool_group(r0, g):
        w = POOL_WINDOWS[g]
        c0, c1 = g * POOL_GROUP_DIM, (g + 1) * POOL_GROUP_DIM
        e = u_rows(r0, c0, c1)
        f, span = e, 1
        while 2 * span < w:
            f = f + pltpu.roll(f, n - span, 0)
            span *= 2
        win = (pltpu.roll(f, span, 0) + f)[c:c + rc]
        t = i * tm + r0 + lax.broadcasted_iota(jnp.int32, (rc, 1), 0)
        lo = jnp.clip(t - w // 2, 0, SEQ)
        hi = jnp.clip(t - w // 2 + w, 0, SEQ)
        cnt = (hi - lo).astype(F32)
        pg = win / cnt - um_ref[r0:r0 + rc, c0:c1]
        mixed = jnp.dot(pg.astype(BF16), pw_ref[g].astype(BF16), preferred_element_type=F32)
        return (mixed * ps_ref[:, c0:c1]).astype(BF16)

    def prenorm(r0, nrows):
        rows = slice(r0, r0 + nrows)
        hm_ref[rows, :] = _rms_modulate(x1_ref[rows, :], nw_ref[...], sh_ref[LATENT_ROW],
                                        sc_ref[LATENT_ROW]).astype(BF16)

    chunk_starts = list(range(0, tm, rc))
    for ci, r0 in enumerate(chunk_starts):
        rows = slice(r0, r0 + rc)
        attn_rows = attn_ref[rows, :]
        pooled = []
        for j in range(ngroups):
            cols = slice(j * ncol, (j + 1) * ncol)
            x1_ref[rows, cols] = jnp.dot(attn_rows, wout_ref[0:ATTN_WIDTH, cols],
                                         preferred_element_type=F32)
            pooled.append(pool_group(r0, j))
        pool_rows = jnp.concatenate(pooled, axis=1)
        for j in range(ngroups):
            cols = slice(j * ncol, (j + 1) * ncol)
            o = jnp.dot(pool_rows, wout_ref[ATTN_WIDTH:, cols], preferred_element_type=F32)
            x1_ref[rows, cols] = x_ref[rows, cols] + ga_ref[LATENT_ROW, cols] * (x1_ref[rows, cols] + o)
            if ci > 0:
                prenorm(chunk_starts[ci - 1] + j * (rc // ngroups), rc // ngroups)
    for j in range(ngroups):
        prenorm(chunk_starts[-1] + j * (rc // ngroups), rc // ngroups)


def _mixout(x, attn, u, w_out_bf16, pool_w, pool_scale, mod, norm_w):
    tm = MIX_TM
    hb = tm // POOL_HALO
    nhalo = SEQ // POOL_HALO
    row = lambda i: (i, 0)
    fixed = lambda i: (0, 0)
    return pl.pallas_call(
        _mixout_kernel,
        grid=(SEQ // tm,),
        in_specs=[
            pl.BlockSpec((tm, D_MODEL), row),
            pl.BlockSpec((tm, ATTN_WIDTH), row),
            pl.BlockSpec((POOL_HALO, POOL_WIDTH), lambda i: (jnp.maximum(i * hb - 1, 0), 0)),
            pl.BlockSpec((tm, POOL_WIDTH), row),
            pl.BlockSpec((POOL_HALO, POOL_WIDTH), lambda i: (jnp.minimum((i + 1) * hb, nhalo - 1), 0)),
            pl.BlockSpec((D_MODEL, D_MODEL), fixed),
            pl.BlockSpec((len(POOL_WINDOWS), POOL_GROUP_DIM, POOL_GROUP_DIM), lambda i: (0, 0, 0)),
            pl.BlockSpec((1, POOL_WIDTH), fixed),
            _mod_spec(0),
            pl.BlockSpec((1, D_MODEL), fixed),
            _mod_spec(1),
            _mod_spec(2),
        ],
        out_specs=[pl.BlockSpec((tm, D_MODEL), row), pl.BlockSpec((tm, D_MODEL), row)],
        out_shape=[jax.ShapeDtypeStruct((SEQ, D_MODEL), F32),
                   jax.ShapeDtypeStruct((SEQ, D_MODEL), BF16)],
        compiler_params=_params("arbitrary"),
        name="mixout",
    )(x, attn, u, u, u, w_out_bf16, pool_w, pool_scale, mod, norm_w, mod, mod)


def _mlp_kernel(hm_ref, x1r_ref, wup_ref, wdn_ref, gm_ref, fw_ref, o_ref):
    f = pl.program_id(1)
    last = MLP_NF - 1
    row_chunks = [slice(r0, r0 + MLP_RC) for r0 in range(0, MLP_TM, MLP_RC)]
    gm = gm_ref[LATENT_ROW]

    def branch_out(rows):
        a = jnp.maximum(jnp.dot(hm_ref[rows, :], wup_ref[...], preferred_element_type=F32), 0.0)
        return gm * jnp.dot((a * a).astype(BF16), wdn_ref[...], preferred_element_type=F32)

    def final_norm(rows):
        x2 = o_ref[rows, :]
        y = x2 * lax.rsqrt(jnp.mean(x2 * x2, axis=-1, keepdims=True) + EPS)
        o_ref[rows, :] = y * fw_ref[...]

    @pl.when(f == 0)
    def _():
        for rows in row_chunks:
            o_ref[rows, :] = branch_out(rows)
        o_ref[0:MLP_XR, :] += x1r_ref[...]

    @pl.when(jnp.logical_and(f > 0, f < last))
    def _():
        x_rows = pl.ds(pl.multiple_of(f * MLP_XR, MLP_XR), MLP_XR)
        o_ref[x_rows, :] += x1r_ref[...]
        for rows in row_chunks:
            o_ref[rows, :] += branch_out(rows)

    @pl.when(f == last)
    def _():
        o_ref[last * MLP_XR:(last + 1) * MLP_XR, :] += x1r_ref[...]
        npiece = 4
        cw, rp = D_MODEL // npiece, MLP_RC // npiece
        for ci, rows in enumerate(row_chunks):
            a = jnp.maximum(jnp.dot(hm_ref[rows, :], wup_ref[...], preferred_element_type=F32), 0.0)
            a = (a * a).astype(BF16)
            for j in range(npiece):
                cols = slice(j * cw, (j + 1) * cw)
                o_ref[rows, cols] += gm[:, cols] * jnp.dot(a, wdn_ref[:, cols],
                                                           preferred_element_type=F32)
                if ci > 0:
                    r0 = row_chunks[ci - 1].start + j * rp
                    final_norm(slice(r0, r0 + rp))
        final_norm(row_chunks[-1])


def _mlp(hm, x1, w_up_bf16, w_dn_bf16, mod, final_w):
    tm, tf = MLP_TM, MLP_TF
    return pl.pallas_call(
        _mlp_kernel,
        grid=(SEQ // tm, MLP_NF),
        in_specs=[
            pl.BlockSpec((tm, D_MODEL), lambda i, f: (i, 0)),
            pl.BlockSpec((MLP_XR, D_MODEL), lambda i, f: (i * MLP_NF + f, 0)),
            pl.BlockSpec((None, D_MODEL, tf), lambda i, f: (f, 0, 0)),
            pl.BlockSpec((tf, D_MODEL), lambda i, f: (f, 0)),
            _mod_spec(3),
            pl.BlockSpec((1, D_MODEL), lambda i, f: (0, 0)),
        ],
        out_specs=pl.BlockSpec((tm, D_MODEL), lambda i, f: (i, 0)),
        out_shape=jax.ShapeDtypeStruct((SEQ, D_MODEL), F32),
        compiler_params=_params("arbitrary", "arbitrary"),
        name="mlp",
    )(hm, x1, w_up_bf16, w_dn_bf16, mod, final_w)


def kernel(x, c, ctx, c_ctx, norm_attn_w, norm_mlp_w, w_ada, b_ada, w_in, attn_sink,
           pool_w, pool_scale, w_out, w_mlp_up, w_mlp_down, final_norm_w):
    assert x.shape == (1, SEQ, D_MODEL) and ctx.shape == (1, CTX_LEN, D_MODEL)
    assert w_ada.shape[0] == 1, "single layer"
    x2d, ctx2d = x[0], ctx[0]

    c_ctx_row = c_ctx[None, :]
    b_row = b_ada[0][None, :]
    n_early = 2 * D_MODEL
    mod_a = _ada(c, c_ctx_row, w_ada[0], b_row, n_early)

    nw_a = norm_attn_w[0][None, :]
    tables = tuple(jnp.asarray(t) for t in _rope_tables())
    qt, k, vt, u = _inproj(x2d, nw_a, mod_a, w_in[0], tables)
    kc, vtc = _ctx_kv(ctx2d, nw_a, mod_a, w_in[0])

    sink_row = jnp.repeat(attn_sink[0].astype(F32), BLOCK)[None, :]
    attn, w_out_b, w_up_b, w_dn_b, mod_b = _attn(
        qt, k, vt, kc, vtc, sink_row, w_out[0], w_mlp_up[0], w_mlp_down[0],
        c, c_ctx_row, w_ada[0], b_row, n_early)

    x1, hm = _mixout(x2d, attn, u, w_out_b, pool_w[0], pool_scale[0][None, :], mod_b,
                     norm_mlp_w[0][None, :])
    out = _mlp(hm, x1, w_up_b, w_dn_b, mod_b, final_norm_w[None, :])
    return out[None]
```

```python
import numpy as np
import jax
import jax.numpy as jnp
from jax import lax
from jax.experimental import pallas as pl
from jax.experimental.pallas import tpu as pltpu

F32 = jnp.float32
BF16 = jnp.bfloat16

D_MODEL = 2048
SEQ = 8192
CTX_LEN = 256
GRID_W = 64
HEAD_DIM = 64
N_Q_HEADS = 16
N_KV_HEADS = 4
GQA = N_Q_HEADS // N_KV_HEADS
ATTN_WIDTH = N_Q_HEADS * HEAD_DIM
KV_WIDTH = N_KV_HEADS * HEAD_DIM
POOL_WINDOWS = (2, 4, 8, 16)
POOL_WIDTH = D_MODEL - ATTN_WIDTH
POOL_GROUP_DIM = POOL_WIDTH // len(POOL_WINDOWS)
IN_WIDTH = ATTN_WIDTH + 2 * KV_WIDTH + POOL_WIDTH
D_FF = 4 * D_MODEL
BLOCK = 128
ROPE_BASE = 10000.0
N_MOD = 6
EPS = 1e-6
NEG_INF = -1e30
LOG2E = 1.4426950408889634
Q_SCALE = HEAD_DIM ** -0.5 * LOG2E

LANES = 128
SUBLANES = 8
POOL_HALO = 8
LATENT_ROW = slice(0, 1)
CONTEXT_ROW = slice(1, 2)
VMEM_LIMIT = 56 * 1024 * 1024

ADA_TN = 512
INPROJ_TM = 512
ATTN_TQ = 512
CAST_JOB_ELEMS = 64 * 1024
ATTN_STAGE_LAG = (2, 5)
MIX_TM = 512
MIX_CHUNK = 256
MLP_TM = 1024
MLP_TF = 1024
MLP_RC = 512
MLP_NF = D_FF // MLP_TF
MLP_XR = MLP_TM // MLP_NF


def _params(*sem):
    return pltpu.CompilerParams(dimension_semantics=sem, vmem_limit_bytes=VMEM_LIMIT)


def _mod_spec(chunk):
    return pl.BlockSpec((SUBLANES, D_MODEL), lambda *_: (0, chunk))


def _rms_modulate(x, norm_w, shift, scale):
    gain = norm_w * (1.0 + scale)
    y = x * lax.rsqrt(jnp.mean(x * x, axis=-1, keepdims=True) + EPS)
    return y * gain + shift


def _silu_rows(c_ref, cctx_ref):
    cnd = jnp.concatenate([c_ref[...], cctx_ref[...], jnp.zeros((SUBLANES - 2, D_MODEL), F32)], axis=0)
    return cnd * (1.0 / (1.0 + jnp.exp(-cnd)))


def _ada_cols(s, w_ref, b_ref, cols=slice(None)):
    return jnp.dot(s, w_ref[:, cols], preferred_element_type=F32) + b_ref[:, cols]


def _ada_kernel(c_ref, cctx_ref, w_ref, b_ref, o_ref):
    o_ref[...] = _ada_cols(_silu_rows(c_ref, cctx_ref), w_ref, b_ref)


def _ada(c, c_ctx, w_ada, b_ada, n):
    return pl.pallas_call(
        _ada_kernel,
        grid=(n // ADA_TN,),
        in_specs=[
            pl.BlockSpec((1, D_MODEL), lambda j: (0, 0)),
            pl.BlockSpec((1, D_MODEL), lambda j: (0, 0)),
            pl.BlockSpec((D_MODEL, ADA_TN), lambda j: (0, j)),
            pl.BlockSpec((1, ADA_TN), lambda j: (0, j)),
        ],
        out_specs=pl.BlockSpec((SUBLANES, ADA_TN), lambda j: (0, j)),
        out_shape=jax.ShapeDtypeStruct((SUBLANES, n), F32),
        compiler_params=_params("arbitrary"),
        name="ada",
    )(c, c_ctx, w_ada, b_ada)


def _rope_tables():
    half = HEAD_DIM // 2
    inv_freq = ROPE_BASE ** (-np.arange(0, half, 2, dtype=np.float64) / half)
    t = np.arange(SEQ)
    row = (t // GRID_W).astype(np.float64)
    col = (t % GRID_W).astype(np.float64)
    ang_r = row[:, None] * inv_freq[None, :]
    ang_c = col[:, None] * inv_freq[None, :]
    cos_t = Q_SCALE * np.concatenate([np.cos(ang_r), np.cos(ang_c)], axis=1).T
    sin_t = Q_SCALE * np.concatenate([np.sin(ang_r), np.sin(ang_c)], axis=1).T
    cos_l = np.concatenate([np.cos(ang_r), np.cos(ang_r), np.cos(ang_c), np.cos(ang_c)], axis=1)
    zeros = np.zeros_like(ang_r)
    s_up = np.concatenate([-np.sin(ang_r), zeros, -np.sin(ang_c), zeros], axis=1)
    s_dn = np.concatenate([zeros, np.sin(ang_r), zeros, np.sin(ang_c)], axis=1)
    tile2 = lambda a: np.concatenate([a, a], axis=1)
    q_table = np.concatenate([cos_t, sin_t], axis=0).astype(np.float32)
    k_table = np.concatenate([tile2(cos_l), tile2(s_up), tile2(s_dn)], axis=1).astype(np.float32)
    return q_table, k_table


def _inproj_kernel(x_ref, nw_ref, sh_ref, sc_ref, w_ref, qtab_ref, ktab_ref,
                   qt_ref, k_ref, vt_ref, u_ref, wb_ref):
    @pl.when(pl.program_id(0) == 0)
    def _():
        for c0 in range(0, IN_WIDTH, 2 * LANES):
            wb_ref[:, c0:c0 + 2 * LANES] = w_ref[:, c0:c0 + 2 * LANES].astype(BF16)

    h = _rms_modulate(x_ref[...], nw_ref[...], sh_ref[LATENT_ROW], sc_ref[LATENT_ROW]).astype(BF16)
    p = jnp.dot(h, wb_ref[...], preferred_element_type=F32)

    qt = p[:, :ATTN_WIDTH].T
    cr, cc = qtab_ref[0:16, :], qtab_ref[16:32, :]
    sr, sc = qtab_ref[32:48, :], qtab_ref[48:64, :]
    pieces = []
    for hd in range(N_Q_HEADS):
        b = hd * HEAD_DIM
        x1r, x2r = qt[b:b + 16, :], qt[b + 16:b + 32, :]
        x1c, x2c = qt[b + 32:b + 48, :], qt[b + 48:b + 64, :]
        pieces += [x1r * cr - x2r * sr, x1r * sr + x2r * cr,
                   x1c * cc - x2c * sc, x1c * sc + x2c * cc]
    qt_ref[...] = jnp.concatenate(pieces, axis=0).astype(BF16)

    k = p[:, ATTN_WIDTH:ATTN_WIDTH + KV_WIDTH]
    cos_l, s_up, s_dn = (ktab_ref[:, j * LANES:(j + 1) * LANES] for j in range(3))
    kparts = []
    for g in range(KV_WIDTH // LANES):
        kg = k[:, g * LANES:(g + 1) * LANES]
        kparts.append(kg * cos_l + pltpu.roll(kg, LANES - 16, 1) * s_up
                      + pltpu.roll(kg, 16, 1) * s_dn)
    k_ref[...] = jnp.concatenate(kparts, axis=1).astype(BF16)

    vt_ref[...] = p[:, ATTN_WIDTH + KV_WIDTH:ATTN_WIDTH + 2 * KV_WIDTH].T.astype(BF16)
    u_ref[...] = p[:, ATTN_WIDTH + 2 * KV_WIDTH:]


def _inproj(x, norm_w, mod, w_in, tables):
    q_table, k_table = tables
    tm = INPROJ_TM
    row = lambda i: (i, 0)
    colb = lambda i: (0, i)
    fixed = lambda i: (0, 0)
    return pl.pallas_call(
        _inproj_kernel,
        grid=(SEQ // tm,),
        in_specs=[
            pl.BlockSpec((tm, D_MODEL), row),
            pl.BlockSpec((1, D_MODEL), fixed),
            _mod_spec(0),
            _mod_spec(1),
            pl.BlockSpec((D_MODEL, IN_WIDTH), fixed, pipeline_mode=pl.Buffered(1)),
            pl.BlockSpec((q_table.shape[0], tm), colb),
            pl.BlockSpec((tm, k_table.shape[1]), row),
        ],
        out_specs=[
            pl.BlockSpec((ATTN_WIDTH, tm), colb),
            pl.BlockSpec((tm, KV_WIDTH), row),
            pl.BlockSpec((KV_WIDTH, tm), colb),
            pl.BlockSpec((tm, POOL_WIDTH), row),
        ],
        out_shape=[
            jax.ShapeDtypeStruct((ATTN_WIDTH, SEQ), BF16),
            jax.ShapeDtypeStruct((SEQ, KV_WIDTH), BF16),
            jax.ShapeDtypeStruct((KV_WIDTH, SEQ), BF16),
            jax.ShapeDtypeStruct((SEQ, POOL_WIDTH), F32),
        ],
        scratch_shapes=[pltpu.VMEM((D_MODEL, IN_WIDTH), BF16)],
        compiler_params=_params("arbitrary"),
        name="inproj",
    )(x, norm_w, mod, mod, w_in, q_table, k_table)


def _ctx_kv_kernel(x_ref, nw_ref, sh_ref, sc_ref, w_ref, k_ref, vt_ref):
    h = _rms_modulate(x_ref[...], nw_ref[...], sh_ref[CONTEXT_ROW], sc_ref[CONTEXT_ROW]).astype(BF16)
    p = jnp.dot(h, w_ref[...].astype(BF16), preferred_element_type=F32)
    k_ref[...] = p[:, :KV_WIDTH].astype(BF16)
    vt_ref[...] = p[:, KV_WIDTH:].T.astype(BF16)


def _ctx_kv(ctx, norm_w, mod, w_in):
    fixed = lambda i: (0, 0)
    kv_block = ATTN_WIDTH // (2 * KV_WIDTH)
    return pl.pallas_call(
        _ctx_kv_kernel,
        grid=(1,),
        in_specs=[
            pl.BlockSpec((CTX_LEN, D_MODEL), fixed),
            pl.BlockSpec((1, D_MODEL), fixed),
            _mod_spec(0),
            _mod_spec(1),
            pl.BlockSpec((D_MODEL, 2 * KV_WIDTH), lambda i: (0, kv_block)),
        ],
        out_specs=[
            pl.BlockSpec((CTX_LEN, KV_WIDTH), fixed),
            pl.BlockSpec((KV_WIDTH, CTX_LEN), fixed),
        ],
        out_shape=[
            jax.ShapeDtypeStruct((CTX_LEN, KV_WIDTH), BF16),
            jax.ShapeDtypeStruct((KV_WIDTH, CTX_LEN), BF16),
        ],
        compiler_params=_params("arbitrary"),
        name="ctx_kv",
    )(ctx, norm_w, mod, mod, w_in)


def _attn_kernel(qt_ref, kp_ref, km_ref, kn_ref, vtp_ref, vtm_ref, vtn_ref,
                 kc_ref, vtc_ref, sink_ref, wout_ref, wup_ref, wdn_ref,
                 c_ref, cctx_ref, wada_ref, bada_ref,
                 o_ref, wout_b_ref, wup_b_ref, wdn_b_ref, mod_ref):
    cond_act = _silu_rows(c_ref, cctx_ref)
    ada_cols = [slice(c0, c0 + 2 * LANES) for c0 in range(0, mod_ref.shape[1], 2 * LANES)]

    def ada_piece(idx, n):
        for cols in ada_cols[idx * len(ada_cols) // n:(idx + 1) * len(ada_cols) // n]:
            mod_ref[:, cols] = _ada_cols(cond_act, wada_ref, bada_ref, cols)

    i = pl.program_id(0)
    nsub = ATTN_TQ // BLOCK
    nblk = SEQ // BLOCK
    pw = 2 * BLOCK
    nkeys = 3 * BLOCK + CTX_LEN

    key_j = lax.broadcasted_iota(jnp.int32, (BLOCK, pw), 0)
    qry_i = lax.broadcasted_iota(jnp.int32, (BLOCK, pw), 1) % BLOCK
    neg = jnp.full((BLOCK, pw), NEG_INF, F32)
    zero = jnp.zeros((BLOCK, pw), F32)
    band_prev = jnp.where(key_j >= qry_i, zero, neg)
    band_next = jnp.where(key_j <= qry_i, zero, neg)

    kc = kc_ref[...]
    vtc = vtc_ref[...]
    ones_rows = jnp.ones((2 * SUBLANES, nkeys), BF16)
    sink_row = sink_ref[...] * LOG2E

    windows = []
    for sb in range(nsub):
        n = i * nsub + sb
        lo, hi = sb * BLOCK, (sb + 1) * BLOCK
        k_prev = kp_ref[...] if sb == 0 else km_ref[lo - BLOCK:lo, :]
        k_next = kn_ref[...] if sb == nsub - 1 else km_ref[hi:hi + BLOCK, :]
        vt_prev = vtp_ref[...] if sb == 0 else vtm_ref[:, lo - BLOCK:lo]
        vt_next = vtn_ref[...] if sb == nsub - 1 else vtm_ref[:, hi:hi + BLOCK]
        windows.append(dict(
            kwin=jnp.concatenate([k_prev, km_ref[lo:hi, :], k_next, kc], axis=0),
            vtwin=jnp.concatenate([vt_prev, vtm_ref[:, lo:hi], vt_next, vtc], axis=1),
            bias_prev=band_prev + jnp.where(n > 0, 0.0, NEG_INF),
            bias_next=band_next + jnp.where(n < nblk - 1, 0.0, NEG_INF)))

    units = [(sb, h, half) for sb in range(nsub) for h in range(N_KV_HEADS)
             for half in range(GQA // 2)]

    def scores(unit):
        sb, h, half = unit
        win = windows[sb]
        lo, hi = sb * BLOCK, (sb + 1) * BLOCK
        hd0 = GQA * h + 2 * half
        qh = jnp.concatenate([qt_ref[(hd0 + g) * HEAD_DIM:(hd0 + g + 1) * HEAD_DIM, lo:hi]
                              for g in range(2)], axis=1)
        blocks = []
        if h > 0:
            blocks.append(jnp.zeros((h * HEAD_DIM, pw), BF16))
        blocks.append(qh)
        if h < N_KV_HEADS - 1:
            blocks.append(jnp.zeros(((N_KV_HEADS - 1 - h) * HEAD_DIM, pw), BF16))
        s = jnp.dot(win["kwin"], jnp.concatenate(blocks, axis=0),
                    preferred_element_type=F32)
        s = jnp.concatenate([s[0:BLOCK] + win["bias_prev"], s[BLOCK:2 * BLOCK],
                             s[2 * BLOCK:3 * BLOCK] + win["bias_next"], s[3 * BLOCK:]], axis=0)
        return s, jnp.max(s, axis=0, keepdims=True)

    def probs(unit, s, smax):
        sb, h, half = unit
        hd0 = GQA * h + 2 * half
        sink = sink_row[:, hd0 * BLOCK:(hd0 + 2) * BLOCK]
        m = jnp.maximum(smax, sink)
        return jnp.exp2(s - m).astype(BF16), jnp.exp2(sink - m)

    def finish(unit, e, e_sink):
        sb, h, half = unit
        lo, hi = sb * BLOCK, (sb + 1) * BLOCK
        hd0 = GQA * h + 2 * half
        vt_ones = jnp.concatenate(
            [windows[sb]["vtwin"][h * HEAD_DIM:(h + 1) * HEAD_DIM, :], ones_rows], axis=0)
        o = jnp.dot(vt_ones, e, preferred_element_type=F32)
        denom = o[HEAD_DIM:HEAD_DIM + 1, :] + e_sink
        o = o[:HEAD_DIM, :] * (1.0 / denom)
        blk = jnp.concatenate([o[:, 0:BLOCK], o[:, BLOCK:2 * BLOCK]], axis=0)
        o_ref[lo:hi, hd0 * HEAD_DIM:(hd0 + 2) * HEAD_DIM] = blk.T.astype(BF16)

    cast_jobs = []
    for src, dst in ((wout_ref, wout_b_ref), (wup_ref, wup_b_ref), (wdn_ref, wdn_b_ref)):
        wc = max(LANES, CAST_JOB_ELEMS // src.shape[0])
        cast_jobs += [(src, dst, c0) + (wc,) for c0 in range(0, src.shape[1], wc)]

    def cast_piece(idx, n):
        for src, dst, c0, wc in cast_jobs[idx * len(cast_jobs) // n:(idx + 1) * len(cast_jobs) // n]:
            val = src[:, c0:c0 + wc].astype(BF16)
            if len(dst.shape) == 3:
                tf = dst.shape[2]
                dst[c0 // tf, :, c0 % tf:c0 % tf + wc] = val
            else:
                dst[:, c0:c0 + wc] = val

    nu = len(units)
    s_vals, p_vals = {}, {}
    d_probs, d_finish = ATTN_STAGE_LAG
    for step in range(nu + d_finish):
        if step < nu:
            s_vals[step] = scores(units[step])
        if d_probs <= step < nu + d_probs:
            u = step - d_probs
            p_vals[u] = probs(units[u], *s_vals.pop(u))
            cast_piece(u, nu)
            ada_piece(u, nu)
        if step >= d_finish:
            u = step - d_finish
            finish(units[u], *p_vals.pop(u))


def _attn(qt, k, vt, kc, vtc, sink, w_out, w_up, w_dn, c, c_ctx, w_ada, b_ada, ada_col0):
    tq = ATTN_TQ
    r = tq // BLOCK
    nblk = SEQ // BLOCK
    nsteps = SEQ // tq
    ada_tn = (w_ada.shape[1] - ada_col0) // nsteps
    ada_b0 = ada_col0 // ada_tn
    fixed = lambda i: (0, 0)
    row = lambda i: (i, 0)
    slab = lambda w: pl.BlockSpec((w.shape[0] // nsteps, w.shape[1]), row)
    bf16_like = lambda w: jax.ShapeDtypeStruct(w.shape, BF16)
    prev = lambda i: jnp.maximum(i * r - 1, 0)
    nxt = lambda i: jnp.minimum((i + 1) * r, nblk - 1)
    return pl.pallas_call(
        _attn_kernel,
        grid=(SEQ // tq,),
        in_specs=[
            pl.BlockSpec((ATTN_WIDTH, tq), lambda i: (0, i)),
            pl.BlockSpec((BLOCK, KV_WIDTH), lambda i: (prev(i), 0)),
            pl.BlockSpec((tq, KV_WIDTH), lambda i: (i, 0)),
            pl.BlockSpec((BLOCK, KV_WIDTH), lambda i: (nxt(i), 0)),
            pl.BlockSpec((KV_WIDTH, BLOCK), lambda i: (0, prev(i))),
            pl.BlockSpec((KV_WIDTH, tq), lambda i: (0, i)),
            pl.BlockSpec((KV_WIDTH, BLOCK), lambda i: (0, nxt(i))),
            pl.BlockSpec((CTX_LEN, KV_WIDTH), fixed),
            pl.BlockSpec((KV_WIDTH, CTX_LEN), fixed),
            pl.BlockSpec((1, N_Q_HEADS * BLOCK), fixed),
            slab(w_out), slab(w_up), slab(w_dn),
            pl.BlockSpec((1, D_MODEL), fixed),
            pl.BlockSpec((1, D_MODEL), fixed),
            pl.BlockSpec((D_MODEL, ada_tn), lambda i: (0, ada_b0 + i)),
            pl.BlockSpec((1, ada_tn), lambda i: (0, ada_b0 + i)),
        ],
        out_specs=[pl.BlockSpec((tq, ATTN_WIDTH), row), slab(w_out),
                   pl.BlockSpec((MLP_NF, w_up.shape[0] // nsteps, MLP_TF), lambda i: (0, i, 0)),
                   slab(w_dn),
                   pl.BlockSpec((SUBLANES, ada_tn), lambda i: (0, i))],
        out_shape=[jax.ShapeDtypeStruct((SEQ, ATTN_WIDTH), BF16),
                   bf16_like(w_out),
                   jax.ShapeDtypeStruct((MLP_NF, w_up.shape[0], MLP_TF), BF16),
                   bf16_like(w_dn),
                   jax.ShapeDtypeStruct((SUBLANES, w_ada.shape[1] - ada_col0), F32)],
        compiler_params=_params("arbitrary"),
        name="attn",
    )(qt, k, k, k, vt, vt, vt, kc, vtc, sink, w_out, w_up, w_dn, c, c_ctx, w_ada, b_ada)


def _mixout_kernel(x_ref, attn_ref, up_ref, um_ref, un_ref, wout_ref, pw_ref, ps_ref,
                   ga_ref, nw_ref, sh_ref, sc_ref, x1_ref, hm_ref):
    i = pl.program_id(0)
    tm = MIX_TM
    nsteps = SEQ // tm

    def u_rows(r0, c0, c1):
        parts = []
        if r0 == 0:
            parts.append(jnp.where(i > 0, up_ref[:, c0:c1], 0.0))
        lo, hi = max(r0 - POOL_HALO, 0), min(r0 + rc + POOL_HALO, tm)
        parts.append(um_ref[lo:hi, c0:c1])
        if r0 + rc == tm:
            parts.append(jnp.where(i < nsteps - 1, un_ref[:, c0:c1], 0.0))
        return jnp.concatenate(parts, axis=0)

    rc = MIX_CHUNK
    n = rc + 2 * POOL_HALO
    c = POOL_HALO
    ngroups = len(POOL_WINDOWS)
    ncol = D_MODEL // ngroups

    def pool_group(r0, g):
        w = POOL_WINDOWS[g]
        c0, c1 = g * POOL_GROUP_DIM, (g + 1) * POOL_GROUP_DIM
        e = u_rows(r0, c0, c1)
        f, span = e, 1
        while 2 * span < w:
            f = f + pltpu.roll(f, n - span, 0)
            span *= 2
        win = (pltpu.roll(f, span, 0) + f)[c:c + rc]
        t = i * tm + r0 + lax.broadcasted_iota(jnp.int32, (rc, 1), 0)
        lo = jnp.clip(t - w // 2, 0, SEQ)
        hi = jnp.clip(t - w // 2 + w, 0, SEQ)
        cnt = (hi - lo).astype(F32)
        pg = win / cnt - um_ref[r0:r0 + rc, c0:c1]
        mixed = jnp.dot(pg.astype(BF16), pw_ref[g].astype(BF16), preferred_element_type=F32)
        return (mixed * ps_ref[:, c0:c1]).astype(BF16)

    def zero_after(val):
        bits = pltpu.bitcast(val[0:SUBLANES, 0:LANES], jnp.uint32)
        bits = lax.shift_right_logical(lax.shift_right_logical(bits, jnp.uint32(16)), jnp.uint32(16))
        zrow = bits[0:1, :].astype(F32)
        return jnp.concatenate([zrow] * (D_MODEL // LANES), axis=1)

    def prenorm(r0, nrows, anchor=None):
        rows = slice(r0, r0 + nrows)
        shift = sh_ref[LATENT_ROW] if anchor is None else sh_ref[LATENT_ROW] + zero_after(anchor)
        hm_ref[rows, :] = _rms_modulate(x1_ref[rows, :], nw_ref[...], shift,
                                        sc_ref[LATENT_ROW]).astype(BF16)

    chunk_starts = list(range(0, tm, rc))
    for ci, r0 in enumerate(chunk_starts):
        rows = slice(r0, r0 + rc)
        attn_rows = attn_ref[rows, :]
        pooled = []
        npn = rc // (2 * ngroups)
        for j in range(ngroups):
            cols = slice(j * ncol, (j + 1) * ncol)
            o = jnp.dot(attn_rows, wout_ref[0:ATTN_WIDTH, cols], preferred_element_type=F32)
            x1_ref[rows, cols] = o
            pooled.append(pool_group(r0, j))
            if ci > 0:
                prenorm(chunk_starts[ci - 1] + j * npn, npn, anchor=o)
        pool_rows = jnp.concatenate(pooled, axis=1)
        for j in range(ngroups):
            cols = slice(j * ncol, (j + 1) * ncol)
            o = jnp.dot(pool_rows, wout_ref[ATTN_WIDTH:, cols], preferred_element_type=F32)
            x1_ref[rows, cols] = x_ref[rows, cols] + ga_ref[LATENT_ROW, cols] * (x1_ref[rows, cols] + o)
            if ci > 0:
                prenorm(chunk_starts[ci - 1] + (ngroups + j) * npn, npn, anchor=o)
    for j in range(ngroups):
        prenorm(chunk_starts[-1] + j * (rc // ngroups), rc // ngroups)


def _mixout(x, attn, u, w_out_bf16, pool_w, pool_scale, mod, norm_w):
    tm = MIX_TM
    hb = tm // POOL_HALO
    nhalo = SEQ // POOL_HALO
    row = lambda i: (i, 0)
    fixed = lambda i: (0, 0)
    return pl.pallas_call(
        _mixout_kernel,
        grid=(SEQ // tm,),
        in_specs=[
            pl.BlockSpec((tm, D_MODEL), row),
            pl.BlockSpec((tm, ATTN_WIDTH), row),
            pl.BlockSpec((POOL_HALO, POOL_WIDTH), lambda i: (jnp.maximum(i * hb - 1, 0), 0)),
            pl.BlockSpec((tm, POOL_WIDTH), row),
            pl.BlockSpec((POOL_HALO, POOL_WIDTH), lambda i: (jnp.minimum((i + 1) * hb, nhalo - 1), 0)),
            pl.BlockSpec((D_MODEL, D_MODEL), fixed),
            pl.BlockSpec((len(POOL_WINDOWS), POOL_GROUP_DIM, POOL_GROUP_DIM), lambda i: (0, 0, 0)),
            pl.BlockSpec((1, POOL_WIDTH), fixed),
            _mod_spec(0),
            pl.BlockSpec((1, D_MODEL), fixed),
            _mod_spec(1),
            _mod_spec(2),
        ],
        out_specs=[pl.BlockSpec((tm, D_MODEL), row), pl.BlockSpec((tm, D_MODEL), row)],
        out_shape=[jax.ShapeDtypeStruct((SEQ, D_MODEL), F32),
                   jax.ShapeDtypeStruct((SEQ, D_MODEL), BF16)],
        compiler_params=_params("arbitrary"),
        name="mixout",
    )(x, attn, u, u, u, w_out_bf16, pool_w, pool_scale, mod, norm_w, mod, mod)


def _mlp_kernel(hm_ref, x1r_ref, wup_ref, wdn_ref, gm_ref, fw_ref, o_ref):
    f = pl.program_id(1)
    last = MLP_NF - 1
    row_chunks = [slice(r0, r0 + MLP_RC) for r0 in range(0, MLP_TM, MLP_RC)]
    gm = gm_ref[LATENT_ROW]

    def branch_out(rows):
        a = jnp.maximum(jnp.dot(hm_ref[rows, :], wup_ref[...], preferred_element_type=F32), 0.0)
        return gm * jnp.dot((a * a).astype(BF16), wdn_ref[...], preferred_element_type=F32)

    def final_norm(rows):
        x2 = o_ref[rows, :]
        y = x2 * lax.rsqrt(jnp.mean(x2 * x2, axis=-1, keepdims=True) + EPS)
        o_ref[rows, :] = y * fw_ref[...]

    @pl.when(f == 0)
    def _():
        for rows in row_chunks:
            o_ref[rows, :] = branch_out(rows)
        o_ref[0:MLP_XR, :] += x1r_ref[...]

    @pl.when(jnp.logical_and(f > 0, f < last))
    def _():
        x_rows = pl.ds(pl.multiple_of(f * MLP_XR, MLP_XR), MLP_XR)
        o_ref[x_rows, :] += x1r_ref[...]
        for rows in row_chunks:
            o_ref[rows, :] += branch_out(rows)

    @pl.when(f == last)
    def _():
        o_ref[last * MLP_XR:(last + 1) * MLP_XR, :] += x1r_ref[...]
        npiece = 4
        cw, rp = D_MODEL // npiece, MLP_RC // npiece
        for ci, rows in enumerate(row_chunks):
            a = jnp.maximum(jnp.dot(hm_ref[rows, :], wup_ref[...], preferred_element_type=F32), 0.0)
            a = (a * a).astype(BF16)
            for j in range(npiece):
                cols = slice(j * cw, (j + 1) * cw)
                o_ref[rows, cols] += gm[:, cols] * jnp.dot(a, wdn_ref[:, cols],
                                                           preferred_element_type=F32)
                if ci > 0:
                    r0 = row_chunks[ci - 1].start + j * rp
                    final_norm(slice(r0, r0 + rp))
        final_norm(row_chunks[-1])


def _mlp(hm, x1, w_up_bf16, w_dn_bf16, mod, final_w):
    tm, tf = MLP_TM, MLP_TF
    return pl.pallas_call(
        _mlp_kernel,
        grid=(SEQ // tm, MLP_NF),
        in_specs=[
            pl.BlockSpec((tm, D_MODEL), lambda i, f: (i, 0)),
            pl.BlockSpec((MLP_XR, D_MODEL), lambda i, f: (i * MLP_NF + f, 0)),
            pl.BlockSpec((None, D_MODEL, tf), lambda i, f: (f, 0, 0)),
            pl.BlockSpec((tf, D_MODEL), lambda i, f: (f, 0)),
            _mod_spec(3),
            pl.BlockSpec((1, D_MODEL), lambda i, f: (0, 0)),
        ],
        out_specs=pl.BlockSpec((tm, D_MODEL), lambda i, f: (i, 0)),
        out_shape=jax.ShapeDtypeStruct((SEQ, D_MODEL), F32),
        compiler_params=_params("arbitrary", "arbitrary"),
        name="mlp",
    )(hm, x1, w_up_bf16, w_dn_bf16, mod, final_w)


def kernel(x, c, ctx, c_ctx, norm_attn_w, norm_mlp_w, w_ada, b_ada, w_in, attn_sink,
           pool_w, pool_scale, w_out, w_mlp_up, w_mlp_down, final_norm_w):
    assert x.shape == (1, SEQ, D_MODEL) and ctx.shape == (1, CTX_LEN, D_MODEL)
    assert w_ada.shape[0] == 1, "single layer"
    x2d, ctx2d = x[0], ctx[0]

    c_ctx_row = c_ctx[None, :]
    b_row = b_ada[0][None, :]
    n_early = 2 * D_MODEL
    mod_a = _ada(c, c_ctx_row, w_ada[0], b_row, n_early)

    nw_a = norm_attn_w[0][None, :]
    tables = tuple(jnp.asarray(t) for t in _rope_tables())
    qt, k, vt, u = _inproj(x2d, nw_a, mod_a, w_in[0], tables)
    kc, vtc = _ctx_kv(ctx2d, nw_a, mod_a, w_in[0])

    sink_row = jnp.repeat(attn_sink[0].astype(F32), BLOCK)[None, :]
    attn, w_out_b, w_up_b, w_dn_b, mod_b = _attn(
        qt, k, vt, kc, vtc, sink_row, w_out[0], w_mlp_up[0], w_mlp_down[0],
        c, c_ctx_row, w_ada[0], b_row, n_early)

    x1, hm = _mixout(x2d, attn, u, w_out_b, pool_w[0], pool_scale[0][None, :], mod_b,
                     norm_mlp_w[0][None, :])
    out = _mlp(hm, x1, w_up_b, w_dn_b, mod_b, final_norm_w[None, :])
    return out[None]
```

```python
import numpy as np
import jax
import jax.numpy as jnp
from jax import lax
from jax.experimental import pallas as pl
from jax.experimental.pallas import tpu as pltpu

F32 = jnp.float32
BF16 = jnp.bfloat16

D_MODEL = 2048
SEQ = 8192
CTX_LEN = 256
GRID_W = 64
HEAD_DIM = 64
N_Q_HEADS = 16
N_KV_HEADS = 4
GQA = N_Q_HEADS // N_KV_HEADS
ATTN_WIDTH = N_Q_HEADS * HEAD_DIM
KV_WIDTH = N_KV_HEADS * HEAD_DIM
POOL_WINDOWS = (2, 4, 8, 16)
POOL_WIDTH = D_MODEL - ATTN_WIDTH
POOL_GROUP_DIM = POOL_WIDTH // len(POOL_WINDOWS)
IN_WIDTH = ATTN_WIDTH + 2 * KV_WIDTH + POOL_WIDTH
D_FF = 4 * D_MODEL
BLOCK = 128
ROPE_BASE = 10000.0
N_MOD = 6
EPS = 1e-6
NEG_INF = -1e30
LOG2E = 1.4426950408889634
Q_SCALE = HEAD_DIM ** -0.5 * LOG2E

LANES = 128
SUBLANES = 8
POOL_HALO = 8
LATENT_ROW = slice(0, 1)
CONTEXT_ROW = slice(1, 2)
VMEM_LIMIT = 56 * 1024 * 1024

ADA_TK = 256
INPROJ_TM = 512
ATTN_TQ = 512
CAST_JOB_ELEMS = 64 * 1024
ATTN_STAGE_LAG = (2, 5)
MIX_TM = 512
MIX_CHUNK = 256
MLP_TM = 1024
MLP_TF = 1024
MLP_RC = 512
MLP_NF = D_FF // MLP_TF
MLP_XR = MLP_TM // MLP_NF


def _params(*sem):
    return pltpu.CompilerParams(dimension_semantics=sem, vmem_limit_bytes=VMEM_LIMIT)


def _mod_spec(chunk):
    return pl.BlockSpec((SUBLANES, D_MODEL), lambda *_: (0, chunk))


def _rms_modulate(x, norm_w, shift, scale):
    gain = norm_w * (1.0 + scale)
    y = x * lax.rsqrt(jnp.mean(x * x, axis=-1, keepdims=True) + EPS)
    return y * gain + shift


def _silu_rows(c_ref, cctx_ref):
    cnd = jnp.concatenate([c_ref[...], cctx_ref[...], jnp.zeros((SUBLANES - 2, D_MODEL), F32)], axis=0)
    return cnd * (1.0 / (1.0 + jnp.exp(-cnd)))


def _ada_cols(s, w_ref, b_ref, cols=slice(None)):
    return jnp.dot(s, w_ref[:, cols], preferred_element_type=F32) + b_ref[:, cols]


def _ada_kernel(c_ref, cctx_ref, w_ref, b_ref, o_ref):
    cnd = jnp.concatenate([c_ref[...], cctx_ref[...], jnp.zeros((SUBLANES - 2, ADA_TK), F32)], axis=0)
    part = jnp.dot(cnd * (1.0 / (1.0 + jnp.exp(-cnd))), w_ref[...], preferred_element_type=F32)

    @pl.when(pl.program_id(0) == 0)
    def _():
        o_ref[...] = part + b_ref[...]

    @pl.when(pl.program_id(0) > 0)
    def _():
        o_ref[...] += part


def _ada(c, c_ctx, w_ada, b_ada, n):
    return pl.pallas_call(
        _ada_kernel,
        grid=(D_MODEL // ADA_TK,),
        in_specs=[
            pl.BlockSpec((1, ADA_TK), lambda j: (0, j)),
            pl.BlockSpec((1, ADA_TK), lambda j: (0, j)),
            pl.BlockSpec((ADA_TK, n), lambda j: (j, 0)),
            pl.BlockSpec((1, n), lambda j: (0, 0)),
        ],
        out_specs=pl.BlockSpec((SUBLANES, n), lambda j: (0, 0)),
        out_shape=jax.ShapeDtypeStruct((SUBLANES, n), F32),
        compiler_params=_params("arbitrary"),
        name="ada",
    )(c, c_ctx, w_ada, b_ada)


def _rope_tables():
    half = HEAD_DIM // 2
    inv_freq = ROPE_BASE ** (-np.arange(0, half, 2, dtype=np.float64) / half)
    t = np.arange(SEQ)
    row = (t // GRID_W).astype(np.float64)
    col = (t % GRID_W).astype(np.float64)
    ang_r = row[:, None] * inv_freq[None, :]
    ang_c = col[:, None] * inv_freq[None, :]
    cos_t = Q_SCALE * np.concatenate([np.cos(ang_r), np.cos(ang_c)], axis=1).T
    sin_t = Q_SCALE * np.concatenate([np.sin(ang_r), np.sin(ang_c)], axis=1).T
    cos_l = np.concatenate([np.cos(ang_r), np.cos(ang_r), np.cos(ang_c), np.cos(ang_c)], axis=1)
    zeros = np.zeros_like(ang_r)
    s_up = np.concatenate([-np.sin(ang_r), zeros, -np.sin(ang_c), zeros], axis=1)
    s_dn = np.concatenate([zeros, np.sin(ang_r), zeros, np.sin(ang_c)], axis=1)
    tile2 = lambda a: np.concatenate([a, a], axis=1)
    q_table = np.concatenate([cos_t, sin_t], axis=0).astype(np.float32)
    k_table = np.concatenate([tile2(cos_l), tile2(s_up), tile2(s_dn)], axis=1).astype(np.float32)
    return q_table, k_table


def _inproj_kernel(x_ref, nw_ref, sh_ref, sc_ref, w_ref, qtab_ref, ktab_ref,
                   qt_ref, k_ref, vt_ref, u_ref, wb_ref):
    @pl.when(pl.program_id(0) == 0)
    def _():
        for c0 in range(0, IN_WIDTH, 2 * LANES):
            wb_ref[:, c0:c0 + 2 * LANES] = w_ref[:, c0:c0 + 2 * LANES].astype(BF16)

    h = _rms_modulate(x_ref[...], nw_ref[...], sh_ref[LATENT_ROW], sc_ref[LATENT_ROW]).astype(BF16)
    p = jnp.dot(h, wb_ref[...], preferred_element_type=F32)

    qt = p[:, :ATTN_WIDTH].T
    cr, cc = qtab_ref[0:16, :], qtab_ref[16:32, :]
    sr, sc = qtab_ref[32:48, :], qtab_ref[48:64, :]
    pieces = []
    for hd in range(N_Q_HEADS):
        b = hd * HEAD_DIM
        x1r, x2r = qt[b:b + 16, :], qt[b + 16:b + 32, :]
        x1c, x2c = qt[b + 32:b + 48, :], qt[b + 48:b + 64, :]
        pieces += [x1r * cr - x2r * sr, x1r * sr + x2r * cr,
                   x1c * cc - x2c * sc, x1c * sc + x2c * cc]
    qt_ref[...] = jnp.concatenate(pieces, axis=0).astype(BF16)

    k = p[:, ATTN_WIDTH:ATTN_WIDTH + KV_WIDTH]
    cos_l, s_up, s_dn = (ktab_ref[:, j * LANES:(j + 1) * LANES] for j in range(3))
    kparts = []
    for g in range(KV_WIDTH // LANES):
        kg = k[:, g * LANES:(g + 1) * LANES]
        kparts.append(kg * cos_l + pltpu.roll(kg, LANES - 16, 1) * s_up
                      + pltpu.roll(kg, 16, 1) * s_dn)
    k_ref[...] = jnp.concatenate(kparts, axis=1).astype(BF16)

    vt_ref[...] = p[:, ATTN_WIDTH + KV_WIDTH:ATTN_WIDTH + 2 * KV_WIDTH].T.astype(BF16)
    u_ref[...] = p[:, ATTN_WIDTH + 2 * KV_WIDTH:]


def _inproj(x, norm_w, mod, w_in, tables):
    q_table, k_table = tables
    tm = INPROJ_TM
    row = lambda i: (i, 0)
    colb = lambda i: (0, i)
    fixed = lambda i: (0, 0)
    return pl.pallas_call(
        _inproj_kernel,
        grid=(SEQ // tm,),
        in_specs=[
            pl.BlockSpec((tm, D_MODEL), row),
            pl.BlockSpec((1, D_MODEL), fixed),
            _mod_spec(0),
            _mod_spec(1),
            pl.BlockSpec((D_MODEL, IN_WIDTH), fixed, pipeline_mode=pl.Buffered(1)),
            pl.BlockSpec((q_table.shape[0], tm), colb),
            pl.BlockSpec((tm, k_table.shape[1]), row),
        ],
        out_specs=[
            pl.BlockSpec((ATTN_WIDTH, tm), colb),
            pl.BlockSpec((tm, KV_WIDTH), row),
            pl.BlockSpec((KV_WIDTH, tm), colb),
            pl.BlockSpec((tm, POOL_WIDTH), row),
        ],
        out_shape=[
            jax.ShapeDtypeStruct((ATTN_WIDTH, SEQ), BF16),
            jax.ShapeDtypeStruct((SEQ, KV_WIDTH), BF16),
            jax.ShapeDtypeStruct((KV_WIDTH, SEQ), BF16),
            jax.ShapeDtypeStruct((SEQ, POOL_WIDTH), F32),
        ],
        scratch_shapes=[pltpu.VMEM((D_MODEL, IN_WIDTH), BF16)],
        compiler_params=_params("arbitrary"),
        name="inproj",
    )(x, norm_w, mod, mod, w_in, q_table, k_table)


def _ctx_kv_kernel(x_ref, nw_ref, sh_ref, sc_ref, w_ref, k_ref, vt_ref):
    h = _rms_modulate(x_ref[...], nw_ref[...], sh_ref[CONTEXT_ROW], sc_ref[CONTEXT_ROW]).astype(BF16)
    p = jnp.dot(h, w_ref[...].astype(BF16), preferred_element_type=F32)
    k_ref[...] = p[:, :KV_WIDTH].astype(BF16)
    vt_ref[...] = p[:, KV_WIDTH:].T.astype(BF16)


def _ctx_kv(ctx, norm_w, mod, w_in):
    fixed = lambda i: (0, 0)
    kv_block = ATTN_WIDTH // (2 * KV_WIDTH)
    return pl.pallas_call(
        _ctx_kv_kernel,
        grid=(1,),
        in_specs=[
            pl.BlockSpec((CTX_LEN, D_MODEL), fixed),
            pl.BlockSpec((1, D_MODEL), fixed),
            _mod_spec(0),
            _mod_spec(1),
            pl.BlockSpec((D_MODEL, 2 * KV_WIDTH), lambda i: (0, kv_block)),
        ],
        out_specs=[
            pl.BlockSpec((CTX_LEN, KV_WIDTH), fixed),
            pl.BlockSpec((KV_WIDTH, CTX_LEN), fixed),
        ],
        out_shape=[
            jax.ShapeDtypeStruct((CTX_LEN, KV_WIDTH), BF16),
            jax.ShapeDtypeStruct((KV_WIDTH, CTX_LEN), BF16),
        ],
        compiler_params=_params("arbitrary"),
        name="ctx_kv",
    )(ctx, norm_w, mod, mod, w_in)


def _attn_kernel(qt_ref, kp_ref, km_ref, kn_ref, vtp_ref, vtm_ref, vtn_ref,
                 kc_ref, vtc_ref, sink_ref, wout_ref, wup_ref, wdn_ref,
                 c_ref, cctx_ref, wada_ref, bada_ref,
                 o_ref, wout_b_ref, wup_b_ref, wdn_b_ref, mod_ref):
    cond_act = _silu_rows(c_ref, cctx_ref)
    ada_cols = [slice(c0, c0 + 2 * LANES) for c0 in range(0, mod_ref.shape[1], 2 * LANES)]

    def ada_piece(idx, n):
        for cols in ada_cols[idx * len(ada_cols) // n:(idx + 1) * len(ada_cols) // n]:
            mod_ref[:, cols] = _ada_cols(cond_act, wada_ref, bada_ref, cols)

    i = pl.program_id(0)
    nsub = ATTN_TQ // BLOCK
    nblk = SEQ // BLOCK
    pw = 2 * BLOCK
    nkeys = 3 * BLOCK + CTX_LEN

    key_j = lax.broadcasted_iota(jnp.int32, (BLOCK, pw), 0)
    qry_i = lax.broadcasted_iota(jnp.int32, (BLOCK, pw), 1) % BLOCK
    neg = jnp.full((BLOCK, pw), NEG_INF, F32)
    zero = jnp.zeros((BLOCK, pw), F32)
    band_prev = jnp.where(key_j >= qry_i, zero, neg)
    band_next = jnp.where(key_j <= qry_i, zero, neg)

    kc = kc_ref[...]
    vtc = vtc_ref[...]
    ones_rows = jnp.ones((2 * SUBLANES, nkeys), BF16)
    sink_row = sink_ref[...] * LOG2E

    windows = []
    for sb in range(nsub):
        n = i * nsub + sb
        lo, hi = sb * BLOCK, (sb + 1) * BLOCK
        k_prev = kp_ref[...] if sb == 0 else km_ref[lo - BLOCK:lo, :]
        k_next = kn_ref[...] if sb == nsub - 1 else km_ref[hi:hi + BLOCK, :]
        vt_prev = vtp_ref[...] if sb == 0 else vtm_ref[:, lo - BLOCK:lo]
        vt_next = vtn_ref[...] if sb == nsub - 1 else vtm_ref[:, hi:hi + BLOCK]
        windows.append(dict(
            kwin=jnp.concatenate([k_prev, km_ref[lo:hi, :], k_next, kc], axis=0),
            vtwin=jnp.concatenate([vt_prev, vtm_ref[:, lo:hi], vt_next, vtc], axis=1),
            bias_prev=band_prev + jnp.where(n > 0, 0.0, NEG_INF),
            bias_next=band_next + jnp.where(n < nblk - 1, 0.0, NEG_INF)))

    units = [(sb, h, half) for sb in range(nsub) for h in range(N_KV_HEADS)
             for half in range(GQA // 2)]

    def scores(unit):
        sb, h, half = unit
        win = windows[sb]
        lo, hi = sb * BLOCK, (sb + 1) * BLOCK
        hd0 = GQA * h + 2 * half
        qh = jnp.concatenate([qt_ref[(hd0 + g) * HEAD_DIM:(hd0 + g + 1) * HEAD_DIM, lo:hi]
                              for g in range(2)], axis=1)
        blocks = []
        if h > 0:
            blocks.append(jnp.zeros((h * HEAD_DIM, pw), BF16))
        blocks.append(qh)
        if h < N_KV_HEADS - 1:
            blocks.append(jnp.zeros(((N_KV_HEADS - 1 - h) * HEAD_DIM, pw), BF16))
        s = jnp.dot(win["kwin"], jnp.concatenate(blocks, axis=0),
                    preferred_element_type=F32)
        s = jnp.concatenate([s[0:BLOCK] + win["bias_prev"], s[BLOCK:2 * BLOCK],
                             s[2 * BLOCK:3 * BLOCK] + win["bias_next"], s[3 * BLOCK:]], axis=0)
        return s, jnp.max(s, axis=0, keepdims=True)

    def probs(unit, s, smax):
        sb, h, half = unit
        hd0 = GQA * h + 2 * half
        sink = sink_row[:, hd0 * BLOCK:(hd0 + 2) * BLOCK]
        m = jnp.maximum(smax, sink)
        return jnp.exp2(s - m).astype(BF16), jnp.exp2(sink - m)

    def finish(unit, e, e_sink):
        sb, h, half = unit
        lo, hi = sb * BLOCK, (sb + 1) * BLOCK
        hd0 = GQA * h + 2 * half
        vt_ones = jnp.concatenate(
            [windows[sb]["vtwin"][h * HEAD_DIM:(h + 1) * HEAD_DIM, :], ones_rows], axis=0)
        o = jnp.dot(vt_ones, e, preferred_element_type=F32)
        denom = o[HEAD_DIM:HEAD_DIM + 1, :] + e_sink
        o = o[:HEAD_DIM, :] * (1.0 / denom)
        blk = jnp.concatenate([o[:, 0:BLOCK], o[:, BLOCK:2 * BLOCK]], axis=0)
        o_ref[lo:hi, hd0 * HEAD_DIM:(hd0 + 2) * HEAD_DIM] = blk.T.astype(BF16)

    cast_jobs = []
    for src, dst in ((wout_ref, wout_b_ref), (wup_ref, wup_b_ref), (wdn_ref, wdn_b_ref)):
        wc = max(LANES, CAST_JOB_ELEMS // src.shape[0])
        cast_jobs += [(src, dst, c0) + (wc,) for c0 in range(0, src.shape[1], wc)]

    def cast_piece(idx, n):
        for src, dst, c0, wc in cast_jobs[idx * len(cast_jobs) // n:(idx + 1) * len(cast_jobs) // n]:
            val = src[:, c0:c0 + wc].astype(BF16)
            if len(dst.shape) == 3:
                tf = dst.shape[2]
                dst[c0 // tf, :, c0 % tf:c0 % tf + wc] = val
            else:
                dst[:, c0:c0 + wc] = val

    nu = len(units)
    s_vals, p_vals = {}, {}
    d_probs, d_finish = ATTN_STAGE_LAG
    for step in range(nu + d_finish):
        if step < nu:
            s_vals[step] = scores(units[step])
        if d_probs <= step < nu + d_probs:
            u = step - d_probs
            p_vals[u] = probs(units[u], *s_vals.pop(u))
            cast_piece(u, nu)
            ada_piece(u, nu)
        if step >= d_finish:
            u = step - d_finish
            finish(units[u], *p_vals.pop(u))


def _attn(qt, k, vt, kc, vtc, sink, w_out, w_up, w_dn, c, c_ctx, w_ada, b_ada, ada_col0):
    tq = ATTN_TQ
    r = tq // BLOCK
    nblk = SEQ // BLOCK
    nsteps = SEQ // tq
    ada_tn = (w_ada.shape[1] - ada_col0) // nsteps
    ada_b0 = ada_col0 // ada_tn
    fixed = lambda i: (0, 0)
    row = lambda i: (i, 0)
    slab = lambda w: pl.BlockSpec((w.shape[0] // nsteps, w.shape[1]), row)
    bf16_like = lambda w: jax.ShapeDtypeStruct(w.shape, BF16)
    prev = lambda i: jnp.maximum(i * r - 1, 0)
    nxt = lambda i: jnp.minimum((i + 1) * r, nblk - 1)
    return pl.pallas_call(
        _attn_kernel,
        grid=(SEQ // tq,),
        in_specs=[
            pl.BlockSpec((ATTN_WIDTH, tq), lambda i: (0, i)),
            pl.BlockSpec((BLOCK, KV_WIDTH), lambda i: (prev(i), 0)),
            pl.BlockSpec((tq, KV_WIDTH), lambda i: (i, 0)),
            pl.BlockSpec((BLOCK, KV_WIDTH), lambda i: (nxt(i), 0)),
            pl.BlockSpec((KV_WIDTH, BLOCK), lambda i: (0, prev(i))),
            pl.BlockSpec((KV_WIDTH, tq), lambda i: (0, i)),
            pl.BlockSpec((KV_WIDTH, BLOCK), lambda i: (0, nxt(i))),
            pl.BlockSpec((CTX_LEN, KV_WIDTH), fixed),
            pl.BlockSpec((KV_WIDTH, CTX_LEN), fixed),
            pl.BlockSpec((1, N_Q_HEADS * BLOCK), fixed),
            slab(w_out), slab(w_up), slab(w_dn),
            pl.BlockSpec((1, D_MODEL), fixed),
            pl.BlockSpec((1, D_MODEL), fixed),
            pl.BlockSpec((D_MODEL, ada_tn), lambda i: (0, ada_b0 + i)),
            pl.BlockSpec((1, ada_tn), lambda i: (0, ada_b0 + i)),
        ],
        out_specs=[pl.BlockSpec((tq, ATTN_WIDTH), row), slab(w_out),
                   pl.BlockSpec((MLP_NF, w_up.shape[0] // nsteps, MLP_TF), lambda i: (0, i, 0)),
                   slab(w_dn),
                   pl.BlockSpec((SUBLANES, ada_tn), lambda i: (0, i))],
        out_shape=[jax.ShapeDtypeStruct((SEQ, ATTN_WIDTH), BF16),
                   bf16_like(w_out),
                   jax.ShapeDtypeStruct((MLP_NF, w_up.shape[0], MLP_TF), BF16),
                   bf16_like(w_dn),
                   jax.ShapeDtypeStruct((SUBLANES, w_ada.shape[1] - ada_col0), F32)],
        compiler_params=_params("arbitrary"),
        name="attn",
    )(qt, k, k, k, vt, vt, vt, kc, vtc, sink, w_out, w_up, w_dn, c, c_ctx, w_ada, b_ada)


def _mixout_kernel(x_ref, attn_ref, up_ref, um_ref, un_ref, wout_ref, pw_ref, ps_ref,
                   ga_ref, nw_ref, sh_ref, sc_ref, x1_ref, hm_ref):
    i = pl.program_id(0)
    tm = MIX_TM
    nsteps = SEQ // tm

    def u_rows(r0, c0, c1):
        parts = []
        if r0 == 0:
            parts.append(jnp.where(i > 0, up_ref[:, c0:c1], 0.0))
        lo, hi = max(r0 - POOL_HALO, 0), min(r0 + rc + POOL_HALO, tm)
        parts.append(um_ref[lo:hi, c0:c1])
        if r0 + rc == tm:
            parts.append(jnp.where(i < nsteps - 1, un_ref[:, c0:c1], 0.0))
        return jnp.concatenate(parts, axis=0)

    rc = MIX_CHUNK
    n = rc + 2 * POOL_HALO
    c = POOL_HALO
    ngroups = len(POOL_WINDOWS)
    ncol = D_MODEL // ngroups

    def pool_group(r0, g):
        w = POOL_WINDOWS[g]
        c0, c1 = g * POOL_GROUP_DIM, (g + 1) * POOL_GROUP_DIM
        e = u_rows(r0, c0, c1)
        f, span = e, 1
        while 2 * span < w:
            f = f + pltpu.roll(f, n - span, 0)
            span *= 2
        win = (pltpu.roll(f, span, 0) + f)[c:c + rc]
        t = i * tm + r0 + lax.broadcasted_iota(jnp.int32, (rc, 1), 0)
        lo = jnp.clip(t - w // 2, 0, SEQ)
        hi = jnp.clip(t - w // 2 + w, 0, SEQ)
        cnt = (hi - lo).astype(F32)
        pg = win / cnt - um_ref[r0:r0 + rc, c0:c1]
        mixed = jnp.dot(pg.astype(BF16), pw_ref[g].astype(BF16), preferred_element_type=F32)
        return (mixed * ps_ref[:, c0:c1]).astype(BF16)

    def prenorm(r0, nrows):
        rows = slice(r0, r0 + nrows)
        hm_ref[rows, :] = _rms_modulate(x1_ref[rows, :], nw_ref[...], sh_ref[LATENT_ROW],
                                        sc_ref[LATENT_ROW]).astype(BF16)

    chunk_starts = list(range(0, tm, rc))
    for ci, r0 in enumerate(chunk_starts):
        rows = slice(r0, r0 + rc)
        attn_rows = attn_ref[rows, :]
        pooled = []
        for j in range(ngroups):
            cols = slice(j * ncol, (j + 1) * ncol)
            x1_ref[rows, cols] = jnp.dot(attn_rows, wout_ref[0:ATTN_WIDTH, cols],
                                         preferred_element_type=F32)
            pooled.append(pool_group(r0, j))
        pool_rows = jnp.concatenate(pooled, axis=1)
        for j in range(ngroups):
            cols = slice(j * ncol, (j + 1) * ncol)
            o = jnp.dot(pool_rows, wout_ref[ATTN_WIDTH:, cols], preferred_element_type=F32)
            x1_ref[rows, cols] = x_ref[rows, cols] + ga_ref[LATENT_ROW, cols] * (x1_ref[rows, cols] + o)
            if ci > 0:
                prenorm(chunk_starts[ci - 1] + j * (rc // ngroups), rc // ngroups)
    for j in range(ngroups):
        prenorm(chunk_starts[-1] + j * (rc // ngroups), rc // ngroups)


def _mixout(x, attn, u, w_out_bf16, pool_w, pool_scale, mod, norm_w):
    tm = MIX_TM
    hb = tm // POOL_HALO
    nhalo = SEQ // POOL_HALO
    row = lambda i: (i, 0)
    fixed = lambda i: (0, 0)
    return pl.pallas_call(
        _mixout_kernel,
        grid=(SEQ // tm,),
        in_specs=[
            pl.BlockSpec((tm, D_MODEL), row),
            pl.BlockSpec((tm, ATTN_WIDTH), row),
            pl.BlockSpec((POOL_HALO, POOL_WIDTH), lambda i: (jnp.maximum(i * hb - 1, 0), 0)),
            pl.BlockSpec((tm, POOL_WIDTH), row),
            pl.BlockSpec((POOL_HALO, POOL_WIDTH), lambda i: (jnp.minimum((i + 1) * hb, nhalo - 1), 0)),
            pl.BlockSpec((D_MODEL, D_MODEL), fixed),
            pl.BlockSpec((len(POOL_WINDOWS), POOL_GROUP_DIM, POOL_GROUP_DIM), lambda i: (0, 0, 0)),
            pl.BlockSpec((1, POOL_WIDTH), fixed),
            _mod_spec(0),
            pl.BlockSpec((1, D_MODEL), fixed),
            _mod_spec(1),
            _mod_spec(2),
        ],
        out_specs=[pl.BlockSpec((tm, D_MODEL), row), pl.BlockSpec((tm, D_MODEL), row)],
        out_shape=[jax.ShapeDtypeStruct((SEQ, D_MODEL), F32),
                   jax.ShapeDtypeStruct((SEQ, D_MODEL), BF16)],
        compiler_params=_params("arbitrary"),
        name="mixout",
    )(x, attn, u, u, u, w_out_bf16, pool_w, pool_scale, mod, norm_w, mod, mod)


def _mlp_kernel(hm_ref, x1r_ref, wup_ref, wdn_ref, gm_ref, fw_ref, o_ref):
    f = pl.program_id(1)
    last = MLP_NF - 1
    row_chunks = [slice(r0, r0 + MLP_RC) for r0 in range(0, MLP_TM, MLP_RC)]
    gm = gm_ref[LATENT_ROW]

    def branch_out(rows):
        a = jnp.maximum(jnp.dot(hm_ref[rows, :], wup_ref[...], preferred_element_type=F32), 0.0)
        return gm * jnp.dot((a * a).astype(BF16), wdn_ref[...], preferred_element_type=F32)

    def final_norm(rows):
        x2 = o_ref[rows, :]
        y = x2 * lax.rsqrt(jnp.mean(x2 * x2, axis=-1, keepdims=True) + EPS)
        o_ref[rows, :] = y * fw_ref[...]

    @pl.when(f == 0)
    def _():
        for rows in row_chunks:
            o_ref[rows, :] = branch_out(rows)
        o_ref[0:MLP_XR, :] += x1r_ref[...]

    @pl.when(jnp.logical_and(f > 0, f < last))
    def _():
        x_rows = pl.ds(pl.multiple_of(f * MLP_XR, MLP_XR), MLP_XR)
        o_ref[x_rows, :] += x1r_ref[...]
        for rows in row_chunks:
            o_ref[rows, :] += branch_out(rows)

    @pl.when(f == last)
    def _():
        o_ref[last * MLP_XR:(last + 1) * MLP_XR, :] += x1r_ref[...]
        npiece = 4
        cw, rp = D_MODEL // npiece, MLP_RC // npiece
        for ci, rows in enumerate(row_chunks):
            a = jnp.maximum(jnp.dot(hm_ref[rows, :], wup_ref[...], preferred_element_type=F32), 0.0)
            a = (a * a).astype(BF16)
            for j in range(npiece):
                cols = slice(j * cw, (j + 1) * cw)
                o_ref[rows, cols] += gm[:, cols] * jnp.dot(a, wdn_ref[:, cols],
                                                           preferred_element_type=F32)
                if ci > 0:
                    r0 = row_chunks[ci - 1].start + j * rp
                    final_norm(slice(r0, r0 + rp))
        final_norm(row_chunks[-1])


def _mlp(hm, x1, w_up_bf16, w_dn_bf16, mod, final_w):
    tm, tf = MLP_TM, MLP_TF
    return pl.pallas_call(
        _mlp_kernel,
        grid=(SEQ // tm, MLP_NF),
        in_specs=[
            pl.BlockSpec((tm, D_MODEL), lambda i, f: (i, 0)),
            pl.BlockSpec((MLP_XR, D_MODEL), lambda i, f: (i * MLP_NF + f, 0)),
            pl.BlockSpec((None, D_MODEL, tf), lambda i, f: (f, 0, 0)),
            pl.BlockSpec((tf, D_MODEL), lambda i, f: (f, 0)),
            _mod_spec(3),
            pl.BlockSpec((1, D_MODEL), lambda i, f: (0, 0)),
        ],
        out_specs=pl.BlockSpec((tm, D_MODEL), lambda i, f: (i, 0)),
        out_shape=jax.ShapeDtypeStruct((SEQ, D_MODEL), F32),
        compiler_params=_params("arbitrary", "arbitrary"),
        name="mlp",
    )(hm, x1, w_up_bf16, w_dn_bf16, mod, final_w)


def kernel(x, c, ctx, c_ctx, norm_attn_w, norm_mlp_w, w_ada, b_ada, w_in, attn_sink,
           pool_w, pool_scale, w_out, w_mlp_up, w_mlp_down, final_norm_w):
    assert x.shape == (1, SEQ, D_MODEL) and ctx.shape == (1, CTX_LEN, D_MODEL)
    assert w_ada.shape[0] == 1, "single layer"
    x2d, ctx2d = x[0], ctx[0]

    c_ctx_row = c_ctx[None, :]
    b_row = b_ada[0][None, :]
    n_early = 2 * D_MODEL
    mod_a = _ada(c, c_ctx_row, w_ada[0], b_row, n_early)

    nw_a = norm_attn_w[0][None, :]
    tables = tuple(jnp.asarray(t) for t in _rope_tables())
    qt, k, vt, u = _inproj(x2d, nw_a, mod_a, w_in[0], tables)
    kc, vtc = _ctx_kv(ctx2d, nw_a, mod_a, w_in[0])

    sink_row = jnp.repeat(attn_sink[0].astype(F32), BLOCK)[None, :]
    attn, w_out_b, w_up_b, w_dn_b, mod_b = _attn(
        qt, k, vt, kc, vtc, sink_row, w_out[0], w_mlp_up[0], w_mlp_down[0],
        c, c_ctx_row, w_ada[0], b_row, n_early)

    x1, hm = _mixout(x2d, attn, u, w_out_b, pool_w[0], pool_scale[0][None, :], mod_b,
                     norm_mlp_w[0][None, :])
    out = _mlp(hm, x1, w_up_b, w_dn_b, mod_b, final_norm_w[None, :])
    return out[None]
```

```python
import numpy as np
import jax
import jax.numpy as jnp
from jax import lax
from jax.experimental import pallas as pl
from jax.experimental.pallas import tpu as pltpu

F32 = jnp.float32
BF16 = jnp.bfloat16

D_MODEL = 2048
SEQ = 8192
CTX_LEN = 256
GRID_W = 64
HEAD_DIM = 64
N_Q_HEADS = 16
N_KV_HEADS = 4
GQA = N_Q_HEADS // N_KV_HEADS
ATTN_WIDTH = N_Q_HEADS * HEAD_DIM
KV_WIDTH = N_KV_HEADS * HEAD_DIM
POOL_WINDOWS = (2, 4, 8, 16)
POOL_WIDTH = D_MODEL - ATTN_WIDTH
POOL_GROUP_DIM = POOL_WIDTH // len(POOL_WINDOWS)
IN_WIDTH = ATTN_WIDTH + 2 * KV_WIDTH + POOL_WIDTH
D_FF = 4 * D_MODEL
BLOCK = 128
ROPE_BASE = 10000.0
N_MOD = 6
EPS = 1e-6
NEG_INF = -1e30
LOG2E = 1.4426950408889634
Q_SCALE = HEAD_DIM ** -0.5 * LOG2E

LANES = 128
SUBLANES = 8
POOL_HALO = 8
LATENT_ROW = slice(0, 1)
CONTEXT_ROW = slice(1, 2)
VMEM_LIMIT = 56 * 1024 * 1024

ADA_TK = 256
INPROJ_TM = 512
ATTN_TQ = 512
CAST_JOB_ELEMS = 64 * 1024
ATTN_STAGE_LAG = (2, 5)
MIX_TM = 512
MIX_CHUNK = 256
MLP_TM = 1024
MLP_TF = 1024
MLP_RC = 512
MLP_NF = D_FF // MLP_TF
MLP_XR = MLP_TM // MLP_NF


def _params(*sem):
    return pltpu.CompilerParams(dimension_semantics=sem, vmem_limit_bytes=VMEM_LIMIT)


def _mod_spec(chunk):
    return pl.BlockSpec((SUBLANES, D_MODEL), lambda *_: (0, chunk))


def _rms_modulate(x, norm_w, shift, scale):
    gain = norm_w * (1.0 + scale)
    y = x * lax.rsqrt(jnp.mean(x * x, axis=-1, keepdims=True) + EPS)
    return y * gain + shift


def _silu_rows(c_ref, cctx_ref):
    cnd = jnp.concatenate([c_ref[...], cctx_ref[...], jnp.zeros((SUBLANES - 2, D_MODEL), F32)], axis=0)
    return cnd * (1.0 / (1.0 + jnp.exp(-cnd)))


def _ada_cols(s, w_ref, b_ref, cols=slice(None)):
    return jnp.dot(s, w_ref[:, cols], preferred_element_type=F32) + b_ref[:, cols]


def _ada_kernel(c_ref, cctx_ref, w_ref, b_ref, x_ref, nw_ref, wkv_ref, o_ref, k_ref, vt_ref):
    j = pl.program_id(0)
    last = pl.num_programs(0) - 1

    @pl.when(j == 0)
    def _():
        o_ref[...] = jnp.broadcast_to(b_ref[...], o_ref.shape)

    @pl.when(j < last)
    def _():
        cnd = jnp.concatenate([c_ref[...], cctx_ref[...], jnp.zeros((SUBLANES - 2, ADA_TK), F32)], axis=0)
        o_ref[...] += jnp.dot(cnd * (1.0 / (1.0 + jnp.exp(-cnd))), w_ref[...], preferred_element_type=F32)

    @pl.when(j == last)
    def _():
        shift = o_ref[CONTEXT_ROW, 0:D_MODEL]
        scale = o_ref[CONTEXT_ROW, D_MODEL:2 * D_MODEL]
        h = _rms_modulate(x_ref[...], nw_ref[...], shift, scale).astype(BF16)
        p = jnp.dot(h, wkv_ref[...].astype(BF16), preferred_element_type=F32)
        k_ref[...] = p[:, :KV_WIDTH].astype(BF16)
        vt_ref[...] = p[:, KV_WIDTH:].T.astype(BF16)


def _ada(c, c_ctx, w_ada, b_ada, n, ctx, norm_w, w_in):
    nk = D_MODEL // ADA_TK
    fixed = lambda j: (0, 0)
    slab = lambda j: jnp.minimum(j, nk - 1)
    kv_block = ATTN_WIDTH // (2 * KV_WIDTH)
    return pl.pallas_call(
        _ada_kernel,
        grid=(nk + 1,),
        in_specs=[
            pl.BlockSpec((1, ADA_TK), lambda j: (0, slab(j))),
            pl.BlockSpec((1, ADA_TK), lambda j: (0, slab(j))),
            pl.BlockSpec((ADA_TK, n), lambda j: (slab(j), 0)),
            pl.BlockSpec((1, n), fixed),
            pl.BlockSpec((CTX_LEN, D_MODEL), fixed),
            pl.BlockSpec((1, D_MODEL), fixed),
            pl.BlockSpec((D_MODEL, 2 * KV_WIDTH), lambda j: (0, kv_block)),
        ],
        out_specs=[
            pl.BlockSpec((SUBLANES, n), fixed),
            pl.BlockSpec((CTX_LEN, KV_WIDTH), fixed),
            pl.BlockSpec((KV_WIDTH, CTX_LEN), fixed),
        ],
        out_shape=[
            jax.ShapeDtypeStruct((SUBLANES, n), F32),
            jax.ShapeDtypeStruct((CTX_LEN, KV_WIDTH), BF16),
            jax.ShapeDtypeStruct((KV_WIDTH, CTX_LEN), BF16),
        ],
        compiler_params=_params("arbitrary"),
        name="ada",
    )(c, c_ctx, w_ada, b_ada, ctx, norm_w, w_in)


def _rope_tables():
    half = HEAD_DIM // 2
    inv_freq = ROPE_BASE ** (-np.arange(0, half, 2, dtype=np.float64) / half)
    t = np.arange(SEQ)
    row = (t // GRID_W).astype(np.float64)
    col = (t % GRID_W).astype(np.float64)
    ang_r = row[:, None] * inv_freq[None, :]
    ang_c = col[:, None] * inv_freq[None, :]
    cos_t = Q_SCALE * np.concatenate([np.cos(ang_r), np.cos(ang_c)], axis=1).T
    sin_t = Q_SCALE * np.concatenate([np.sin(ang_r), np.sin(ang_c)], axis=1).T
    cos_l = np.concatenate([np.cos(ang_r), np.cos(ang_r), np.cos(ang_c), np.cos(ang_c)], axis=1)
    zeros = np.zeros_like(ang_r)
    s_up = np.concatenate([-np.sin(ang_r), zeros, -np.sin(ang_c), zeros], axis=1)
    s_dn = np.concatenate([zeros, np.sin(ang_r), zeros, np.sin(ang_c)], axis=1)
    tile2 = lambda a: np.concatenate([a, a], axis=1)
    q_table = np.concatenate([cos_t, sin_t], axis=0).astype(np.float32)
    k_table = np.concatenate([tile2(cos_l), tile2(s_up), tile2(s_dn)], axis=1).astype(np.float32)
    return q_table, k_table


def _inproj_kernel(x_ref, nw_ref, sh_ref, sc_ref, w_ref, qtab_ref, ktab_ref,
                   qt_ref, k_ref, vt_ref, u_ref, wb_ref):
    @pl.when(pl.program_id(0) == 0)
    def _():
        for c0 in range(0, IN_WIDTH, 2 * LANES):
            wb_ref[:, c0:c0 + 2 * LANES] = w_ref[:, c0:c0 + 2 * LANES].astype(BF16)

    h = _rms_modulate(x_ref[...], nw_ref[...], sh_ref[LATENT_ROW], sc_ref[LATENT_ROW]).astype(BF16)
    p = jnp.dot(h, wb_ref[...], preferred_element_type=F32)

    qt = p[:, :ATTN_WIDTH].T
    cr, cc = qtab_ref[0:16, :], qtab_ref[16:32, :]
    sr, sc = qtab_ref[32:48, :], qtab_ref[48:64, :]
    pieces = []
    for hd in range(N_Q_HEADS):
        b = hd * HEAD_DIM
        x1r, x2r = qt[b:b + 16, :], qt[b + 16:b + 32, :]
        x1c, x2c = qt[b + 32:b + 48, :], qt[b + 48:b + 64, :]
        pieces += [x1r * cr - x2r * sr, x1r * sr + x2r * cr,
                   x1c * cc - x2c * sc, x1c * sc + x2c * cc]
    qt_ref[...] = jnp.concatenate(pieces, axis=0).astype(BF16)

    k = p[:, ATTN_WIDTH:ATTN_WIDTH + KV_WIDTH]
    cos_l, s_up, s_dn = (ktab_ref[:, j * LANES:(j + 1) * LANES] for j in range(3))
    kparts = []
    for g in range(KV_WIDTH // LANES):
        kg = k[:, g * LANES:(g + 1) * LANES]
        kparts.append(kg * cos_l + pltpu.roll(kg, LANES - 16, 1) * s_up
                      + pltpu.roll(kg, 16, 1) * s_dn)
    k_ref[...] = jnp.concatenate(kparts, axis=1).astype(BF16)

    vt_ref[...] = p[:, ATTN_WIDTH + KV_WIDTH:ATTN_WIDTH + 2 * KV_WIDTH].T.astype(BF16)
    u_ref[...] = p[:, ATTN_WIDTH + 2 * KV_WIDTH:]


def _inproj(x, norm_w, mod, w_in, tables):
    q_table, k_table = tables
    tm = INPROJ_TM
    row = lambda i: (i, 0)
    colb = lambda i: (0, i)
    fixed = lambda i: (0, 0)
    return pl.pallas_call(
        _inproj_kernel,
        grid=(SEQ // tm,),
        in_specs=[
            pl.BlockSpec((tm, D_MODEL), row),
            pl.BlockSpec((1, D_MODEL), fixed),
            _mod_spec(0),
            _mod_spec(1),
            pl.BlockSpec((D_MODEL, IN_WIDTH), fixed, pipeline_mode=pl.Buffered(1)),
            pl.BlockSpec((q_table.shape[0], tm), colb),
            pl.BlockSpec((tm, k_table.shape[1]), row),
        ],
        out_specs=[
            pl.BlockSpec((ATTN_WIDTH, tm), colb),
            pl.BlockSpec((tm, KV_WIDTH), row),
            pl.BlockSpec((KV_WIDTH, tm), colb),
            pl.BlockSpec((tm, POOL_WIDTH), row),
        ],
        out_shape=[
            jax.ShapeDtypeStruct((ATTN_WIDTH, SEQ), BF16),
            jax.ShapeDtypeStruct((SEQ, KV_WIDTH), BF16),
            jax.ShapeDtypeStruct((KV_WIDTH, SEQ), BF16),
            jax.ShapeDtypeStruct((SEQ, POOL_WIDTH), F32),
        ],
        scratch_shapes=[pltpu.VMEM((D_MODEL, IN_WIDTH), BF16)],
        compiler_params=_params("arbitrary"),
        name="inproj",
    )(x, norm_w, mod, mod, w_in, q_table, k_table)


def _attn_kernel(qt_ref, kp_ref, km_ref, kn_ref, vtp_ref, vtm_ref, vtn_ref,
                 kc_ref, vtc_ref, sink_ref, wout_ref, wup_ref, wdn_ref,
                 c_ref, cctx_ref, wada_ref, bada_ref,
                 o_ref, wout_b_ref, wup_b_ref, wdn_b_ref, mod_ref):
    cond_act = _silu_rows(c_ref, cctx_ref)
    ada_cols = [slice(c0, c0 + 2 * LANES) for c0 in range(0, mod_ref.shape[1], 2 * LANES)]

    def ada_piece(idx, n):
        for cols in ada_cols[idx * len(ada_cols) // n:(idx + 1) * len(ada_cols) // n]:
            mod_ref[:, cols] = _ada_cols(cond_act, wada_ref, bada_ref, cols)

    i = pl.program_id(0)
    nsub = ATTN_TQ // BLOCK
    nblk = SEQ // BLOCK
    pw = 2 * BLOCK
    nkeys = 3 * BLOCK + CTX_LEN

    key_j = lax.broadcasted_iota(jnp.int32, (BLOCK, pw), 0)
    qry_i = lax.broadcasted_iota(jnp.int32, (BLOCK, pw), 1) % BLOCK
    neg = jnp.full((BLOCK, pw), NEG_INF, F32)
    zero = jnp.zeros((BLOCK, pw), F32)
    band_prev = jnp.where(key_j >= qry_i, zero, neg)
    band_next = jnp.where(key_j <= qry_i, zero, neg)

    kc = kc_ref[...]
    vtc = vtc_ref[...]
    ones_rows = jnp.ones((2 * SUBLANES, nkeys), BF16)
    sink_row = sink_ref[...] * LOG2E

    windows = []
    for sb in range(nsub):
        n = i * nsub + sb
        lo, hi = sb * BLOCK, (sb + 1) * BLOCK
        k_prev = kp_ref[...] if sb == 0 else km_ref[lo - BLOCK:lo, :]
        k_next = kn_ref[...] if sb == nsub - 1 else km_ref[hi:hi + BLOCK, :]
        vt_prev = vtp_ref[...] if sb == 0 else vtm_ref[:, lo - BLOCK:lo]
        vt_next = vtn_ref[...] if sb == nsub - 1 else vtm_ref[:, hi:hi + BLOCK]
        windows.append(dict(
            kwin=jnp.concatenate([k_prev, km_ref[lo:hi, :], k_next, kc], axis=0),
            vtwin=jnp.concatenate([vt_prev, vtm_ref[:, lo:hi], vt_next, vtc], axis=1),
            bias_prev=band_prev + jnp.where(n > 0, 0.0, NEG_INF),
            bias_next=band_next + jnp.where(n < nblk - 1, 0.0, NEG_INF)))

    units = [(sb, h, half) for sb in range(nsub) for h in range(N_KV_HEADS)
             for half in range(GQA // 2)]

    def scores(unit):
        sb, h, half = unit
        win = windows[sb]
        lo, hi = sb * BLOCK, (sb + 1) * BLOCK
        hd0 = GQA * h + 2 * half
        qh = jnp.concatenate([qt_ref[(hd0 + g) * HEAD_DIM:(hd0 + g + 1) * HEAD_DIM, lo:hi]
                              for g in range(2)], axis=1)
        blocks = []
        if h > 0:
            blocks.append(jnp.zeros((h * HEAD_DIM, pw), BF16))
        blocks.append(qh)
        if h < N_KV_HEADS - 1:
            blocks.append(jnp.zeros(((N_KV_HEADS - 1 - h) * HEAD_DIM, pw), BF16))
        s = jnp.dot(win["kwin"], jnp.concatenate(blocks, axis=0),
                    preferred_element_type=F32)
        s = jnp.concatenate([s[0:BLOCK] + win["bias_prev"], s[BLOCK:2 * BLOCK],
                             s[2 * BLOCK:3 * BLOCK] + win["bias_next"], s[3 * BLOCK:]], axis=0)
        return s, jnp.max(s, axis=0, keepdims=True)

    def probs(unit, s, smax):
        sb, h, half = unit
        hd0 = GQA * h + 2 * half
        sink = sink_row[:, hd0 * BLOCK:(hd0 + 2) * BLOCK]
        m = jnp.maximum(smax, sink)
        return jnp.exp2(s - m).astype(BF16), jnp.exp2(sink - m)

    def finish(unit, e, e_sink):
        sb, h, half = unit
        lo, hi = sb * BLOCK, (sb + 1) * BLOCK
        hd0 = GQA * h + 2 * half
        vt_ones = jnp.concatenate(
            [windows[sb]["vtwin"][h * HEAD_DIM:(h + 1) * HEAD_DIM, :], ones_rows], axis=0)
        o = jnp.dot(vt_ones, e, preferred_element_type=F32)
        denom = o[HEAD_DIM:HEAD_DIM + 1, :] + e_sink
        o = o[:HEAD_DIM, :] * (1.0 / denom)
        blk = jnp.concatenate([o[:, 0:BLOCK], o[:, BLOCK:2 * BLOCK]], axis=0)
        o_ref[lo:hi, hd0 * HEAD_DIM:(hd0 + 2) * HEAD_DIM] = blk.T.astype(BF16)

    cast_jobs = []
    for src, dst in ((wout_ref, wout_b_ref), (wup_ref, wup_b_ref), (wdn_ref, wdn_b_ref)):
        wc = max(LANES, CAST_JOB_ELEMS // src.shape[0])
        cast_jobs += [(src, dst, c0) + (wc,) for c0 in range(0, src.shape[1], wc)]

    def cast_piece(idx, n):
        for src, dst, c0, wc in cast_jobs[idx * len(cast_jobs) // n:(idx + 1) * len(cast_jobs) // n]:
            val = src[:, c0:c0 + wc].astype(BF16)
            if len(dst.shape) == 3:
                tf = dst.shape[2]
                dst[c0 // tf, :, c0 % tf:c0 % tf + wc] = val
            else:
                dst[:, c0:c0 + wc] = val

    nu = len(units)
    s_vals, p_vals = {}, {}
    d_probs, d_finish = ATTN_STAGE_LAG
    for step in range(nu + d_finish):
        if step < nu:
            s_vals[step] = scores(units[step])
        if d_probs <= step < nu + d_probs:
            u = step - d_probs
            p_vals[u] = probs(units[u], *s_vals.pop(u))
            cast_piece(u, nu)
            ada_piece(u, nu)
        if step >= d_finish:
            u = step - d_finish
            finish(units[u], *p_vals.pop(u))


def _attn(qt, k, vt, kc, vtc, sink, w_out, w_up, w_dn, c, c_ctx, w_ada, b_ada, ada_col0):
    tq = ATTN_TQ
    r = tq // BLOCK
    nblk = SEQ // BLOCK
    nsteps = SEQ // tq
    ada_tn = (w_ada.shape[1] - ada_col0) // nsteps
    ada_b0 = ada_col0 // ada_tn
    fixed = lambda i: (0, 0)
    row = lambda i: (i, 0)
    slab = lambda w: pl.BlockSpec((w.shape[0] // nsteps, w.shape[1]), row)
    bf16_like = lambda w: jax.ShapeDtypeStruct(w.shape, BF16)
    prev = lambda i: jnp.maximum(i * r - 1, 0)
    nxt = lambda i: jnp.minimum((i + 1) * r, nblk - 1)
    return pl.pallas_call(
        _attn_kernel,
        grid=(SEQ // tq,),
        in_specs=[
            pl.BlockSpec((ATTN_WIDTH, tq), lambda i: (0, i)),
            pl.BlockSpec((BLOCK, KV_WIDTH), lambda i: (prev(i), 0)),
            pl.BlockSpec((tq, KV_WIDTH), lambda i: (i, 0)),
            pl.BlockSpec((BLOCK, KV_WIDTH), lambda i: (nxt(i), 0)),
            pl.BlockSpec((KV_WIDTH, BLOCK), lambda i: (0, prev(i))),
            pl.BlockSpec((KV_WIDTH, tq), lambda i: (0, i)),
            pl.BlockSpec((KV_WIDTH, BLOCK), lambda i: (0, nxt(i))),
            pl.BlockSpec((CTX_LEN, KV_WIDTH), fixed),
            pl.BlockSpec((KV_WIDTH, CTX_LEN), fixed),
            pl.BlockSpec((1, N_Q_HEADS * BLOCK), fixed),
            slab(w_out), slab(w_up), slab(w_dn),
            pl.BlockSpec((1, D_MODEL), fixed),
            pl.BlockSpec((1, D_MODEL), fixed),
            pl.BlockSpec((D_MODEL, ada_tn), lambda i: (0, ada_b0 + i)),
            pl.BlockSpec((1, ada_tn), lambda i: (0, ada_b0 + i)),
        ],
        out_specs=[pl.BlockSpec((tq, ATTN_WIDTH), row), slab(w_out),
                   pl.BlockSpec((MLP_NF, w_up.shape[0] // nsteps, MLP_TF), lambda i: (0, i, 0)),
                   slab(w_dn),
                   pl.BlockSpec((SUBLANES, ada_tn), lambda i: (0, i))],
        out_shape=[jax.ShapeDtypeStruct((SEQ, ATTN_WIDTH), BF16),
                   bf16_like(w_out),
                   jax.ShapeDtypeStruct((MLP_NF, w_up.shape[0], MLP_TF), BF16),
                   bf16_like(w_dn),
                   jax.ShapeDtypeStruct((SUBLANES, w_ada.shape[1] - ada_col0), F32)],
        compiler_params=_params("arbitrary"),
        name="attn",
    )(qt, k, k, k, vt, vt, vt, kc, vtc, sink, w_out, w_up, w_dn, c, c_ctx, w_ada, b_ada)


def _mixout_kernel(x_ref, attn_ref, up_ref, um_ref, un_ref, wout_ref, pw_ref, ps_ref,
                   ga_ref, nw_ref, sh_ref, sc_ref, x1_ref, hm_ref):
    i = pl.program_id(0)
    tm = MIX_TM
    nsteps = SEQ // tm

    def u_rows(r0, c0, c1):
        parts = []
        if r0 == 0:
            parts.append(jnp.where(i > 0, up_ref[:, c0:c1], 0.0))
        lo, hi = max(r0 - POOL_HALO, 0), min(r0 + rc + POOL_HALO, tm)
        parts.append(um_ref[lo:hi, c0:c1])
        if r0 + rc == tm:
            parts.append(jnp.where(i < nsteps - 1, un_ref[:, c0:c1], 0.0))
        return jnp.concatenate(parts, axis=0)

    rc = MIX_CHUNK
    n = rc + 2 * POOL_HALO
    c = POOL_HALO
    ngroups = len(POOL_WINDOWS)
    ncol = D_MODEL // ngroups

    def pool_group(r0, g):
        w = POOL_WINDOWS[g]
        c0, c1 = g * POOL_GROUP_DIM, (g + 1) * POOL_GROUP_DIM
        e = u_rows(r0, c0, c1)
        f, span = e, 1
        while 2 * span < w:
            f = f + pltpu.roll(f, n - span, 0)
            span *= 2
        win = (pltpu.roll(f, span, 0) + f)[c:c + rc]
        t = i * tm + r0 + lax.broadcasted_iota(jnp.int32, (rc, 1), 0)
        lo = jnp.clip(t - w // 2, 0, SEQ)
        hi = jnp.clip(t - w // 2 + w, 0, SEQ)
        cnt = (hi - lo).astype(F32)
        pg = win / cnt - um_ref[r0:r0 + rc, c0:c1]
        mixed = jnp.dot(pg.astype(BF16), pw_ref[g].astype(BF16), preferred_element_type=F32)
        return (mixed * ps_ref[:, c0:c1]).astype(BF16)

    def prenorm(r0, nrows):
        rows = slice(r0, r0 + nrows)
        hm_ref[rows, :] = _rms_modulate(x1_ref[rows, :], nw_ref[...], sh_ref[LATENT_ROW],
                                        sc_ref[LATENT_ROW]).astype(BF16)

    chunk_starts = list(range(0, tm, rc))
    for ci, r0 in enumerate(chunk_starts):
        rows = slice(r0, r0 + rc)
        attn_rows = attn_ref[rows, :]
        pooled = []
        for j in range(ngroups):
            cols = slice(j * ncol, (j + 1) * ncol)
            x1_ref[rows, cols] = jnp.dot(attn_rows, wout_ref[0:ATTN_WIDTH, cols],
                                         preferred_element_type=F32)
            pooled.append(pool_group(r0, j))
        pool_rows = jnp.concatenate(pooled, axis=1)
        for j in range(ngroups):
            cols = slice(j * ncol, (j + 1) * ncol)
            o = jnp.dot(pool_rows, wout_ref[ATTN_WIDTH:, cols], preferred_element_type=F32)
            x1_ref[rows, cols] = x_ref[rows, cols] + ga_ref[LATENT_ROW, cols] * (x1_ref[rows, cols] + o)
            if ci > 0:
                prenorm(chunk_starts[ci - 1] + j * (rc // ngroups), rc // ngroups)
    for j in range(ngroups):
        prenorm(chunk_starts[-1] + j * (rc // ngroups), rc // ngroups)


def _mixout(x, attn, u, w_out_bf16, pool_w, pool_scale, mod, norm_w):
    tm = MIX_TM
    hb = tm // POOL_HALO
    nhalo = SEQ // POOL_HALO
    row = lambda i: (i, 0)
    fixed = lambda i: (0, 0)
    return pl.pallas_call(
        _mixout_kernel,
        grid=(SEQ // tm,),
        in_specs=[
            pl.BlockSpec((tm, D_MODEL), row),
            pl.BlockSpec((tm, ATTN_WIDTH), row),
            pl.BlockSpec((POOL_HALO, POOL_WIDTH), lambda i: (jnp.maximum(i * hb - 1, 0), 0)),
            pl.BlockSpec((tm, POOL_WIDTH), row),
            pl.BlockSpec((POOL_HALO, POOL_WIDTH), lambda i: (jnp.minimum((i + 1) * hb, nhalo - 1), 0)),
            pl.BlockSpec((D_MODEL, D_MODEL), fixed),
            pl.BlockSpec((len(POOL_WINDOWS), POOL_GROUP_DIM, POOL_GROUP_DIM), lambda i: (0, 0, 0)),
            pl.BlockSpec((1, POOL_WIDTH), fixed),
            _mod_spec(0),
            pl.BlockSpec((1, D_MODEL), fixed),
            _mod_spec(1),
            _mod_spec(2),
        ],
        out_specs=[pl.BlockSpec((tm, D_MODEL), row), pl.BlockSpec((tm, D_MODEL), row)],
        out_shape=[jax.ShapeDtypeStruct((SEQ, D_MODEL), F32),
                   jax.ShapeDtypeStruct((SEQ, D_MODEL), BF16)],
        compiler_params=_params("arbitrary"),
        name="mixout",
    )(x, attn, u, u, u, w_out_bf16, pool_w, pool_scale, mod, norm_w, mod, mod)


def _mlp_kernel(hm_ref, x1r_ref, wup_ref, wdn_ref, gm_ref, fw_ref, o_ref):
    f = pl.program_id(1)
    last = MLP_NF - 1
    row_chunks = [slice(r0, r0 + MLP_RC) for r0 in range(0, MLP_TM, MLP_RC)]
    gm = gm_ref[LATENT_ROW]

    def branch_out(rows):
        a = jnp.maximum(jnp.dot(hm_ref[rows, :], wup_ref[...], preferred_element_type=F32), 0.0)
        return gm * jnp.dot((a * a).astype(BF16), wdn_ref[...], preferred_element_type=F32)

    def final_norm(rows):
        x2 = o_ref[rows, :]
        y = x2 * lax.rsqrt(jnp.mean(x2 * x2, axis=-1, keepdims=True) + EPS)
        o_ref[rows, :] = y * fw_ref[...]

    @pl.when(f == 0)
    def _():
        for rows in row_chunks:
            o_ref[rows, :] = branch_out(rows)
        o_ref[0:MLP_XR, :] += x1r_ref[...]

    @pl.when(jnp.logical_and(f > 0, f < last))
    def _():
        x_rows = pl.ds(pl.multiple_of(f * MLP_XR, MLP_XR), MLP_XR)
        o_ref[x_rows, :] += x1r_ref[...]
        for rows in row_chunks:
            o_ref[rows, :] += branch_out(rows)

    @pl.when(f == last)
    def _():
        o_ref[last * MLP_XR:(last + 1) * MLP_XR, :] += x1r_ref[...]
        npiece = 4
        cw, rp = D_MODEL // npiece, MLP_RC // npiece
        for ci, rows in enumerate(row_chunks):
            a = jnp.maximum(jnp.dot(hm_ref[rows, :], wup_ref[...], preferred_element_type=F32), 0.0)
            a = (a * a).astype(BF16)
            for j in range(npiece):
                cols = slice(j * cw, (j + 1) * cw)
                o_ref[rows, cols] += gm[:, cols] * jnp.dot(a, wdn_ref[:, cols],
                                                           preferred_element_type=F32)
                if ci > 0:
                    r0 = row_chunks[ci - 1].start + j * rp
                    final_norm(slice(r0, r0 + rp))
        final_norm(row_chunks[-1])


def _mlp(hm, x1, w_up_bf16, w_dn_bf16, mod, final_w):
    tm, tf = MLP_TM, MLP_TF
    return pl.pallas_call(
        _mlp_kernel,
        grid=(SEQ // tm, MLP_NF),
        in_specs=[
            pl.BlockSpec((tm, D_MODEL), lambda i, f: (i, 0)),
            pl.BlockSpec((MLP_XR, D_MODEL), lambda i, f: (i * MLP_NF + f, 0)),
            pl.BlockSpec((None, D_MODEL, tf), lambda i, f: (f, 0, 0)),
            pl.BlockSpec((tf, D_MODEL), lambda i, f: (f, 0)),
            _mod_spec(3),
            pl.BlockSpec((1, D_MODEL), lambda i, f: (0, 0)),
        ],
        out_specs=pl.BlockSpec((tm, D_MODEL), lambda i, f: (i, 0)),
        out_shape=jax.ShapeDtypeStruct((SEQ, D_MODEL), F32),
        compiler_params=_params("arbitrary", "arbitrary"),
        name="mlp",
    )(hm, x1, w_up_bf16, w_dn_bf16, mod, final_w)


def kernel(x, c, ctx, c_ctx, norm_attn_w, norm_mlp_w, w_ada, b_ada, w_in, attn_sink,
           pool_w, pool_scale, w_out, w_mlp_up, w_mlp_down, final_norm_w):
    assert x.shape == (1, SEQ, D_MODEL) and ctx.shape == (1, CTX_LEN, D_MODEL)
    assert w_ada.shape[0] == 1, "single layer"
    x2d, ctx2d = x[0], ctx[0]

    c_ctx_row = c_ctx[None, :]
    b_row = b_ada[0][None, :]
    n_early = 2 * D_MODEL
    nw_a = norm_attn_w[0][None, :]
    mod_a, kc, vtc = _ada(c, c_ctx_row, w_ada[0], b_row, n_early, ctx2d, nw_a, w_in[0])

    tables = tuple(jnp.asarray(t) for t in _rope_tables())
    qt, k, vt, u = _inproj(x2d, nw_a, mod_a, w_in[0], tables)

    sink_row = jnp.repeat(attn_sink[0].astype(F32), BLOCK)[None, :]
    attn, w_out_b, w_up_b, w_dn_b, mod_b = _attn(
        qt, k, vt, kc, vtc, sink_row, w_out[0], w_mlp_up[0], w_mlp_down[0],
        c, c_ctx_row, w_ada[0], b_row, n_early)

    x1, hm = _mixout(x2d, attn, u, w_out_b, pool_w[0], pool_scale[0][None, :], mod_b,
                     norm_mlp_w[0][None, :])
    out = _mlp(hm, x1, w_up_b, w_dn_b, mod_b, final_norm_w[None, :])
    return out[None]
```

```python
import numpy as np
import jax
import jax.numpy as jnp
from jax import lax
from jax.experimental import pallas as pl
from jax.experimental.pallas import tpu as pltpu

F32 = jnp.float32
BF16 = jnp.bfloat16

D_MODEL = 2048
SEQ = 8192
CTX_LEN = 256
GRID_W = 64
HEAD_DIM = 64
N_Q_HEADS = 16
N_KV_HEADS = 4
GQA = N_Q_HEADS // N_KV_HEADS
ATTN_WIDTH = N_Q_HEADS * HEAD_DIM
KV_WIDTH = N_KV_HEADS * HEAD_DIM
POOL_WINDOWS = (2, 4, 8, 16)
POOL_WIDTH = D_MODEL - ATTN_WIDTH
POOL_GROUP_DIM = POOL_WIDTH // len(POOL_WINDOWS)
IN_WIDTH = ATTN_WIDTH + 2 * KV_WIDTH + POOL_WIDTH
D_FF = 4 * D_MODEL
BLOCK = 128
ROPE_BASE = 10000.0
N_MOD = 6
EPS = 1e-6
NEG_INF = -1e30
LOG2E = 1.4426950408889634
Q_SCALE = HEAD_DIM ** -0.5 * LOG2E

LANES = 128
SUBLANES = 8
POOL_HALO = 8
LATENT_ROW = slice(0, 1)
CONTEXT_ROW = slice(1, 2)
VMEM_LIMIT = 56 * 1024 * 1024

ADA_TK = 256
INPROJ_TM = 512
ATTN_TQ = 512
CAST_JOB_ELEMS = 64 * 1024
ATTN_STAGE_LAG = (2, 5)
MIX_TM = 512
MIX_CHUNK = 256
MLP_TM = 1024
MLP_TF = 1024
MLP_RC = 512
MLP_NF = D_FF // MLP_TF
MLP_XR = MLP_TM // MLP_NF


def _params(*sem):
    return pltpu.CompilerParams(dimension_semantics=sem, vmem_limit_bytes=VMEM_LIMIT)


def _mod_spec(chunk):
    return pl.BlockSpec((SUBLANES, D_MODEL), lambda *_: (0, chunk))


def _rms_modulate(x, norm_w, shift, scale):
    gain = norm_w * (1.0 + scale)
    y = x * lax.rsqrt(jnp.mean(x * x, axis=-1, keepdims=True) + EPS)
    return y * gain + shift


def _silu_rows(c_ref, cctx_ref):
    cnd = jnp.concatenate([c_ref[...], cctx_ref[...], jnp.zeros((SUBLANES - 2, D_MODEL), F32)], axis=0)
    return cnd * (1.0 / (1.0 + jnp.exp(-cnd)))


def _ada_cols(s, w_ref, b_ref, cols=slice(None)):
    return jnp.dot(s, w_ref[:, cols], preferred_element_type=F32) + b_ref[:, cols]


def _ada_kernel(c_ref, cctx_ref, w_ref, b_ref, x_ref, nw_ref, wkv_ref, sink_ref,
                o_ref, k_ref, vt_ref, srow_ref):
    j = pl.program_id(0)
    last = pl.num_programs(0) - 1

    @pl.when(j == 0)
    def _():
        o_ref[...] = jnp.broadcast_to(b_ref[...], o_ref.shape)

    @pl.when(j < last)
    def _():
        cnd = jnp.concatenate([c_ref[...], cctx_ref[...], jnp.zeros((SUBLANES - 2, ADA_TK), F32)], axis=0)
        o_ref[...] += jnp.dot(cnd * (1.0 / (1.0 + jnp.exp(-cnd))), w_ref[...], preferred_element_type=F32)

    @pl.when(j == last)
    def _():
        shift = o_ref[CONTEXT_ROW, 0:D_MODEL]
        scale = o_ref[CONTEXT_ROW, D_MODEL:2 * D_MODEL]
        h = _rms_modulate(x_ref[...], nw_ref[...], shift, scale).astype(BF16)
        p = jnp.dot(h, wkv_ref[...].astype(BF16), preferred_element_type=F32)
        k_ref[...] = p[:, :KV_WIDTH].astype(BF16)
        vt_ref[...] = p[:, KV_WIDTH:].T.astype(BF16)
        sink = sink_ref[...].astype(F32)
        group = lax.broadcasted_iota(jnp.int32, srow_ref.shape, 1) // BLOCK
        row = jnp.zeros(srow_ref.shape, F32)
        for hd in range(N_Q_HEADS):
            row = jnp.where(group == hd, sink[:, hd:hd + 1], row)
        srow_ref[...] = row


def _ada(c, c_ctx, w_ada, b_ada, n, ctx, norm_w, w_in, sink):
    nk = D_MODEL // ADA_TK
    fixed = lambda j: (0, 0)
    slab = lambda j: jnp.minimum(j, nk - 1)
    kv_block = ATTN_WIDTH // (2 * KV_WIDTH)
    return pl.pallas_call(
        _ada_kernel,
        grid=(nk + 1,),
        in_specs=[
            pl.BlockSpec((1, ADA_TK), lambda j: (0, slab(j))),
            pl.BlockSpec((1, ADA_TK), lambda j: (0, slab(j))),
            pl.BlockSpec((ADA_TK, n), lambda j: (slab(j), 0)),
            pl.BlockSpec((1, n), fixed),
            pl.BlockSpec((CTX_LEN, D_MODEL), fixed),
            pl.BlockSpec((1, D_MODEL), fixed),
            pl.BlockSpec((D_MODEL, 2 * KV_WIDTH), lambda j: (0, kv_block)),
            pl.BlockSpec((1, N_Q_HEADS), fixed),
        ],
        out_specs=[
            pl.BlockSpec((SUBLANES, n), fixed),
            pl.BlockSpec((CTX_LEN, KV_WIDTH), fixed),
            pl.BlockSpec((KV_WIDTH, CTX_LEN), fixed),
            pl.BlockSpec((1, N_Q_HEADS * BLOCK), fixed),
        ],
        out_shape=[
            jax.ShapeDtypeStruct((SUBLANES, n), F32),
            jax.ShapeDtypeStruct((CTX_LEN, KV_WIDTH), BF16),
            jax.ShapeDtypeStruct((KV_WIDTH, CTX_LEN), BF16),
            jax.ShapeDtypeStruct((1, N_Q_HEADS * BLOCK), F32),
        ],
        compiler_params=_params("arbitrary"),
        name="ada",
    )(c, c_ctx, w_ada, b_ada, ctx, norm_w, w_in, sink)


def _rope_tables():
    half = HEAD_DIM // 2
    inv_freq = ROPE_BASE ** (-np.arange(0, half, 2, dtype=np.float64) / half)
    t = np.arange(SEQ)
    row = (t // GRID_W).astype(np.float64)
    col = (t % GRID_W).astype(np.float64)
    ang_r = row[:, None] * inv_freq[None, :]
    ang_c = col[:, None] * inv_freq[None, :]
    cos_t = Q_SCALE * np.concatenate([np.cos(ang_r), np.cos(ang_c)], axis=1).T
    sin_t = Q_SCALE * np.concatenate([np.sin(ang_r), np.sin(ang_c)], axis=1).T
    cos_l = np.concatenate([np.cos(ang_r), np.cos(ang_r), np.cos(ang_c), np.cos(ang_c)], axis=1)
    zeros = np.zeros_like(ang_r)
    s_up = np.concatenate([-np.sin(ang_r), zeros, -np.sin(ang_c), zeros], axis=1)
    s_dn = np.concatenate([zeros, np.sin(ang_r), zeros, np.sin(ang_c)], axis=1)
    tile2 = lambda a: np.concatenate([a, a], axis=1)
    q_table = np.concatenate([cos_t, sin_t], axis=0).astype(np.float32)
    k_table = np.concatenate([tile2(cos_l), tile2(s_up), tile2(s_dn)], axis=1).astype(np.float32)
    return q_table, k_table


def _inproj_kernel(x_ref, nw_ref, sh_ref, sc_ref, w_ref, qtab_ref, ktab_ref,
                   qt_ref, k_ref, vt_ref, u_ref, wb_ref):
    @pl.when(pl.program_id(0) == 0)
    def _():
        for c0 in range(0, IN_WIDTH, 2 * LANES):
            wb_ref[:, c0:c0 + 2 * LANES] = w_ref[:, c0:c0 + 2 * LANES].astype(BF16)

    h = _rms_modulate(x_ref[...], nw_ref[...], sh_ref[LATENT_ROW], sc_ref[LATENT_ROW]).astype(BF16)
    p = jnp.dot(h, wb_ref[...], preferred_element_type=F32)

    qt = p[:, :ATTN_WIDTH].T
    cr, cc = qtab_ref[0:16, :], qtab_ref[16:32, :]
    sr, sc = qtab_ref[32:48, :], qtab_ref[48:64, :]
    pieces = []
    for hd in range(N_Q_HEADS):
        b = hd * HEAD_DIM
        x1r, x2r = qt[b:b + 16, :], qt[b + 16:b + 32, :]
        x1c, x2c = qt[b + 32:b + 48, :], qt[b + 48:b + 64, :]
        pieces += [x1r * cr - x2r * sr, x1r * sr + x2r * cr,
                   x1c * cc - x2c * sc, x1c * sc + x2c * cc]
    qt_ref[...] = jnp.concatenate(pieces, axis=0).astype(BF16)

    k = p[:, ATTN_WIDTH:ATTN_WIDTH + KV_WIDTH]
    cos_l, s_up, s_dn = (ktab_ref[:, j * LANES:(j + 1) * LANES] for j in range(3))
    kparts = []
    for g in range(KV_WIDTH // LANES):
        kg = k[:, g * LANES:(g + 1) * LANES]
        kparts.append(kg * cos_l + pltpu.roll(kg, LANES - 16, 1) * s_up
                      + pltpu.roll(kg, 16, 1) * s_dn)
    k_ref[...] = jnp.concatenate(kparts, axis=1).astype(BF16)

    vt_ref[...] = p[:, ATTN_WIDTH + KV_WIDTH:ATTN_WIDTH + 2 * KV_WIDTH].T.astype(BF16)
    u_ref[...] = p[:, ATTN_WIDTH + 2 * KV_WIDTH:]


def _inproj(x, norm_w, mod, w_in, tables):
    q_table, k_table = tables
    tm = INPROJ_TM
    row = lambda i: (i, 0)
    colb = lambda i: (0, i)
    fixed = lambda i: (0, 0)
    return pl.pallas_call(
        _inproj_kernel,
        grid=(SEQ // tm,),
        in_specs=[
            pl.BlockSpec((tm, D_MODEL), row),
            pl.BlockSpec((1, D_MODEL), fixed),
            _mod_spec(0),
            _mod_spec(1),
            pl.BlockSpec((D_MODEL, IN_WIDTH), fixed, pipeline_mode=pl.Buffered(1)),
            pl.BlockSpec((q_table.shape[0], tm), colb),
            pl.BlockSpec((tm, k_table.shape[1]), row),
        ],
        out_specs=[
            pl.BlockSpec((ATTN_WIDTH, tm), colb),
            pl.BlockSpec((tm, KV_WIDTH), row),
            pl.BlockSpec((KV_WIDTH, tm), colb),
            pl.BlockSpec((tm, POOL_WIDTH), row),
        ],
        out_shape=[
            jax.ShapeDtypeStruct((ATTN_WIDTH, SEQ), BF16),
            jax.ShapeDtypeStruct((SEQ, KV_WIDTH), BF16),
            jax.ShapeDtypeStruct((KV_WIDTH, SEQ), BF16),
            jax.ShapeDtypeStruct((SEQ, POOL_WIDTH), F32),
        ],
        scratch_shapes=[pltpu.VMEM((D_MODEL, IN_WIDTH), BF16)],
        compiler_params=_params("arbitrary"),
        name="inproj",
    )(x, norm_w, mod, mod, w_in, q_table, k_table)


def _attn_kernel(qt_ref, kp_ref, km_ref, kn_ref, vtp_ref, vtm_ref, vtn_ref,
                 kc_ref, vtc_ref, sink_ref, wout_ref, wup_ref, wdn_ref,
                 c_ref, cctx_ref, wada_ref, bada_ref,
                 o_ref, wout_b_ref, wup_b_ref, wdn_b_ref, mod_ref):
    cond_act = _silu_rows(c_ref, cctx_ref)
    ada_cols = [slice(c0, c0 + 2 * LANES) for c0 in range(0, mod_ref.shape[1], 2 * LANES)]

    def ada_piece(idx, n):
        for cols in ada_cols[idx * len(ada_cols) // n:(idx + 1) * len(ada_cols) // n]:
            mod_ref[:, cols] = _ada_cols(cond_act, wada_ref, bada_ref, cols)

    i = pl.program_id(0)
    nsub = ATTN_TQ // BLOCK
    nblk = SEQ // BLOCK
    pw = 2 * BLOCK
    nkeys = 3 * BLOCK + CTX_LEN

    key_j = lax.broadcasted_iota(jnp.int32, (BLOCK, pw), 0)
    qry_i = lax.broadcasted_iota(jnp.int32, (BLOCK, pw), 1) % BLOCK
    neg = jnp.full((BLOCK, pw), NEG_INF, F32)
    zero = jnp.zeros((BLOCK, pw), F32)
    band_prev = jnp.where(key_j >= qry_i, zero, neg)
    band_next = jnp.where(key_j <= qry_i, zero, neg)

    kc = kc_ref[...]
    vtc = vtc_ref[...]
    ones_rows = jnp.ones((2 * SUBLANES, nkeys), BF16)
    sink_row = sink_ref[...] * LOG2E

    windows = []
    for sb in range(nsub):
        n = i * nsub + sb
        lo, hi = sb * BLOCK, (sb + 1) * BLOCK
        k_prev = kp_ref[...] if sb == 0 else km_ref[lo - BLOCK:lo, :]
        k_next = kn_ref[...] if sb == nsub - 1 else km_ref[hi:hi + BLOCK, :]
        vt_prev = vtp_ref[...] if sb == 0 else vtm_ref[:, lo - BLOCK:lo]
        vt_next = vtn_ref[...] if sb == nsub - 1 else vtm_ref[:, hi:hi + BLOCK]
        windows.append(dict(
            kwin=jnp.concatenate([k_prev, km_ref[lo:hi, :], k_next, kc], axis=0),
            vtwin=jnp.concatenate([vt_prev, vtm_ref[:, lo:hi], vt_next, vtc], axis=1),
            bias_prev=band_prev + jnp.where(n > 0, 0.0, NEG_INF),
            bias_next=band_next + jnp.where(n < nblk - 1, 0.0, NEG_INF)))

    units = [(sb, h, half) for sb in range(nsub) for h in range(N_KV_HEADS)
             for half in range(GQA // 2)]

    def scores(unit):
        sb, h, half = unit
        win = windows[sb]
        lo, hi = sb * BLOCK, (sb + 1) * BLOCK
        hd0 = GQA * h + 2 * half
        qh = jnp.concatenate([qt_ref[(hd0 + g) * HEAD_DIM:(hd0 + g + 1) * HEAD_DIM, lo:hi]
                              for g in range(2)], axis=1)
        blocks = []
        if h > 0:
            blocks.append(jnp.zeros((h * HEAD_DIM, pw), BF16))
        blocks.append(qh)
        if h < N_KV_HEADS - 1:
            blocks.append(jnp.zeros(((N_KV_HEADS - 1 - h) * HEAD_DIM, pw), BF16))
        s = jnp.dot(win["kwin"], jnp.concatenate(blocks, axis=0),
                    preferred_element_type=F32)
        s = jnp.concatenate([s[0:BLOCK] + win["bias_prev"], s[BLOCK:2 * BLOCK],
                             s[2 * BLOCK:3 * BLOCK] + win["bias_next"], s[3 * BLOCK:]], axis=0)
        return s, jnp.max(s, axis=0, keepdims=True)

    def probs(unit, s, smax):
        sb, h, half = unit
        hd0 = GQA * h + 2 * half
        sink = sink_row[:, hd0 * BLOCK:(hd0 + 2) * BLOCK]
        m = jnp.maximum(smax, sink)
        return jnp.exp2(s - m).astype(BF16), jnp.exp2(sink - m)

    def finish(unit, e, e_sink):
        sb, h, half = unit
        lo, hi = sb * BLOCK, (sb + 1) * BLOCK
        hd0 = GQA * h + 2 * half
        vt_ones = jnp.concatenate(
            [windows[sb]["vtwin"][h * HEAD_DIM:(h + 1) * HEAD_DIM, :], ones_rows], axis=0)
        o = jnp.dot(vt_ones, e, preferred_element_type=F32)
        denom = o[HEAD_DIM:HEAD_DIM + 1, :] + e_sink
        o = o[:HEAD_DIM, :] * (1.0 / denom)
        blk = jnp.concatenate([o[:, 0:BLOCK], o[:, BLOCK:2 * BLOCK]], axis=0)
        o_ref[lo:hi, hd0 * HEAD_DIM:(hd0 + 2) * HEAD_DIM] = blk.T.astype(BF16)

    cast_jobs = []
    for src, dst in ((wout_ref, wout_b_ref), (wup_ref, wup_b_ref), (wdn_ref, wdn_b_ref)):
        wc = max(LANES, CAST_JOB_ELEMS // src.shape[0])
        cast_jobs += [(src, dst, c0) + (wc,) for c0 in range(0, src.shape[1], wc)]

    def cast_piece(idx, n):
        for src, dst, c0, wc in cast_jobs[idx * len(cast_jobs) // n:(idx + 1) * len(cast_jobs) // n]:
            val = src[:, c0:c0 + wc].astype(BF16)
            if len(dst.shape) == 3:
                tf = dst.shape[2]
                dst[c0 // tf, :, c0 % tf:c0 % tf + wc] = val
            else:
                dst[:, c0:c0 + wc] = val

    nu = len(units)
    s_vals, p_vals = {}, {}
    d_probs, d_finish = ATTN_STAGE_LAG
    for step in range(nu + d_finish):
        if step < nu:
            s_vals[step] = scores(units[step])
        if d_probs <= step < nu + d_probs:
            u = step - d_probs
            p_vals[u] = probs(units[u], *s_vals.pop(u))
            cast_piece(u, nu)
            ada_piece(u, nu)
        if step >= d_finish:
            u = step - d_finish
            finish(units[u], *p_vals.pop(u))


def _attn(qt, k, vt, kc, vtc, sink, w_out, w_up, w_dn, c, c_ctx, w_ada, b_ada, ada_col0):
    tq = ATTN_TQ
    r = tq // BLOCK
    nblk = SEQ // BLOCK
    nsteps = SEQ // tq
    ada_tn = (w_ada.shape[1] - ada_col0) // nsteps
    ada_b0 = ada_col0 // ada_tn
    fixed = lambda i: (0, 0)
    row = lambda i: (i, 0)
    slab = lambda w: pl.BlockSpec((w.shape[0] // nsteps, w.shape[1]), row)
    bf16_like = lambda w: jax.ShapeDtypeStruct(w.shape, BF16)
    prev = lambda i: jnp.maximum(i * r - 1, 0)
    nxt = lambda i: jnp.minimum((i + 1) * r, nblk - 1)
    return pl.pallas_call(
        _attn_kernel,
        grid=(SEQ // tq,),
        in_specs=[
            pl.BlockSpec((ATTN_WIDTH, tq), lambda i: (0, i)),
            pl.BlockSpec((BLOCK, KV_WIDTH), lambda i: (prev(i), 0)),
            pl.BlockSpec((tq, KV_WIDTH), lambda i: (i, 0)),
            pl.BlockSpec((BLOCK, KV_WIDTH), lambda i: (nxt(i), 0)),
            pl.BlockSpec((KV_WIDTH, BLOCK), lambda i: (0, prev(i))),
            pl.BlockSpec((KV_WIDTH, tq), lambda i: (0, i)),
            pl.BlockSpec((KV_WIDTH, BLOCK), lambda i: (0, nxt(i))),
            pl.BlockSpec((CTX_LEN, KV_WIDTH), fixed),
            pl.BlockSpec((KV_WIDTH, CTX_LEN), fixed),
            pl.BlockSpec((1, N_Q_HEADS * BLOCK), fixed),
            slab(w_out), slab(w_up), slab(w_dn),
            pl.BlockSpec((1, D_MODEL), fixed),
            pl.BlockSpec((1, D_MODEL), fixed),
            pl.BlockSpec((D_MODEL, ada_tn), lambda i: (0, ada_b0 + i)),
            pl.BlockSpec((1, ada_tn), lambda i: (0, ada_b0 + i)),
        ],
        out_specs=[pl.BlockSpec((tq, ATTN_WIDTH), row), slab(w_out),
                   pl.BlockSpec((MLP_NF, w_up.shape[0] // nsteps, MLP_TF), lambda i: (0, i, 0)),
                   slab(w_dn),
                   pl.BlockSpec((SUBLANES, ada_tn), lambda i: (0, i))],
        out_shape=[jax.ShapeDtypeStruct((SEQ, ATTN_WIDTH), BF16),
                   bf16_like(w_out),
                   jax.ShapeDtypeStruct((MLP_NF, w_up.shape[0], MLP_TF), BF16),
                   bf16_like(w_dn),
                   jax.ShapeDtypeStruct((SUBLANES, w_ada.shape[1] - ada_col0), F32)],
        compiler_params=_params("arbitrary"),
        name="attn",
    )(qt, k, k, k, vt, vt, vt, kc, vtc, sink, w_out, w_up, w_dn, c, c_ctx, w_ada, b_ada)


def _mixout_kernel(x_ref, attn_ref, up_ref, um_ref, un_ref, wout_ref, pw_ref, ps_ref,
                   ga_ref, nw_ref, sh_ref, sc_ref, x1_ref, hm_ref):
    i = pl.program_id(0)
    tm = MIX_TM
    nsteps = SEQ // tm

    def u_rows(r0, c0, c1):
        parts = []
        if r0 == 0:
            parts.append(jnp.where(i > 0, up_ref[:, c0:c1], 0.0))
        lo, hi = max(r0 - POOL_HALO, 0), min(r0 + rc + POOL_HALO, tm)
        parts.append(um_ref[lo:hi, c0:c1])
        if r0 + rc == tm:
            parts.append(jnp.where(i < nsteps - 1, un_ref[:, c0:c1], 0.0))
        return jnp.concatenate(parts, axis=0)

    rc = MIX_CHUNK
    n = rc + 2 * POOL_HALO
    c = POOL_HALO
    ngroups = len(POOL_WINDOWS)
    ncol = D_MODEL // ngroups

    def pool_group(r0, g):
        w = POOL_WINDOWS[g]
        c0, c1 = g * POOL_GROUP_DIM, (g + 1) * POOL_GROUP_DIM
        e = u_rows(r0, c0, c1)
        f, span = e, 1
        while 2 * span < w:
            f = f + pltpu.roll(f, n - span, 0)
            span *= 2
        win = (pltpu.roll(f, span, 0) + f)[c:c + rc]
        t = i * tm + r0 + lax.broadcasted_iota(jnp.int32, (rc, 1), 0)
        lo = jnp.clip(t - w // 2, 0, SEQ)
        hi = jnp.clip(t - w // 2 + w, 0, SEQ)
        cnt = (hi - lo).astype(F32)
        pg = win / cnt - um_ref[r0:r0 + rc, c0:c1]
        mixed = jnp.dot(pg.astype(BF16), pw_ref[g].astype(BF16), preferred_element_type=F32)
        return (mixed * ps_ref[:, c0:c1]).astype(BF16)

    def prenorm(r0, nrows):
        rows = slice(r0, r0 + nrows)
        hm_ref[rows, :] = _rms_modulate(x1_ref[rows, :], nw_ref[...], sh_ref[LATENT_ROW],
                                        sc_ref[LATENT_ROW]).astype(BF16)

    chunk_starts = list(range(0, tm, rc))
    for ci, r0 in enumerate(chunk_starts):
        rows = slice(r0, r0 + rc)
        attn_rows = attn_ref[rows, :]
        pooled = []
        for j in range(ngroups):
            cols = slice(j * ncol, (j + 1) * ncol)
            x1_ref[rows, cols] = jnp.dot(attn_rows, wout_ref[0:ATTN_WIDTH, cols],
                                         preferred_element_type=F32)
            pooled.append(pool_group(r0, j))
        pool_rows = jnp.concatenate(pooled, axis=1)
        for j in range(ngroups):
            cols = slice(j * ncol, (j + 1) * ncol)
            o = jnp.dot(pool_rows, wout_ref[ATTN_WIDTH:, cols], preferred_element_type=F32)
            x1_ref[rows, cols] = x_ref[rows, cols] + ga_ref[LATENT_ROW, cols] * (x1_ref[rows, cols] + o)
            if ci > 0:
                prenorm(chunk_starts[ci - 1] + j * (rc // ngroups), rc // ngroups)
    for j in range(ngroups):
        prenorm(chunk_starts[-1] + j * (rc // ngroups), rc // ngroups)


def _mixout(x, attn, u, w_out_bf16, pool_w, pool_scale, mod, norm_w):
    tm = MIX_TM
    hb = tm // POOL_HALO
    nhalo = SEQ // POOL_HALO
    row = lambda i: (i, 0)
    fixed = lambda i: (0, 0)
    return pl.pallas_call(
        _mixout_kernel,
        grid=(SEQ // tm,),
        in_specs=[
            pl.BlockSpec((tm, D_MODEL), row),
            pl.BlockSpec((tm, ATTN_WIDTH), row),
            pl.BlockSpec((POOL_HALO, POOL_WIDTH), lambda i: (jnp.maximum(i * hb - 1, 0), 0)),
            pl.BlockSpec((tm, POOL_WIDTH), row),
            pl.BlockSpec((POOL_HALO, POOL_WIDTH), lambda i: (jnp.minimum((i + 1) * hb, nhalo - 1), 0)),
            pl.BlockSpec((D_MODEL, D_MODEL), fixed),
            pl.BlockSpec((len(POOL_WINDOWS), POOL_GROUP_DIM, POOL_GROUP_DIM), lambda i: (0, 0, 0)),
            pl.BlockSpec((1, POOL_WIDTH), fixed),
            _mod_spec(0),
            pl.BlockSpec((1, D_MODEL), fixed),
            _mod_spec(1),
            _mod_spec(2),
        ],
        out_specs=[pl.BlockSpec((tm, D_MODEL), row), pl.BlockSpec((tm, D_MODEL), row)],
        out_shape=[jax.ShapeDtypeStruct((SEQ, D_MODEL), F32),
                   jax.ShapeDtypeStruct((SEQ, D_MODEL), BF16)],
        compiler_params=_params("arbitrary"),
        name="mixout",
    )(x, attn, u, u, u, w_out_bf16, pool_w, pool_scale, mod, norm_w, mod, mod)


def _mlp_kernel(hm_ref, x1r_ref, wup_ref, wdn_ref, gm_ref, fw_ref, o_ref):
    f = pl.program_id(1)
    last = MLP_NF - 1
    row_chunks = [slice(r0, r0 + MLP_RC) for r0 in range(0, MLP_TM, MLP_RC)]
    gm = gm_ref[LATENT_ROW]

    def branch_out(rows):
        a = jnp.maximum(jnp.dot(hm_ref[rows, :], wup_ref[...], preferred_element_type=F32), 0.0)
        return gm * jnp.dot((a * a).astype(BF16), wdn_ref[...], preferred_element_type=F32)

    def final_norm(rows):
        x2 = o_ref[rows, :]
        y = x2 * lax.rsqrt(jnp.mean(x2 * x2, axis=-1, keepdims=True) + EPS)
        o_ref[rows, :] = y * fw_ref[...]

    @pl.when(f == 0)
    def _():
        for rows in row_chunks:
            o_ref[rows, :] = branch_out(rows)
        o_ref[0:MLP_XR, :] += x1r_ref[...]

    @pl.when(jnp.logical_and(f > 0, f < last))
    def _():
        x_rows = pl.ds(pl.multiple_of(f * MLP_XR, MLP_XR), MLP_XR)
        o_ref[x_rows, :] += x1r_ref[...]
        for rows in row_chunks:
            o_ref[rows, :] += branch_out(rows)

    @pl.when(f == last)
    def _():
        o_ref[last * MLP_XR:(last + 1) * MLP_XR, :] += x1r_ref[...]
        npiece = 4
        cw, rp = D_MODEL // npiece, MLP_RC // npiece
        for ci, rows in enumerate(row_chunks):
            a = jnp.maximum(jnp.dot(hm_ref[rows, :], wup_ref[...], preferred_element_type=F32), 0.0)
            a = (a * a).astype(BF16)
            for j in range(npiece):
                cols = slice(j * cw, (j + 1) * cw)
                o_ref[rows, cols] += gm[:, cols] * jnp.dot(a, wdn_ref[:, cols],
                                                           preferred_element_type=F32)
                if ci > 0:
                    r0 = row_chunks[ci - 1].start + j * rp
                    final_norm(slice(r0, r0 + rp))
        final_norm(row_chunks[-1])


def _mlp(hm, x1, w_up_bf16, w_dn_bf16, mod, final_w):
    tm, tf = MLP_TM, MLP_TF
    return pl.pallas_call(
        _mlp_kernel,
        grid=(SEQ // tm, MLP_NF),
        in_specs=[
            pl.BlockSpec((tm, D_MODEL), lambda i, f: (i, 0)),
            pl.BlockSpec((MLP_XR, D_MODEL), lambda i, f: (i * MLP_NF + f, 0)),
            pl.BlockSpec((None, D_MODEL, tf), lambda i, f: (f, 0, 0)),
            pl.BlockSpec((tf, D_MODEL), lambda i, f: (f, 0)),
            _mod_spec(3),
            pl.BlockSpec((1, D_MODEL), lambda i, f: (0, 0)),
        ],
        out_specs=pl.BlockSpec((tm, D_MODEL), lambda i, f: (i, 0)),
        out_shape=jax.ShapeDtypeStruct((SEQ, D_MODEL), F32),
        compiler_params=_params("arbitrary", "arbitrary"),
        name="mlp",
    )(hm, x1, w_up_bf16, w_dn_bf16, mod, final_w)


def kernel(x, c, ctx, c_ctx, norm_attn_w, norm_mlp_w, w_ada, b_ada, w_in, attn_sink,
           pool_w, pool_scale, w_out, w_mlp_up, w_mlp_down, final_norm_w):
    assert x.shape == (1, SEQ, D_MODEL) and ctx.shape == (1, CTX_LEN, D_MODEL)
    assert w_ada.shape[0] == 1, "single layer"
    x2d, ctx2d = x[0], ctx[0]

    c_ctx_row = c_ctx[None, :]
    b_row = b_ada[0][None, :]
    n_early = 2 * D_MODEL
    nw_a = norm_attn_w[0][None, :]
    mod_a, kc, vtc, sink_row = _ada(c, c_ctx_row, w_ada[0], b_row, n_early, ctx2d, nw_a, w_in[0],
                                    attn_sink[0][None, :])

    tables = tuple(jnp.asarray(t) for t in _rope_tables())
    qt, k, vt, u = _inproj(x2d, nw_a, mod_a, w_in[0], tables)
    attn, w_out_b, w_up_b, w_dn_b, mod_b = _attn(
        qt, k, vt, kc, vtc, sink_row, w_out[0], w_mlp_up[0], w_mlp_down[0],
        c, c_ctx_row, w_ada[0], b_row, n_early)

    x1, hm = _mixout(x2d, attn, u, w_out_b, pool_w[0], pool_scale[0][None, :], mod_b,
                     norm_mlp_w[0][None, :])
    out = _mlp(hm, x1, w_up_b, w_dn_b, mod_b, final_norm_w[None, :])
    return out[None]
```

```python
import numpy as np
import jax
import jax.numpy as jnp
from jax import lax
from jax.experimental import pallas as pl
from jax.experimental.pallas import tpu as pltpu

F32 = jnp.float32
BF16 = jnp.bfloat16

D_MODEL = 2048
SEQ = 8192
CTX_LEN = 256
GRID_W = 64
HEAD_DIM = 64
N_Q_HEADS = 16
N_KV_HEADS = 4
GQA = N_Q_HEADS // N_KV_HEADS
ATTN_WIDTH = N_Q_HEADS * HEAD_DIM
KV_WIDTH = N_KV_HEADS * HEAD_DIM
POOL_WINDOWS = (2, 4, 8, 16)
POOL_WIDTH = D_MODEL - ATTN_WIDTH
POOL_GROUP_DIM = POOL_WIDTH // len(POOL_WINDOWS)
IN_WIDTH = ATTN_WIDTH + 2 * KV_WIDTH + POOL_WIDTH
D_FF = 4 * D_MODEL
BLOCK = 128
ROPE_BASE = 10000.0
N_MOD = 6
EPS = 1e-6
NEG_INF = -1e30
LOG2E = 1.4426950408889634
Q_SCALE = HEAD_DIM ** -0.5 * LOG2E

LANES = 128
SUBLANES = 8
POOL_HALO = 8
LATENT_ROW = slice(0, 1)
CONTEXT_ROW = slice(1, 2)
VMEM_LIMIT = 56 * 1024 * 1024

ADA_TK = 256
INPROJ_TM = 512
ATTN_TQ = 512
CAST_JOB_ELEMS = 64 * 1024
ATTN_STAGE_LAG = (2, 5)
MIX_TM = 512
MIX_CHUNK = 256
MLP_TM = 1024
MLP_TF = 1024
MLP_RC = 512
MLP_NF = D_FF // MLP_TF
MLP_XR = MLP_TM // MLP_NF


def _params(*sem):
    return pltpu.CompilerParams(dimension_semantics=sem, vmem_limit_bytes=VMEM_LIMIT)


def _mod_spec(chunk):
    return pl.BlockSpec((SUBLANES, D_MODEL), lambda *_: (0, chunk))


def _rms_modulate(x, norm_w, shift, scale):
    gain = norm_w * (1.0 + scale)
    y = x * lax.rsqrt(jnp.mean(x * x, axis=-1, keepdims=True) + EPS)
    return y * gain + shift


def _silu_rows(c_ref, cctx_ref):
    cnd = jnp.concatenate([c_ref[...], cctx_ref[...], jnp.zeros((SUBLANES - 2, D_MODEL), F32)], axis=0)
    return cnd * (1.0 / (1.0 + jnp.exp(-cnd)))


def _ada_cols(s, w_ref, b_ref, cols=slice(None)):
    return jnp.dot(s, w_ref[:, cols], preferred_element_type=F32) + b_ref[:, cols]


def _ada_kernel(c_ref, cctx_ref, wl_ref, wr_ref, b_ref, x_ref, nw_ref, wkv_ref, o_ref, k_ref, vt_ref):
    j = pl.program_id(0)
    last = pl.num_programs(0) - 1

    @pl.when(j == 0)
    def _():
        o_ref[...] = jnp.broadcast_to(b_ref[...], o_ref.shape)

    @pl.when(j < last)
    def _():
        cnd = jnp.concatenate([c_ref[...], cctx_ref[...], jnp.zeros((SUBLANES - 2, ADA_TK), F32)], axis=0)
        act = cnd * (1.0 / (1.0 + jnp.exp(-cnd)))
        half = wl_ref.shape[1]
        o_ref[:, :half] += jnp.dot(act, wl_ref[...], preferred_element_type=F32)
        o_ref[:, half:] += jnp.dot(act, wr_ref[...], preferred_element_type=F32)

    @pl.when(j == last)
    def _():
        shift = o_ref[CONTEXT_ROW, 0:D_MODEL]
        scale = o_ref[CONTEXT_ROW, D_MODEL:2 * D_MODEL]
        h = _rms_modulate(x_ref[...], nw_ref[...], shift, scale).astype(BF16)
        p = jnp.dot(h, wkv_ref[...].astype(BF16), preferred_element_type=F32)
        k_ref[...] = p[:, :KV_WIDTH].astype(BF16)
        vt_ref[...] = p[:, KV_WIDTH:].T.astype(BF16)


def _ada(c, c_ctx, w_ada, b_ada, n, ctx, norm_w, w_in):
    nk = D_MODEL // ADA_TK
    fixed = lambda j: (0, 0)
    slab = lambda j: jnp.minimum(j, nk - 1)
    kv_block = ATTN_WIDTH // (2 * KV_WIDTH)
    return pl.pallas_call(
        _ada_kernel,
        grid=(nk + 1,),
        in_specs=[
            pl.BlockSpec((1, ADA_TK), lambda j: (0, slab(j))),
            pl.BlockSpec((1, ADA_TK), lambda j: (0, slab(j))),
            pl.BlockSpec((ADA_TK, n // 2), lambda j: (slab(j), 0)),
            pl.BlockSpec((ADA_TK, n // 2), lambda j: (slab(j), 1)),
            pl.BlockSpec((1, n), fixed),
            pl.BlockSpec((CTX_LEN, D_MODEL), fixed),
            pl.BlockSpec((1, D_MODEL), fixed),
            pl.BlockSpec((D_MODEL, 2 * KV_WIDTH), lambda j: (0, kv_block)),
        ],
        out_specs=[
            pl.BlockSpec((SUBLANES, n), fixed),
            pl.BlockSpec((CTX_LEN, KV_WIDTH), fixed),
            pl.BlockSpec((KV_WIDTH, CTX_LEN), fixed),
        ],
        out_shape=[
            jax.ShapeDtypeStruct((SUBLANES, n), F32),
            jax.ShapeDtypeStruct((CTX_LEN, KV_WIDTH), BF16),
            jax.ShapeDtypeStruct((KV_WIDTH, CTX_LEN), BF16),
        ],
        compiler_params=_params("arbitrary"),
        name="ada",
    )(c, c_ctx, w_ada, w_ada, b_ada, ctx, norm_w, w_in)


def _rope_tables():
    half = HEAD_DIM // 2
    inv_freq = ROPE_BASE ** (-np.arange(0, half, 2, dtype=np.float64) / half)
    t = np.arange(SEQ)
    row = (t // GRID_W).astype(np.float64)
    col = (t % GRID_W).astype(np.float64)
    ang_r = row[:, None] * inv_freq[None, :]
    ang_c = col[:, None] * inv_freq[None, :]
    cos_t = Q_SCALE * np.concatenate([np.cos(ang_r), np.cos(ang_c)], axis=1).T
    sin_t = Q_SCALE * np.concatenate([np.sin(ang_r), np.sin(ang_c)], axis=1).T
    cos_l = np.concatenate([np.cos(ang_r), np.cos(ang_r), np.cos(ang_c), np.cos(ang_c)], axis=1)
    zeros = np.zeros_like(ang_r)
    s_up = np.concatenate([-np.sin(ang_r), zeros, -np.sin(ang_c), zeros], axis=1)
    s_dn = np.concatenate([zeros, np.sin(ang_r), zeros, np.sin(ang_c)], axis=1)
    tile2 = lambda a: np.concatenate([a, a], axis=1)
    q_table = np.concatenate([cos_t, sin_t], axis=0).astype(np.float32)
    k_table = np.concatenate([tile2(cos_l), tile2(s_up), tile2(s_dn)], axis=1).astype(np.float32)
    return q_table, k_table


def _inproj_kernel(x_ref, nw_ref, sh_ref, sc_ref, w_ref, qtab_ref, ktab_ref,
                   qt_ref, k_ref, vt_ref, u_ref, wb_ref):
    @pl.when(pl.program_id(0) == 0)
    def _():
        for c0 in range(0, IN_WIDTH, 2 * LANES):
            wb_ref[:, c0:c0 + 2 * LANES] = w_ref[:, c0:c0 + 2 * LANES].astype(BF16)

    h = _rms_modulate(x_ref[...], nw_ref[...], sh_ref[LATENT_ROW], sc_ref[LATENT_ROW]).astype(BF16)
    p = jnp.dot(h, wb_ref[...], preferred_element_type=F32)

    qt = p[:, :ATTN_WIDTH].T
    cr, cc = qtab_ref[0:16, :], qtab_ref[16:32, :]
    sr, sc = qtab_ref[32:48, :], qtab_ref[48:64, :]
    pieces = []
    for hd in range(N_Q_HEADS):
        b = hd * HEAD_DIM
        x1r, x2r = qt[b:b + 16, :], qt[b + 16:b + 32, :]
        x1c, x2c = qt[b + 32:b + 48, :], qt[b + 48:b + 64, :]
        pieces += [x1r * cr - x2r * sr, x1r * sr + x2r * cr,
                   x1c * cc - x2c * sc, x1c * sc + x2c * cc]
    qt_ref[...] = jnp.concatenate(pieces, axis=0).astype(BF16)

    k = p[:, ATTN_WIDTH:ATTN_WIDTH + KV_WIDTH]
    cos_l, s_up, s_dn = (ktab_ref[:, j * LANES:(j + 1) * LANES] for j in range(3))
    kparts = []
    for g in range(KV_WIDTH // LANES):
        kg = k[:, g * LANES:(g + 1) * LANES]
        kparts.append(kg * cos_l + pltpu.roll(kg, LANES - 16, 1) * s_up
                      + pltpu.roll(kg, 16, 1) * s_dn)
    k_ref[...] = jnp.concatenate(kparts, axis=1).astype(BF16)

    vt_ref[...] = p[:, ATTN_WIDTH + KV_WIDTH:ATTN_WIDTH + 2 * KV_WIDTH].T.astype(BF16)
    u_ref[...] = p[:, ATTN_WIDTH + 2 * KV_WIDTH:]


def _inproj(x, norm_w, mod, w_in, tables):
    q_table, k_table = tables
    tm = INPROJ_TM
    row = lambda i: (i, 0)
    colb = lambda i: (0, i)
    fixed = lambda i: (0, 0)
    return pl.pallas_call(
        _inproj_kernel,
        grid=(SEQ // tm,),
        in_specs=[
            pl.BlockSpec((tm, D_MODEL), row),
            pl.BlockSpec((1, D_MODEL), fixed),
            _mod_spec(0),
            _mod_spec(1),
            pl.BlockSpec((D_MODEL, IN_WIDTH), fixed, pipeline_mode=pl.Buffered(1)),
            pl.BlockSpec((q_table.shape[0], tm), colb),
            pl.BlockSpec((tm, k_table.shape[1]), row),
        ],
        out_specs=[
            pl.BlockSpec((ATTN_WIDTH, tm), colb),
            pl.BlockSpec((tm, KV_WIDTH), row),
            pl.BlockSpec((KV_WIDTH, tm), colb),
            pl.BlockSpec((tm, POOL_WIDTH), row),
        ],
        out_shape=[
            jax.ShapeDtypeStruct((ATTN_WIDTH, SEQ), BF16),
            jax.ShapeDtypeStruct((SEQ, KV_WIDTH), BF16),
            jax.ShapeDtypeStruct((KV_WIDTH, SEQ), BF16),
            jax.ShapeDtypeStruct((SEQ, POOL_WIDTH), F32),
        ],
        scratch_shapes=[pltpu.VMEM((D_MODEL, IN_WIDTH), BF16)],
        compiler_params=_params("arbitrary"),
        name="inproj",
    )(x, norm_w, mod, mod, w_in, q_table, k_table)


def _attn_kernel(qt_ref, kp_ref, km_ref, kn_ref, vtp_ref, vtm_ref, vtn_ref,
                 kc_ref, vtc_ref, sink_ref, wout_ref, wup_ref, wdn_ref,
                 c_ref, cctx_ref, wada_ref, bada_ref,
                 o_ref, wout_b_ref, wup_b_ref, wdn_b_ref, mod_ref):
    cond_act = _silu_rows(c_ref, cctx_ref)
    ada_cols = [slice(c0, c0 + 2 * LANES) for c0 in range(0, mod_ref.shape[1], 2 * LANES)]

    def ada_piece(idx, n):
        for cols in ada_cols[idx * len(ada_cols) // n:(idx + 1) * len(ada_cols) // n]:
            mod_ref[:, cols] = _ada_cols(cond_act, wada_ref, bada_ref, cols)

    i = pl.program_id(0)
    nsub = ATTN_TQ // BLOCK
    nblk = SEQ // BLOCK
    pw = 2 * BLOCK
    nkeys = 3 * BLOCK + CTX_LEN

    key_j = lax.broadcasted_iota(jnp.int32, (BLOCK, pw), 0)
    qry_i = lax.broadcasted_iota(jnp.int32, (BLOCK, pw), 1) % BLOCK
    neg = jnp.full((BLOCK, pw), NEG_INF, F32)
    zero = jnp.zeros((BLOCK, pw), F32)
    band_prev = jnp.where(key_j >= qry_i, zero, neg)
    band_next = jnp.where(key_j <= qry_i, zero, neg)

    kc = kc_ref[...]
    vtc = vtc_ref[...]
    ones_rows = jnp.ones((2 * SUBLANES, nkeys), BF16)
    sink_row = sink_ref[...] * LOG2E

    windows = []
    for sb in range(nsub):
        n = i * nsub + sb
        lo, hi = sb * BLOCK, (sb + 1) * BLOCK
        k_prev = kp_ref[...] if sb == 0 else km_ref[lo - BLOCK:lo, :]
        k_next = kn_ref[...] if sb == nsub - 1 else km_ref[hi:hi + BLOCK, :]
        vt_prev = vtp_ref[...] if sb == 0 else vtm_ref[:, lo - BLOCK:lo]
        vt_next = vtn_ref[...] if sb == nsub - 1 else vtm_ref[:, hi:hi + BLOCK]
        windows.append(dict(
            kwin=jnp.concatenate([k_prev, km_ref[lo:hi, :], k_next, kc], axis=0),
            vtwin=jnp.concatenate([vt_prev, vtm_ref[:, lo:hi], vt_next, vtc], axis=1),
            bias_prev=band_prev + jnp.where(n > 0, 0.0, NEG_INF),
            bias_next=band_next + jnp.where(n < nblk - 1, 0.0, NEG_INF)))

    units = [(sb, h, half) for sb in range(nsub) for h in range(N_KV_HEADS)
             for half in range(GQA // 2)]

    def scores(unit):
        sb, h, half = unit
        win = windows[sb]
        lo, hi = sb * BLOCK, (sb + 1) * BLOCK
        hd0 = GQA * h + 2 * half
        qh = jnp.concatenate([qt_ref[(hd0 + g) * HEAD_DIM:(hd0 + g + 1) * HEAD_DIM, lo:hi]
                              for g in range(2)], axis=1)
        blocks = []
        if h > 0:
            blocks.append(jnp.zeros((h * HEAD_DIM, pw), BF16))
        blocks.append(qh)
        if h < N_KV_HEADS - 1:
            blocks.append(jnp.zeros(((N_KV_HEADS - 1 - h) * HEAD_DIM, pw), BF16))
        s = jnp.dot(win["kwin"], jnp.concatenate(blocks, axis=0),
                    preferred_element_type=F32)
        s = jnp.concatenate([s[0:BLOCK] + win["bias_prev"], s[BLOCK:2 * BLOCK],
                             s[2 * BLOCK:3 * BLOCK] + win["bias_next"], s[3 * BLOCK:]], axis=0)
        return s, jnp.max(s, axis=0, keepdims=True)

    def probs(unit, s, smax):
        sb, h, half = unit
        hd0 = GQA * h + 2 * half
        sink = sink_row[:, hd0 * BLOCK:(hd0 + 2) * BLOCK]
        m = jnp.maximum(smax, sink)
        return jnp.exp2(s - m).astype(BF16), jnp.exp2(sink - m)

    def finish(unit, e, e_sink):
        sb, h, half = unit
        lo, hi = sb * BLOCK, (sb + 1) * BLOCK
        hd0 = GQA * h + 2 * half
        vt_ones = jnp.concatenate(
            [windows[sb]["vtwin"][h * HEAD_DIM:(h + 1) * HEAD_DIM, :], ones_rows], axis=0)
        o = jnp.dot(vt_ones, e, preferred_element_type=F32)
        denom = o[HEAD_DIM:HEAD_DIM + 1, :] + e_sink
        o = o[:HEAD_DIM, :] * (1.0 / denom)
        blk = jnp.concatenate([o[:, 0:BLOCK], o[:, BLOCK:2 * BLOCK]], axis=0)
        o_ref[lo:hi, hd0 * HEAD_DIM:(hd0 + 2) * HEAD_DIM] = blk.T.astype(BF16)

    cast_jobs = []
    for src, dst in ((wout_ref, wout_b_ref), (wup_ref, wup_b_ref), (wdn_ref, wdn_b_ref)):
        wc = max(LANES, CAST_JOB_ELEMS // src.shape[0])
        cast_jobs += [(src, dst, c0) + (wc,) for c0 in range(0, src.shape[1], wc)]

    def cast_piece(idx, n):
        for src, dst, c0, wc in cast_jobs[idx * len(cast_jobs) // n:(idx + 1) * len(cast_jobs) // n]:
            val = src[:, c0:c0 + wc].astype(BF16)
            if len(dst.shape) == 3:
                tf = dst.shape[2]
                dst[c0 // tf, :, c0 % tf:c0 % tf + wc] = val
            else:
                dst[:, c0:c0 + wc] = val

    nu = len(units)
    s_vals, p_vals = {}, {}
    d_probs, d_finish = ATTN_STAGE_LAG
    for step in range(nu + d_finish):
        if step < nu:
            s_vals[step] = scores(units[step])
        if d_probs <= step < nu + d_probs:
            u = step - d_probs
            p_vals[u] = probs(units[u], *s_vals.pop(u))
            cast_piece(u, nu)
            ada_piece(u, nu)
        if step >= d_finish:
            u = step - d_finish
            finish(units[u], *p_vals.pop(u))


def _attn(qt, k, vt, kc, vtc, sink, w_out, w_up, w_dn, c, c_ctx, w_ada, b_ada, ada_col0):
    tq = ATTN_TQ
    r = tq // BLOCK
    nblk = SEQ // BLOCK
    nsteps = SEQ // tq
    ada_tn = (w_ada.shape[1] - ada_col0) // nsteps
    ada_b0 = ada_col0 // ada_tn
    fixed = lambda i: (0, 0)
    row = lambda i: (i, 0)
    slab = lambda w: pl.BlockSpec((w.shape[0] // nsteps, w.shape[1]), row)
    bf16_like = lambda w: jax.ShapeDtypeStruct(w.shape, BF16)
    prev = lambda i: jnp.maximum(i * r - 1, 0)
    nxt = lambda i: jnp.minimum((i + 1) * r, nblk - 1)
    return pl.pallas_call(
        _attn_kernel,
        grid=(SEQ // tq,),
        in_specs=[
            pl.BlockSpec((ATTN_WIDTH, tq), lambda i: (0, i)),
            pl.BlockSpec((BLOCK, KV_WIDTH), lambda i: (prev(i), 0)),
            pl.BlockSpec((tq, KV_WIDTH), lambda i: (i, 0)),
            pl.BlockSpec((BLOCK, KV_WIDTH), lambda i: (nxt(i), 0)),
            pl.BlockSpec((KV_WIDTH, BLOCK), lambda i: (0, prev(i))),
            pl.BlockSpec((KV_WIDTH, tq), lambda i: (0, i)),
            pl.BlockSpec((KV_WIDTH, BLOCK), lambda i: (0, nxt(i))),
            pl.BlockSpec((CTX_LEN, KV_WIDTH), fixed),
            pl.BlockSpec((KV_WIDTH, CTX_LEN), fixed),
            pl.BlockSpec((1, N_Q_HEADS * BLOCK), fixed),
            slab(w_out), slab(w_up), slab(w_dn),
            pl.BlockSpec((1, D_MODEL), fixed),
            pl.BlockSpec((1, D_MODEL), fixed),
            pl.BlockSpec((D_MODEL, ada_tn), lambda i: (0, ada_b0 + i)),
            pl.BlockSpec((1, ada_tn), lambda i: (0, ada_b0 + i)),
        ],
        out_specs=[pl.BlockSpec((tq, ATTN_WIDTH), row), slab(w_out),
                   pl.BlockSpec((MLP_NF, w_up.shape[0] // nsteps, MLP_TF), lambda i: (0, i, 0)),
                   slab(w_dn),
                   pl.BlockSpec((SUBLANES, ada_tn), lambda i: (0, i))],
        out_shape=[jax.ShapeDtypeStruct((SEQ, ATTN_WIDTH), BF16),
                   bf16_like(w_out),
                   jax.ShapeDtypeStruct((MLP_NF, w_up.shape[0], MLP_TF), BF16),
                   bf16_like(w_dn),
                   jax.ShapeDtypeStruct((SUBLANES, w_ada.shape[1] - ada_col0), F32)],
        compiler_params=_params("arbitrary"),
        name="attn",
    )(qt, k, k, k, vt, vt, vt, kc, vtc, sink, w_out, w_up, w_dn, c, c_ctx, w_ada, b_ada)


def _mixout_kernel(x_ref, attn_ref, up_ref, um_ref, un_ref, wout_ref, pw_ref, ps_ref,
                   ga_ref, nw_ref, sh_ref, sc_ref, x1_ref, hm_ref):
    i = pl.program_id(0)
    tm = MIX_TM
    nsteps = SEQ // tm

    def u_rows(r0, c0, c1):
        parts = []
        if r0 == 0:
            parts.append(jnp.where(i > 0, up_ref[:, c0:c1], 0.0))
        lo, hi = max(r0 - POOL_HALO, 0), min(r0 + rc + POOL_HALO, tm)
        parts.append(um_ref[lo:hi, c0:c1])
        if r0 + rc == tm:
            parts.append(jnp.where(i < nsteps - 1, un_ref[:, c0:c1], 0.0))
        return jnp.concatenate(parts, axis=0)

    rc = MIX_CHUNK
    n = rc + 2 * POOL_HALO
    c = POOL_HALO
    ngroups = len(POOL_WINDOWS)
    ncol = D_MODEL // ngroups

    def pool_group(r0, g):
        w = POOL_WINDOWS[g]
        c0, c1 = g * POOL_GROUP_DIM, (g + 1) * POOL_GROUP_DIM
        e = u_rows(r0, c0, c1)
        f, span = e, 1
        while 2 * span < w:
            f = f + pltpu.roll(f, n - span, 0)
            span *= 2
        win = (pltpu.roll(f, span, 0) + f)[c:c + rc]
        t = i * tm + r0 + lax.broadcasted_iota(jnp.int32, (rc, 1), 0)
        lo = jnp.clip(t - w // 2, 0, SEQ)
        hi = jnp.clip(t - w // 2 + w, 0, SEQ)
        cnt = (hi - lo).astype(F32)
        pg = win / cnt - um_ref[r0:r0 + rc, c0:c1]
        mixed = jnp.dot(pg.astype(BF16), pw_ref[g].astype(BF16), preferred_element_type=F32)
        return (mixed * ps_ref[:, c0:c1]).astype(BF16)

    def prenorm(r0, nrows):
        rows = slice(r0, r0 + nrows)
        hm_ref[rows, :] = _rms_modulate(x1_ref[rows, :], nw_ref[...], sh_ref[LATENT_ROW],
                                        sc_ref[LATENT_ROW]).astype(BF16)

    chunk_starts = list(range(0, tm, rc))
    for ci, r0 in enumerate(chunk_starts):
        rows = slice(r0, r0 + rc)
        attn_rows = attn_ref[rows, :]
        pooled = []
        for j in range(ngroups):
            cols = slice(j * ncol, (j + 1) * ncol)
            x1_ref[rows, cols] = jnp.dot(attn_rows, wout_ref[0:ATTN_WIDTH, cols],
                                         preferred_element_type=F32)
            pooled.append(pool_group(r0, j))
        pool_rows = jnp.concatenate(pooled, axis=1)
        for j in range(ngroups):
            cols = slice(j * ncol, (j + 1) * ncol)
            o = jnp.dot(pool_rows, wout_ref[ATTN_WIDTH:, cols], preferred_element_type=F32)
            x1_ref[rows, cols] = x_ref[rows, cols] + ga_ref[LATENT_ROW, cols] * (x1_ref[rows, cols] + o)
            if ci > 0:
                prenorm(chunk_starts[ci - 1] + j * (rc // ngroups), rc // ngroups)
    for j in range(ngroups):
        prenorm(chunk_starts[-1] + j * (rc // ngroups), rc // ngroups)


def _mixout(x, attn, u, w_out_bf16, pool_w, pool_scale, mod, norm_w):
    tm = MIX_TM
    hb = tm // POOL_HALO
    nhalo = SEQ // POOL_HALO
    row = lambda i: (i, 0)
    fixed = lambda i: (0, 0)
    return pl.pallas_call(
        _mixout_kernel,
        grid=(SEQ // tm,),
        in_specs=[
            pl.BlockSpec((tm, D_MODEL), row),
            pl.BlockSpec((tm, ATTN_WIDTH), row),
            pl.BlockSpec((POOL_HALO, POOL_WIDTH), lambda i: (jnp.maximum(i * hb - 1, 0), 0)),
            pl.BlockSpec((tm, POOL_WIDTH), row),
            pl.BlockSpec((POOL_HALO, POOL_WIDTH), lambda i: (jnp.minimum((i + 1) * hb, nhalo - 1), 0)),
            pl.BlockSpec((D_MODEL, D_MODEL), fixed),
            pl.BlockSpec((len(POOL_WINDOWS), POOL_GROUP_DIM, POOL_GROUP_DIM), lambda i: (0, 0, 0)),
            pl.BlockSpec((1, POOL_WIDTH), fixed),
            _mod_spec(0),
            pl.BlockSpec((1, D_MODEL), fixed),
            _mod_spec(1),
            _mod_spec(2),
        ],
        out_specs=[pl.BlockSpec((tm, D_MODEL), row), pl.BlockSpec((tm, D_MODEL), row)],
        out_shape=[jax.ShapeDtypeStruct((SEQ, D_MODEL), F32),
                   jax.ShapeDtypeStruct((SEQ, D_MODEL), BF16)],
        compiler_params=_params("arbitrary"),
        name="mixout",
    )(x, attn, u, u, u, w_out_bf16, pool_w, pool_scale, mod, norm_w, mod, mod)


def _mlp_kernel(hm_ref, x1r_ref, wup_ref, wdn_ref, gm_ref, fw_ref, o_ref):
    f = pl.program_id(1)
    last = MLP_NF - 1
    row_chunks = [slice(r0, r0 + MLP_RC) for r0 in range(0, MLP_TM, MLP_RC)]
    gm = gm_ref[LATENT_ROW]

    def branch_out(rows):
        a = jnp.maximum(jnp.dot(hm_ref[rows, :], wup_ref[...], preferred_element_type=F32), 0.0)
        return gm * jnp.dot((a * a).astype(BF16), wdn_ref[...], preferred_element_type=F32)

    def final_norm(rows):
        x2 = o_ref[rows, :]
        y = x2 * lax.rsqrt(jnp.mean(x2 * x2, axis=-1, keepdims=True) + EPS)
        o_ref[rows, :] = y * fw_ref[...]

    @pl.when(f == 0)
    def _():
        for rows in row_chunks:
            o_ref[rows, :] = branch_out(rows)
        o_ref[0:MLP_XR, :] += x1r_ref[...]

    @pl.when(jnp.logical_and(f > 0, f < last))
    def _():
        x_rows = pl.ds(pl.multiple_of(f * MLP_XR, MLP_XR), MLP_XR)
        o_ref[x_rows, :] += x1r_ref[...]
        for rows in row_chunks:
            o_ref[rows, :] += branch_out(rows)

    @pl.when(f == last)
    def _():
        o_ref[last * MLP_XR:(last + 1) * MLP_XR, :] += x1r_ref[...]
        npiece = 4
        cw, rp = D_MODEL // npiece, MLP_RC // npiece
        for ci, rows in enumerate(row_chunks):
            a = jnp.maximum(jnp.dot(hm_ref[rows, :], wup_ref[...], preferred_element_type=F32), 0.0)
            a = (a * a).astype(BF16)
            for j in range(npiece):
                cols = slice(j * cw, (j + 1) * cw)
                o_ref[rows, cols] += gm[:, cols] * jnp.dot(a, wdn_ref[:, cols],
                                                           preferred_element_type=F32)
                if ci > 0:
                    r0 = row_chunks[ci - 1].start + j * rp
                    final_norm(slice(r0, r0 + rp))
        final_norm(row_chunks[-1])


def _mlp(hm, x1, w_up_bf16, w_dn_bf16, mod, final_w):
    tm, tf = MLP_TM, MLP_TF
    return pl.pallas_call(
        _mlp_kernel,
        grid=(SEQ // tm, MLP_NF),
        in_specs=[
            pl.BlockSpec((tm, D_MODEL), lambda i, f: (i, 0)),
            pl.BlockSpec((MLP_XR, D_MODEL), lambda i, f: (i * MLP_NF + f, 0)),
            pl.BlockSpec((None, D_MODEL, tf), lambda i, f: (f, 0, 0)),
            pl.BlockSpec((tf, D_MODEL), lambda i, f: (f, 0)),
            _mod_spec(3),
            pl.BlockSpec((1, D_MODEL), lambda i, f: (0, 0)),
        ],
        out_specs=pl.BlockSpec((tm, D_MODEL), lambda i, f: (i, 0)),
        out_shape=jax.ShapeDtypeStruct((SEQ, D_MODEL), F32),
        compiler_params=_params("arbitrary", "arbitrary"),
        name="mlp",
    )(hm, x1, w_up_bf16, w_dn_bf16, mod, final_w)


def kernel(x, c, ctx, c_ctx, norm_attn_w, norm_mlp_w, w_ada, b_ada, w_in, attn_sink,
           pool_w, pool_scale, w_out, w_mlp_up, w_mlp_down, final_norm_w):
    assert x.shape == (1, SEQ, D_MODEL) and ctx.shape == (1, CTX_LEN, D_MODEL)
    assert w_ada.shape[0] == 1, "single layer"
    x2d, ctx2d = x[0], ctx[0]

    c_ctx_row = c_ctx[None, :]
    b_row = b_ada[0][None, :]
    n_early = 2 * D_MODEL
    nw_a = norm_attn_w[0][None, :]
    mod_a, kc, vtc = _ada(c, c_ctx_row, w_ada[0], b_row, n_early, ctx2d, nw_a, w_in[0])

    tables = tuple(jnp.asarray(t) for t in _rope_tables())
    qt, k, vt, u = _inproj(x2d, nw_a, mod_a, w_in[0], tables)

    sink_row = jnp.repeat(attn_sink[0].astype(F32), BLOCK)[None, :]
    attn, w_out_b, w_up_b, w_dn_b, mod_b = _attn(
        qt, k, vt, kc, vtc, sink_row, w_out[0], w_mlp_up[0], w_mlp_down[0],
        c, c_ctx_row, w_ada[0], b_row, n_early)

    x1, hm = _mixout(x2d, attn, u, w_out_b, pool_w[0], pool_scale[0][None, :], mod_b,
                     norm_mlp_w[0][None, :])
    out = _mlp(hm, x1, w_up_b, w_dn_b, mod_b, final_norm_w[None, :])
    return out[None]
```
